```python
import math
import jax
import jax.numpy as jnp
from jax import lax
import numpy as np

D_MODEL = 1024
BATCH = 2
SEQ = 8192
DEPTH = 2

HEAD_DIM = 64
BRANCH_WIDTH = 256
N_BRANCH = 4
SB_HEADS = 4
Q_BLOCK = 128
S5_GROUPS = 16
S5_GROUP_CH = 16
S5_STATE = 64
S5_WIDTH = S5_GROUPS * S5_GROUP_CH
GLA_HEADS = 4
GLA_DK = 32
GLA_DV = 64
GLA_GATE_RANK = 16
GLA_GATE_NORM = 16.0
GLA_CHUNK = 64
DSA_HEADS = 4
DSA_PATTERNS = ((128, 1), (512, 4), (2048, 16))
DSA_BLOCK = 128
ROPE_THETA = 10000.0
D_FF = 2816
RMS_EPS = 1e-6
IN_SPLITS = (
    3 * SB_HEADS * HEAD_DIM,
    S5_WIDTH,
    GLA_HEADS * GLA_DK,
    GLA_HEADS * GLA_DK,
    GLA_HEADS * GLA_DV,
    GLA_HEADS * GLA_DV,
    GLA_GATE_RANK,
    3 * DSA_HEADS * HEAD_DIM,
    N_BRANCH * D_MODEL,
)
IN_WIDTH = sum(IN_SPLITS)

kernel_name = "hybrid_gated_four_mixer_macaron"


def rms_norm(x, g):
    x32 = x.astype(jnp.float32)
    y = x32 * lax.rsqrt(jnp.mean(x32 * x32, axis=-1, keepdims=True) + RMS_EPS)
    return (y * g.astype(jnp.float32)).astype(x.dtype)


def swiglu(x, w_in, w_out):
    a, b = jnp.split(x @ w_in, 2, axis=-1)
    return (jax.nn.silu(a) * b) @ w_out


def rope_tables(seq):
    inv = ROPE_THETA ** (-jnp.arange(0, HEAD_DIM, 2, dtype=jnp.float32) / HEAD_DIM)
    ang = jnp.arange(seq, dtype=jnp.float32)[:, None] * inv[None, :]
    return jnp.cos(ang), jnp.sin(ang)


def apply_rope(x, cos, sin):
    half = x.shape[-1] // 2
    x1, x2 = x[..., :half], x[..., half:]
    c, s = cos[None, :, None, :], sin[None, :, None, :]
    return jnp.concatenate([x1 * c - x2 * s, x2 * c + x1 * s], axis=-1)


def stick_breaking_attention(q, k, v):
    bsz, seq, heads, hd = q.shape
    nb = seq // Q_BLOCK
    qf = q.astype(jnp.float32) * (hd ** -0.5)
    kf = k.astype(jnp.float32)
    vf = v.astype(jnp.float32)
    q_blocks = qf.reshape(bsz, nb, Q_BLOCK, heads, hd).transpose(1, 0, 3, 2, 4)
    key_pos = jnp.arange(seq)

    def one_block(args):
        qb, blk = args
        z = jnp.einsum('bhqd,bshd->bhqs', qb, kf)
        q_pos = blk * Q_BLOCK + jnp.arange(Q_BLOCK)
        strict = key_pos[None, :] < q_pos[:, None]
        log_keep = jnp.where(strict, jax.nn.log_sigmoid(-z), 0.0)
        log_after = lax.cumsum(log_keep, axis=3, reverse=True) - log_keep
        w = jnp.where(strict, jnp.exp(jax.nn.log_sigmoid(z) + log_after), 0.0)
        return jnp.einsum('bhqs,bshd->bqhd', w, vf)

    out = lax.map(one_block, (q_blocks, jnp.arange(nb)))
    return out.transpose(1, 0, 2, 3, 4).reshape(bsz, seq, heads * hd)


def s5_mixer(u, lam_re, lam_im, log_step, b_re, b_im, c_re, c_im, d_skip, w_glu):
    bsz, seq, _ = u.shape
    uf = u.astype(jnp.float32).reshape(bsz, seq, S5_GROUPS, S5_GROUP_CH)
    lam = lax.complex(lam_re.astype(jnp.float32), lam_im.astype(jnp.float32))
    dt = jnp.exp(log_step.astype(jnp.float32))[:, None]
    lam_bar = jnp.exp(lam * dt)
    b_mat = lax.complex(b_re.astype(jnp.float32), b_im.astype(jnp.float32))
    b_bar = ((lam_bar - 1.0) / lam)[..., None] * b_mat
    c_mat = lax.complex(c_re.astype(jnp.float32), c_im.astype(jnp.float32))
    bu = jnp.einsum('bsgh,gph->bsgp', uf.astype(jnp.complex64), b_bar)
    a = jnp.broadcast_to(lam_bar, bu.shape)

    def combine(e1, e2):
        a1, b1 = e1
        a2, b2 = e2
        return a1 * a2, a2 * b1 + b2

    _, states = lax.associative_scan(combine, (a, bu), axis=1)
    y = jnp.einsum('bsgp,ghp->bsgh', states, c_mat).real + d_skip.astype(jnp.float32) * uf
    y = jax.nn.gelu(y.reshape(bsz, seq, S5_WIDTH))
    ga, gb = jnp.split(y @ w_glu.astype(jnp.float32), 2, axis=-1)
    return ga * jax.nn.sigmoid(gb)


def gla_mixer(q, k, v, r, g1, w_gate, b_gate, norm_g):
    bsz, seq, _ = q.shape
    nc = seq // GLA_CHUNK
    z = g1.astype(jnp.float32) @ w_gate.astype(jnp.float32) + b_gate.astype(jnp.float32)
    log_a = jax.nn.log_sigmoid(z) / GLA_GATE_NORM

    def heads(t, dh):
        t = t.astype(jnp.float32).reshape(bsz, nc, GLA_CHUNK, GLA_HEADS, dh)
        return t.transpose(1, 0, 3, 2, 4)

    qc = heads(q, GLA_DK) * (GLA_DK ** -0.5)
    kc = heads(k, GLA_DK)
    vc = heads(v, GLA_DV)
    gc = heads(log_a, GLA_DK)
    causal = jnp.tril(jnp.ones((GLA_CHUNK, GLA_CHUNK), dtype=bool))

    def step(state, inp):
        qi, ki, vi, gi = inp
        bcum = jnp.cumsum(gi, axis=2)
        o_inter = jnp.einsum('bhcd,bhde->bhce', qi * jnp.exp(bcum), state)
        diff = bcum[:, :, :, None, :] - bcum[:, :, None, :, :]
        decay = jnp.exp(jnp.where(causal[:, :, None], diff, -jnp.inf))
        att = jnp.einsum('bhid,bhjd,bhijd->bhij', qi, ki, decay)
        o = o_inter + jnp.einsum('bhij,bhje->bhie', att, vi)
        blast = bcum[:, :, -1:, :]
        state = jnp.exp(blast[:, :, 0, :])[..., None] * state + jnp.einsum(
            'bhcd,bhce->bhde', ki * jnp.exp(blast - bcum), vi)
        return state, o

    s0 = jnp.zeros((bsz, GLA_HEADS, GLA_DK, GLA_DV), jnp.float32)
    _, o = lax.scan(step, s0, (qc, kc, vc, gc))
    o = o.transpose(1, 0, 3, 2, 4).reshape(bsz, seq, GLA_HEADS, GLA_DV)
    o = rms_norm(o, norm_g.reshape(GLA_HEADS, GLA_DV))
    return o.reshape(bsz, seq, GLA_HEADS * GLA_DV) * jax.nn.silu(r.astype(jnp.float32))


def to_residue(t, dil, padded):
    bsz, seq, heads, hd = t.shape
    t = jnp.pad(t, ((0, 0), (0, padded - seq), (0, 0), (0, 0)))
    t = t.reshape(bsz, padded // dil, dil, heads, hd).transpose(0, 2, 3, 1, 4)
    return t.reshape(bsz * dil * heads, padded // dil, hd)


def from_residue(t, bsz, dil, heads, seq):
    rest = t.shape[2:]
    length = t.shape[1]
    t = jnp.moveaxis(t.reshape((bsz, dil, heads, length) + rest), 3, 1)
    return t.reshape((bsz, length * dil, heads) + rest)[:, :seq]


def banded_window_attention(q, k, v, n_steps):
    rows, length, hd = q.shape
    nb = length // DSA_BLOCK
    qb = q.reshape(rows, nb, DSA_BLOCK, hd)

    def key_blocks(t):
        tp = jnp.pad(t, ((0, 0), (DSA_BLOCK, 0), (0, 0)))
        prev = tp[:, :length].reshape(rows, nb, DSA_BLOCK, hd)
        cur = tp[:, DSA_BLOCK:].reshape(rows, nb, DSA_BLOCK, hd)
        return jnp.concatenate([prev, cur], axis=2)

    kb, vb = key_blocks(k), key_blocks(v)
    s = jnp.einsum('nbqd,nbkd->nbqk', qb, kb) * (hd ** -0.5)
    qi = jnp.arange(DSA_BLOCK)[:, None]
    kj = jnp.arange(2 * DSA_BLOCK)[None, :]
    dist = qi + DSA_BLOCK - kj
    key_idx = jnp.arange(nb)[:, None, None] * DSA_BLOCK + kj[None] - DSA_BLOCK
    valid = ((dist >= 0) & (dist <= n_steps))[None] & (key_idx >= 0)
    s = jnp.where(valid, s, -jnp.inf)
    lse = jax.nn.logsumexp(s, axis=-1)
    p = jnp.exp(s - lse[..., None])
    o = jnp.einsum('nbqk,nbkd->nbqd', p, vb)
    return o.reshape(rows, length, hd), lse.reshape(rows, length)


def dilated_mixer(q, k, v, q_norm, k_norm, cos, sin):
    bsz, seq, heads, hd = q.shape
    qf = apply_rope(rms_norm(q.astype(jnp.float32), q_norm), cos, sin)
    kf = apply_rope(rms_norm(k.astype(jnp.float32), k_norm), cos, sin)
    vf = v.astype(jnp.float32)
    outs, lses = [], []
    for window, dil in DSA_PATTERNS:
        span = dil * DSA_BLOCK
        padded = -(-seq // span) * span
        o, lse = banded_window_attention(to_residue(qf, dil, padded), to_residue(kf, dil, padded),
                                         to_residue(vf, dil, padded), window // dil)
        outs.append(from_residue(o, bsz, dil, heads, seq))
        lses.append(from_residue(lse, bsz, dil, heads, seq))
    w = jax.nn.softmax(jnp.stack(lses, axis=0), axis=0)
    out = jnp.sum(w[..., None] * jnp.stack(outs, axis=0), axis=0)
    return out.reshape(bsz, seq, heads * hd)


def hybrid_layer(x, cos, sin, ffn1_norm, ffn1_w_in, ffn1_w_out, mix_norm, w_in,
                 s5_lam_re, s5_lam_im, s5_log_step, s5_b_re, s5_b_im, s5_c_re, s5_c_im, s5_d, s5_w_glu,
                 gla_w_gate, gla_b_gate, gla_norm, dsa_q_norm, dsa_k_norm, w_branch, w_out,
                 ffn2_norm, ffn2_w_in, ffn2_w_out):
    bsz, seq, _ = x.shape
    h = x + 0.5 * swiglu(rms_norm(x, ffn1_norm), ffn1_w_in, ffn1_w_out)
    u = rms_norm(h, mix_norm)
    proj = u @ w_in
    (sb_qkv, s5_u, gq, gk, gv, gr, gg, dsa_qkv, gates) = jnp.split(
        proj, np.cumsum(IN_SPLITS)[:-1].tolist(), axis=-1)

    sb_q, sb_k, sb_v = [t.reshape(bsz, seq, SB_HEADS, HEAD_DIM) for t in jnp.split(sb_qkv, 3, axis=-1)]
    y_sb = stick_breaking_attention(sb_q, sb_k, sb_v)
    y_s5 = s5_mixer(s5_u, s5_lam_re, s5_lam_im, s5_log_step, s5_b_re, s5_b_im, s5_c_re, s5_c_im,
                    s5_d, s5_w_glu)
    y_gla = gla_mixer(gq, gk, gv, gr, gg, gla_w_gate, gla_b_gate, gla_norm)
    d_q, d_k, d_v = [t.reshape(bsz, seq, DSA_HEADS, HEAD_DIM) for t in jnp.split(dsa_qkv, 3, axis=-1)]
    y_dsa = dilated_mixer(d_q, d_k, d_v, dsa_q_norm, dsa_k_norm, cos, sin)

    branches = jnp.stack([y_sb, y_s5, y_gla, y_dsa], axis=2).astype(x.dtype)
    gate = jax.nn.sigmoid(gates.reshape(bsz, seq, N_BRANCH, D_MODEL))
    per_branch = jnp.einsum('bsnw,nwd->bsnd', branches, w_branch)
    mixed = jnp.sum(gate * per_branch, axis=2)
    h = h + mixed @ w_out
    return h + 0.5 * swiglu(rms_norm(h, ffn2_norm), ffn2_w_in, ffn2_w_out)


def setup_inputs(seed: int = 0) -> dict:
    key = jax.random.key(seed)
    ks = jax.random.split(key, 26)
    f32 = jnp.float32
    L = DEPTH
    G, P, H = S5_GROUPS, S5_STATE, S5_GROUP_CH

    def normal(k, shape, scale):
        return jax.random.normal(k, shape, f32) * scale

    def gain(k, n):
        return 1.0 + 0.02 * jax.random.normal(k, (L, n), f32)

    return {
        'x': normal(ks[0], (BATCH, SEQ, D_MODEL), 1.0),
        'ffn1_norm': gain(ks[1], D_MODEL),
        'ffn1_w_in': normal(ks[2], (L, D_MODEL, 2 * D_FF), D_MODEL ** -0.5),
        'ffn1_w_out': normal(ks[3], (L, D_FF, D_MODEL), D_FF ** -0.5),
        'mix_norm': gain(ks[4], D_MODEL),
        'w_in': normal(ks[5], (L, D_MODEL, IN_WIDTH), D_MODEL ** -0.5),
        's5_lam_re': -0.5 + normal(ks[6], (L, G, P), 0.01),
        's5_lam_im': jnp.pi * jnp.arange(P, dtype=f32) + normal(ks[7], (L, G, P), 0.01),
        's5_log_step': jax.random.uniform(ks[8], (L, G), f32, math.log(1e-3), math.log(1e-1)),
        's5_b_re': normal(ks[9], (L, G, P, H), (2 * H) ** -0.5),
        's5_b_im': normal(ks[10], (L, G, P, H), (2 * H) ** -0.5),
        's5_c_re': normal(ks[11], (L, G, H, P), P ** -0.5),
        's5_c_im': normal(ks[12], (L, G, H, P), P ** -0.5),
        's5_d': normal(ks[13], (L, G, H), 1.0),
        's5_w_glu': normal(ks[14], (L, S5_WIDTH, 2 * S5_WIDTH), S5_WIDTH ** -0.5),
        'gla_w_gate': normal(ks[15], (L, GLA_GATE_RANK, GLA_HEADS * GLA_DK), GLA_GATE_RANK ** -0.5),
        'gla_b_gate': normal(ks[16], (L, GLA_HEADS * GLA_DK), 0.01),
        'gla_norm': gain(ks[17], GLA_HEADS * GLA_DV),
        'dsa_q_norm': gain(ks[18], HEAD_DIM),
        'dsa_k_norm': gain(ks[19], HEAD_DIM),
        'w_branch': normal(ks[20], (L, N_BRANCH, BRANCH_WIDTH, D_MODEL), BRANCH_WIDTH ** -0.5),
        'w_out': normal(ks[21], (L, D_MODEL, D_MODEL), D_MODEL ** -0.5),
        'ffn2_norm': gain(ks[22], D_MODEL),
        'ffn2_w_in': normal(ks[23], (L, D_MODEL, 2 * D_FF), D_MODEL ** -0.5),
        'ffn2_w_out': normal(ks[24], (L, D_FF, D_MODEL), D_FF ** -0.5),
    }


def reference(x, ffn1_norm, ffn1_w_in, ffn1_w_out, mix_norm, w_in, s5_lam_re, s5_lam_im, s5_log_step,
              s5_b_re, s5_b_im, s5_c_re, s5_c_im, s5_d, s5_w_glu, gla_w_gate, gla_b_gate, gla_norm,
              dsa_q_norm, dsa_k_norm, w_branch, w_out, ffn2_norm, ffn2_w_in, ffn2_w_out):
    cos, sin = rope_tables(x.shape[1])
    h = x
    for l in range(DEPTH):
        h = hybrid_layer(h, cos, sin, ffn1_norm[l], ffn1_w_in[l], ffn1_w_out[l], mix_norm[l], w_in[l],
                         s5_lam_re[l], s5_lam_im[l], s5_log_step[l], s5_b_re[l], s5_b_im[l],
                         s5_c_re[l], s5_c_im[l], s5_d[l], s5_w_glu[l], gla_w_gate[l], gla_b_gate[l],
                         gla_norm[l], dsa_q_norm[l], dsa_k_norm[l], w_branch[l], w_out[l],
                         ffn2_norm[l], ffn2_w_in[l], ffn2_w_out[l])
    return h
```

```python
import functools
import math

import jax
import jax.numpy as jnp
from jax import lax
from jax.experimental import pallas as pl
from jax.experimental.pallas import tpu as pltpu

D_MODEL = 1024
HEAD_DIM = 64
BRANCH_WIDTH = 256
N_BRANCH = 4
SB_HEADS = 4
S5_GROUPS = 16
S5_GROUP_CH = 16
S5_STATE = 64
GLA_HEADS = 4
GLA_DK = 32
GLA_DV = 64
GLA_GATE_RANK = 16
GLA_GATE_NORM = 16.0
DSA_PATTERNS = ((128, 1), (512, 4), (2048, 16))
DSA_BLOCK = 128
ROPE_THETA = 10000.0
D_FF = 2816
RMS_EPS = 1e-6

LANES = 128
VMEM_LIMIT_BYTES = 58 * 1024 * 1024
FF_CHUNK = 256
N_FF_CHUNKS = D_FF // FF_CHUNK
S5_CHUNK = 8
S5_ROW_TILE = 256
GLA_TILE = 256
GLA_BLOCK = 16
SB_Q = 128
SB_WIN = 3 * SB_Q
SB_LOG_ZERO = -104.0
NEG_BIG = -1e30

PROJ_SB = (0, 768)
PROJ_S5 = (768, 1024)
PROJ_GLA = (1024, 1792)
PROJ_DSA = (1792, 2560)
PROJ_GG = (2560, 2688)
PROJ_WIDTH = 2688

F32 = jnp.float32
BF16 = jnp.bfloat16


def _cparams(sem):
    return pltpu.CompilerParams(dimension_semantics=sem, vmem_limit_bytes=VMEM_LIMIT_BYTES)


def _const_spec(shape):
    zeros = (0,) * len(shape)
    return pl.BlockSpec(shape, lambda *_: zeros, pipeline_mode=pl.Buffered(1))


def _rms_rows(x, g):
    ms = jnp.mean(x * x, axis=-1, keepdims=True)
    return x * lax.rsqrt(ms + RMS_EPS) * g


def _hi_lo(x):
    hi = x.astype(BF16)
    lo = (x - hi.astype(F32)).astype(BF16)
    return hi, lo


def _dot(a, b):
    return jnp.dot(a, b, preferred_element_type=F32)


def _dot_nt(a, b):
    return lax.dot_general(a, b, (((1,), (1,)), ((), ())), preferred_element_type=F32)


def _dot_tn(a, b):
    return lax.dot_general(a, b, (((0,), (0,)), ((), ())), preferred_element_type=F32)


def _swiglu_into(acc_ref, xb, w1_ref, w2_ref):
    acc_ref[...] = jnp.zeros_like(acc_ref)

    def body(c, carry):
        ab = _dot(xb, w1_ref[c])
        a = ab[:, :FF_CHUNK]
        b = ab[:, FF_CHUNK:]
        hm = (a * jax.nn.sigmoid(a) * b).astype(BF16)
        acc_ref[...] += _dot(hm, w2_ref[c])
        return carry

    lax.fori_loop(0, N_FF_CHUNKS, body, 0)


def _group_mean_matrix(width, group):
    r = lax.broadcasted_iota(jnp.int32, (width, width), 0) // group
    c = lax.broadcasted_iota(jnp.int32, (width, width), 1) // group
    return jnp.where(r == c, 1.0 / group, 0.0).astype(BF16)


def _group_rms(x, gain):
    hi, lo = _hi_lo(x * x)
    gm = _group_mean_matrix(x.shape[-1], HEAD_DIM)
    ms = _dot(hi, gm) + _dot(lo, gm)
    return x * lax.rsqrt(ms + RMS_EPS) * gain


def _swap_half_heads(x):
    half = HEAD_DIM // 2
    outs = []
    for s in range(x.shape[-1] // LANES):
        xs = x[:, s * LANES:(s + 1) * LANES]
        lane = lax.broadcasted_iota(jnp.int32, xs.shape, 1)
        up = pltpu.roll(xs, LANES - half, axis=1)
        down = pltpu.roll(xs, half, axis=1)
        outs.append(jnp.where((lane % HEAD_DIM) < half, up, down))
    return jnp.concatenate(outs, axis=-1)


def _ffn_proj_kernel(x_ref, n1_ref, w1_ref, w2_ref, nm_ref, wp_ref, qn_ref, kn_ref, cos_ref, sin_ref,
                     h_ref, sb_ref, s5_ref, gla_ref, gg_ref, dsa_ref, acc_ref):
    x = x_ref[...]
    xb = _rms_rows(x, n1_ref[...]).astype(BF16)
    _swiglu_into(acc_ref, xb, w1_ref, w2_ref)
    h = x + 0.5 * acc_ref[...]
    h_ref[...] = h
    ub = _rms_rows(h, nm_ref[...]).astype(BF16)

    sb_ref[...] = _dot(ub, wp_ref[:, PROJ_SB[0]:PROJ_SB[1]]).astype(BF16)
    s5_ref[...] = _dot(ub, wp_ref[:, PROJ_S5[0]:PROJ_S5[1]])
    gla_ref[...] = _dot(ub, wp_ref[:, PROJ_GLA[0]:PROJ_GLA[1]]).astype(BF16)
    gg_ref[...] = _dot(ub, wp_ref[:, PROJ_GG[0]:PROJ_GG[1]])

    dsa = _dot(ub, wp_ref[:, PROJ_DSA[0]:PROJ_DSA[1]])
    cos = cos_ref[...]
    sin = sin_ref[...]
    for i, gain_ref in enumerate((qn_ref, kn_ref)):
        t = _group_rms(dsa[:, i * 256:(i + 1) * 256], gain_ref[...])
        t = t * cos + _swap_half_heads(t) * sin
        dsa_ref[:, i * 256:(i + 1) * 256] = t.astype(BF16)
    dsa_ref[:, 512:768] = dsa[:, 512:768].astype(BF16)


def _ffn_proj(x2, n1, w1, w2, nm, wp, qn, kn, cos_t, sin_t, seq, tm):
    tokens = x2.shape[0]
    seq_tiles = seq // tm
    row = lambda i: (i, 0)
    tab = lambda i: (i % seq_tiles, 0)
    tile = lambda w: pl.BlockSpec((tm, w), row)
    return pl.pallas_call(
        _ffn_proj_kernel,
        grid=(tokens // tm,),
        in_specs=[tile(D_MODEL), _const_spec(n1.shape), _const_spec(w1.shape), _const_spec(w2.shape),
                  _const_spec(nm.shape), _const_spec(wp.shape), _const_spec(qn.shape), _const_spec(kn.shape),
                  pl.BlockSpec((tm, 256), tab), pl.BlockSpec((tm, 256), tab)],
        out_specs=[tile(D_MODEL), tile(768), tile(256), tile(768), tile(LANES), tile(768)],
        out_shape=[jax.ShapeDtypeStruct((tokens, D_MODEL), F32),
                   jax.ShapeDtypeStruct((tokens, 768), BF16),
                   jax.ShapeDtypeStruct((tokens, 256), F32),
                   jax.ShapeDtypeStruct((tokens, 768), BF16),
                   jax.ShapeDtypeStruct((tokens, LANES), F32),
                   jax.ShapeDtypeStruct((tokens, 768), BF16)],
        scratch_shapes=[pltpu.VMEM((tm, D_MODEL), F32)],
        compiler_params=_cparams(("parallel",)),
        name="ffn_proj",
    )(x2, n1, w1, w2, nm, wp, qn, kn, cos_t, sin_t)


def _merge_ffn_kernel(h_ref, ysb_ref, ys5_ref, ygla_ref, o1_ref, o2_ref, o3_ref, l1_ref, l2_ref, l3_ref,
                      nm_ref, wg_ref, wglu_ref, wb_ref, wo_ref, n2_ref, w1_ref, w2_ref,
                      out_ref, acc_ref):
    h = h_ref[...]
    ub = _rms_rows(h, nm_ref[...]).astype(BF16)

    glu = _dot(ys5_ref[...], wglu_ref[...])
    y_s5 = glu[:, :BRANCH_WIDTH] * jax.nn.sigmoid(glu[:, BRANCH_WIDTH:])

    l1, l2, l3 = l1_ref[...], l2_ref[...], l3_ref[...]
    lm = jnp.maximum(jnp.maximum(l1, l2), l3)
    e1, e2, e3 = jnp.exp(l1 - lm), jnp.exp(l2 - lm), jnp.exp(l3 - lm)
    y_dsa = (e1 * o1_ref[...].astype(F32) + e2 * o2_ref[...].astype(F32) + e3 * o3_ref[...].astype(F32)) / (e1 + e2 + e3)

    branches = (ysb_ref[...], y_s5.astype(BF16), ygla_ref[...], y_dsa.astype(BF16))
    mixed = jnp.zeros(h.shape, F32)
    for n in range(N_BRANCH):
        gate = jax.nn.sigmoid(_dot(ub, wg_ref[n]))
        mixed = mixed + gate * _dot(branches[n], wb_ref[n])
    h2 = h + _dot(mixed.astype(BF16), wo_ref[...])

    xb = _rms_rows(h2, n2_ref[...]).astype(BF16)
    _swiglu_into(acc_ref, xb, w1_ref, w2_ref)
    out_ref[...] = h2 + 0.5 * acc_ref[...]


def _merge_ffn(h, ysb, ys5, ygla, dsa_o, dsa_l, nm, wg, wglu, wb, wo, n2, w1, w2, tm):
    tokens = h.shape[0]
    row = lambda i: (i, 0)
    tile = lambda w: pl.BlockSpec((tm, w), row)
    consts = (nm, wg, wglu, wb, wo, n2, w1, w2)
    return pl.pallas_call(
        _merge_ffn_kernel,
        grid=(tokens // tm,),
        in_specs=[tile(D_MODEL)] + [tile(BRANCH_WIDTH)] * 9 + [_const_spec(c.shape) for c in consts],
        out_specs=tile(D_MODEL),
        out_shape=jax.ShapeDtypeStruct((tokens, D_MODEL), F32),
        scratch_shapes=[pltpu.VMEM((tm, D_MODEL), F32)],
        compiler_params=_cparams(("parallel",)),
        name="merge_ffn",
    )(h, ysb, ys5, ygla, *dsa_o, *dsa_l, *consts)


def _strict_upper_and_ones(n):
    j = lax.broadcasted_iota(jnp.int32, (n, n + LANES), 0)
    s = lax.broadcasted_iota(jnp.int32, (n, n + LANES), 1)
    return jnp.where((j > s) | (s >= n), 1.0, 0.0).astype(BF16)


def _sb_scores(qm, k):
    z = _dot_nt(qm, k)
    lk = -(jnp.maximum(z, 0.0) + jnp.log(1.0 + jnp.exp(-jnp.abs(z))))
    return z, lk


def _sb_kernel(q_ref, k_ref, v_ref, o_ref):
    seq = q_ref.shape[1]
    u_win = _strict_upper_and_ones(SB_WIN)
    u_blk = _strict_upper_and_ones(SB_Q)
    lane_q = lax.broadcasted_iota(jnp.int32, (SB_Q, LANES), 1)

    def q_block(qb, carry):
        t0 = pl.multiple_of(qb * SB_Q, SB_Q)
        start = pl.multiple_of(jnp.maximum(t0 - (SB_WIN - SB_Q), 0), SB_Q)
        q = q_ref[0, pl.ds(t0, SB_Q), :] * (HEAD_DIM ** -0.5)
        kw = k_ref[0, pl.ds(start, SB_WIN), :]
        vw = v_ref[0, pl.ds(start, SB_WIN), :]
        qidx = t0 + lax.broadcasted_iota(jnp.int32, (SB_Q, SB_WIN), 0)
        kidx = start + lax.broadcasted_iota(jnp.int32, (SB_Q, SB_WIN), 1)
        strict = kidx < qidx
        outs = []
        for head in range(2):
            head_lanes = (lane_q // HEAD_DIM) == head
            qm = jnp.where(head_lanes, q, jnp.zeros_like(q))
            z, lk = _sb_scores(qm, kw)
            lk = jnp.where(strict, lk, 0.0)
            hi, lo = _hi_lo(lk)
            ac = _dot(hi, u_win) + _dot(lo, u_win)
            w = jnp.where(strict, jnp.exp(z + lk + ac[:, :SB_WIN]), 0.0)
            o = _dot(w.astype(BF16), vw)
            csum = ac[:, SB_WIN:]

            def cond(state):
                pos, cmax, _, _ = state
                return jnp.logical_and(pos > 0, cmax > SB_LOG_ZERO)

            def body(state):
                pos, _, csum, o = state
                p0 = pl.multiple_of(pos - SB_Q, SB_Q)
                kb = k_ref[0, pl.ds(p0, SB_Q), :]
                vb = v_ref[0, pl.ds(p0, SB_Q), :]
                z, lk = _sb_scores(qm, kb)
                hi, lo = _hi_lo(lk)
                ac = _dot(hi, u_blk) + _dot(lo, u_blk)
                w = jnp.exp(z + lk + ac[:, :SB_Q] + csum)
                o = o + _dot(w.astype(BF16), vb)
                csum = csum + ac[:, SB_Q:]
                return p0, jnp.max(csum), csum, o

            _, _, _, o = lax.while_loop(cond, body, (start, jnp.max(csum), csum, o))
            outs.append(o)
        o_ref[0, pl.ds(t0, SB_Q), :] = jnp.where(lane_q < HEAD_DIM, outs[0], outs[1]).astype(o_ref.dtype)
        return carry

    lax.fori_loop(0, seq // SB_Q, q_block, 0)


def _stick_breaking(sb3):
    bsz, seq, _ = sb3.shape
    spec = lambda off: pl.BlockSpec((1, seq, LANES), lambda b, hp: (b, 0, off + hp))
    return pl.pallas_call(
        _sb_kernel,
        grid=(bsz, 2),
        in_specs=[spec(0), spec(2), spec(4)],
        out_specs=pl.BlockSpec((1, seq, LANES), lambda b, hp: (b, 0, hp)),
        out_shape=jax.ShapeDtypeStruct((bsz, seq, BRANCH_WIDTH), BF16),
        compiler_params=_cparams(("parallel", "parallel")),
        name="stick_breaking",
    )(sb3, sb3, sb3)


def _s5_kernel(u_ref, m_ref, e_ref, g_ref, pw_ref, d_ref, y_ref, state_ref):
    half = S5_GROUPS * S5_STATE

    @pl.when(pl.program_id(1) == 0)
    def _():
        state_ref[...] = jnp.zeros_like(state_ref)

    u = u_ref[0]
    ub = u.astype(BF16)
    rows = u.shape[0]

    def cmul(ar, ai, xr, xi):
        return ar * xr - ai * xi, ar * xi + ai * xr

    w = _dot(ub, e_ref[...])
    wr, wi = w[:, :half], w[:, half:]
    row = lax.broadcasted_iota(jnp.int32, (rows, half), 0)
    a1r, a1i = pw_ref[0:1, :half], pw_ref[0:1, half:]
    sr, si = state_ref[0:1, :half], state_ref[0:1, half:]
    cr, ci = cmul(a1r, a1i, sr, si)
    wr = jnp.where(row == 0, wr + cr, wr)
    wi = jnp.where(row == 0, wi + ci, wi)
    step = 0
    d = 1
    while d < rows:
        ar, ai = pw_ref[step:step + 1, :half], pw_ref[step:step + 1, half:]
        pr = jnp.where(row >= d, pltpu.roll(wr, d, axis=0), 0.0)
        pi = jnp.where(row >= d, pltpu.roll(wi, d, axis=0), 0.0)
        mr, mi = cmul(ar, ai, pr, pi)
        wr, wi = wr + mr, wi + mi
        d *= 2
        step += 1
    xr = jnp.where(row == 0, sr, pltpu.roll(wr, 1, axis=0))
    xi = jnp.where(row == 0, si, pltpu.roll(wi, 1, axis=0))
    state_ref[0:1, :half] = wr[rows - 1:rows, :]
    state_ref[0:1, half:] = wi[rows - 1:rows, :]

    y = _dot(ub, m_ref[...])
    y = y + _dot(xr.astype(BF16), g_ref[:half, :]) + _dot(xi.astype(BF16), g_ref[half:, :])
    y = y + d_ref[...] * u
    y_ref[0] = jax.nn.gelu(y).astype(y_ref.dtype)


def _s5(u3, m, e, g, pw, dskip):
    bsz, rows, width = u3.shape
    tr = min(S5_ROW_TILE, rows)
    blk = pl.BlockSpec((1, tr, width), lambda b, i: (b, i, 0))
    return pl.pallas_call(
        _s5_kernel,
        grid=(bsz, rows // tr),
        in_specs=[blk, _const_spec(m.shape), _const_spec(e.shape), _const_spec(g.shape),
                  _const_spec(pw.shape), _const_spec(dskip.shape)],
        out_specs=blk,
        out_shape=jax.ShapeDtypeStruct(u3.shape, BF16),
        scratch_shapes=[pltpu.VMEM((8, width), F32)],
        compiler_params=_cparams(("parallel", "arbitrary")),
        name="s5",
    )(u3, m, e, g, pw, dskip)


def _s5_operators(lam_re, lam_im, log_step, b_re, b_im, c_re, c_im, d_skip, row_tile):
    n_g, n_p, n_h, L = S5_GROUPS, S5_STATE, S5_GROUP_CH, S5_CHUNK
    lam = lax.complex(lam_re, lam_im)
    dt = jnp.exp(log_step)[:, None]
    lam_dt = lam * dt
    lam_bar = jnp.exp(lam_dt)
    b_bar = ((lam_bar - 1.0) / lam)[..., None] * lax.complex(b_re, b_im)
    c_mat = lax.complex(c_re, c_im)
    pows = jnp.exp(lam_dt[None] * jnp.arange(L + 1, dtype=F32)[:, None, None])
    eye = jnp.eye(n_g, dtype=F32)

    kern = jnp.einsum('ghp,kgp,gpi->kghi', c_mat, pows[:L], b_bar).real
    j = jnp.arange(L)[:, None]
    i = jnp.arange(L)[None, :]
    ksel = kern[jnp.clip(i - j, 0, L - 1)] * (i >= j)[:, :, None, None, None]
    m_op = jnp.einsum('jigoh,gf->jghifo', ksel, eye).reshape(L * 256, L * 256)

    e_c = pows[L - 1 - jnp.arange(L)][:, :, :, None] * b_bar[None]
    e_c = jnp.einsum('jgph,gf->jghfp', e_c, eye.astype(e_c.dtype))
    e_op = jnp.concatenate([e_c.real.reshape(L * 256, n_g * n_p), e_c.imag.reshape(L * 256, n_g * n_p)], axis=1)

    g_c = c_mat[None] * pows[1:L + 1][:, :, None, :]
    g_c = jnp.einsum('igop,gf->fpigo', g_c, eye.astype(g_c.dtype))
    g_op = jnp.concatenate([g_c.real.reshape(n_g * n_p, L * 256), -g_c.imag.reshape(n_g * n_p, L * 256)], axis=0)

    n_steps = max(1, int(math.log2(row_tile)))
    step_pow = jnp.exp(lam_dt[None] * (L * 2.0 ** jnp.arange(n_steps, dtype=F32))[:, None, None])
    step_pow = step_pow.reshape(n_steps, n_g * n_p)
    pw = jnp.concatenate([step_pow.real, step_pow.imag], axis=1)
    pw = jnp.pad(pw, ((0, (-n_steps) % 8), (0, 0)))
    dskip = jnp.tile(d_skip.reshape(1, 256), (1, L))
    return m_op.astype(BF16), e_op.astype(BF16), g_op.astype(BF16), pw.astype(F32), dskip.astype(F32)


def _gla_kernel(x_ref, gg_ref, wgate_ref, bgate_ref, norm_ref, y_ref, state_ref):
    tile = x_ref.shape[1]
    hk = GLA_HEADS * GLA_DK
    hv = GLA_HEADS * GLA_DV

    @pl.when(pl.program_id(1) == 0)
    def _():
        state_ref[...] = jnp.zeros_like(state_ref)

    x = x_ref[0]
    q = x[:, 0:hk].astype(F32) * (GLA_DK ** -0.5)
    k = x[:, hk:2 * hk].astype(F32)
    v = x[:, 2 * hk:2 * hk + hv].astype(F32)
    r = x[:, 2 * hk + hv:].astype(F32)

    z = _dot(gg_ref[0].astype(BF16), wgate_ref[...]) + bgate_ref[...]
    g = -(jnp.maximum(-z, 0.0) + jnp.log(1.0 + jnp.exp(-jnp.abs(z)))) * (1.0 / GLA_GATE_NORM)

    ri = lax.broadcasted_iota(jnp.int32, (tile, tile), 0)
    ci = lax.broadcasted_iota(jnp.int32, (tile, tile), 1)
    same = (ri // GLA_BLOCK) == (ci // GLA_BLOCK)
    t_incl = jnp.where(same & (ci <= ri), 1.0, 0.0).astype(BF16)
    t_full = jnp.where(same, 1.0, 0.0).astype(BF16)
    g_hi, g_lo = _hi_lo(g)
    b = _dot(t_incl, g_hi) + _dot(t_incl, g_lo)
    b_end = _dot(t_full, g_hi) + _dot(t_full, g_lo)

    rk = lax.broadcasted_iota(jnp.int32, (hk, hv), 0) // GLA_DK
    cv = lax.broadcasted_iota(jnp.int32, (hk, hv), 1) // GLA_DV
    head_sum = jnp.where(rk == cv, 1.0, 0.0).astype(BF16)
    row_in_blk = lax.broadcasted_iota(jnp.int32, (tile, hk), 0) % GLA_BLOCK
    acc = jnp.zeros((tile, hv), F32)
    for off in range(GLA_BLOCK):
        k_o = pltpu.roll(k, off, axis=0) if off else k
        b_o = pltpu.roll(b, off, axis=0) if off else b
        v_o = pltpu.roll(v, off, axis=0) if off else v
        decay = jnp.exp(jnp.where(row_in_blk >= off, b - b_o, NEG_BIG))
        att = _dot((q * k_o * decay).astype(BF16), head_sum)
        acc = acc + att * v_o

    q_dec = (q * jnp.exp(b)).astype(BF16)
    k_dec = (k * jnp.exp(b_end - b)).astype(BF16)
    vb = v.astype(BF16)
    rs = lax.broadcasted_iota(jnp.int32, (hv, hk), 0) // GLA_DV
    cs = lax.broadcasted_iota(jnp.int32, (hv, hk), 1) // GLA_DK
    head_diag = rs == cs
    state = state_ref[...]
    cross = []
    for blk in range(tile // GLA_BLOCK):
        r0 = blk * GLA_BLOCK
        cross.append(_dot_nt(q_dec[r0:r0 + GLA_BLOCK], state.astype(BF16)))
        upd = _dot_tn(vb[r0:r0 + GLA_BLOCK], k_dec[r0:r0 + GLA_BLOCK])
        state = jnp.exp(b_end[r0:r0 + 1, :]) * state + jnp.where(head_diag, upd, 0.0)
    state_ref[...] = state
    o = acc + jnp.concatenate(cross, axis=0)

    o = _group_rms(o, norm_ref[...])
    y_ref[0] = (o * (r * jax.nn.sigmoid(r))).astype(y_ref.dtype)


def _gla(x3, gg3, wgate, bgate, norm):
    bsz, seq, _ = x3.shape
    tile = min(GLA_TILE, seq)
    return pl.pallas_call(
        _gla_kernel,
        grid=(bsz, seq // tile),
        in_specs=[pl.BlockSpec((1, tile, 768), lambda b, i: (b, i, 0)),
                  pl.BlockSpec((1, tile, LANES), lambda b, i: (b, i, 0)),
                  _const_spec(wgate.shape), _const_spec(bgate.shape), _const_spec(norm.shape)],
        out_specs=pl.BlockSpec((1, tile, BRANCH_WIDTH), lambda b, i: (b, i, 0)),
        out_shape=jax.ShapeDtypeStruct((bsz, seq, BRANCH_WIDTH), BF16),
        scratch_shapes=[pltpu.VMEM((GLA_HEADS * GLA_DV, GLA_HEADS * GLA_DK), F32)],
        compiler_params=_cparams(("parallel", "arbitrary")),
        name="gla",
    )(x3, gg3, wgate, bgate, norm)


def _dsa_kernel(q_ref, k_ref, v_ref, o_ref, l_ref, *, n_steps):
    length = q_ref.shape[1]
    lane_q = lax.broadcasted_iota(jnp.int32, (DSA_BLOCK, LANES), 1)

    def q_block(jb, carry):
        m0 = pl.multiple_of(jb * DSA_BLOCK, DSA_BLOCK)
        start = pl.multiple_of(jnp.maximum(m0 - DSA_BLOCK, 0), DSA_BLOCK)
        q = q_ref[0, pl.ds(m0, DSA_BLOCK), :] * (HEAD_DIM ** -0.5)
        kw = k_ref[0, pl.ds(start, 2 * DSA_BLOCK), :]
        vw = v_ref[0, pl.ds(start, 2 * DSA_BLOCK), :]
        qidx = m0 + lax.broadcasted_iota(jnp.int32, (DSA_BLOCK, 2 * DSA_BLOCK), 0)
        kidx = start + lax.broadcasted_iota(jnp.int32, (DSA_BLOCK, 2 * DSA_BLOCK), 1)
        dist = qidx - kidx
        valid = (dist >= 0) & (dist <= n_steps)
        outs, lses = [], []
        for head in range(2):
            qm = jnp.where((lane_q // HEAD_DIM) == head, q, jnp.zeros_like(q))
            s = jnp.where(valid, _dot_nt(qm, kw), NEG_BIG)
            m = jnp.max(s, axis=1, keepdims=True)
            p = jnp.exp(s - m)
            l = jnp.sum(p, axis=1, keepdims=True)
            outs.append(_dot(p.astype(BF16), vw) / l)
            lses.append(jnp.broadcast_to(m + jnp.log(l), (DSA_BLOCK, LANES)))
        first = lane_q < HEAD_DIM
        o_ref[0, pl.ds(m0, DSA_BLOCK), :] = jnp.where(first, outs[0], outs[1]).astype(o_ref.dtype)
        l_ref[0, pl.ds(m0, DSA_BLOCK), :] = jnp.where(first, lses[0], lses[1])
        return carry

    lax.fori_loop(0, length // DSA_BLOCK, q_block, 0)


def _dsa_pattern(dsa2, bsz, seq, window, dil):
    length = seq // dil
    qkv = dsa2.reshape(bsz, length, dil * 768)
    spec = lambda off: pl.BlockSpec((1, length, LANES), lambda b, r, hp: (b, 0, r * 6 + off + hp))
    out_spec = pl.BlockSpec((1, length, LANES), lambda b, r, hp: (b, 0, r * 2 + hp))
    o, lse = pl.pallas_call(
        functools.partial(_dsa_kernel, n_steps=window // dil),
        grid=(bsz, dil, 2),
        in_specs=[spec(0), spec(2), spec(4)],
        out_specs=[out_spec, out_spec],
        out_shape=[jax.ShapeDtypeStruct((bsz, length, dil * BRANCH_WIDTH), BF16),
                   jax.ShapeDtypeStruct((bsz, length, dil * BRANCH_WIDTH), F32)],
        compiler_params=_cparams(("parallel", "parallel", "parallel")),
        name=f"dsa_dil{dil}",
    )(qkv, qkv, qkv)
    return o.reshape(bsz * seq, BRANCH_WIDTH), lse.reshape(bsz * seq, BRANCH_WIDTH)


def _ffn_weights(w_in, w_out):
    a = w_in[:, :D_FF].reshape(D_MODEL, N_FF_CHUNKS, FF_CHUNK)
    b = w_in[:, D_FF:].reshape(D_MODEL, N_FF_CHUNKS, FF_CHUNK)
    w1 = jnp.concatenate([a, b], axis=-1).transpose(1, 0, 2).astype(BF16)
    w2 = w_out.reshape(N_FF_CHUNKS, FF_CHUNK, D_MODEL).astype(BF16)
    return w1, w2


def _proj_weights(w_in):
    sb, s5, gq, gk, gv, gr, gg, dsa, gates = jnp.split(
        w_in, [768, 1024, 1152, 1280, 1536, 1792, 1808, 2576], axis=1)
    gg = jnp.pad(gg, ((0, 0), (0, LANES - GLA_GATE_RANK)))
    wp = jnp.concatenate([sb, s5, gq, gk, gv, gr, dsa, gg], axis=1).astype(BF16)
    wg = gates.reshape(D_MODEL, N_BRANCH, D_MODEL).transpose(1, 0, 2).astype(BF16)
    return wp, wg


def _rope_tables(seq):
    inv = ROPE_THETA ** (-jnp.arange(0, HEAD_DIM, 2, dtype=F32) / HEAD_DIM)
    ang = jnp.arange(seq, dtype=F32)[:, None] * inv[None, :]
    cos, sin = jnp.cos(ang), jnp.sin(ang)
    cos_t = jnp.tile(jnp.concatenate([cos, cos], axis=1), (1, 4))
    sin_t = jnp.tile(jnp.concatenate([-sin, sin], axis=1), (1, 4))
    return cos_t, sin_t


def _pick_tile(n, pref):
    t = min(pref, n)
    while n % t:
        t //= 2
    return t


def kernel(x, ffn1_norm, ffn1_w_in, ffn1_w_out, mix_norm, w_in, s5_lam_re, s5_lam_im, s5_log_step, s5_b_re, s5_b_im, s5_c_re, s5_c_im, s5_d, s5_w_glu, gla_w_gate, gla_b_gate, gla_norm, dsa_q_norm, dsa_k_norm, w_branch, w_out, ffn2_norm, ffn2_w_in, ffn2_w_out):
    bsz, seq, _ = x.shape
    tokens = bsz * seq
    depth = ffn1_norm.shape[0]
    assert seq % (DSA_PATTERNS[-1][1] * DSA_BLOCK) == 0 and seq >= SB_WIN
    tm_a = _pick_tile(seq, 512)
    tm_b = _pick_tile(tokens, 256)
    s5_rows = seq // S5_CHUNK
    s5_tile = min(S5_ROW_TILE, s5_rows)
    cos_t, sin_t = _rope_tables(seq)

    h = x.reshape(tokens, D_MODEL)
    for l in range(depth):
        w1a, w2a = _ffn_weights(ffn1_w_in[l], ffn1_w_out[l])
        w1b, w2b = _ffn_weights(ffn2_w_in[l], ffn2_w_out[l])
        wp, wg = _proj_weights(w_in[l])
        row = lambda p: p[l].reshape(1, -1)
        qn = jnp.tile(row(dsa_q_norm), (1, 4))
        kn = jnp.tile(row(dsa_k_norm), (1, 4))
        h, sb, s5u, gla_in, gg, dsa = _ffn_proj(
            h, row(ffn1_norm), w1a, w2a, row(mix_norm), wp, qn, kn, cos_t, sin_t, seq, tm_a)

        y_sb = _stick_breaking(sb.reshape(bsz, seq, 768)).reshape(tokens, BRANCH_WIDTH)

        ops = _s5_operators(s5_lam_re[l], s5_lam_im[l], s5_log_step[l], s5_b_re[l], s5_b_im[l],
                            s5_c_re[l], s5_c_im[l], s5_d[l], s5_tile)
        y_s5 = _s5(s5u.reshape(bsz, s5_rows, S5_CHUNK * 256), *ops).reshape(tokens, BRANCH_WIDTH)

        wgate = jnp.pad(gla_w_gate[l], ((0, LANES - GLA_GATE_RANK), (0, 0))).astype(BF16)
        y_gla = _gla(gla_in.reshape(bsz, seq, 768), gg.reshape(bsz, seq, LANES), wgate,
                     row(gla_b_gate), row(gla_norm)).reshape(tokens, BRANCH_WIDTH)

        dsa_o, dsa_l = zip(*[_dsa_pattern(dsa, bsz, seq, window, dil) for window, dil in DSA_PATTERNS])

        h = _merge_ffn(h, y_sb, y_s5, y_gla, dsa_o, dsa_l, row(mix_norm), wg,
                       s5_w_glu[l].astype(BF16), w_branch[l].astype(BF16), w_out[l].astype(BF16),
                       row(ffn2_norm), w1b, w2b, tm_b)
    return h.reshape(bsz, seq, D_MODEL)
```

```python
import functools
import math

import jax
import jax.numpy as jnp
from jax import lax
from jax.experimental import pallas as pl
from jax.experimental.pallas import tpu as pltpu

D_MODEL = 1024
HEAD_DIM = 64
BRANCH_WIDTH = 256
N_BRANCH = 4
SB_HEADS = 4
S5_GROUPS = 16
S5_GROUP_CH = 16
S5_STATE = 64
GLA_HEADS = 4
GLA_DK = 32
GLA_DV = 64
GLA_GATE_RANK = 16
GLA_GATE_NORM = 16.0
DSA_PATTERNS = ((128, 1), (512, 4), (2048, 16))
DSA_BLOCK = 128
ROPE_THETA = 10000.0
D_FF = 2816
RMS_EPS = 1e-6

LANES = 128
VMEM_LIMIT_BYTES = 58 * 1024 * 1024
FF_CHUNK = 256
N_FF_CHUNKS = D_FF // FF_CHUNK
S5_CHUNK = 8
S5_ROW_TILE = 256
GLA_TILE = 256
GLA_BLOCK = 16
SB_Q = 128
SB_WIN = 3 * SB_Q
SB_GROUP = 2
SB_LOG_ZERO = -104.0
NEG_BIG = -1e30
DSA_UNROLL = 4

PROJ_SB = (0, 768)
PROJ_S5 = (768, 1024)
PROJ_GLA = (1024, 1792)
PROJ_DSA = (1792, 2560)
PROJ_GG = (2560, 2688)
PROJ_WIDTH = 2688

F32 = jnp.float32
BF16 = jnp.bfloat16


def _cparams(sem):
    return pltpu.CompilerParams(dimension_semantics=sem, vmem_limit_bytes=VMEM_LIMIT_BYTES)


def _const_spec(shape):
    zeros = (0,) * len(shape)
    return pl.BlockSpec(shape, lambda *_: zeros, pipeline_mode=pl.Buffered(1))


def _rms_rows(x, g):
    ms = jnp.mean(x * x, axis=-1, keepdims=True)
    return x * lax.rsqrt(ms + RMS_EPS) * g


def _hi_lo(x):
    hi = x.astype(BF16)
    lo = (x - hi.astype(F32)).astype(BF16)
    return hi, lo


def _dot(a, b):
    return jnp.dot(a, b, preferred_element_type=F32)


def _dot_nt(a, b):
    return lax.dot_general(a, b, (((1,), (1,)), ((), ())), preferred_element_type=F32)


def _dot_tn(a, b):
    return lax.dot_general(a, b, (((0,), (0,)), ((), ())), preferred_element_type=F32)


def _swiglu_into(acc_ref, xb, w1_ref, w2_ref):
    acc_ref[...] = jnp.zeros_like(acc_ref)

    def body(c, carry):
        ab = _dot(xb, w1_ref[c])
        a = ab[:, :FF_CHUNK]
        b = ab[:, FF_CHUNK:]
        hm = (a * jax.nn.sigmoid(a) * b).astype(BF16)
        acc_ref[...] += _dot(hm, w2_ref[c])
        return carry

    lax.fori_loop(0, N_FF_CHUNKS, body, 0, unroll=True)


def _group_mean_matrix(width, group):
    r = lax.broadcasted_iota(jnp.int32, (width, width), 0) // group
    c = lax.broadcasted_iota(jnp.int32, (width, width), 1) // group
    return jnp.where(r == c, 1.0 / group, 0.0).astype(BF16)


def _group_rms(x, gain):
    hi, lo = _hi_lo(x * x)
    gm = _group_mean_matrix(x.shape[-1], HEAD_DIM)
    ms = _dot(hi, gm) + _dot(lo, gm)
    return x * lax.rsqrt(ms + RMS_EPS) * gain


def _swap_half_heads(x):
    half = HEAD_DIM // 2
    outs = []
    for s in range(x.shape[-1] // LANES):
        xs = x[:, s * LANES:(s + 1) * LANES]
        lane = lax.broadcasted_iota(jnp.int32, xs.shape, 1)
        up = pltpu.roll(xs, LANES - half, axis=1)
        down = pltpu.roll(xs, half, axis=1)
        outs.append(jnp.where((lane % HEAD_DIM) < half, up, down))
    return jnp.concatenate(outs, axis=-1)


def _ffn_proj_kernel(x_ref, n1_ref, w1_ref, w2_ref, nm_ref, wp_ref, qn_ref, kn_ref, cos_ref, sin_ref,
                     h_ref, sb_ref, s5_ref, gla_ref, gg_ref, dsa_ref, acc_ref):
    x = x_ref[...]
    xb = _rms_rows(x, n1_ref[...]).astype(BF16)
    _swiglu_into(acc_ref, xb, w1_ref, w2_ref)
    h = x + 0.5 * acc_ref[...]
    h_ref[...] = h
    ub = _rms_rows(h, nm_ref[...]).astype(BF16)

    sb_ref[...] = _dot(ub, wp_ref[:, PROJ_SB[0]:PROJ_SB[1]]).astype(BF16)
    s5_ref[...] = _dot(ub, wp_ref[:, PROJ_S5[0]:PROJ_S5[1]])
    gla_ref[...] = _dot(ub, wp_ref[:, PROJ_GLA[0]:PROJ_GLA[1]]).astype(BF16)
    gg_ref[...] = _dot(ub, wp_ref[:, PROJ_GG[0]:PROJ_GG[1]])

    dsa = _dot(ub, wp_ref[:, PROJ_DSA[0]:PROJ_DSA[1]])
    cos = cos_ref[...]
    sin = sin_ref[...]
    for i, gain_ref in enumerate((qn_ref, kn_ref)):
        t = _group_rms(dsa[:, i * 256:(i + 1) * 256], gain_ref[...])
        t = t * cos + _swap_half_heads(t) * sin
        dsa_ref[:, i * 256:(i + 1) * 256] = t.astype(BF16)
    dsa_ref[:, 512:768] = dsa[:, 512:768].astype(BF16)


def _ffn_proj(x2, n1, w1, w2, nm, wp, qn, kn, cos_t, sin_t, seq, tm):
    tokens = x2.shape[0]
    seq_tiles = seq // tm
    row = lambda i: (i, 0)
    tab = lambda i: (i % seq_tiles, 0)
    tile = lambda w: pl.BlockSpec((tm, w), row)
    return pl.pallas_call(
        _ffn_proj_kernel,
        grid=(tokens // tm,),
        in_specs=[tile(D_MODEL), _const_spec(n1.shape), _const_spec(w1.shape), _const_spec(w2.shape),
                  _const_spec(nm.shape), _const_spec(wp.shape), _const_spec(qn.shape), _const_spec(kn.shape),
                  pl.BlockSpec((tm, 256), tab), pl.BlockSpec((tm, 256), tab)],
        out_specs=[tile(D_MODEL), tile(768), tile(256), tile(768), tile(LANES), tile(768)],
        out_shape=[jax.ShapeDtypeStruct((tokens, D_MODEL), F32),
                   jax.ShapeDtypeStruct((tokens, 768), BF16),
                   jax.ShapeDtypeStruct((tokens, 256), F32),
                   jax.ShapeDtypeStruct((tokens, 768), BF16),
                   jax.ShapeDtypeStruct((tokens, LANES), F32),
                   jax.ShapeDtypeStruct((tokens, 768), BF16)],
        scratch_shapes=[pltpu.VMEM((tm, D_MODEL), F32)],
        compiler_params=_cparams(("parallel",)),
        name="ffn_proj",
    )(x2, n1, w1, w2, nm, wp, qn, kn, cos_t, sin_t)


def _merge_ffn_kernel(h_ref, ysb_ref, ys5_ref, ygla_ref, o1_ref, o2_ref, o3_ref, l1_ref, l2_ref, l3_ref,
                      nm_ref, wg_ref, wglu_ref, wb_ref, wo_ref, n2_ref, w1_ref, w2_ref,
                      out_ref, acc_ref):
    h = h_ref[...]
    ub = _rms_rows(h, nm_ref[...]).astype(BF16)

    glu = _dot(ys5_ref[...], wglu_ref[...])
    y_s5 = glu[:, :BRANCH_WIDTH] * jax.nn.sigmoid(glu[:, BRANCH_WIDTH:])

    l1, l2, l3 = l1_ref[...], l2_ref[...], l3_ref[...]
    lm = jnp.maximum(jnp.maximum(l1, l2), l3)
    e1, e2, e3 = jnp.exp(l1 - lm), jnp.exp(l2 - lm), jnp.exp(l3 - lm)
    y_dsa = (e1 * o1_ref[...].astype(F32) + e2 * o2_ref[...].astype(F32) + e3 * o3_ref[...].astype(F32)) / (e1 + e2 + e3)

    branches = (ysb_ref[...], y_s5.astype(BF16), ygla_ref[...], y_dsa.astype(BF16))
    mixed = jnp.zeros(h.shape, F32)
    for n in range(N_BRANCH):
        gate = jax.nn.sigmoid(_dot(ub, wg_ref[n]))
        mixed = mixed + gate * _dot(branches[n], wb_ref[n])
    h2 = h + _dot(mixed.astype(BF16), wo_ref[...])

    xb = _rms_rows(h2, n2_ref[...]).astype(BF16)
    _swiglu_into(acc_ref, xb, w1_ref, w2_ref)
    out_ref[...] = h2 + 0.5 * acc_ref[...]


def _merge_ffn(h, ysb, ys5, ygla, dsa_o, dsa_l, nm, wg, wglu, wb, wo, n2, w1, w2, tm):
    tokens = h.shape[0]
    row = lambda i: (i, 0)
    tile = lambda w: pl.BlockSpec((tm, w), row)
    consts = (nm, wg, wglu, wb, wo, n2, w1, w2)
    return pl.pallas_call(
        _merge_ffn_kernel,
        grid=(tokens // tm,),
        in_specs=[tile(D_MODEL)] + [tile(BRANCH_WIDTH)] * 9 + [_const_spec(c.shape) for c in consts],
        out_specs=tile(D_MODEL),
        out_shape=jax.ShapeDtypeStruct((tokens, D_MODEL), F32),
        scratch_shapes=[pltpu.VMEM((tm, D_MODEL), F32)],
        compiler_params=_cparams(("parallel",)),
        name="merge_ffn",
    )(h, ysb, ys5, ygla, *dsa_o, *dsa_l, *consts)


def _later_key_matrix(n):
    j = lax.broadcasted_iota(jnp.int32, (n, n), 0)
    s = lax.broadcasted_iota(jnp.int32, (n, n), 1)
    return jnp.where(j > s, 1.0, 0.0).astype(BF16)


def _sb_scores(qm, k):
    z = _dot_nt(qm, k)
    lk = -(jnp.maximum(z, 0.0) + jnp.log(1.0 + jnp.exp(-jnp.abs(z))))
    return z, lk


def _sb_kernel(q_ref, k_ref, v_ref, o_ref):
    seq = q_ref.shape[1]
    u_win = _later_key_matrix(SB_WIN)
    u_blk = _later_key_matrix(SB_Q)
    lane_q = lax.broadcasted_iota(jnp.int32, (SB_Q, LANES), 1)
    n_streams = 2 * SB_GROUP

    def q_group(qg, carry):
        starts, qms, csums, outs = [], [], [], []
        for sub in range(SB_GROUP):
            t0 = pl.multiple_of((qg * SB_GROUP + sub) * SB_Q, SB_Q)
            start = pl.multiple_of(jnp.maximum(t0 - (SB_WIN - SB_Q), 0), SB_Q)
            q = q_ref[0, pl.ds(t0, SB_Q), :] * (HEAD_DIM ** -0.5)
            kw = k_ref[0, pl.ds(start, SB_WIN), :]
            vw = v_ref[0, pl.ds(start, SB_WIN), :]
            qidx = t0 + lax.broadcasted_iota(jnp.int32, (SB_Q, SB_WIN), 0)
            kidx = start + lax.broadcasted_iota(jnp.int32, (SB_Q, SB_WIN), 1)
            strict = kidx < qidx
            starts.append(start)
            for head in range(2):
                qm = jnp.where((lane_q // HEAD_DIM) == head, q, jnp.zeros_like(q))
                z, lk = _sb_scores(qm, kw)
                lk = jnp.where(strict, lk, 0.0)
                later = _dot(lk.astype(BF16), u_win)
                w = jnp.where(strict, jnp.exp(z + lk + later), 0.0)
                qms.append(qm)
                outs.append(_dot(w.astype(BF16), vw))
                csums.append(later[:, 0:1] + lk[:, 0:1])

        def cond(state):
            step, cmaxes, _, _ = state
            flags = [jnp.logical_and(starts[s // 2] - (step + 1) * SB_Q >= 0, cmaxes[s] > SB_LOG_ZERO)
                     for s in range(n_streams)]
            return functools.reduce(jnp.logical_or, flags)

        def body(state):
            step, _, csums, outs = state
            new_c, new_o = [], []
            for s in range(n_streams):
                nxt = starts[s // 2] - (step + 1) * SB_Q
                in_range = nxt >= 0
                p0 = pl.multiple_of(jnp.maximum(nxt, 0), SB_Q)
                kb = k_ref[0, pl.ds(p0, SB_Q), :]
                vb = v_ref[0, pl.ds(p0, SB_Q), :]
                z, lk = _sb_scores(qms[s], kb)
                later = _dot(lk.astype(BF16), u_blk)
                w = jnp.where(in_range, jnp.exp(z + lk + later + csums[s]), 0.0)
                new_o.append(outs[s] + _dot(w.astype(BF16), vb))
                new_c.append(csums[s] + jnp.where(in_range, later[:, 0:1] + lk[:, 0:1], 0.0))
            return step + 1, tuple(jnp.max(c) for c in new_c), tuple(new_c), tuple(new_o)

        init = (jnp.int32(0), tuple(jnp.max(c) for c in csums), tuple(csums), tuple(outs))
        _, _, _, outs = lax.while_loop(cond, body, init)
        for sub in range(SB_GROUP):
            t0 = pl.multiple_of((qg * SB_GROUP + sub) * SB_Q, SB_Q)
            o_ref[0, pl.ds(t0, SB_Q), :] = jnp.where(
                lane_q < HEAD_DIM, outs[2 * sub], outs[2 * sub + 1]).astype(o_ref.dtype)
        return carry

    lax.fori_loop(0, seq // (SB_Q * SB_GROUP), q_group, 0)


def _stick_breaking(sb3):
    bsz, seq, _ = sb3.shape
    spec = lambda off: pl.BlockSpec((1, seq, LANES), lambda b, hp: (b, 0, off + hp))
    return pl.pallas_call(
        _sb_kernel,
        grid=(bsz, 2),
        in_specs=[spec(0), spec(2), spec(4)],
        out_specs=pl.BlockSpec((1, seq, LANES), lambda b, hp: (b, 0, hp)),
        out_shape=jax.ShapeDtypeStruct((bsz, seq, BRANCH_WIDTH), BF16),
        compiler_params=_cparams(("parallel", "parallel")),
        name="stick_breaking",
    )(sb3, sb3, sb3)


def _s5_kernel(u_ref, m_ref, e_ref, g_ref, pw_ref, d_ref, y_ref, state_ref):
    half = S5_GROUPS * S5_STATE

    @pl.when(pl.program_id(1) == 0)
    def _():
        state_ref[...] = jnp.zeros_like(state_ref)

    u = u_ref[0]
    ub = u.astype(BF16)
    rows = u.shape[0]

    def cmul(ar, ai, xr, xi):
        return ar * xr - ai * xi, ar * xi + ai * xr

    w = _dot(ub, e_ref[...])
    wr, wi = w[:, :half], w[:, half:]
    row = lax.broadcasted_iota(jnp.int32, (rows, half), 0)
    a1r, a1i = pw_ref[0:1, :half], pw_ref[0:1, half:]
    sr, si = state_ref[0:1, :half], state_ref[0:1, half:]
    cr, ci = cmul(a1r, a1i, sr, si)
    wr = jnp.where(row == 0, wr + cr, wr)
    wi = jnp.where(row == 0, wi + ci, wi)
    step = 0
    d = 1
    while d < rows:
        ar, ai = pw_ref[step:step + 1, :half], pw_ref[step:step + 1, half:]
        pr = jnp.where(row >= d, pltpu.roll(wr, d, axis=0), 0.0)
        pi = jnp.where(row >= d, pltpu.roll(wi, d, axis=0), 0.0)
        mr, mi = cmul(ar, ai, pr, pi)
        wr, wi = wr + mr, wi + mi
        d *= 2
        step += 1
    xr = jnp.where(row == 0, sr, pltpu.roll(wr, 1, axis=0))
    xi = jnp.where(row == 0, si, pltpu.roll(wi, 1, axis=0))
    state_ref[0:1, :half] = wr[rows - 1:rows, :]
    state_ref[0:1, half:] = wi[rows - 1:rows, :]

    y = _dot(ub, m_ref[...])
    y = y + _dot(xr.astype(BF16), g_ref[:half, :]) + _dot(xi.astype(BF16), g_ref[half:, :])
    y = y + d_ref[...] * u
    y_ref[0] = jax.nn.gelu(y).astype(y_ref.dtype)


def _s5(u3, m, e, g, pw, dskip):
    bsz, rows, width = u3.shape
    tr = min(S5_ROW_TILE, rows)
    blk = pl.BlockSpec((1, tr, width), lambda b, i: (b, i, 0))
    return pl.pallas_call(
        _s5_kernel,
        grid=(bsz, rows // tr),
        in_specs=[blk, _const_spec(m.shape), _const_spec(e.shape), _const_spec(g.shape),
                  _const_spec(pw.shape), _const_spec(dskip.shape)],
        out_specs=blk,
        out_shape=jax.ShapeDtypeStruct(u3.shape, BF16),
        scratch_shapes=[pltpu.VMEM((8, width), F32)],
        compiler_params=_cparams(("parallel", "arbitrary")),
        name="s5",
    )(u3, m, e, g, pw, dskip)


def _s5_operators(lam_re, lam_im, log_step, b_re, b_im, c_re, c_im, d_skip, row_tile):
    n_g, n_p, n_h, L = S5_GROUPS, S5_STATE, S5_GROUP_CH, S5_CHUNK
    lam = lax.complex(lam_re, lam_im)
    dt = jnp.exp(log_step)[:, None]
    lam_dt = lam * dt
    lam_bar = jnp.exp(lam_dt)
    b_bar = ((lam_bar - 1.0) / lam)[..., None] * lax.complex(b_re, b_im)
    c_mat = lax.complex(c_re, c_im)
    pows = jnp.exp(lam_dt[None] * jnp.arange(L + 1, dtype=F32)[:, None, None])
    eye = jnp.eye(n_g, dtype=F32)

    kern = jnp.einsum('ghp,kgp,gpi->kghi', c_mat, pows[:L], b_bar).real
    j = jnp.arange(L)[:, None]
    i = jnp.arange(L)[None, :]
    ksel = kern[jnp.clip(i - j, 0, L - 1)] * (i >= j)[:, :, None, None, None]
    m_op = jnp.einsum('jigoh,gf->jghifo', ksel, eye).reshape(L * 256, L * 256)

    e_c = pows[L - 1 - jnp.arange(L)][:, :, :, None] * b_bar[None]
    e_c = jnp.einsum('jgph,gf->jghfp', e_c, eye.astype(e_c.dtype))
    e_op = jnp.concatenate([e_c.real.reshape(L * 256, n_g * n_p), e_c.imag.reshape(L * 256, n_g * n_p)], axis=1)

    g_c = c_mat[None] * pows[1:L + 1][:, :, None, :]
    g_c = jnp.einsum('igop,gf->fpigo', g_c, eye.astype(g_c.dtype))
    g_op = jnp.concatenate([g_c.real.reshape(n_g * n_p, L * 256), -g_c.imag.reshape(n_g * n_p, L * 256)], axis=0)

    n_steps = max(1, int(math.log2(row_tile)))
    step_pow = jnp.exp(lam_dt[None] * (L * 2.0 ** jnp.arange(n_steps, dtype=F32))[:, None, None])
    step_pow = step_pow.reshape(n_steps, n_g * n_p)
    pw = jnp.concatenate([step_pow.real, step_pow.imag], axis=1)
    pw = jnp.pad(pw, ((0, (-n_steps) % 8), (0, 0)))
    dskip = jnp.tile(d_skip.reshape(1, 256), (1, L))
    return m_op.astype(BF16), e_op.astype(BF16), g_op.astype(BF16), pw.astype(F32), dskip.astype(F32)


def _gla_kernel(x_ref, gg_ref, wgate_ref, bgate_ref, norm_ref, y_ref, state_ref):
    tile = x_ref.shape[1]
    hk = GLA_HEADS * GLA_DK
    hv = GLA_HEADS * GLA_DV

    @pl.when(pl.program_id(1) == 0)
    def _():
        state_ref[...] = jnp.zeros_like(state_ref)

    x = x_ref[0]
    q = x[:, 0:hk].astype(F32) * (GLA_DK ** -0.5)
    k = x[:, hk:2 * hk].astype(F32)
    v = x[:, 2 * hk:2 * hk + hv].astype(F32)
    r = x[:, 2 * hk + hv:].astype(F32)

    z = _dot(gg_ref[0].astype(BF16), wgate_ref[...]) + bgate_ref[...]
    g = -(jnp.maximum(-z, 0.0) + jnp.log(1.0 + jnp.exp(-jnp.abs(z)))) * (1.0 / GLA_GATE_NORM)

    ri = lax.broadcasted_iota(jnp.int32, (tile, tile), 0)
    ci = lax.broadcasted_iota(jnp.int32, (tile, tile), 1)
    same = (ri // GLA_BLOCK) == (ci // GLA_BLOCK)
    t_incl = jnp.where(same & (ci <= ri), 1.0, 0.0).astype(BF16)
    t_full = jnp.where(same, 1.0, 0.0).astype(BF16)
    g_hi, g_lo = _hi_lo(g)
    b = _dot(t_incl, g_hi) + _dot(t_incl, g_lo)
    b_end = _dot(t_full, g_hi) + _dot(t_full, g_lo)

    rk = lax.broadcasted_iota(jnp.int32, (hk, hv), 0) // GLA_DK
    cv = lax.broadcasted_iota(jnp.int32, (hk, hv), 1) // GLA_DV
    head_sum = jnp.where(rk == cv, 1.0, 0.0).astype(BF16)
    row_in_blk = lax.broadcasted_iota(jnp.int32, (tile, hk), 0) % GLA_BLOCK
    acc = jnp.zeros((tile, hv), F32)
    for off in range(GLA_BLOCK):
        k_o = pltpu.roll(k, off, axis=0) if off else k
        b_o = pltpu.roll(b, off, axis=0) if off else b
        v_o = pltpu.roll(v, off, axis=0) if off else v
        decay = jnp.exp(jnp.where(row_in_blk >= off, b - b_o, NEG_BIG))
        att = _dot((q * k_o * decay).astype(BF16), head_sum)
        acc = acc + att * v_o

    q_dec = (q * jnp.exp(b)).astype(BF16)
    k_dec = (k * jnp.exp(b_end - b)).astype(BF16)
    vb = v.astype(BF16)
    rs = lax.broadcasted_iota(jnp.int32, (hv, hk), 0) // GLA_DV
    cs = lax.broadcasted_iota(jnp.int32, (hv, hk), 1) // GLA_DK
    head_diag = rs == cs
    state = state_ref[...]
    cross = []
    for blk in range(tile // GLA_BLOCK):
        r0 = blk * GLA_BLOCK
        cross.append(_dot_nt(q_dec[r0:r0 + GLA_BLOCK], state.astype(BF16)))
        upd = _dot_tn(vb[r0:r0 + GLA_BLOCK], k_dec[r0:r0 + GLA_BLOCK])
        state = jnp.exp(b_end[r0:r0 + 1, :]) * state + jnp.where(head_diag, upd, 0.0)
    state_ref[...] = state
    o = acc + jnp.concatenate(cross, axis=0)

    o = _group_rms(o, norm_ref[...])
    y_ref[0] = (o * (r * jax.nn.sigmoid(r))).astype(y_ref.dtype)


def _gla(x3, gg3, wgate, bgate, norm):
    bsz, seq, _ = x3.shape
    tile = min(GLA_TILE, seq)
    return pl.pallas_call(
        _gla_kernel,
        grid=(bsz, seq // tile),
        in_specs=[pl.BlockSpec((1, tile, 768), lambda b, i: (b, i, 0)),
                  pl.BlockSpec((1, tile, LANES), lambda b, i: (b, i, 0)),
                  _const_spec(wgate.shape), _const_spec(bgate.shape), _const_spec(norm.shape)],
        out_specs=pl.BlockSpec((1, tile, BRANCH_WIDTH), lambda b, i: (b, i, 0)),
        out_shape=jax.ShapeDtypeStruct((bsz, seq, BRANCH_WIDTH), BF16),
        scratch_shapes=[pltpu.VMEM((GLA_HEADS * GLA_DV, GLA_HEADS * GLA_DK), F32)],
        compiler_params=_cparams(("parallel", "arbitrary")),
        name="gla",
    )(x3, gg3, wgate, bgate, norm)


def _dsa_kernel(q_ref, k_ref, v_ref, o_ref, l_ref, *, n_steps):
    length = q_ref.shape[1]
    lane_q = lax.broadcasted_iota(jnp.int32, (DSA_BLOCK, LANES), 1)

    def q_block(jb, carry):
        m0 = pl.multiple_of(jb * DSA_BLOCK, DSA_BLOCK)
        start = pl.multiple_of(jnp.maximum(m0 - DSA_BLOCK, 0), DSA_BLOCK)
        q = q_ref[0, pl.ds(m0, DSA_BLOCK), :] * (HEAD_DIM ** -0.5)
        kw = k_ref[0, pl.ds(start, 2 * DSA_BLOCK), :]
        vw = v_ref[0, pl.ds(start, 2 * DSA_BLOCK), :]
        qidx = m0 + lax.broadcasted_iota(jnp.int32, (DSA_BLOCK, 2 * DSA_BLOCK), 0)
        kidx = start + lax.broadcasted_iota(jnp.int32, (DSA_BLOCK, 2 * DSA_BLOCK), 1)
        dist = qidx - kidx
        valid = (dist >= 0) & (dist <= n_steps)
        outs, lses = [], []
        for head in range(2):
            qm = jnp.where((lane_q // HEAD_DIM) == head, q, jnp.zeros_like(q))
            s = jnp.where(valid, _dot_nt(qm, kw), NEG_BIG)
            m = jnp.max(s, axis=1, keepdims=True)
            p = jnp.exp(s - m)
            l = jnp.sum(p, axis=1, keepdims=True)
            outs.append(_dot(p.astype(BF16), vw) / l)
            lses.append(jnp.broadcast_to(m + jnp.log(l), (DSA_BLOCK, LANES)))
        first = lane_q < HEAD_DIM
        o_ref[0, pl.ds(m0, DSA_BLOCK), :] = jnp.where(first, outs[0], outs[1]).astype(o_ref.dtype)
        l_ref[0, pl.ds(m0, DSA_BLOCK), :] = jnp.where(first, lses[0], lses[1])
        return carry

    lax.fori_loop(0, length // DSA_BLOCK, q_block, 0, unroll=DSA_UNROLL)


def _dsa_pattern(dsa2, bsz, seq, window, dil):
    length = seq // dil
    qkv = dsa2.reshape(bsz, length, dil * 768)
    spec = lambda off: pl.BlockSpec((1, length, LANES), lambda b, r, hp: (b, 0, r * 6 + off + hp))
    out_spec = pl.BlockSpec((1, length, LANES), lambda b, r, hp: (b, 0, r * 2 + hp))
    o, lse = pl.pallas_call(
        functools.partial(_dsa_kernel, n_steps=window // dil),
        grid=(bsz, dil, 2),
        in_specs=[spec(0), spec(2), spec(4)],
        out_specs=[out_spec, out_spec],
        out_shape=[jax.ShapeDtypeStruct((bsz, length, dil * BRANCH_WIDTH), BF16),
                   jax.ShapeDtypeStruct((bsz, length, dil * BRANCH_WIDTH), F32)],
        compiler_params=_cparams(("parallel", "parallel", "parallel")),
        name=f"dsa_dil{dil}",
    )(qkv, qkv, qkv)
    return o.reshape(bsz * seq, BRANCH_WIDTH), lse.reshape(bsz * seq, BRANCH_WIDTH)


def _ffn_weights(w_in, w_out):
    a = w_in[:, :D_FF].reshape(D_MODEL, N_FF_CHUNKS, FF_CHUNK)
    b = w_in[:, D_FF:].reshape(D_MODEL, N_FF_CHUNKS, FF_CHUNK)
    w1 = jnp.concatenate([a, b], axis=-1).transpose(1, 0, 2).astype(BF16)
    w2 = w_out.reshape(N_FF_CHUNKS, FF_CHUNK, D_MODEL).astype(BF16)
    return w1, w2


def _proj_weights(w_in):
    sb, s5, gq, gk, gv, gr, gg, dsa, gates = jnp.split(
        w_in, [768, 1024, 1152, 1280, 1536, 1792, 1808, 2576], axis=1)
    gg = jnp.pad(gg, ((0, 0), (0, LANES - GLA_GATE_RANK)))
    wp = jnp.concatenate([sb, s5, gq, gk, gv, gr, dsa, gg], axis=1).astype(BF16)
    wg = gates.reshape(D_MODEL, N_BRANCH, D_MODEL).transpose(1, 0, 2).astype(BF16)
    return wp, wg


def _rope_tables(seq):
    inv = ROPE_THETA ** (-jnp.arange(0, HEAD_DIM, 2, dtype=F32) / HEAD_DIM)
    ang = jnp.arange(seq, dtype=F32)[:, None] * inv[None, :]
    cos, sin = jnp.cos(ang), jnp.sin(ang)
    cos_t = jnp.tile(jnp.concatenate([cos, cos], axis=1), (1, 4))
    sin_t = jnp.tile(jnp.concatenate([-sin, sin], axis=1), (1, 4))
    return cos_t, sin_t


def _pick_tile(n, pref):
    t = min(pref, n)
    while n % t:
        t //= 2
    return t


def kernel(x, ffn1_norm, ffn1_w_in, ffn1_w_out, mix_norm, w_in, s5_lam_re, s5_lam_im, s5_log_step, s5_b_re, s5_b_im, s5_c_re, s5_c_im, s5_d, s5_w_glu, gla_w_gate, gla_b_gate, gla_norm, dsa_q_norm, dsa_k_norm, w_branch, w_out, ffn2_norm, ffn2_w_in, ffn2_w_out):
    bsz, seq, _ = x.shape
    tokens = bsz * seq
    depth = ffn1_norm.shape[0]
    assert seq % (DSA_PATTERNS[-1][1] * DSA_BLOCK) == 0 and seq >= SB_WIN
    tm_a = _pick_tile(seq, 512)
    tm_b = _pick_tile(tokens, 512)
    s5_rows = seq // S5_CHUNK
    s5_tile = min(S5_ROW_TILE, s5_rows)
    cos_t, sin_t = _rope_tables(seq)

    h = x.reshape(tokens, D_MODEL)
    for l in range(depth):
        w1a, w2a = _ffn_weights(ffn1_w_in[l], ffn1_w_out[l])
        w1b, w2b = _ffn_weights(ffn2_w_in[l], ffn2_w_out[l])
        wp, wg = _proj_weights(w_in[l])
        row = lambda p: p[l].reshape(1, -1)
        qn = jnp.tile(row(dsa_q_norm), (1, 4))
        kn = jnp.tile(row(dsa_k_norm), (1, 4))
        h, sb, s5u, gla_in, gg, dsa = _ffn_proj(
            h, row(ffn1_norm), w1a, w2a, row(mix_norm), wp, qn, kn, cos_t, sin_t, seq, tm_a)

        y_sb = _stick_breaking(sb.reshape(bsz, seq, 768)).reshape(tokens, BRANCH_WIDTH)

        ops = _s5_operators(s5_lam_re[l], s5_lam_im[l], s5_log_step[l], s5_b_re[l], s5_b_im[l],
                            s5_c_re[l], s5_c_im[l], s5_d[l], s5_tile)
        y_s5 = _s5(s5u.reshape(bsz, s5_rows, S5_CHUNK * 256), *ops).reshape(tokens, BRANCH_WIDTH)

        wgate = jnp.pad(gla_w_gate[l], ((0, LANES - GLA_GATE_RANK), (0, 0))).astype(BF16)
        y_gla = _gla(gla_in.reshape(bsz, seq, 768), gg.reshape(bsz, seq, LANES), wgate,
                     row(gla_b_gate), row(gla_norm)).reshape(tokens, BRANCH_WIDTH)

        dsa_o, dsa_l = zip(*[_dsa_pattern(dsa, bsz, seq, window, dil) for window, dil in DSA_PATTERNS])

        h = _merge_ffn(h, y_sb, y_s5, y_gla, dsa_o, dsa_l, row(mix_norm), wg,
                       s5_w_glu[l].astype(BF16), w_branch[l].astype(BF16), w_out[l].astype(BF16),
                       row(ffn2_norm), w1b, w2b, tm_b)
    return h.reshape(bsz, seq, D_MODEL)
```

```python
import functools
import math

import jax
import jax.numpy as jnp
from jax import lax
from jax.experimental import pallas as pl
from jax.experimental.pallas import tpu as pltpu

D_MODEL = 1024
HEAD_DIM = 64
BRANCH_WIDTH = 256
N_BRANCH = 4
SB_HEADS = 4
S5_GROUPS = 16
S5_GROUP_CH = 16
S5_STATE = 64
GLA_HEADS = 4
GLA_DK = 32
GLA_DV = 64
GLA_GATE_RANK = 16
GLA_GATE_NORM = 16.0
DSA_PATTERNS = ((128, 1), (512, 4), (2048, 16))
DSA_BLOCK = 128
ROPE_THETA = 10000.0
D_FF = 2816
RMS_EPS = 1e-6

LANES = 128
VMEM_LIMIT_BYTES = 58 * 1024 * 1024
FF_CHUNK = 256
N_FF_CHUNKS = D_FF // FF_CHUNK
S5_CHUNK = 8
S5_ROW_TILE = 256
GLA_TILE = 256
GLA_BLOCK = 16
SB_Q = 128
SB_WIN = 3 * SB_Q
SB_GROUP = 2
SB_LOG_ZERO = -104.0
NEG_BIG = -1e30
DSA_UNROLL = 4
DSA_SPAN = DSA_PATTERNS[-1][1] * DSA_BLOCK

PROJ_SB = (0, 768)
PROJ_S5 = (768, 1024)
PROJ_GLA = (1024, 1792)
PROJ_DSA = (1792, 2560)
PROJ_GG = (2560, 2688)
PROJ_WIDTH = 2688

F32 = jnp.float32
BF16 = jnp.bfloat16


def _cparams(sem):
    return pltpu.CompilerParams(dimension_semantics=sem, vmem_limit_bytes=VMEM_LIMIT_BYTES)


def _const_spec(shape):
    zeros = (0,) * len(shape)
    return pl.BlockSpec(shape, lambda *_: zeros, pipeline_mode=pl.Buffered(1))


def _rms_rows(x, g):
    ms = jnp.mean(x * x, axis=-1, keepdims=True)
    return x * lax.rsqrt(ms + RMS_EPS) * g


def _hi_lo(x):
    hi = x.astype(BF16)
    lo = (x - hi.astype(F32)).astype(BF16)
    return hi, lo


def _dot(a, b):
    return jnp.dot(a, b, preferred_element_type=F32)


def _dot_nt(a, b):
    return lax.dot_general(a, b, (((1,), (1,)), ((), ())), preferred_element_type=F32)


def _dot_tn(a, b):
    return lax.dot_general(a, b, (((0,), (0,)), ((), ())), preferred_element_type=F32)


def _swiglu_into(acc_ref, xb, w1_ref, w2_ref):
    for c in range(N_FF_CHUNKS):
        lo, hi = c * FF_CHUNK, (c + 1) * FF_CHUNK
        a = _dot(xb, w1_ref[:, lo:hi])
        b = _dot(xb, w1_ref[:, D_FF + lo:D_FF + hi])
        hm = (a * jax.nn.sigmoid(a) * b).astype(BF16)
        part = _dot(hm, w2_ref[lo:hi, :])
        if c == 0:
            acc_ref[...] = part
        else:
            acc_ref[...] += part


def _group_mean_matrix(width, group):
    r = lax.broadcasted_iota(jnp.int32, (width, width), 0) // group
    c = lax.broadcasted_iota(jnp.int32, (width, width), 1) // group
    return jnp.where(r == c, 1.0 / group, 0.0).astype(BF16)


def _group_rms(x, gain):
    hi, lo = _hi_lo(x * x)
    gm = _group_mean_matrix(x.shape[-1], HEAD_DIM)
    ms = _dot(hi, gm) + _dot(lo, gm)
    return x * lax.rsqrt(ms + RMS_EPS) * gain


def _swap_half_heads(x):
    half = HEAD_DIM // 2
    outs = []
    for s in range(x.shape[-1] // LANES):
        xs = x[:, s * LANES:(s + 1) * LANES]
        lane = lax.broadcasted_iota(jnp.int32, xs.shape, 1)
        up = pltpu.roll(xs, LANES - half, axis=1)
        down = pltpu.roll(xs, half, axis=1)
        outs.append(jnp.where((lane % HEAD_DIM) < half, up, down))
    return jnp.concatenate(outs, axis=-1)


def _ffn_proj_kernel(x_ref, n1_ref, w1_ref, w2_ref, nm_ref, wp_ref, qn_ref, kn_ref, cos_ref, sin_ref,
                     h_ref, sb_ref, s5_ref, gla_ref, gg_ref, dsa1_ref, dsa4_ref, dsa16_ref,
                     acc_ref, slab_ref):
    tm = x_ref.shape[0]
    x = x_ref[...]
    xb = _rms_rows(x, n1_ref[...]).astype(BF16)
    _swiglu_into(acc_ref, xb, w1_ref, w2_ref)
    h = x + 0.5 * acc_ref[...]
    h_ref[...] = h
    ub = _rms_rows(h, nm_ref[...]).astype(BF16)

    sb_ref[...] = _dot(ub, wp_ref[:, PROJ_SB[0]:PROJ_SB[1]]).astype(BF16)
    gla_ref[...] = _dot(ub, wp_ref[:, PROJ_GLA[0]:PROJ_GLA[1]]).astype(BF16)
    gg_ref[...] = _dot(ub, wp_ref[:, PROJ_GG[0]:PROJ_GG[1]])

    s5 = _dot(ub, wp_ref[:, PROJ_S5[0]:PROJ_S5[1]])
    for s in range(2):
        slab_ref[s] = s5[:, s * LANES:(s + 1) * LANES]
    for j in range(S5_CHUNK):
        for s in range(2):
            col = j * 256 + s * LANES
            s5_ref[:, col:col + LANES] = slab_ref[s, pl.ds(j, tm // S5_CHUNK, stride=S5_CHUNK), :]

    dsa = _dot(ub, wp_ref[:, PROJ_DSA[0]:PROJ_DSA[1]])
    cos = cos_ref[...]
    sin = sin_ref[...]
    parts = []
    for i, gain_ref in enumerate((qn_ref, kn_ref)):
        t = _group_rms(dsa[:, i * 256:(i + 1) * 256], gain_ref[...])
        parts.append(t * cos + _swap_half_heads(t) * sin)
    parts.append(dsa[:, 512:768])
    dsa = jnp.concatenate(parts, axis=-1)
    dsa1_ref[...] = dsa.astype(BF16)
    for s in range(6):
        slab_ref[s] = dsa[:, s * LANES:(s + 1) * LANES]
    for dil, ref in ((4, dsa4_ref), (16, dsa16_ref)):
        for r in range(dil):
            for s in range(6):
                rows = slab_ref[s, pl.ds(r, tm // dil, stride=dil), :]
                ref[0, r, :, s * LANES:(s + 1) * LANES] = rows.astype(BF16)


def _ffn_proj(x2, n1, w1, w2, nm, wp, qn, kn, cos_t, sin_t, bsz, seq, tm):
    tokens = x2.shape[0]
    seq_tiles = seq // tm
    row = lambda i: (i, 0)
    tab = lambda i: (i % seq_tiles, 0)
    res = lambda i: (i // seq_tiles, 0, i % seq_tiles, 0)
    tile = lambda w: pl.BlockSpec((tm, w), row)
    s5_w = S5_CHUNK * 256
    return pl.pallas_call(
        _ffn_proj_kernel,
        grid=(tokens // tm,),
        in_specs=[tile(D_MODEL), _const_spec(n1.shape), _const_spec(w1.shape), _const_spec(w2.shape),
                  _const_spec(nm.shape), _const_spec(wp.shape), _const_spec(qn.shape), _const_spec(kn.shape),
                  pl.BlockSpec((tm, 256), tab), pl.BlockSpec((tm, 256), tab)],
        out_specs=[tile(D_MODEL), tile(768), pl.BlockSpec((tm // S5_CHUNK, s5_w), row), tile(768), tile(LANES),
                   tile(768), pl.BlockSpec((1, 4, tm // 4, 768), res), pl.BlockSpec((1, 16, tm // 16, 768), res)],
        out_shape=[jax.ShapeDtypeStruct((tokens, D_MODEL), F32),
                   jax.ShapeDtypeStruct((tokens, 768), BF16),
                   jax.ShapeDtypeStruct((tokens // S5_CHUNK, s5_w), F32),
                   jax.ShapeDtypeStruct((tokens, 768), BF16),
                   jax.ShapeDtypeStruct((tokens, LANES), F32),
                   jax.ShapeDtypeStruct((tokens, 768), BF16),
                   jax.ShapeDtypeStruct((bsz, 4, seq // 4, 768), BF16),
                   jax.ShapeDtypeStruct((bsz, 16, seq // 16, 768), BF16)],
        scratch_shapes=[pltpu.VMEM((tm, D_MODEL), F32), pltpu.VMEM((6, tm, LANES), F32)],
        compiler_params=_cparams(("parallel",)),
        name="ffn_proj",
    )(x2, n1, w1, w2, nm, wp, qn, kn, cos_t, sin_t)


def _merge_ffn_kernel(h_ref, ysb_ref, ys5_ref, ygla_ref, ydsa_ref,
                      nm_ref, wg_ref, wglu_ref, wb_ref, wo_ref, n2_ref, w1_ref, w2_ref,
                      out_ref, acc_ref, slab_ref):
    tm = h_ref.shape[0]
    h = h_ref[...]
    ub = _rms_rows(h, nm_ref[...]).astype(BF16)

    for j in range(S5_CHUNK):
        for s in range(2):
            col = j * 256 + s * LANES
            slab_ref[s, pl.ds(j, tm // S5_CHUNK, stride=S5_CHUNK), :] = ys5_ref[:, col:col + LANES].astype(F32)
    y_s5 = jnp.concatenate([slab_ref[0], slab_ref[1]], axis=-1).astype(BF16)
    glu = _dot(y_s5, wglu_ref[...])
    y_s5 = glu[:, :BRANCH_WIDTH] * jax.nn.sigmoid(glu[:, BRANCH_WIDTH:])

    branches = (ysb_ref[...], y_s5.astype(BF16), ygla_ref[...], ydsa_ref[...])
    mixed = jnp.zeros(h.shape, F32)
    for n in range(N_BRANCH):
        gate = jax.nn.sigmoid(_dot(ub, wg_ref[:, n * D_MODEL:(n + 1) * D_MODEL]))
        mixed = mixed + gate * _dot(branches[n], wb_ref[n])
    h2 = h + _dot(mixed.astype(BF16), wo_ref[...])

    xb = _rms_rows(h2, n2_ref[...]).astype(BF16)
    _swiglu_into(acc_ref, xb, w1_ref, w2_ref)
    out_ref[...] = h2 + 0.5 * acc_ref[...]


def _merge_ffn(h, ysb, ys5, ygla, ydsa, nm, wg, wglu, wb, wo, n2, w1, w2, tm):
    tokens = h.shape[0]
    row = lambda i: (i, 0)
    tile = lambda w: pl.BlockSpec((tm, w), row)
    consts = (nm, wg, wglu, wb, wo, n2, w1, w2)
    return pl.pallas_call(
        _merge_ffn_kernel,
        grid=(tokens // tm,),
        in_specs=[tile(D_MODEL), tile(BRANCH_WIDTH), pl.BlockSpec((tm // S5_CHUNK, S5_CHUNK * 256), row),
                  tile(BRANCH_WIDTH), tile(BRANCH_WIDTH)] + [_const_spec(c.shape) for c in consts],
        out_specs=tile(D_MODEL),
        out_shape=jax.ShapeDtypeStruct((tokens, D_MODEL), F32),
        scratch_shapes=[pltpu.VMEM((tm, D_MODEL), F32), pltpu.VMEM((2, tm, LANES), F32)],
        compiler_params=_cparams(("parallel",)),
        name="merge_ffn",
    )(h, ysb, ys5, ygla, ydsa, *consts)


def _later_key_matrix(n):
    j = lax.broadcasted_iota(jnp.int32, (n, n), 0)
    s = lax.broadcasted_iota(jnp.int32, (n, n), 1)
    return jnp.where(j > s, 1.0, 0.0).astype(BF16)


def _sb_scores(qm, k):
    z = _dot_nt(qm, k)
    lk = -(jnp.maximum(z, 0.0) + jnp.log(1.0 + jnp.exp(-jnp.abs(z))))
    return z, lk


def _sb_kernel(q_ref, k_ref, v_ref, o_ref):
    seq = q_ref.shape[1]
    u_win = _later_key_matrix(SB_WIN)
    u_blk = _later_key_matrix(SB_Q)
    lane_q = lax.broadcasted_iota(jnp.int32, (SB_Q, LANES), 1)
    n_streams = 2 * SB_GROUP

    def q_group(qg, carry):
        starts, qms, csums, outs = [], [], [], []
        for sub in range(SB_GROUP):
            t0 = pl.multiple_of((qg * SB_GROUP + sub) * SB_Q, SB_Q)
            start = pl.multiple_of(jnp.maximum(t0 - (SB_WIN - SB_Q), 0), SB_Q)
            q = q_ref[0, pl.ds(t0, SB_Q), :] * (HEAD_DIM ** -0.5)
            kw = k_ref[0, pl.ds(start, SB_WIN), :]
            vw = v_ref[0, pl.ds(start, SB_WIN), :]
            qidx = t0 + lax.broadcasted_iota(jnp.int32, (SB_Q, SB_WIN), 0)
            kidx = start + lax.broadcasted_iota(jnp.int32, (SB_Q, SB_WIN), 1)
            strict = kidx < qidx
            starts.append(start)
            for head in range(2):
                qm = jnp.where((lane_q // HEAD_DIM) == head, q, jnp.zeros_like(q))
                z, lk = _sb_scores(qm, kw)
                lk = jnp.where(strict, lk, 0.0)
                later = _dot(lk.astype(BF16), u_win)
                w = jnp.where(strict, jnp.exp(z + lk + later), 0.0)
                qms.append(qm)
                outs.append(_dot(w.astype(BF16), vw))
                csums.append(later[:, 0:1] + lk[:, 0:1])

        def cond(state):
            step, cmaxes, _, _ = state
            flags = [jnp.logical_and(starts[s // 2] - (step + 1) * SB_Q >= 0, cmaxes[s] > SB_LOG_ZERO)
                     for s in range(n_streams)]
            return functools.reduce(jnp.logical_or, flags)

        def body(state):
            step, _, csums, outs = state
            new_c, new_o = [], []
            for s in range(n_streams):
                nxt = starts[s // 2] - (step + 1) * SB_Q
                in_range = nxt >= 0
                p0 = pl.multiple_of(jnp.maximum(nxt, 0), SB_Q)
                kb = k_ref[0, pl.ds(p0, SB_Q), :]
                vb = v_ref[0, pl.ds(p0, SB_Q), :]
                z, lk = _sb_scores(qms[s], kb)
                later = _dot(lk.astype(BF16), u_blk)
                w = jnp.where(in_range, jnp.exp(z + lk + later + csums[s]), 0.0)
                new_o.append(outs[s] + _dot(w.astype(BF16), vb))
                new_c.append(csums[s] + jnp.where(in_range, later[:, 0:1] + lk[:, 0:1], 0.0))
            return step + 1, tuple(jnp.max(c) for c in new_c), tuple(new_c), tuple(new_o)

        init = (jnp.int32(0), tuple(jnp.max(c) for c in csums), tuple(csums), tuple(outs))
        _, _, _, outs = lax.while_loop(cond, body, init)
        for sub in range(SB_GROUP):
            t0 = pl.multiple_of((qg * SB_GROUP + sub) * SB_Q, SB_Q)
            o_ref[0, pl.ds(t0, SB_Q), :] = jnp.where(
                lane_q < HEAD_DIM, outs[2 * sub], outs[2 * sub + 1]).astype(o_ref.dtype)
        return carry

    lax.fori_loop(0, seq // (SB_Q * SB_GROUP), q_group, 0)


def _stick_breaking(sb3):
    bsz, seq, _ = sb3.shape
    spec = lambda off: pl.BlockSpec((1, seq, LANES), lambda b, hp: (b, 0, off + hp))
    return pl.pallas_call(
        _sb_kernel,
        grid=(bsz, 2),
        in_specs=[spec(0), spec(2), spec(4)],
        out_specs=pl.BlockSpec((1, seq, LANES), lambda b, hp: (b, 0, hp)),
        out_shape=jax.ShapeDtypeStruct((bsz, seq, BRANCH_WIDTH), BF16),
        compiler_params=_cparams(("parallel", "parallel")),
        name="stick_breaking",
    )(sb3, sb3, sb3)


def _s5_kernel(u_ref, m_ref, e_ref, g_ref, pw_ref, d_ref, y_ref, state_ref):
    half = S5_GROUPS * S5_STATE

    @pl.when(pl.program_id(1) == 0)
    def _():
        state_ref[...] = jnp.zeros_like(state_ref)

    u = u_ref[0]
    ub = u.astype(BF16)
    rows = u.shape[0]

    def cmul(ar, ai, xr, xi):
        return ar * xr - ai * xi, ar * xi + ai * xr

    w = _dot(ub, e_ref[...])
    wr, wi = w[:, :half], w[:, half:]
    row = lax.broadcasted_iota(jnp.int32, (rows, half), 0)
    a1r, a1i = pw_ref[0:1, :half], pw_ref[0:1, half:]
    sr, si = state_ref[0:1, :half], state_ref[0:1, half:]
    cr, ci = cmul(a1r, a1i, sr, si)
    wr = jnp.where(row == 0, wr + cr, wr)
    wi = jnp.where(row == 0, wi + ci, wi)
    step = 0
    d = 1
    while d < rows:
        ar, ai = pw_ref[step:step + 1, :half], pw_ref[step:step + 1, half:]
        pr = jnp.where(row >= d, pltpu.roll(wr, d, axis=0), 0.0)
        pi = jnp.where(row >= d, pltpu.roll(wi, d, axis=0), 0.0)
        mr, mi = cmul(ar, ai, pr, pi)
        wr, wi = wr + mr, wi + mi
        d *= 2
        step += 1
    xr = jnp.where(row == 0, sr, pltpu.roll(wr, 1, axis=0))
    xi = jnp.where(row == 0, si, pltpu.roll(wi, 1, axis=0))
    state_ref[0:1, :half] = wr[rows - 1:rows, :]
    state_ref[0:1, half:] = wi[rows - 1:rows, :]

    y = _dot(ub, m_ref[...])
    y = y + _dot(xr.astype(BF16), g_ref[:half, :]) + _dot(xi.astype(BF16), g_ref[half:, :])
    y = y + d_ref[...] * u
    y_ref[0] = jax.nn.gelu(y).astype(y_ref.dtype)


def _s5(u3, m, e, g, pw, dskip):
    bsz, rows, width = u3.shape
    tr = min(S5_ROW_TILE, rows)
    blk = pl.BlockSpec((1, tr, width), lambda b, i: (b, i, 0))
    return pl.pallas_call(
        _s5_kernel,
        grid=(bsz, rows // tr),
        in_specs=[blk, _const_spec(m.shape), _const_spec(e.shape), _const_spec(g.shape),
                  _const_spec(pw.shape), _const_spec(dskip.shape)],
        out_specs=blk,
        out_shape=jax.ShapeDtypeStruct(u3.shape, BF16),
        scratch_shapes=[pltpu.VMEM((8, width), F32)],
        compiler_params=_cparams(("parallel", "arbitrary")),
        name="s5",
    )(u3, m, e, g, pw, dskip)


def _s5_operators(lam_re, lam_im, log_step, b_re, b_im, c_re, c_im, d_skip, row_tile):
    n_g, n_p, n_h, L = S5_GROUPS, S5_STATE, S5_GROUP_CH, S5_CHUNK
    lam = lax.complex(lam_re, lam_im)
    dt = jnp.exp(log_step)[:, None]
    lam_dt = lam * dt
    lam_bar = jnp.exp(lam_dt)
    b_bar = ((lam_bar - 1.0) / lam)[..., None] * lax.complex(b_re, b_im)
    c_mat = lax.complex(c_re, c_im)
    pows = jnp.exp(lam_dt[None] * jnp.arange(L + 1, dtype=F32)[:, None, None])
    eye = jnp.eye(n_g, dtype=F32)

    kern = jnp.einsum('ghp,kgp,gpi->kghi', c_mat, pows[:L], b_bar).real
    j = jnp.arange(L)[:, None]
    i = jnp.arange(L)[None, :]
    ksel = kern[jnp.clip(i - j, 0, L - 1)] * (i >= j)[:, :, None, None, None]
    m_op = jnp.einsum('jigoh,gf->jghifo', ksel, eye).reshape(L * 256, L * 256)

    e_c = pows[L - 1 - jnp.arange(L)][:, :, :, None] * b_bar[None]
    e_c = jnp.einsum('jgph,gf->jghfp', e_c, eye.astype(e_c.dtype))
    e_op = jnp.concatenate([e_c.real.reshape(L * 256, n_g * n_p), e_c.imag.reshape(L * 256, n_g * n_p)], axis=1)

    g_c = c_mat[None] * pows[1:L + 1][:, :, None, :]
    g_c = jnp.einsum('igop,gf->fpigo', g_c, eye.astype(g_c.dtype))
    g_op = jnp.concatenate([g_c.real.reshape(n_g * n_p, L * 256), -g_c.imag.reshape(n_g * n_p, L * 256)], axis=0)

    n_steps = max(1, int(math.log2(row_tile)))
    step_pow = jnp.exp(lam_dt[None] * (L * 2.0 ** jnp.arange(n_steps, dtype=F32))[:, None, None])
    step_pow = step_pow.reshape(n_steps, n_g * n_p)
    pw = jnp.concatenate([step_pow.real, step_pow.imag], axis=1)
    pw = jnp.pad(pw, ((0, (-n_steps) % 8), (0, 0)))
    dskip = jnp.tile(d_skip.reshape(1, 256), (1, L))
    return m_op.astype(BF16), e_op.astype(BF16), g_op.astype(BF16), pw.astype(F32), dskip.astype(F32)


def _gla_kernel(x_ref, gg_ref, wgate_ref, bgate_ref, norm_ref, y_ref, state_ref):
    tile = x_ref.shape[1]
    hk = GLA_HEADS * GLA_DK
    hv = GLA_HEADS * GLA_DV

    @pl.when(pl.program_id(1) == 0)
    def _():
        state_ref[...] = jnp.zeros_like(state_ref)

    x = x_ref[0]
    q = x[:, 0:hk].astype(F32) * (GLA_DK ** -0.5)
    k = x[:, hk:2 * hk].astype(F32)
    v = x[:, 2 * hk:2 * hk + hv].astype(F32)
    r = x[:, 2 * hk + hv:].astype(F32)

    z = _dot(gg_ref[0].astype(BF16), wgate_ref[...]) + bgate_ref[...]
    g = -(jnp.maximum(-z, 0.0) + jnp.log(1.0 + jnp.exp(-jnp.abs(z)))) * (1.0 / GLA_GATE_NORM)

    ri = lax.broadcasted_iota(jnp.int32, (tile, tile), 0)
    ci = lax.broadcasted_iota(jnp.int32, (tile, tile), 1)
    same = (ri // GLA_BLOCK) == (ci // GLA_BLOCK)
    t_incl = jnp.where(same & (ci <= ri), 1.0, 0.0).astype(BF16)
    t_full = jnp.where(same, 1.0, 0.0).astype(BF16)
    g_hi, g_lo = _hi_lo(g)
    b = _dot(t_incl, g_hi) + _dot(t_incl, g_lo)
    b_end = _dot(t_full, g_hi) + _dot(t_full, g_lo)

    rk = lax.broadcasted_iota(jnp.int32, (hk, hv), 0) // GLA_DK
    cv = lax.broadcasted_iota(jnp.int32, (hk, hv), 1) // GLA_DV
    head_sum = jnp.where(rk == cv, 1.0, 0.0).astype(BF16)
    row_in_blk = lax.broadcasted_iota(jnp.int32, (tile, hk), 0) % GLA_BLOCK
    acc = jnp.zeros((tile, hv), F32)
    for off in range(GLA_BLOCK):
        k_o = pltpu.roll(k, off, axis=0) if off else k
        b_o = pltpu.roll(b, off, axis=0) if off else b
        v_o = pltpu.roll(v, off, axis=0) if off else v
        decay = jnp.exp(jnp.where(row_in_blk >= off, b - b_o, NEG_BIG))
        att = _dot((q * k_o * decay).astype(BF16), head_sum)
        acc = acc + att * v_o

    q_dec = (q * jnp.exp(b)).astype(BF16)
    k_dec = (k * jnp.exp(b_end - b)).astype(BF16)
    vb = v.astype(BF16)
    rs = lax.broadcasted_iota(jnp.int32, (hv, hk), 0) // GLA_DV
    cs = lax.broadcasted_iota(jnp.int32, (hv, hk), 1) // GLA_DK
    head_diag = rs == cs
    state = state_ref[...]
    cross = []
    for blk in range(tile // GLA_BLOCK):
        r0 = blk * GLA_BLOCK
        cross.append(_dot_nt(q_dec[r0:r0 + GLA_BLOCK], state.astype(BF16)))
        upd = _dot_tn(vb[r0:r0 + GLA_BLOCK], k_dec[r0:r0 + GLA_BLOCK])
        state = jnp.exp(b_end[r0:r0 + 1, :]) * state + jnp.where(head_diag, upd, 0.0)
    state_ref[...] = state
    o = acc + jnp.concatenate(cross, axis=0)

    o = _group_rms(o, norm_ref[...])
    y_ref[0] = (o * (r * jax.nn.sigmoid(r))).astype(y_ref.dtype)


def _gla(x3, gg3, wgate, bgate, norm):
    bsz, seq, _ = x3.shape
    tile = min(GLA_TILE, seq)
    return pl.pallas_call(
        _gla_kernel,
        grid=(bsz, seq // tile),
        in_specs=[pl.BlockSpec((1, tile, 768), lambda b, i: (b, i, 0)),
                  pl.BlockSpec((1, tile, LANES), lambda b, i: (b, i, 0)),
                  _const_spec(wgate.shape), _const_spec(bgate.shape), _const_spec(norm.shape)],
        out_specs=pl.BlockSpec((1, tile, BRANCH_WIDTH), lambda b, i: (b, i, 0)),
        out_shape=jax.ShapeDtypeStruct((bsz, seq, BRANCH_WIDTH), BF16),
        scratch_shapes=[pltpu.VMEM((GLA_HEADS * GLA_DV, GLA_HEADS * GLA_DK), F32)],
        compiler_params=_cparams(("parallel", "arbitrary")),
        name="gla",
    )(x3, gg3, wgate, bgate, norm)


def _dsa_attend(q, kw, vw, m0, start, n_steps):
    lane_q = lax.broadcasted_iota(jnp.int32, (DSA_BLOCK, LANES), 1)
    qidx = m0 + lax.broadcasted_iota(jnp.int32, (DSA_BLOCK, 2 * DSA_BLOCK), 0)
    kidx = start + lax.broadcasted_iota(jnp.int32, (DSA_BLOCK, 2 * DSA_BLOCK), 1)
    dist = qidx - kidx
    valid = (dist >= 0) & (dist <= n_steps)
    q = q * (HEAD_DIM ** -0.5)
    outs, lses = [], []
    for head in range(2):
        qm = jnp.where((lane_q // HEAD_DIM) == head, q, jnp.zeros_like(q))
        s = jnp.where(valid, _dot_nt(qm, kw), NEG_BIG)
        m = jnp.max(s, axis=1, keepdims=True)
        p = jnp.exp(s - m)
        l = jnp.sum(p, axis=1, keepdims=True)
        outs.append(_dot(p.astype(BF16), vw) / l)
        lses.append(jnp.broadcast_to(m + jnp.log(l), (DSA_BLOCK, LANES)))
    first = lane_q < HEAD_DIM
    return jnp.where(first, outs[0], outs[1]), jnp.where(first, lses[0], lses[1])


def _dsa_kernel(q1_ref, k1_ref, v1_ref, q4_ref, k4_ref, v4_ref, q16_ref, k16_ref, v16_ref,
                y_ref, o_scr, l_scr):
    seq = q1_ref.shape[1]
    refs = ((q1_ref, k1_ref, v1_ref), (q4_ref, k4_ref, v4_ref), (q16_ref, k16_ref, v16_ref))
    blocks_per_span = DSA_SPAN // DSA_BLOCK

    def span(sp, carry):
        for p, (window, dil) in enumerate(DSA_PATTERNS):
            q_ref, k_ref, v_ref = refs[p]
            per_res = blocks_per_span // dil

            def unit(idx, c, p=p, dil=dil, per_res=per_res, q_ref=q_ref, k_ref=k_ref, v_ref=v_ref,
                     n_steps=window // dil):
                r = idx // per_res
                jb = idx % per_res
                m0 = pl.multiple_of(sp * (DSA_SPAN // dil) + jb * DSA_BLOCK, DSA_BLOCK)
                start = pl.multiple_of(jnp.maximum(m0 - DSA_BLOCK, 0), DSA_BLOCK)
                if dil == 1:
                    q = q_ref[0, pl.ds(m0, DSA_BLOCK), :]
                    kw = k_ref[0, pl.ds(start, 2 * DSA_BLOCK), :]
                    vw = v_ref[0, pl.ds(start, 2 * DSA_BLOCK), :]
                else:
                    q = q_ref[0, r, pl.ds(m0, DSA_BLOCK), :]
                    kw = k_ref[0, r, pl.ds(start, 2 * DSA_BLOCK), :]
                    vw = v_ref[0, r, pl.ds(start, 2 * DSA_BLOCK), :]
                o, lse = _dsa_attend(q, kw, vw, m0, start, n_steps)
                rows = pl.ds(r + dil * jb * DSA_BLOCK, DSA_BLOCK, stride=dil) if dil > 1 else \
                    pl.ds(pl.multiple_of(jb * DSA_BLOCK, DSA_BLOCK), DSA_BLOCK)
                o_scr[p, rows, :] = o
                l_scr[p, rows, :] = lse
                return c

            lax.fori_loop(0, blocks_per_span, unit, 0, unroll=DSA_UNROLL)

        l1, l2, l3 = l_scr[0], l_scr[1], l_scr[2]
        lm = jnp.maximum(jnp.maximum(l1, l2), l3)
        e1, e2, e3 = jnp.exp(l1 - lm), jnp.exp(l2 - lm), jnp.exp(l3 - lm)
        y = (e1 * o_scr[0] + e2 * o_scr[1] + e3 * o_scr[2]) / (e1 + e2 + e3)
        y_ref[0, pl.ds(pl.multiple_of(sp * DSA_SPAN, DSA_SPAN), DSA_SPAN), :] = y.astype(y_ref.dtype)
        return carry

    lax.fori_loop(0, seq // DSA_SPAN, span, 0)


def _dsa(dsa1, dsa4, dsa16):
    bsz, seq, _ = dsa1.shape
    nat = lambda off: pl.BlockSpec((1, seq, LANES), lambda b, hp: (b, 0, off + hp))
    res = lambda dil, off: pl.BlockSpec((1, dil, seq // dil, LANES), lambda b, hp: (b, 0, 0, off + hp))
    return pl.pallas_call(
        _dsa_kernel,
        grid=(bsz, 2),
        in_specs=[nat(0), nat(2), nat(4), res(4, 0), res(4, 2), res(4, 4), res(16, 0), res(16, 2), res(16, 4)],
        out_specs=pl.BlockSpec((1, seq, LANES), lambda b, hp: (b, 0, hp)),
        out_shape=jax.ShapeDtypeStruct((bsz, seq, BRANCH_WIDTH), BF16),
        scratch_shapes=[pltpu.VMEM((3, DSA_SPAN, LANES), F32), pltpu.VMEM((3, DSA_SPAN, LANES), F32)],
        compiler_params=_cparams(("parallel", "parallel")),
        name="dsa",
    )(dsa1, dsa1, dsa1, dsa4, dsa4, dsa4, dsa16, dsa16, dsa16)


def _proj_weights(w_in):
    sb, s5, gq, gk, gv, gr, gg, dsa, gates = jnp.split(
        w_in, [768, 1024, 1152, 1280, 1536, 1792, 1808, 2576], axis=1)
    gg = jnp.pad(gg, ((0, 0), (0, LANES - GLA_GATE_RANK)))
    wp = jnp.concatenate([sb, s5, gq, gk, gv, gr, dsa, gg], axis=1).astype(BF16)
    return wp, gates.astype(BF16)


def _rope_tables(seq):
    inv = ROPE_THETA ** (-jnp.arange(0, HEAD_DIM, 2, dtype=F32) / HEAD_DIM)
    ang = jnp.arange(seq, dtype=F32)[:, None] * inv[None, :]
    cos, sin = jnp.cos(ang), jnp.sin(ang)
    cos_t = jnp.tile(jnp.concatenate([cos, cos], axis=1), (1, 4))
    sin_t = jnp.tile(jnp.concatenate([-sin, sin], axis=1), (1, 4))
    return cos_t, sin_t


def _pick_tile(n, pref):
    t = min(pref, n)
    while n % t:
        t //= 2
    return t


def kernel(x, ffn1_norm, ffn1_w_in, ffn1_w_out, mix_norm, w_in, s5_lam_re, s5_lam_im, s5_log_step, s5_b_re, s5_b_im, s5_c_re, s5_c_im, s5_d, s5_w_glu, gla_w_gate, gla_b_gate, gla_norm, dsa_q_norm, dsa_k_norm, w_branch, w_out, ffn2_norm, ffn2_w_in, ffn2_w_out):
    bsz, seq, _ = x.shape
    tokens = bsz * seq
    depth = ffn1_norm.shape[0]
    assert seq % DSA_SPAN == 0 and seq // DSA_PATTERNS[-1][1] >= 2 * DSA_BLOCK and seq >= SB_WIN
    tm_a = _pick_tile(seq, 512)
    tm_b = _pick_tile(tokens, 512)
    s5_rows = seq // S5_CHUNK
    s5_tile = min(S5_ROW_TILE, s5_rows)
    cos_t, sin_t = _rope_tables(seq)

    h = x.reshape(tokens, D_MODEL)
    for l in range(depth):
        wp, wg = _proj_weights(w_in[l])
        row = lambda p: p[l].reshape(1, -1)
        qn = jnp.tile(row(dsa_q_norm), (1, 4))
        kn = jnp.tile(row(dsa_k_norm), (1, 4))
        h, sb, s5u, gla_in, gg, dsa1, dsa4, dsa16 = _ffn_proj(
            h, row(ffn1_norm), ffn1_w_in[l].astype(BF16), ffn1_w_out[l].astype(BF16), row(mix_norm), wp,
            qn, kn, cos_t, sin_t, bsz, seq, tm_a)

        y_sb = _stick_breaking(sb.reshape(bsz, seq, 768)).reshape(tokens, BRANCH_WIDTH)

        ops = _s5_operators(s5_lam_re[l], s5_lam_im[l], s5_log_step[l], s5_b_re[l], s5_b_im[l],
                            s5_c_re[l], s5_c_im[l], s5_d[l], s5_tile)
        y_s5 = _s5(s5u.reshape(bsz, s5_rows, S5_CHUNK * 256), *ops).reshape(tokens // S5_CHUNK, S5_CHUNK * 256)

        wgate = jnp.pad(gla_w_gate[l], ((0, LANES - GLA_GATE_RANK), (0, 0))).astype(BF16)
        y_gla = _gla(gla_in.reshape(bsz, seq, 768), gg.reshape(bsz, seq, LANES), wgate,
                     row(gla_b_gate), row(gla_norm)).reshape(tokens, BRANCH_WIDTH)

        y_dsa = _dsa(dsa1.reshape(bsz, seq, 768), dsa4, dsa16).reshape(tokens, BRANCH_WIDTH)

        h = _merge_ffn(h, y_sb, y_s5, y_gla, y_dsa, row(mix_norm), wg,
                       s5_w_glu[l].astype(BF16), w_branch[l].astype(BF16), w_out[l].astype(BF16),
                       row(ffn2_norm), ffn2_w_in[l].astype(BF16), ffn2_w_out[l].astype(BF16), tm_b)
    return h.reshape(bsz, seq, D_MODEL)
```

```python
import functools
import math

import jax
import jax.numpy as jnp
from jax import lax
from jax.experimental import pallas as pl
from jax.experimental.pallas import tpu as pltpu

D_MODEL = 1024
HEAD_DIM = 64
BRANCH_WIDTH = 256
N_BRANCH = 4
SB_HEADS = 4
S5_GROUPS = 16
S5_GROUP_CH = 16
S5_STATE = 64
GLA_HEADS = 4
GLA_DK = 32
GLA_DV = 64
GLA_GATE_RANK = 16
GLA_GATE_NORM = 16.0
DSA_PATTERNS = ((128, 1), (512, 4), (2048, 16))
DSA_BLOCK = 128
ROPE_THETA = 10000.0
D_FF = 2816
RMS_EPS = 1e-6

LANES = 128
VMEM_LIMIT_BYTES = 58 * 1024 * 1024
FF_CHUNK = 256
N_FF_CHUNKS = D_FF // FF_CHUNK
S5_CHUNK = 8
S5_ROW_TILE = 256
GLA_TILE = 256
GLA_BLOCK = 64
GLA_LEVELS = 6
GLA_SUB = 128
SB_Q = 128
SB_WIN = 3 * SB_Q
SB_GROUP = 2
SB_LOG_ZERO = -104.0
NEG_BIG = -1e30
DSA_UNROLL = 4
DSA_SPAN = DSA_PATTERNS[-1][1] * DSA_BLOCK

PROJ_SB = (0, 768)
PROJ_S5 = (768, 1024)
PROJ_GLA = (1024, 1792)
PROJ_DSA = (1792, 2560)
PROJ_GG = (2560, 2688)
PROJ_WIDTH = 2688

F32 = jnp.float32
BF16 = jnp.bfloat16


def _cparams(sem):
    return pltpu.CompilerParams(dimension_semantics=sem, vmem_limit_bytes=VMEM_LIMIT_BYTES)


def _const_spec(shape):
    zeros = (0,) * len(shape)
    return pl.BlockSpec(shape, lambda *_: zeros, pipeline_mode=pl.Buffered(1))


def _rms_rows(x, g):
    ms = jnp.mean(x * x, axis=-1, keepdims=True)
    return x * lax.rsqrt(ms + RMS_EPS) * g


def _hi_lo(x):
    hi = x.astype(BF16)
    lo = (x - hi.astype(F32)).astype(BF16)
    return hi, lo


def _dot(a, b):
    return jnp.dot(a, b, preferred_element_type=F32)


def _dot_nt(a, b):
    return lax.dot_general(a, b, (((1,), (1,)), ((), ())), preferred_element_type=F32)


def _dot_tn(a, b):
    return lax.dot_general(a, b, (((0,), (0,)), ((), ())), preferred_element_type=F32)


def _swiglu_into(acc_ref, xb, w1_ref, w2_ref):
    for c in range(N_FF_CHUNKS):
        lo, hi = c * FF_CHUNK, (c + 1) * FF_CHUNK
        a = _dot(xb, w1_ref[:, lo:hi])
        b = _dot(xb, w1_ref[:, D_FF + lo:D_FF + hi])
        hm = (a * jax.nn.sigmoid(a) * b).astype(BF16)
        part = _dot(hm, w2_ref[lo:hi, :])
        if c == 0:
            acc_ref[...] = part
        else:
            acc_ref[...] += part


def _group_mean_matrix(width, group):
    r = lax.broadcasted_iota(jnp.int32, (width, width), 0) // group
    c = lax.broadcasted_iota(jnp.int32, (width, width), 1) // group
    return jnp.where(r == c, 1.0 / group, 0.0).astype(BF16)


def _group_rms(x, gain):
    hi, lo = _hi_lo(x * x)
    gm = _group_mean_matrix(x.shape[-1], HEAD_DIM)
    ms = _dot(hi, gm) + _dot(lo, gm)
    return x * lax.rsqrt(ms + RMS_EPS) * gain


def _swap_half_heads(x):
    half = HEAD_DIM // 2
    outs = []
    for s in range(x.shape[-1] // LANES):
        xs = x[:, s * LANES:(s + 1) * LANES]
        lane = lax.broadcasted_iota(jnp.int32, xs.shape, 1)
        up = pltpu.roll(xs, LANES - half, axis=1)
        down = pltpu.roll(xs, half, axis=1)
        outs.append(jnp.where((lane % HEAD_DIM) < half, up, down))
    return jnp.concatenate(outs, axis=-1)


def _ffn_proj_kernel(x_ref, n1_ref, w1_ref, w2_ref, nm_ref, wp_ref, qn_ref, kn_ref, cos_ref, sin_ref,
                     h_ref, sb_ref, s5_ref, gla_ref, gg_ref, dsa1_ref, dsa4_ref, dsa16_ref,
                     acc_ref, slab_ref):
    tm = x_ref.shape[0]
    x = x_ref[...]
    xb = _rms_rows(x, n1_ref[...]).astype(BF16)
    _swiglu_into(acc_ref, xb, w1_ref, w2_ref)
    h = x + 0.5 * acc_ref[...]
    h_ref[...] = h
    ub = _rms_rows(h, nm_ref[...]).astype(BF16)

    sb_ref[...] = _dot(ub, wp_ref[:, PROJ_SB[0]:PROJ_SB[1]]).astype(BF16)
    gla_ref[...] = _dot(ub, wp_ref[:, PROJ_GLA[0]:PROJ_GLA[1]]).astype(BF16)
    gg_ref[...] = _dot(ub, wp_ref[:, PROJ_GG[0]:PROJ_GG[1]])

    s5 = _dot(ub, wp_ref[:, PROJ_S5[0]:PROJ_S5[1]])
    for s in range(2):
        slab_ref[s] = s5[:, s * LANES:(s + 1) * LANES]
    for j in range(S5_CHUNK):
        for s in range(2):
            col = j * 256 + s * LANES
            s5_ref[:, col:col + LANES] = slab_ref[s, pl.ds(j, tm // S5_CHUNK, stride=S5_CHUNK), :]

    dsa = _dot(ub, wp_ref[:, PROJ_DSA[0]:PROJ_DSA[1]])
    cos = cos_ref[...]
    sin = sin_ref[...]
    parts = []
    for i, gain_ref in enumerate((qn_ref, kn_ref)):
        t = _group_rms(dsa[:, i * 256:(i + 1) * 256], gain_ref[...])
        parts.append(t * cos + _swap_half_heads(t) * sin)
    parts.append(dsa[:, 512:768])
    dsa = jnp.concatenate(parts, axis=-1)
    dsa1_ref[...] = dsa.astype(BF16)
    for s in range(6):
        slab_ref[s] = dsa[:, s * LANES:(s + 1) * LANES]
    for dil, ref in ((4, dsa4_ref), (16, dsa16_ref)):
        for r in range(dil):
            for s in range(6):
                rows = slab_ref[s, pl.ds(r, tm // dil, stride=dil), :]
                ref[0, r, :, s * LANES:(s + 1) * LANES] = rows.astype(BF16)


def _ffn_proj(x2, n1, w1, w2, nm, wp, qn, kn, cos_t, sin_t, bsz, seq, tm):
    tokens = x2.shape[0]
    seq_tiles = seq // tm
    row = lambda i: (i, 0)
    tab = lambda i: (i % seq_tiles, 0)
    res = lambda i: (i // seq_tiles, 0, i % seq_tiles, 0)
    tile = lambda w: pl.BlockSpec((tm, w), row)
    s5_w = S5_CHUNK * 256
    return pl.pallas_call(
        _ffn_proj_kernel,
        grid=(tokens // tm,),
        in_specs=[tile(D_MODEL), _const_spec(n1.shape), _const_spec(w1.shape), _const_spec(w2.shape),
                  _const_spec(nm.shape), _const_spec(wp.shape), _const_spec(qn.shape), _const_spec(kn.shape),
                  pl.BlockSpec((tm, 256), tab), pl.BlockSpec((tm, 256), tab)],
        out_specs=[tile(D_MODEL), tile(768), pl.BlockSpec((tm // S5_CHUNK, s5_w), row), tile(768), tile(LANES),
                   tile(768), pl.BlockSpec((1, 4, tm // 4, 768), res), pl.BlockSpec((1, 16, tm // 16, 768), res)],
        out_shape=[jax.ShapeDtypeStruct((tokens, D_MODEL), F32),
                   jax.ShapeDtypeStruct((tokens, 768), BF16),
                   jax.ShapeDtypeStruct((tokens // S5_CHUNK, s5_w), F32),
                   jax.ShapeDtypeStruct((tokens, 768), BF16),
                   jax.ShapeDtypeStruct((tokens, LANES), F32),
                   jax.ShapeDtypeStruct((tokens, 768), BF16),
                   jax.ShapeDtypeStruct((bsz, 4, seq // 4, 768), BF16),
                   jax.ShapeDtypeStruct((bsz, 16, seq // 16, 768), BF16)],
        scratch_shapes=[pltpu.VMEM((tm, D_MODEL), F32), pltpu.VMEM((6, tm, LANES), F32)],
        compiler_params=_cparams(("parallel",)),
        name="ffn_proj",
    )(x2, n1, w1, w2, nm, wp, qn, kn, cos_t, sin_t)


def _merge_ffn_kernel(h_ref, ysb_ref, ys5_ref, ygla_ref, ydsa_ref,
                      nm_ref, wg_ref, wglu_ref, wb_ref, wo_ref, n2_ref, w1_ref, w2_ref,
                      out_ref, acc_ref, slab_ref):
    tm = h_ref.shape[0]
    h = h_ref[...]
    ub = _rms_rows(h, nm_ref[...]).astype(BF16)

    for j in range(S5_CHUNK):
        for s in range(2):
            col = j * 256 + s * LANES
            slab_ref[s, pl.ds(j, tm // S5_CHUNK, stride=S5_CHUNK), :] = ys5_ref[:, col:col + LANES].astype(F32)
    y_s5 = jnp.concatenate([slab_ref[0], slab_ref[1]], axis=-1).astype(BF16)
    glu = _dot(y_s5, wglu_ref[...])
    y_s5 = glu[:, :BRANCH_WIDTH] * jax.nn.sigmoid(glu[:, BRANCH_WIDTH:])

    branches = (ysb_ref[...], y_s5.astype(BF16), ygla_ref[...], ydsa_ref[...])
    mixed = jnp.zeros(h.shape, F32)
    for n in range(N_BRANCH):
        gate = jax.nn.sigmoid(_dot(ub, wg_ref[:, n * D_MODEL:(n + 1) * D_MODEL]))
        mixed = mixed + gate * _dot(branches[n], wb_ref[n])
    h2 = h + _dot(mixed.astype(BF16), wo_ref[...])

    xb = _rms_rows(h2, n2_ref[...]).astype(BF16)
    _swiglu_into(acc_ref, xb, w1_ref, w2_ref)
    out_ref[...] = h2 + 0.5 * acc_ref[...]


def _merge_ffn(h, ysb, ys5, ygla, ydsa, nm, wg, wglu, wb, wo, n2, w1, w2, tm):
    tokens = h.shape[0]
    row = lambda i: (i, 0)
    tile = lambda w: pl.BlockSpec((tm, w), row)
    consts = (nm, wg, wglu, wb, wo, n2, w1, w2)
    return pl.pallas_call(
        _merge_ffn_kernel,
        grid=(tokens // tm,),
        in_specs=[tile(D_MODEL), tile(BRANCH_WIDTH), pl.BlockSpec((tm // S5_CHUNK, S5_CHUNK * 256), row),
                  tile(BRANCH_WIDTH), tile(BRANCH_WIDTH)] + [_const_spec(c.shape) for c in consts],
        out_specs=tile(D_MODEL),
        out_shape=jax.ShapeDtypeStruct((tokens, D_MODEL), F32),
        scratch_shapes=[pltpu.VMEM((tm, D_MODEL), F32), pltpu.VMEM((2, tm, LANES), F32)],
        compiler_params=_cparams(("parallel",)),
        name="merge_ffn",
    )(h, ysb, ys5, ygla, ydsa, *consts)


def _later_key_matrix(n):
    j = lax.broadcasted_iota(jnp.int32, (n, n), 0)
    s = lax.broadcasted_iota(jnp.int32, (n, n), 1)
    return jnp.where(j > s, 1.0, 0.0).astype(BF16)


def _sb_scores(qm, k):
    z = _dot_nt(qm, k)
    lk = -(jnp.maximum(z, 0.0) + jnp.log(1.0 + jnp.exp(-jnp.abs(z))))
    return z, lk


def _sb_kernel(q_ref, k_ref, v_ref, o_ref):
    seq = q_ref.shape[1]
    u_win = _later_key_matrix(SB_WIN)
    u_blk = _later_key_matrix(SB_Q)
    lane_q = lax.broadcasted_iota(jnp.int32, (SB_Q, LANES), 1)
    n_streams = 2 * SB_GROUP

    def q_group(qg, carry):
        starts, qms, zs, lks, stricts, vws = [], [], [], [], [], []
        for sub in range(SB_GROUP):
            t0 = pl.multiple_of((qg * SB_GROUP + sub) * SB_Q, SB_Q)
            start = pl.multiple_of(jnp.maximum(t0 - (SB_WIN - SB_Q), 0), SB_Q)
            q = q_ref[0, pl.ds(t0, SB_Q), :] * (HEAD_DIM ** -0.5)
            kw = k_ref[0, pl.ds(start, SB_WIN), :]
            vws.append(v_ref[0, pl.ds(start, SB_WIN), :])
            qidx = t0 + lax.broadcasted_iota(jnp.int32, (2 * SB_Q, SB_WIN), 0) % SB_Q
            kidx = start + lax.broadcasted_iota(jnp.int32, (2 * SB_Q, SB_WIN), 1)
            strict = kidx < qidx
            starts.append(start)
            qm2 = jnp.concatenate(
                [jnp.where((lane_q // HEAD_DIM) == head, q, jnp.zeros_like(q)) for head in range(2)], axis=0)
            z, lk = _sb_scores(qm2, kw)
            qms += [qm2[:SB_Q], qm2[SB_Q:]]
            zs.append(z)
            lks.append(jnp.where(strict, lk, 0.0))
            stricts.append(strict)
        lk_all = jnp.concatenate(lks, axis=0)
        later_all = _dot(lk_all.astype(BF16), u_win)
        csums, outs = [], []
        for sub in range(SB_GROUP):
            rows = slice(sub * 2 * SB_Q, (sub + 1) * 2 * SB_Q)
            later = later_all[rows]
            w = jnp.where(stricts[sub], jnp.exp(zs[sub] + lks[sub] + later), 0.0)
            o2 = _dot(w.astype(BF16), vws[sub])
            c2 = later[:, 0:1] + lks[sub][:, 0:1]
            outs += [o2[:SB_Q], o2[SB_Q:]]
            csums += [c2[:SB_Q], c2[SB_Q:]]

        def cond(state):
            step, cmaxes, _, _ = state
            flags = [jnp.logical_and(starts[s // 2] - (step + 1) * SB_Q >= 0, cmaxes[s] > SB_LOG_ZERO)
                     for s in range(n_streams)]
            return functools.reduce(jnp.logical_or, flags)

        def body(state):
            step, _, csums, outs = state
            new_c, new_o = [], []
            for s in range(n_streams):
                nxt = starts[s // 2] - (step + 1) * SB_Q
                in_range = nxt >= 0
                p0 = pl.multiple_of(jnp.maximum(nxt, 0), SB_Q)
                kb = k_ref[0, pl.ds(p0, SB_Q), :]
                vb = v_ref[0, pl.ds(p0, SB_Q), :]
                z, lk = _sb_scores(qms[s], kb)
                later = _dot(lk.astype(BF16), u_blk)
                w = jnp.where(in_range, jnp.exp(z + lk + later + csums[s]), 0.0)
                new_o.append(outs[s] + _dot(w.astype(BF16), vb))
                new_c.append(csums[s] + jnp.where(in_range, later[:, 0:1] + lk[:, 0:1], 0.0))
            return step + 1, tuple(jnp.max(c) for c in new_c), tuple(new_c), tuple(new_o)

        init = (jnp.int32(0), tuple(jnp.max(c) for c in csums), tuple(csums), tuple(outs))
        _, _, _, outs = lax.while_loop(cond, body, init)
        for sub in range(SB_GROUP):
            t0 = pl.multiple_of((qg * SB_GROUP + sub) * SB_Q, SB_Q)
            o_ref[0, pl.ds(t0, SB_Q), :] = jnp.where(
                lane_q < HEAD_DIM, outs[2 * sub], outs[2 * sub + 1]).astype(o_ref.dtype)
        return carry

    lax.fori_loop(0, seq // (SB_Q * SB_GROUP), q_group, 0)


def _stick_breaking(sb3):
    bsz, seq, _ = sb3.shape
    spec = lambda off: pl.BlockSpec((1, seq, LANES), lambda b, hp: (b, 0, off + hp))
    return pl.pallas_call(
        _sb_kernel,
        grid=(bsz, 2),
        in_specs=[spec(0), spec(2), spec(4)],
        out_specs=pl.BlockSpec((1, seq, LANES), lambda b, hp: (b, 0, hp)),
        out_shape=jax.ShapeDtypeStruct((bsz, seq, BRANCH_WIDTH), BF16),
        compiler_params=_cparams(("parallel", "parallel")),
        name="stick_breaking",
    )(sb3, sb3, sb3)


def _s5_kernel(u_ref, m_ref, e_ref, g_ref, pw_ref, d_ref, y_ref, state_ref):
    half = S5_GROUPS * S5_STATE

    @pl.when(pl.program_id(1) == 0)
    def _():
        state_ref[...] = jnp.zeros_like(state_ref)

    u = u_ref[0]
    ub = u.astype(BF16)
    rows = u.shape[0]

    def cmul(ar, ai, xr, xi):
        return ar * xr - ai * xi, ar * xi + ai * xr

    w = _dot(ub, e_ref[...])
    wr, wi = w[:, :half], w[:, half:]
    row = lax.broadcasted_iota(jnp.int32, (rows, half), 0)
    a1r, a1i = pw_ref[0:1, :half], pw_ref[0:1, half:]
    sr, si = state_ref[0:1, :half], state_ref[0:1, half:]
    cr, ci = cmul(a1r, a1i, sr, si)
    wr = jnp.where(row == 0, wr + cr, wr)
    wi = jnp.where(row == 0, wi + ci, wi)
    step = 0
    d = 1
    while d < rows:
        ar, ai = pw_ref[step:step + 1, :half], pw_ref[step:step + 1, half:]
        pr = jnp.where(row >= d, pltpu.roll(wr, d, axis=0), 0.0)
        pi = jnp.where(row >= d, pltpu.roll(wi, d, axis=0), 0.0)
        mr, mi = cmul(ar, ai, pr, pi)
        wr, wi = wr + mr, wi + mi
        d *= 2
        step += 1
    xr = jnp.where(row == 0, sr, pltpu.roll(wr, 1, axis=0))
    xi = jnp.where(row == 0, si, pltpu.roll(wi, 1, axis=0))
    state_ref[0:1, :half] = wr[rows - 1:rows, :]
    state_ref[0:1, half:] = wi[rows - 1:rows, :]

    y = _dot(ub, m_ref[...])
    y = y + _dot(xr.astype(BF16), g_ref[:half, :]) + _dot(xi.astype(BF16), g_ref[half:, :])
    y = y + d_ref[...] * u
    y_ref[0] = jax.nn.gelu(y).astype(y_ref.dtype)


def _s5(u3, m, e, g, pw, dskip):
    bsz, rows, width = u3.shape
    tr = min(S5_ROW_TILE, rows)
    blk = pl.BlockSpec((1, tr, width), lambda b, i: (b, i, 0))
    return pl.pallas_call(
        _s5_kernel,
        grid=(bsz, rows // tr),
        in_specs=[blk, _const_spec(m.shape), _const_spec(e.shape), _const_spec(g.shape),
                  _const_spec(pw.shape), _const_spec(dskip.shape)],
        out_specs=blk,
        out_shape=jax.ShapeDtypeStruct(u3.shape, BF16),
        scratch_shapes=[pltpu.VMEM((8, width), F32)],
        compiler_params=_cparams(("parallel", "arbitrary")),
        name="s5",
    )(u3, m, e, g, pw, dskip)


def _s5_operators(lam_re, lam_im, log_step, b_re, b_im, c_re, c_im, d_skip, row_tile):
    n_g, n_p, n_h, L = S5_GROUPS, S5_STATE, S5_GROUP_CH, S5_CHUNK
    n_c, n_s = n_g * n_h, n_g * n_p
    dt = jnp.exp(log_step)[:, None]

    def lam_bar_pow(k):
        mag = jnp.exp(k[:, None, None] * (lam_re * dt)[None])
        ang = k[:, None, None] * (lam_im * dt)[None]
        return mag * jnp.cos(ang), mag * jnp.sin(ang)

    pr, pi = lam_bar_pow(jnp.arange(L + 1, dtype=F32))
    den = lam_re * lam_re + lam_im * lam_im
    fr = ((pr[1] - 1.0) * lam_re + pi[1] * lam_im) / den
    fi = (pi[1] * lam_re - (pr[1] - 1.0) * lam_im) / den
    bbr = fr[..., None] * b_re - fi[..., None] * b_im
    bbi = fr[..., None] * b_im + fi[..., None] * b_re
    tr = pr[:L, :, :, None] * bbr[None] - pi[:L, :, :, None] * bbi[None]
    ti = pr[:L, :, :, None] * bbi[None] + pi[:L, :, :, None] * bbr[None]

    def expand(small, rows_per_group, cols_per_group):
        rows = small.shape[-2]
        tiled = jnp.tile(small, (1,) * (small.ndim - 1) + (n_g,))
        r = lax.broadcasted_iota(jnp.int32, (rows, n_g * cols_per_group), 0) // rows_per_group
        c = lax.broadcasted_iota(jnp.int32, (rows, n_g * cols_per_group), 1) // cols_per_group
        return jnp.where(r == c, tiled, 0.0)

    hp = lax.Precision.HIGHEST
    kern = (jnp.einsum('ghp,kgpi->kghi', c_re, tr, precision=hp)
            - jnp.einsum('ghp,kgpi->kghi', c_im, ti, precision=hp))
    dk = expand(kern.transpose(0, 1, 3, 2).reshape(L, n_c, n_h), n_h, n_h).astype(BF16)
    zero = jnp.zeros((n_c, n_c), BF16)
    m_op = jnp.concatenate(
        [jnp.concatenate([dk[i - j] if i >= j else zero for i in range(L)], axis=1) for j in range(L)], axis=0)

    e_r = expand(tr[::-1].transpose(0, 1, 3, 2).reshape(L, n_c, n_p), n_h, n_p)
    e_i = expand(ti[::-1].transpose(0, 1, 3, 2).reshape(L, n_c, n_p), n_h, n_p)
    e_op = jnp.concatenate([e_r, e_i], axis=2).astype(BF16).reshape(L * n_c, 2 * n_s)

    clr = c_re[None] * pr[1:, :, None, :] - c_im[None] * pi[1:, :, None, :]
    cli = c_re[None] * pi[1:, :, None, :] + c_im[None] * pr[1:, :, None, :]
    g_r = expand(clr.transpose(0, 1, 3, 2).reshape(L, n_s, n_h), n_p, n_h)
    g_i = expand(cli.transpose(0, 1, 3, 2).reshape(L, n_s, n_h), n_p, n_h)
    g_blocks = jnp.concatenate([g_r, -g_i], axis=1).astype(BF16)
    g_op = jnp.concatenate([g_blocks[i] for i in range(L)], axis=1)

    n_steps = max(1, int(math.log2(row_tile)))
    sr, si = lam_bar_pow(L * 2.0 ** jnp.arange(n_steps, dtype=F32))
    pw = jnp.concatenate([sr.reshape(n_steps, n_s), si.reshape(n_steps, n_s)], axis=1)
    pw = jnp.pad(pw, ((0, (-n_steps) % 8), (0, 0)))
    dskip = jnp.tile(d_skip.reshape(1, n_c), (1, L))
    return m_op, e_op, g_op, pw.astype(F32), dskip.astype(F32)


def _gla_constants():
    n = GLA_SUB
    ri = lax.broadcasted_iota(jnp.int32, (n, n), 0)
    ci = lax.broadcasted_iota(jnp.int32, (n, n), 1)
    same = (ri // GLA_BLOCK) == (ci // GLA_BLOCK)
    mats = [same & (ci <= ri), same & (ci > ri)]
    masks = [ri == ci]
    ups, lows = [], []
    for lev in range(GLA_LEVELS):
        half = 1 << lev
        blk = 2 * half
        mid = (ri // blk) * blk + (half - 1)
        later = (ri % blk) >= half
        ups.append(later & (ci > mid) & (ci <= ri))
        lows.append(jnp.logical_not(later) & (ci > ri) & (ci <= mid))
        masks.append(((ri // blk) == (ci // blk)) & later & ((ci % blk) < half))
    stacked = jnp.concatenate([jnp.where(m, 1.0, 0.0).astype(BF16) for m in mats + ups + lows], axis=0)
    return stacked, masks


def _gla_tile(x, gg, state, wgate, bgate, norm, stacked, masks):
    tile = x.shape[0]
    hk = GLA_HEADS * GLA_DK
    hv = GLA_HEADS * GLA_DV
    q = x[:, 0:hk].astype(F32) * (GLA_DK ** -0.5)
    k = x[:, hk:2 * hk].astype(F32)
    vb = x[:, 2 * hk:2 * hk + hv]
    r = x[:, 2 * hk + hv:].astype(F32)

    z = _dot(gg.astype(BF16), wgate) + bgate
    g = -(jnp.maximum(-z, 0.0) + jnp.log(1.0 + jnp.exp(-jnp.abs(z)))) * (1.0 / GLA_GATE_NORM)
    g_hi, g_lo = _hi_lo(g)
    g2 = jnp.concatenate([g_hi, g_lo], axis=1)

    lane_k = lax.broadcasted_iota(jnp.int32, (GLA_SUB, hk), 1) // GLA_DK
    row_blk = lax.broadcasted_iota(jnp.int32, (GLA_SUB, hk), 0) // GLA_BLOCK
    lane_v = lax.broadcasted_iota(jnp.int32, (GLA_SUB, hv), 1) // GLA_DV
    rs = lax.broadcasted_iota(jnp.int32, (hv, hk), 0) // GLA_DV
    cs = lax.broadcasted_iota(jnp.int32, (hv, hk), 1) // GLA_DK
    head_diag = rs == cs
    outs = []
    for st in range(tile // GLA_SUB):
        rows = slice(st * GLA_SUB, (st + 1) * GLA_SUB)
        sums = _dot(stacked, g2[rows])
        sums = sums[:, :hk] + sums[:, hk:]
        part = lambda i: sums[i * GLA_SUB:(i + 1) * GLA_SUB]
        b, tail = part(0), part(1)
        qs, ks, vs = q[rows], k[rows], vb[rows]

        att = [jnp.zeros((GLA_SUB, GLA_SUB), F32) for _ in range(GLA_HEADS)]
        for lev in range(-1, GLA_LEVELS):
            if lev < 0:
                qt, kt = qs.astype(BF16), ks.astype(BF16)
            else:
                qt = (qs * jnp.exp(part(2 + lev))).astype(BF16)
                kt = (ks * jnp.exp(part(2 + GLA_LEVELS + lev))).astype(BF16)
            q_heads = jnp.concatenate(
                [jnp.where(lane_k == h, qt, jnp.zeros_like(qt)) for h in range(GLA_HEADS)], axis=0)
            a = _dot_nt(q_heads, kt)
            for h in range(GLA_HEADS):
                att[h] = jnp.where(masks[lev + 1], a[h * GLA_SUB:(h + 1) * GLA_SUB], att[h])
        att_all = jnp.concatenate([a.astype(BF16) for a in att], axis=1)
        v_heads = jnp.concatenate([jnp.where(lane_v == h, vs, jnp.zeros_like(vs)) for h in range(GLA_HEADS)], axis=0)
        o = _dot(att_all, v_heads)

        q_dec = (qs * jnp.exp(b)).astype(BF16)
        k_dec = (ks * jnp.exp(tail)).astype(BF16)
        b_end = b + tail
        n_blk = GLA_SUB // GLA_BLOCK
        k_blocks = jnp.concatenate(
            [jnp.where(row_blk == n, k_dec, jnp.zeros_like(k_dec)) for n in range(n_blk)], axis=1)
        upd_all = _dot_tn(vs, k_blocks)
        cross = []
        for blk in range(n_blk):
            r0 = blk * GLA_BLOCK
            cross.append(_dot_nt(q_dec[r0:r0 + GLA_BLOCK], state.astype(BF16)))
            upd = upd_all[:, blk * hk:(blk + 1) * hk]
            state = jnp.exp(b_end[r0:r0 + 1, :]) * state + jnp.where(head_diag, upd, 0.0)
        outs.append(o + jnp.concatenate(cross, axis=0))
    o = jnp.concatenate(outs, axis=0)

    o = _group_rms(o, norm)
    return o * (r * jax.nn.sigmoid(r)), state


def _gla_kernel(x_ref, gg_ref, wgate_ref, bgate_ref, norm_ref, y_ref, state_ref):
    @pl.when(pl.program_id(0) == 0)
    def _():
        state_ref[...] = jnp.zeros_like(state_ref)

    stacked, masks = _gla_constants()
    for bi in range(x_ref.shape[0]):
        y, state = _gla_tile(x_ref[bi], gg_ref[bi], state_ref[bi], wgate_ref[...], bgate_ref[...],
                             norm_ref[...], stacked, masks)
        y_ref[bi] = y.astype(y_ref.dtype)
        state_ref[bi] = state


def _gla(x3, gg3, wgate, bgate, norm):
    bsz, seq, _ = x3.shape
    tile = min(GLA_TILE, seq)
    return pl.pallas_call(
        _gla_kernel,
        grid=(seq // tile,),
        in_specs=[pl.BlockSpec((bsz, tile, 768), lambda i: (0, i, 0)),
                  pl.BlockSpec((bsz, tile, LANES), lambda i: (0, i, 0)),
                  _const_spec(wgate.shape), _const_spec(bgate.shape), _const_spec(norm.shape)],
        out_specs=pl.BlockSpec((bsz, tile, BRANCH_WIDTH), lambda i: (0, i, 0)),
        out_shape=jax.ShapeDtypeStruct((bsz, seq, BRANCH_WIDTH), BF16),
        scratch_shapes=[pltpu.VMEM((bsz, GLA_HEADS * GLA_DV, GLA_HEADS * GLA_DK), F32)],
        compiler_params=_cparams(("arbitrary",)),
        name="gla",
    )(x3, gg3, wgate, bgate, norm)


def _dsa_attend(q, kw, vw, m0, start, n_steps):
    lane_q = lax.broadcasted_iota(jnp.int32, (DSA_BLOCK, LANES), 1)
    qidx = m0 + lax.broadcasted_iota(jnp.int32, (DSA_BLOCK, 2 * DSA_BLOCK), 0)
    kidx = start + lax.broadcasted_iota(jnp.int32, (DSA_BLOCK, 2 * DSA_BLOCK), 1)
    dist = qidx - kidx
    valid = (dist >= 0) & (dist <= n_steps)
    q = q * (HEAD_DIM ** -0.5)
    outs, lses = [], []
    for head in range(2):
        qm = jnp.where((lane_q // HEAD_DIM) == head, q, jnp.zeros_like(q))
        s = jnp.where(valid, _dot_nt(qm, kw), NEG_BIG)
        m = jnp.max(s, axis=1, keepdims=True)
        p = jnp.exp(s - m)
        l = jnp.sum(p, axis=1, keepdims=True)
        outs.append(_dot(p.astype(BF16), vw) / l)
        lses.append(jnp.broadcast_to(m + jnp.log(l), (DSA_BLOCK, LANES)))
    first = lane_q < HEAD_DIM
    return jnp.where(first, outs[0], outs[1]), jnp.where(first, lses[0], lses[1])


def _dsa_kernel(q1_ref, k1_ref, v1_ref, q4_ref, k4_ref, v4_ref, q16_ref, k16_ref, v16_ref,
                y_ref, o_scr, l_scr):
    seq = q1_ref.shape[1]
    refs = ((q1_ref, k1_ref, v1_ref), (q4_ref, k4_ref, v4_ref), (q16_ref, k16_ref, v16_ref))
    blocks_per_span = DSA_SPAN // DSA_BLOCK

    def span(sp, carry):
        for p, (window, dil) in enumerate(DSA_PATTERNS):
            q_ref, k_ref, v_ref = refs[p]
            per_res = blocks_per_span // dil

            def unit(idx, c, p=p, dil=dil, per_res=per_res, q_ref=q_ref, k_ref=k_ref, v_ref=v_ref,
                     n_steps=window // dil):
                r = idx // per_res
                jb = idx % per_res
                m0 = pl.multiple_of(sp * (DSA_SPAN // dil) + jb * DSA_BLOCK, DSA_BLOCK)
                start = pl.multiple_of(jnp.maximum(m0 - DSA_BLOCK, 0), DSA_BLOCK)
                if dil == 1:
                    q = q_ref[0, pl.ds(m0, DSA_BLOCK), :]
                    kw = k_ref[0, pl.ds(start, 2 * DSA_BLOCK), :]
                    vw = v_ref[0, pl.ds(start, 2 * DSA_BLOCK), :]
                else:
                    q = q_ref[0, r, pl.ds(m0, DSA_BLOCK), :]
                    kw = k_ref[0, r, pl.ds(start, 2 * DSA_BLOCK), :]
                    vw = v_ref[0, r, pl.ds(start, 2 * DSA_BLOCK), :]
                o, lse = _dsa_attend(q, kw, vw, m0, start, n_steps)
                rows = pl.ds(r + dil * jb * DSA_BLOCK, DSA_BLOCK, stride=dil) if dil > 1 else \
                    pl.ds(pl.multiple_of(jb * DSA_BLOCK, DSA_BLOCK), DSA_BLOCK)
                o_scr[p, rows, :] = o
                l_scr[p, rows, :] = lse
                return c

            lax.fori_loop(0, blocks_per_span, unit, 0, unroll=DSA_UNROLL)

        l1, l2, l3 = l_scr[0], l_scr[1], l_scr[2]
        lm = jnp.maximum(jnp.maximum(l1, l2), l3)
        e1, e2, e3 = jnp.exp(l1 - lm), jnp.exp(l2 - lm), jnp.exp(l3 - lm)
        y = (e1 * o_scr[0] + e2 * o_scr[1] + e3 * o_scr[2]) / (e1 + e2 + e3)
        y_ref[0, pl.ds(pl.multiple_of(sp * DSA_SPAN, DSA_SPAN), DSA_SPAN), :] = y.astype(y_ref.dtype)
        return carry

    lax.fori_loop(0, seq // DSA_SPAN, span, 0)


def _dsa(dsa1, dsa4, dsa16):
    bsz, seq, _ = dsa1.shape
    nat = lambda off: pl.BlockSpec((1, seq, LANES), lambda b, hp: (b, 0, off + hp))
    res = lambda dil, off: pl.BlockSpec((1, dil, seq // dil, LANES), lambda b, hp: (b, 0, 0, off + hp))
    return pl.pallas_call(
        _dsa_kernel,
        grid=(bsz, 2),
        in_specs=[nat(0), nat(2), nat(4), res(4, 0), res(4, 2), res(4, 4), res(16, 0), res(16, 2), res(16, 4)],
        out_specs=pl.BlockSpec((1, seq, LANES), lambda b, hp: (b, 0, hp)),
        out_shape=jax.ShapeDtypeStruct((bsz, seq, BRANCH_WIDTH), BF16),
        scratch_shapes=[pltpu.VMEM((3, DSA_SPAN, LANES), F32), pltpu.VMEM((3, DSA_SPAN, LANES), F32)],
        compiler_params=_cparams(("parallel", "parallel")),
        name="dsa",
    )(dsa1, dsa1, dsa1, dsa4, dsa4, dsa4, dsa16, dsa16, dsa16)


def _proj_weights(w_in):
    sb, s5, gq, gk, gv, gr, gg, dsa, gates = jnp.split(
        w_in, [768, 1024, 1152, 1280, 1536, 1792, 1808, 2576], axis=1)
    gg = jnp.pad(gg, ((0, 0), (0, LANES - GLA_GATE_RANK)))
    wp = jnp.concatenate([sb, s5, gq, gk, gv, gr, dsa, gg], axis=1).astype(BF16)
    return wp, gates.astype(BF16)


def _rope_tables(seq):
    inv = ROPE_THETA ** (-jnp.arange(0, HEAD_DIM, 2, dtype=F32) / HEAD_DIM)
    ang = jnp.arange(seq, dtype=F32)[:, None] * inv[None, :]
    cos, sin = jnp.cos(ang), jnp.sin(ang)
    cos_t = jnp.tile(jnp.concatenate([cos, cos], axis=1), (1, 4))
    sin_t = jnp.tile(jnp.concatenate([-sin, sin], axis=1), (1, 4))
    return cos_t, sin_t


def _pick_tile(n, pref):
    t = min(pref, n)
    while n % t:
        t //= 2
    return t


def kernel(x, ffn1_norm, ffn1_w_in, ffn1_w_out, mix_norm, w_in, s5_lam_re, s5_lam_im, s5_log_step, s5_b_re, s5_b_im, s5_c_re, s5_c_im, s5_d, s5_w_glu, gla_w_gate, gla_b_gate, gla_norm, dsa_q_norm, dsa_k_norm, w_branch, w_out, ffn2_norm, ffn2_w_in, ffn2_w_out):
    bsz, seq, _ = x.shape
    tokens = bsz * seq
    depth = ffn1_norm.shape[0]
    assert seq % DSA_SPAN == 0 and seq // DSA_PATTERNS[-1][1] >= 2 * DSA_BLOCK and seq >= SB_WIN
    tm_a = _pick_tile(seq, 512)
    tm_b = _pick_tile(tokens, 512)
    s5_rows = seq // S5_CHUNK
    s5_tile = min(S5_ROW_TILE, s5_rows)
    cos_t, sin_t = _rope_tables(seq)

    h = x.reshape(tokens, D_MODEL)
    for l in range(depth):
        wp, wg = _proj_weights(w_in[l])
        row = lambda p: p[l].reshape(1, -1)
        qn = jnp.tile(row(dsa_q_norm), (1, 4))
        kn = jnp.tile(row(dsa_k_norm), (1, 4))
        h, sb, s5u, gla_in, gg, dsa1, dsa4, dsa16 = _ffn_proj(
            h, row(ffn1_norm), ffn1_w_in[l].astype(BF16), ffn1_w_out[l].astype(BF16), row(mix_norm), wp,
            qn, kn, cos_t, sin_t, bsz, seq, tm_a)

        y_sb = _stick_breaking(sb.reshape(bsz, seq, 768)).reshape(tokens, BRANCH_WIDTH)

        ops = _s5_operators(s5_lam_re[l], s5_lam_im[l], s5_log_step[l], s5_b_re[l], s5_b_im[l],
                            s5_c_re[l], s5_c_im[l], s5_d[l], s5_tile)
        y_s5 = _s5(s5u.reshape(bsz, s5_rows, S5_CHUNK * 256), *ops).reshape(tokens // S5_CHUNK, S5_CHUNK * 256)

        wgate = jnp.pad(gla_w_gate[l], ((0, LANES - GLA_GATE_RANK), (0, 0))).astype(BF16)
        y_gla = _gla(gla_in.reshape(bsz, seq, 768), gg.reshape(bsz, seq, LANES), wgate,
                     row(gla_b_gate), row(gla_norm)).reshape(tokens, BRANCH_WIDTH)

        y_dsa = _dsa(dsa1.reshape(bsz, seq, 768), dsa4, dsa16).reshape(tokens, BRANCH_WIDTH)

        h = _merge_ffn(h, y_sb, y_s5, y_gla, y_dsa, row(mix_norm), wg,
                       s5_w_glu[l].astype(BF16), w_branch[l].astype(BF16), w_out[l].astype(BF16),
                       row(ffn2_norm), ffn2_w_in[l].astype(BF16), ffn2_w_out[l].astype(BF16), tm_b)
    return h.reshape(bsz, seq, D_MODEL)
```

```python
import functools
import math

import jax
import jax.numpy as jnp
from jax import lax
from jax.experimental import pallas as pl
from jax.experimental.pallas import tpu as pltpu

D_MODEL = 1024
HEAD_DIM = 64
BRANCH_WIDTH = 256
N_BRANCH = 4
SB_HEADS = 4
S5_GROUPS = 16
S5_GROUP_CH = 16
S5_STATE = 64
GLA_HEADS = 4
GLA_DK = 32
GLA_DV = 64
GLA_GATE_RANK = 16
GLA_GATE_NORM = 16.0
DSA_PATTERNS = ((128, 1), (512, 4), (2048, 16))
DSA_BLOCK = 128
ROPE_THETA = 10000.0
D_FF = 2816
RMS_EPS = 1e-6

LANES = 128
VMEM_LIMIT_BYTES = 58 * 1024 * 1024
FF_CHUNK = 256
N_FF_CHUNKS = D_FF // FF_CHUNK
S5_CHUNK = 8
S5_ROW_TILE = 256
GLA_TILE = 256
GLA_BLOCK = 64
GLA_LEVELS = 6
GLA_SUB = 128
SB_Q = 128
SB_WIN = 3 * SB_Q
SB_GROUP = 2
SB_LOG_ZERO = -104.0
NEG_BIG = -1e30
DSA_UNROLL = 8
DSA_SPAN = DSA_PATTERNS[-1][1] * DSA_BLOCK

PROJ_SB = (0, 768)
PROJ_S5 = (768, 1024)
PROJ_GLA = (1024, 1792)
PROJ_DSA = (1792, 2560)
PROJ_GG = (2560, 2688)
PROJ_WIDTH = 2688

F32 = jnp.float32
BF16 = jnp.bfloat16


def _cparams(sem):
    return pltpu.CompilerParams(dimension_semantics=sem, vmem_limit_bytes=VMEM_LIMIT_BYTES)


def _const_spec(shape, layer=None):
    if layer is None:
        zeros = (0,) * len(shape)
        return pl.BlockSpec(shape, lambda *_: zeros, pipeline_mode=pl.Buffered(1))
    index = (layer,) + (0,) * (len(shape) - 1)
    return pl.BlockSpec((None,) + tuple(shape[1:]), lambda *_: index, pipeline_mode=pl.Buffered(1))


def _rms_rows(x, g):
    ms = jnp.mean(x * x, axis=-1, keepdims=True)
    return x * lax.rsqrt(ms + RMS_EPS) * g


def _hi_lo(x):
    hi = x.astype(BF16)
    lo = (x - hi.astype(F32)).astype(BF16)
    return hi, lo


def _dot(a, b):
    return jnp.dot(a, b, preferred_element_type=F32)


def _dot_nt(a, b):
    return lax.dot_general(a, b, (((1,), (1,)), ((), ())), preferred_element_type=F32)


def _dot_tn(a, b):
    return lax.dot_general(a, b, (((0,), (0,)), ((), ())), preferred_element_type=F32)


def _swiglu_into(acc_ref, xb, w1_ref, w2_ref):
    for c in range(N_FF_CHUNKS):
        lo, hi = c * FF_CHUNK, (c + 1) * FF_CHUNK
        a = _dot(xb, w1_ref[:, lo:hi])
        b = _dot(xb, w1_ref[:, D_FF + lo:D_FF + hi])
        hm = (a * jax.nn.sigmoid(a) * b).astype(BF16)
        part = _dot(hm, w2_ref[lo:hi, :])
        if c == 0:
            acc_ref[...] = part
        else:
            acc_ref[...] += part


def _group_mean_matrix(width, group):
    r = lax.broadcasted_iota(jnp.int32, (width, width), 0) // group
    c = lax.broadcasted_iota(jnp.int32, (width, width), 1) // group
    return jnp.where(r == c, 1.0 / group, 0.0).astype(BF16)


def _group_rms(x, gain):
    hi, lo = _hi_lo(x * x)
    gm = _group_mean_matrix(x.shape[-1], HEAD_DIM)
    ms = _dot(hi, gm) + _dot(lo, gm)
    return x * lax.rsqrt(ms + RMS_EPS) * gain


def _swap_half_heads(x):
    half = HEAD_DIM // 2
    outs = []
    for s in range(x.shape[-1] // LANES):
        xs = x[:, s * LANES:(s + 1) * LANES]
        lane = lax.broadcasted_iota(jnp.int32, xs.shape, 1)
        up = pltpu.roll(xs, LANES - half, axis=1)
        down = pltpu.roll(xs, half, axis=1)
        outs.append(jnp.where((lane % HEAD_DIM) < half, up, down))
    return jnp.concatenate(outs, axis=-1)


def _ffn_proj_kernel(x_ref, n1_ref, w1_ref, w2_ref, nm_ref, wp_ref, qn_ref, kn_ref, cos_ref, sin_ref,
                     h_ref, sb_ref, s5_ref, gla_ref, gg_ref, dsa1_ref, dsa4_ref, dsa16_ref,
                     acc_ref, slab_ref):
    tm = x_ref.shape[0]
    x = x_ref[...]
    xb = _rms_rows(x, n1_ref[...]).astype(BF16)
    _swiglu_into(acc_ref, xb, w1_ref, w2_ref)
    h = x + 0.5 * acc_ref[...]
    h_ref[...] = h
    ub = _rms_rows(h, nm_ref[...]).astype(BF16)

    sb_ref[...] = _dot(ub, wp_ref[:, PROJ_SB[0]:PROJ_SB[1]]).astype(BF16)
    gla_ref[...] = _dot(ub, wp_ref[:, PROJ_GLA[0]:PROJ_GLA[1]]).astype(BF16)
    gg_ref[...] = _dot(ub, wp_ref[:, PROJ_GG[0]:PROJ_GG[1]])

    s5 = _dot(ub, wp_ref[:, PROJ_S5[0]:PROJ_S5[1]])
    for s in range(2):
        slab_ref[s] = s5[:, s * LANES:(s + 1) * LANES]
    for j in range(S5_CHUNK):
        for s in range(2):
            col = j * 256 + s * LANES
            s5_ref[:, col:col + LANES] = slab_ref[s, pl.ds(j, tm // S5_CHUNK, stride=S5_CHUNK), :]

    dsa = _dot(ub, wp_ref[:, PROJ_DSA[0]:PROJ_DSA[1]])
    cos = cos_ref[...]
    sin = sin_ref[...]
    parts = []
    for i, gain_ref in enumerate((qn_ref, kn_ref)):
        t = _group_rms(dsa[:, i * 256:(i + 1) * 256], gain_ref[...])
        parts.append(t * cos + _swap_half_heads(t) * sin)
    parts.append(dsa[:, 512:768])
    dsa = jnp.concatenate(parts, axis=-1)
    dsa1_ref[...] = dsa.astype(BF16)
    for s in range(6):
        slab_ref[s] = dsa[:, s * LANES:(s + 1) * LANES]
    for dil, ref in ((4, dsa4_ref), (16, dsa16_ref)):
        for r in range(dil):
            for s in range(6):
                rows = slab_ref[s, pl.ds(r, tm // dil, stride=dil), :]
                ref[0, r, :, s * LANES:(s + 1) * LANES] = rows.astype(BF16)


def _ffn_proj(x2, n1, w1, w2, nm, wp, qn, kn, cos_t, sin_t, bsz, seq, tm, layer):
    tokens = x2.shape[0]
    seq_tiles = seq // tm
    row = lambda i: (i, 0)
    tab = lambda i: (i % seq_tiles, 0)
    res = lambda i: (i // seq_tiles, 0, i % seq_tiles, 0)
    tile = lambda w: pl.BlockSpec((tm, w), row)
    s5_w = S5_CHUNK * 256
    return pl.pallas_call(
        _ffn_proj_kernel,
        grid=(tokens // tm,),
        in_specs=[tile(D_MODEL), _const_spec(n1.shape), _const_spec(w1.shape, layer), _const_spec(w2.shape, layer),
                  _const_spec(nm.shape), _const_spec(wp.shape, layer), _const_spec(qn.shape), _const_spec(kn.shape),
                  pl.BlockSpec((tm, 256), tab), pl.BlockSpec((tm, 256), tab)],
        out_specs=[tile(D_MODEL), tile(768), pl.BlockSpec((tm // S5_CHUNK, s5_w), row), tile(768), tile(LANES),
                   tile(768), pl.BlockSpec((1, 4, tm // 4, 768), res), pl.BlockSpec((1, 16, tm // 16, 768), res)],
        out_shape=[jax.ShapeDtypeStruct((tokens, D_MODEL), F32),
                   jax.ShapeDtypeStruct((tokens, 768), BF16),
                   jax.ShapeDtypeStruct((tokens // S5_CHUNK, s5_w), F32),
                   jax.ShapeDtypeStruct((tokens, 768), BF16),
                   jax.ShapeDtypeStruct((tokens, LANES), F32),
                   jax.ShapeDtypeStruct((tokens, 768), BF16),
                   jax.ShapeDtypeStruct((bsz, 4, seq // 4, 768), BF16),
                   jax.ShapeDtypeStruct((bsz, 16, seq // 16, 768), BF16)],
        scratch_shapes=[pltpu.VMEM((tm, D_MODEL), F32), pltpu.VMEM((6, tm, LANES), F32)],
        compiler_params=_cparams(("parallel",)),
        name="ffn_proj",
    )(x2, n1, w1, w2, nm, wp, qn, kn, cos_t, sin_t)


def _merge_ffn_kernel(h_ref, ysb_ref, ys5_ref, ygla_ref, ydsa_ref,
                      nm_ref, wg_ref, wglu_ref, wb_ref, wo_ref, n2_ref, w1_ref, w2_ref,
                      out_ref, acc_ref, slab_ref):
    tm = h_ref.shape[0]
    h = h_ref[...]
    ub = _rms_rows(h, nm_ref[...]).astype(BF16)

    for j in range(S5_CHUNK):
        for s in range(2):
            col = j * 256 + s * LANES
            slab_ref[s, pl.ds(j, tm // S5_CHUNK, stride=S5_CHUNK), :] = ys5_ref[:, col:col + LANES].astype(F32)
    y_s5 = jnp.concatenate([slab_ref[0], slab_ref[1]], axis=-1).astype(BF16)
    glu = _dot(y_s5, wglu_ref[...])
    y_s5 = glu[:, :BRANCH_WIDTH] * jax.nn.sigmoid(glu[:, BRANCH_WIDTH:])

    branches = (ysb_ref[...], y_s5.astype(BF16), ygla_ref[...], ydsa_ref[...])
    mixed = jnp.zeros(h.shape, F32)
    for n in range(N_BRANCH):
        gate = jax.nn.sigmoid(_dot(ub, wg_ref[:, n * D_MODEL:(n + 1) * D_MODEL]))
        mixed = mixed + gate * _dot(branches[n], wb_ref[n])
    h2 = h + _dot(mixed.astype(BF16), wo_ref[...])

    xb = _rms_rows(h2, n2_ref[...]).astype(BF16)
    _swiglu_into(acc_ref, xb, w1_ref, w2_ref)
    out_ref[...] = h2 + 0.5 * acc_ref[...]


def _merge_ffn(h, ysb, ys5, ygla, ydsa, nm, wg, wglu, wb, wo, n2, w1, w2, tm, layer):
    tokens = h.shape[0]
    row = lambda i: (i, 0)
    tile = lambda w: pl.BlockSpec((tm, w), row)
    consts = (nm, wg, wglu, wb, wo, n2, w1, w2)
    return pl.pallas_call(
        _merge_ffn_kernel,
        grid=(tokens // tm,),
        in_specs=[tile(D_MODEL), tile(BRANCH_WIDTH), pl.BlockSpec((tm // S5_CHUNK, S5_CHUNK * 256), row),
                  tile(BRANCH_WIDTH), tile(BRANCH_WIDTH)]
                 + [_const_spec(c.shape, None if c.ndim == 2 and c.shape[0] == 1 else layer) for c in consts],
        out_specs=tile(D_MODEL),
        out_shape=jax.ShapeDtypeStruct((tokens, D_MODEL), F32),
        scratch_shapes=[pltpu.VMEM((tm, D_MODEL), F32), pltpu.VMEM((2, tm, LANES), F32)],
        compiler_params=_cparams(("parallel",)),
        name="merge_ffn",
    )(h, ysb, ys5, ygla, ydsa, *consts)


def _later_key_matrix(n):
    j = lax.broadcasted_iota(jnp.int32, (n, n), 0)
    s = lax.broadcasted_iota(jnp.int32, (n, n), 1)
    return jnp.where(j > s, 1.0, 0.0).astype(BF16)


def _sb_scores(qm, k, keep=None):
    z = _dot_nt(qm, k)
    if keep is not None:
        z = jnp.where(keep, z, NEG_BIG)
    lk = -(jnp.maximum(z, 0.0) + jnp.log(1.0 + jnp.exp(-jnp.abs(z))))
    return z, lk


def _sb_kernel(q_ref, k_ref, v_ref, o_ref):
    seq = q_ref.shape[1]
    u_win = _later_key_matrix(SB_WIN)
    u_blk = _later_key_matrix(SB_Q)
    lane_q = lax.broadcasted_iota(jnp.int32, (SB_Q, LANES), 1)
    n_streams = 2 * SB_GROUP
    key_minus_query = (lax.broadcasted_iota(jnp.int32, (2 * SB_Q, SB_WIN), 1)
                       - lax.broadcasted_iota(jnp.int32, (2 * SB_Q, SB_WIN), 0) % SB_Q)

    def q_group(qg, carry):
        starts, qms, zs, lks, vws = [], [], [], [], []
        for sub in range(SB_GROUP):
            t0 = pl.multiple_of((qg * SB_GROUP + sub) * SB_Q, SB_Q)
            start = pl.multiple_of(jnp.maximum(t0 - (SB_WIN - SB_Q), 0), SB_Q)
            q = q_ref[0, pl.ds(t0, SB_Q), :] * (HEAD_DIM ** -0.5)
            kw = k_ref[0, pl.ds(start, SB_WIN), :]
            vws.append(v_ref[0, pl.ds(start, SB_WIN), :])
            strict = key_minus_query < (t0 - start)
            starts.append(start)
            qm2 = jnp.concatenate(
                [jnp.where((lane_q // HEAD_DIM) == head, q, jnp.zeros_like(q)) for head in range(2)], axis=0)
            z, lk = _sb_scores(qm2, kw, strict)
            qms += [qm2[:SB_Q], qm2[SB_Q:]]
            zs.append(z)
            lks.append(lk)
        lk_all = jnp.concatenate(lks, axis=0)
        later_all = _dot(lk_all.astype(BF16), u_win)
        csums, outs = [], []
        for sub in range(SB_GROUP):
            rows = slice(sub * 2 * SB_Q, (sub + 1) * 2 * SB_Q)
            later = later_all[rows]
            w = jnp.exp(zs[sub] + lks[sub] + later)
            o2 = _dot(w.astype(BF16), vws[sub])
            c2 = later[:, 0:1] + lks[sub][:, 0:1]
            outs += [o2[:SB_Q], o2[SB_Q:]]
            csums += [c2[:SB_Q], c2[SB_Q:]]

        def cond(state):
            step, cmaxes, _, _ = state
            flags = [jnp.logical_and(starts[s // 2] - (step + 1) * SB_Q >= 0, cmaxes[s] > SB_LOG_ZERO)
                     for s in range(n_streams)]
            return functools.reduce(jnp.logical_or, flags)

        def body(state):
            step, _, csums, outs = state
            new_c, new_o = [], []
            for s in range(n_streams):
                nxt = starts[s // 2] - (step + 1) * SB_Q
                in_range = nxt >= 0
                p0 = pl.multiple_of(jnp.maximum(nxt, 0), SB_Q)
                kb = k_ref[0, pl.ds(p0, SB_Q), :]
                vb = v_ref[0, pl.ds(p0, SB_Q), :]
                z, lk = _sb_scores(qms[s], kb)
                later = _dot(lk.astype(BF16), u_blk)
                w = jnp.where(in_range, jnp.exp(z + lk + later + csums[s]), 0.0)
                new_o.append(outs[s] + _dot(w.astype(BF16), vb))
                new_c.append(csums[s] + jnp.where(in_range, later[:, 0:1] + lk[:, 0:1], 0.0))
            return step + 1, tuple(jnp.max(c) for c in new_c), tuple(new_c), tuple(new_o)

        init = (jnp.int32(0), tuple(jnp.max(c) for c in csums), tuple(csums), tuple(outs))
        _, _, _, outs = lax.while_loop(cond, body, init)
        for sub in range(SB_GROUP):
            t0 = pl.multiple_of((qg * SB_GROUP + sub) * SB_Q, SB_Q)
            o_ref[0, pl.ds(t0, SB_Q), :] = jnp.where(
                lane_q < HEAD_DIM, outs[2 * sub], outs[2 * sub + 1]).astype(o_ref.dtype)
        return carry

    lax.fori_loop(0, seq // (SB_Q * SB_GROUP), q_group, 0)


def _stick_breaking(sb3):
    bsz, seq, _ = sb3.shape
    spec = lambda off: pl.BlockSpec((1, seq, LANES), lambda b, hp: (b, 0, off + hp))
    return pl.pallas_call(
        _sb_kernel,
        grid=(bsz, 2),
        in_specs=[spec(0), spec(2), spec(4)],
        out_specs=pl.BlockSpec((1, seq, LANES), lambda b, hp: (b, 0, hp)),
        out_shape=jax.ShapeDtypeStruct((bsz, seq, BRANCH_WIDTH), BF16),
        compiler_params=_cparams(("parallel", "parallel")),
        name="stick_breaking",
    )(sb3, sb3, sb3)


def _s5_kernel(u_ref, m_ref, e_ref, g_ref, pw_ref, d_ref, y_ref, state_ref):
    half = S5_GROUPS * S5_STATE

    @pl.when(pl.program_id(1) == 0)
    def _():
        state_ref[...] = jnp.zeros_like(state_ref)

    u = u_ref[0]
    ub = u.astype(BF16)
    rows = u.shape[0]

    def cmul(ar, ai, xr, xi):
        return ar * xr - ai * xi, ar * xi + ai * xr

    w = _dot(ub, e_ref[...])
    wr, wi = w[:, :half], w[:, half:]
    row = lax.broadcasted_iota(jnp.int32, (rows, half), 0)
    a1r, a1i = pw_ref[0:1, :half], pw_ref[0:1, half:]
    sr, si = state_ref[0:1, :half], state_ref[0:1, half:]
    cr, ci = cmul(a1r, a1i, sr, si)
    wr = jnp.where(row == 0, wr + cr, wr)
    wi = jnp.where(row == 0, wi + ci, wi)
    step = 0
    d = 1
    while d < rows:
        ar, ai = pw_ref[step:step + 1, :half], pw_ref[step:step + 1, half:]
        pr = jnp.where(row >= d, pltpu.roll(wr, d, axis=0), 0.0)
        pi = jnp.where(row >= d, pltpu.roll(wi, d, axis=0), 0.0)
        mr, mi = cmul(ar, ai, pr, pi)
        wr, wi = wr + mr, wi + mi
        d *= 2
        step += 1
    xr = jnp.where(row == 0, sr, pltpu.roll(wr, 1, axis=0))
    xi = jnp.where(row == 0, si, pltpu.roll(wi, 1, axis=0))
    state_ref[0:1, :half] = wr[rows - 1:rows, :]
    state_ref[0:1, half:] = wi[rows - 1:rows, :]

    y = _dot(ub, m_ref[...])
    y = y + _dot(xr.astype(BF16), g_ref[:half, :]) + _dot(xi.astype(BF16), g_ref[half:, :])
    y = y + d_ref[...] * u
    y_ref[0] = jax.nn.gelu(y).astype(y_ref.dtype)


def _s5(u3, m, e, g, pw, dskip, layer):
    bsz, rows, width = u3.shape
    tr = min(S5_ROW_TILE, rows)
    blk = pl.BlockSpec((1, tr, width), lambda b, i: (b, i, 0))
    return pl.pallas_call(
        _s5_kernel,
        grid=(bsz, rows // tr),
        in_specs=[blk] + [_const_spec(a.shape, layer) for a in (m, e, g, pw, dskip)],
        out_specs=blk,
        out_shape=jax.ShapeDtypeStruct(u3.shape, BF16),
        scratch_shapes=[pltpu.VMEM((8, width), F32)],
        compiler_params=_cparams(("parallel", "arbitrary")),
        name="s5",
    )(u3, m, e, g, pw, dskip)


def _s5_operators(lam_re, lam_im, log_step, b_re, b_im, c_re, c_im, d_skip, row_tile):
    n_g, n_p, n_h, L = S5_GROUPS, S5_STATE, S5_GROUP_CH, S5_CHUNK
    n_c, n_s = n_g * n_h, n_g * n_p
    dt = jnp.exp(log_step)[:, None]

    def lam_bar_pow(k):
        mag = jnp.exp(k[:, None, None] * (lam_re * dt)[None])
        ang = k[:, None, None] * (lam_im * dt)[None]
        return mag * jnp.cos(ang), mag * jnp.sin(ang)

    pr, pi = lam_bar_pow(jnp.arange(L + 1, dtype=F32))
    den = lam_re * lam_re + lam_im * lam_im
    fr = ((pr[1] - 1.0) * lam_re + pi[1] * lam_im) / den
    fi = (pi[1] * lam_re - (pr[1] - 1.0) * lam_im) / den
    bbr = fr[..., None] * b_re - fi[..., None] * b_im
    bbi = fr[..., None] * b_im + fi[..., None] * b_re
    tr = pr[:L, :, :, None] * bbr[None] - pi[:L, :, :, None] * bbi[None]
    ti = pr[:L, :, :, None] * bbi[None] + pi[:L, :, :, None] * bbr[None]

    def expand(small, rows_per_group, cols_per_group):
        rows = small.shape[-2]
        tiled = jnp.tile(small, (1,) * (small.ndim - 1) + (n_g,))
        r = lax.broadcasted_iota(jnp.int32, (rows, n_g * cols_per_group), 0) // rows_per_group
        c = lax.broadcasted_iota(jnp.int32, (rows, n_g * cols_per_group), 1) // cols_per_group
        return jnp.where(r == c, tiled, 0.0)

    hp = lax.Precision.HIGHEST
    kern = (jnp.einsum('ghp,kgpi->kghi', c_re, tr, precision=hp)
            - jnp.einsum('ghp,kgpi->kghi', c_im, ti, precision=hp))
    dk = expand(kern.transpose(0, 1, 3, 2).reshape(L, n_c, n_h), n_h, n_h).astype(BF16)
    zero = jnp.zeros((n_c, n_c), BF16)
    m_op = jnp.concatenate(
        [jnp.concatenate([dk[i - j] if i >= j else zero for i in range(L)], axis=1) for j in range(L)], axis=0)

    e_r = expand(tr[::-1].transpose(0, 1, 3, 2).reshape(L, n_c, n_p), n_h, n_p)
    e_i = expand(ti[::-1].transpose(0, 1, 3, 2).reshape(L, n_c, n_p), n_h, n_p)
    e_op = jnp.concatenate([e_r, e_i], axis=2).astype(BF16).reshape(L * n_c, 2 * n_s)

    clr = c_re[None] * pr[1:, :, None, :] - c_im[None] * pi[1:, :, None, :]
    cli = c_re[None] * pi[1:, :, None, :] + c_im[None] * pr[1:, :, None, :]
    g_r = expand(clr.transpose(0, 1, 3, 2).reshape(L, n_s, n_h), n_p, n_h)
    g_i = expand(cli.transpose(0, 1, 3, 2).reshape(L, n_s, n_h), n_p, n_h)
    g_blocks = jnp.concatenate([g_r, -g_i], axis=1).astype(BF16)
    g_op = jnp.concatenate([g_blocks[i] for i in range(L)], axis=1)

    n_steps = max(1, int(math.log2(row_tile)))
    sr, si = lam_bar_pow(L * 2.0 ** jnp.arange(n_steps, dtype=F32))
    pw = jnp.concatenate([sr.reshape(n_steps, n_s), si.reshape(n_steps, n_s)], axis=1)
    pw = jnp.pad(pw, ((0, (-n_steps) % 8), (0, 0)))
    dskip = jnp.tile(d_skip.reshape(1, n_c), (1, L))
    return m_op, e_op, g_op, pw.astype(F32), dskip.astype(F32)


def _gla_constants():
    n = GLA_SUB
    ri = lax.broadcasted_iota(jnp.int32, (n, n), 0)
    ci = lax.broadcasted_iota(jnp.int32, (n, n), 1)
    same = (ri // GLA_BLOCK) == (ci // GLA_BLOCK)
    mats = [same & (ci <= ri), same & (ci > ri)]
    masks = [ri == ci]
    ups, lows = [], []
    for lev in range(GLA_LEVELS):
        half = 1 << lev
        blk = 2 * half
        mid = (ri // blk) * blk + (half - 1)
        later = (ri % blk) >= half
        ups.append(later & (ci > mid) & (ci <= ri))
        lows.append(jnp.logical_not(later) & (ci > ri) & (ci <= mid))
        masks.append(((ri // blk) == (ci // blk)) & later & ((ci % blk) < half))
    stacked = jnp.concatenate([jnp.where(m, 1.0, 0.0).astype(BF16) for m in mats + ups + lows], axis=0)
    return stacked, masks


def _gla_tile(x, gg, state, wgate, bgate, norm, stacked, masks):
    tile = x.shape[0]
    hk = GLA_HEADS * GLA_DK
    hv = GLA_HEADS * GLA_DV
    q = x[:, 0:hk].astype(F32) * (GLA_DK ** -0.5)
    k = x[:, hk:2 * hk].astype(F32)
    vb = x[:, 2 * hk:2 * hk + hv]
    r = x[:, 2 * hk + hv:].astype(F32)

    z = _dot(gg.astype(BF16), wgate) + bgate
    g = -(jnp.maximum(-z, 0.0) + jnp.log(1.0 + jnp.exp(-jnp.abs(z)))) * (1.0 / GLA_GATE_NORM)
    g_hi, g_lo = _hi_lo(g)
    g2 = jnp.concatenate([g_hi, g_lo], axis=1)

    lane_k = lax.broadcasted_iota(jnp.int32, (GLA_SUB, hk), 1) // GLA_DK
    row_blk = lax.broadcasted_iota(jnp.int32, (GLA_SUB, hk), 0) // GLA_BLOCK
    lane_v = lax.broadcasted_iota(jnp.int32, (GLA_SUB, hv), 1) // GLA_DV
    rs = lax.broadcasted_iota(jnp.int32, (hv, hk), 0) // GLA_DV
    cs = lax.broadcasted_iota(jnp.int32, (hv, hk), 1) // GLA_DK
    head_diag = rs == cs
    outs = []
    for st in range(tile // GLA_SUB):
        rows = slice(st * GLA_SUB, (st + 1) * GLA_SUB)
        sums = _dot(stacked, g2[rows])
        sums = sums[:, :hk] + sums[:, hk:]
        part = lambda i: sums[i * GLA_SUB:(i + 1) * GLA_SUB]
        b, tail = part(0), part(1)
        qs, ks, vs = q[rows], k[rows], vb[rows]

        att = [jnp.zeros((GLA_SUB, GLA_SUB), F32) for _ in range(GLA_HEADS)]
        for lev in range(-1, GLA_LEVELS):
            if lev < 0:
                qt, kt = qs.astype(BF16), ks.astype(BF16)
            else:
                qt = (qs * jnp.exp(part(2 + lev))).astype(BF16)
                kt = (ks * jnp.exp(part(2 + GLA_LEVELS + lev))).astype(BF16)
            q_heads = jnp.concatenate(
                [jnp.where(lane_k == h, qt, jnp.zeros_like(qt)) for h in range(GLA_HEADS)], axis=0)
            a = _dot_nt(q_heads, kt)
            for h in range(GLA_HEADS):
                att[h] = jnp.where(masks[lev + 1], a[h * GLA_SUB:(h + 1) * GLA_SUB], att[h])
        att_all = jnp.concatenate([a.astype(BF16) for a in att], axis=1)
        v_heads = jnp.concatenate([jnp.where(lane_v == h, vs, jnp.zeros_like(vs)) for h in range(GLA_HEADS)], axis=0)
        o = _dot(att_all, v_heads)

        q_dec = (qs * jnp.exp(b)).astype(BF16)
        k_dec = (ks * jnp.exp(tail)).astype(BF16)
        b_end = b + tail
        n_blk = GLA_SUB // GLA_BLOCK
        k_blocks = jnp.concatenate(
            [jnp.where(row_blk == n, k_dec, jnp.zeros_like(k_dec)) for n in range(n_blk)], axis=1)
        upd_all = _dot_tn(vs, k_blocks)
        cross = []
        for blk in range(n_blk):
            r0 = blk * GLA_BLOCK
            cross.append(_dot_nt(q_dec[r0:r0 + GLA_BLOCK], state.astype(BF16)))
            upd = upd_all[:, blk * hk:(blk + 1) * hk]
            state = jnp.exp(b_end[r0:r0 + 1, :]) * state + jnp.where(head_diag, upd, 0.0)
        outs.append(o + jnp.concatenate(cross, axis=0))
    o = jnp.concatenate(outs, axis=0)

    o = _group_rms(o, norm)
    return o * (r * jax.nn.sigmoid(r)), state


def _gla_kernel(x_ref, gg_ref, wgate_ref, bgate_ref, norm_ref, y_ref, state_ref):
    @pl.when(pl.program_id(0) == 0)
    def _():
        state_ref[...] = jnp.zeros_like(state_ref)

    stacked, masks = _gla_constants()
    for bi in range(x_ref.shape[0]):
        y, state = _gla_tile(x_ref[bi], gg_ref[bi], state_ref[bi], wgate_ref[...], bgate_ref[...],
                             norm_ref[...], stacked, masks)
        y_ref[bi] = y.astype(y_ref.dtype)
        state_ref[bi] = state


def _gla(x3, gg3, wgate, bgate, norm):
    bsz, seq, _ = x3.shape
    tile = min(GLA_TILE, seq)
    return pl.pallas_call(
        _gla_kernel,
        grid=(seq // tile,),
        in_specs=[pl.BlockSpec((bsz, tile, 768), lambda i: (0, i, 0)),
                  pl.BlockSpec((bsz, tile, LANES), lambda i: (0, i, 0)),
                  _const_spec(wgate.shape), _const_spec(bgate.shape), _const_spec(norm.shape)],
        out_specs=pl.BlockSpec((bsz, tile, BRANCH_WIDTH), lambda i: (0, i, 0)),
        out_shape=jax.ShapeDtypeStruct((bsz, seq, BRANCH_WIDTH), BF16),
        scratch_shapes=[pltpu.VMEM((bsz, GLA_HEADS * GLA_DV, GLA_HEADS * GLA_DK), F32)],
        compiler_params=_cparams(("arbitrary",)),
        name="gla",
    )(x3, gg3, wgate, bgate, norm)


def _dsa_band_bias(offset, n_steps):
    row = lax.broadcasted_iota(jnp.int32, (2 * DSA_BLOCK, 2 * DSA_BLOCK), 0) % DSA_BLOCK
    col = lax.broadcasted_iota(jnp.int32, (2 * DSA_BLOCK, 2 * DSA_BLOCK), 1)
    dist = offset + row - col
    return jnp.where((dist >= 0) & (dist <= n_steps), 0.0, NEG_BIG)


def _dsa_attend(q, kw, vw, bias):
    lane_q = lax.broadcasted_iota(jnp.int32, (DSA_BLOCK, LANES), 1)
    q = q * (HEAD_DIM ** -0.5)
    qm2 = jnp.concatenate(
        [jnp.where((lane_q // HEAD_DIM) == head, q, jnp.zeros_like(q)) for head in range(2)], axis=0)
    s = _dot_nt(qm2, kw) + bias
    m = jnp.max(s, axis=1, keepdims=True)
    p = jnp.exp(s - m)
    l = jnp.sum(p, axis=1, keepdims=True)
    o2 = _dot(p.astype(BF16), vw) / l
    lse2 = jnp.broadcast_to(m + jnp.log(l), (2 * DSA_BLOCK, LANES))
    first = lane_q < HEAD_DIM
    return (jnp.where(first, o2[:DSA_BLOCK], o2[DSA_BLOCK:]),
            jnp.where(first, lse2[:DSA_BLOCK], lse2[DSA_BLOCK:]))


def _dsa_kernel(q1_ref, k1_ref, v1_ref, q4_ref, k4_ref, v4_ref, q16_ref, k16_ref, v16_ref,
                y_ref, o_scr, l_scr):
    seq = q1_ref.shape[1]
    refs = ((q1_ref, k1_ref, v1_ref), (q4_ref, k4_ref, v4_ref), (q16_ref, k16_ref, v16_ref))
    blocks_per_span = DSA_SPAN // DSA_BLOCK
    n_steps = DSA_PATTERNS[0][0] // DSA_PATTERNS[0][1]
    assert all(w // d == n_steps for w, d in DSA_PATTERNS)
    bias_inner = _dsa_band_bias(DSA_BLOCK, n_steps)
    bias_first = _dsa_band_bias(0, n_steps)

    def span(sp, carry):
        for p, (_, dil) in enumerate(DSA_PATTERNS):
            q_ref, k_ref, v_ref = refs[p]
            per_res = blocks_per_span // dil

            def unit(idx, c, p=p, dil=dil, per_res=per_res, q_ref=q_ref, k_ref=k_ref, v_ref=v_ref):
                r = idx // per_res
                jb = idx % per_res
                m0 = pl.multiple_of(sp * (DSA_SPAN // dil) + jb * DSA_BLOCK, DSA_BLOCK)
                start = pl.multiple_of(jnp.maximum(m0 - DSA_BLOCK, 0), DSA_BLOCK)
                if dil == 1:
                    q = q_ref[0, pl.ds(m0, DSA_BLOCK), :]
                    kw = k_ref[0, pl.ds(start, 2 * DSA_BLOCK), :]
                    vw = v_ref[0, pl.ds(start, 2 * DSA_BLOCK), :]
                else:
                    q = q_ref[0, r, pl.ds(m0, DSA_BLOCK), :]
                    kw = k_ref[0, r, pl.ds(start, 2 * DSA_BLOCK), :]
                    vw = v_ref[0, r, pl.ds(start, 2 * DSA_BLOCK), :]
                o, lse = _dsa_attend(q, kw, vw, jnp.where(m0 == 0, bias_first, bias_inner))
                rows = pl.ds(r + dil * jb * DSA_BLOCK, DSA_BLOCK, stride=dil) if dil > 1 else \
                    pl.ds(pl.multiple_of(jb * DSA_BLOCK, DSA_BLOCK), DSA_BLOCK)
                o_scr[p, rows, :] = o
                l_scr[p, rows, :] = lse
                return c

            lax.fori_loop(0, blocks_per_span, unit, 0, unroll=DSA_UNROLL)

        l1, l2, l3 = l_scr[0], l_scr[1], l_scr[2]
        lm = jnp.maximum(jnp.maximum(l1, l2), l3)
        e1, e2, e3 = jnp.exp(l1 - lm), jnp.exp(l2 - lm), jnp.exp(l3 - lm)
        y = (e1 * o_scr[0] + e2 * o_scr[1] + e3 * o_scr[2]) / (e1 + e2 + e3)
        y_ref[0, pl.ds(pl.multiple_of(sp * DSA_SPAN, DSA_SPAN), DSA_SPAN), :] = y.astype(y_ref.dtype)
        return carry

    lax.fori_loop(0, seq // DSA_SPAN, span, 0)


def _dsa(dsa1, dsa4, dsa16):
    bsz, seq, _ = dsa1.shape
    nat = lambda off: pl.BlockSpec((1, seq, LANES), lambda b, hp: (b, 0, off + hp))
    res = lambda dil, off: pl.BlockSpec((1, dil, seq // dil, LANES), lambda b, hp: (b, 0, 0, off + hp))
    return pl.pallas_call(
        _dsa_kernel,
        grid=(bsz, 2),
        in_specs=[nat(0), nat(2), nat(4), res(4, 0), res(4, 2), res(4, 4), res(16, 0), res(16, 2), res(16, 4)],
        out_specs=pl.BlockSpec((1, seq, LANES), lambda b, hp: (b, 0, hp)),
        out_shape=jax.ShapeDtypeStruct((bsz, seq, BRANCH_WIDTH), BF16),
        scratch_shapes=[pltpu.VMEM((3, DSA_SPAN, LANES), F32), pltpu.VMEM((3, DSA_SPAN, LANES), F32)],
        compiler_params=_cparams(("parallel", "parallel")),
        name="dsa",
    )(dsa1, dsa1, dsa1, dsa4, dsa4, dsa4, dsa16, dsa16, dsa16)


def _proj_weights(w_in):
    lead = w_in[..., :1792].astype(BF16)
    gg = jnp.pad(w_in[..., 1792:1808].astype(BF16), ((0, 0), (0, 0), (0, LANES - GLA_GATE_RANK)))
    dsa = w_in[..., 1808:2576].astype(BF16)
    wp = jnp.concatenate([lead, dsa, gg], axis=-1)
    return wp, w_in[..., 2576:].astype(BF16)


def _rope_tables(seq):
    inv = ROPE_THETA ** (-jnp.arange(0, HEAD_DIM, 2, dtype=F32) / HEAD_DIM)
    ang = jnp.arange(seq, dtype=F32)[:, None] * inv[None, :]
    cos, sin = jnp.cos(ang), jnp.sin(ang)
    cos_t = jnp.tile(jnp.concatenate([cos, cos], axis=1), (1, 4))
    sin_t = jnp.tile(jnp.concatenate([-sin, sin], axis=1), (1, 4))
    return cos_t, sin_t


def _pick_tile(n, pref):
    t = min(pref, n)
    while n % t:
        t //= 2
    return t


def kernel(x, ffn1_norm, ffn1_w_in, ffn1_w_out, mix_norm, w_in, s5_lam_re, s5_lam_im, s5_log_step, s5_b_re, s5_b_im, s5_c_re, s5_c_im, s5_d, s5_w_glu, gla_w_gate, gla_b_gate, gla_norm, dsa_q_norm, dsa_k_norm, w_branch, w_out, ffn2_norm, ffn2_w_in, ffn2_w_out):
    bsz, seq, _ = x.shape
    tokens = bsz * seq
    depth = ffn1_norm.shape[0]
    assert seq % DSA_SPAN == 0 and seq // DSA_PATTERNS[-1][1] >= 2 * DSA_BLOCK and seq >= SB_WIN
    tm_a = _pick_tile(seq, 512)
    tm_b = _pick_tile(tokens, 512)
    s5_rows = seq // S5_CHUNK
    s5_tile = min(S5_ROW_TILE, s5_rows)
    cos_t, sin_t = _rope_tables(seq)

    wp_all, wg_all = _proj_weights(w_in)
    w1a_all, w2a_all = ffn1_w_in.astype(BF16), ffn1_w_out.astype(BF16)
    w1b_all, w2b_all = ffn2_w_in.astype(BF16), ffn2_w_out.astype(BF16)
    wglu_all, wb_all, wo_all = s5_w_glu.astype(BF16), w_branch.astype(BF16), w_out.astype(BF16)
    s5_ops = jax.vmap(functools.partial(_s5_operators, row_tile=s5_tile))(
        s5_lam_re, s5_lam_im, s5_log_step, s5_b_re, s5_b_im, s5_c_re, s5_c_im, s5_d)

    h = x.reshape(tokens, D_MODEL)
    for l in range(depth):
        row = lambda p: p[l].reshape(1, -1)
        qn = jnp.tile(row(dsa_q_norm), (1, 4))
        kn = jnp.tile(row(dsa_k_norm), (1, 4))
        h, sb, s5u, gla_in, gg, dsa1, dsa4, dsa16 = _ffn_proj(
            h, row(ffn1_norm), w1a_all, w2a_all, row(mix_norm), wp_all, qn, kn, cos_t, sin_t, bsz, seq, tm_a, l)

        y_sb = _stick_breaking(sb.reshape(bsz, seq, 768)).reshape(tokens, BRANCH_WIDTH)

        y_s5 = _s5(s5u.reshape(bsz, s5_rows, S5_CHUNK * 256), *s5_ops, l).reshape(tokens // S5_CHUNK, S5_CHUNK * 256)

        wgate = jnp.pad(gla_w_gate[l], ((0, LANES - GLA_GATE_RANK), (0, 0))).astype(BF16)
        y_gla = _gla(gla_in.reshape(bsz, seq, 768), gg.reshape(bsz, seq, LANES), wgate,
                     row(gla_b_gate), row(gla_norm)).reshape(tokens, BRANCH_WIDTH)

        y_dsa = _dsa(dsa1.reshape(bsz, seq, 768), dsa4, dsa16).reshape(tokens, BRANCH_WIDTH)

        h = _merge_ffn(h, y_sb, y_s5, y_gla, y_dsa, row(mix_norm), wg_all, wglu_all, wb_all, wo_all,
                       row(ffn2_norm), w1b_all, w2b_all, tm_b, l)
    return h.reshape(bsz, seq, D_MODEL)
```

```python
import functools
import math

import jax
import jax.numpy as jnp
from jax import lax
from jax.experimental import pallas as pl
from jax.experimental.pallas import tpu as pltpu

D_MODEL = 1024
HEAD_DIM = 64
BRANCH_WIDTH = 256
N_BRANCH = 4
SB_HEADS = 4
S5_GROUPS = 16
S5_GROUP_CH = 16
S5_STATE = 64
GLA_HEADS = 4
GLA_DK = 32
GLA_DV = 64
GLA_GATE_RANK = 16
GLA_GATE_NORM = 16.0
DSA_PATTERNS = ((128, 1), (512, 4), (2048, 16))
DSA_BLOCK = 128
ROPE_THETA = 10000.0
D_FF = 2816
RMS_EPS = 1e-6

LANES = 128
VMEM_LIMIT_BYTES = 58 * 1024 * 1024
FF_CHUNK = 256
N_FF_CHUNKS = D_FF // FF_CHUNK
S5_CHUNK = 8
S5_ROW_TILE = 256
GLA_TILE = 256
GLA_BLOCK = 64
GLA_LEVELS = 6
GLA_SUB = 128
SB_Q = 64
SB_WIN = 4 * SB_Q
SB_GROUP = 4
SB_LOG_ZERO = -104.0
NEG_BIG = -1e30
DSA_UNROLL = 8
DSA_SPAN = DSA_PATTERNS[-1][1] * DSA_BLOCK

PROJ_SB = (0, 768)
PROJ_S5 = (768, 1024)
PROJ_GLA = (1024, 1792)
PROJ_DSA = (1792, 2560)
PROJ_GG = (2560, 2688)
PROJ_WIDTH = 2688

F32 = jnp.float32
BF16 = jnp.bfloat16


def _cparams(sem):
    return pltpu.CompilerParams(dimension_semantics=sem, vmem_limit_bytes=VMEM_LIMIT_BYTES)


def _const_spec(shape, layer=None):
    if layer is None:
        zeros = (0,) * len(shape)
        return pl.BlockSpec(shape, lambda *_: zeros, pipeline_mode=pl.Buffered(1))
    index = (layer,) + (0,) * (len(shape) - 1)
    return pl.BlockSpec((None,) + tuple(shape[1:]), lambda *_: index, pipeline_mode=pl.Buffered(1))


def _rms_rows(x, g):
    ms = jnp.mean(x * x, axis=-1, keepdims=True)
    return x * lax.rsqrt(ms + RMS_EPS) * g


def _hi_lo(x):
    hi = x.astype(BF16)
    lo = (x - hi.astype(F32)).astype(BF16)
    return hi, lo


def _dot(a, b):
    return jnp.dot(a, b, preferred_element_type=F32)


def _dot_nt(a, b):
    return lax.dot_general(a, b, (((1,), (1,)), ((), ())), preferred_element_type=F32)


def _dot_tn(a, b):
    return lax.dot_general(a, b, (((0,), (0,)), ((), ())), preferred_element_type=F32)


def _swiglu_into(acc_ref, xb, w1_ref, w2_ref):
    for c in range(N_FF_CHUNKS):
        lo, hi = c * FF_CHUNK, (c + 1) * FF_CHUNK
        a = _dot(xb, w1_ref[:, lo:hi])
        b = _dot(xb, w1_ref[:, D_FF + lo:D_FF + hi])
        hm = (a * jax.nn.sigmoid(a) * b).astype(BF16)
        part = _dot(hm, w2_ref[lo:hi, :])
        if c == 0:
            acc_ref[...] = part
        else:
            acc_ref[...] += part


def _group_mean_matrix(width, group):
    r = lax.broadcasted_iota(jnp.int32, (width, width), 0) // group
    c = lax.broadcasted_iota(jnp.int32, (width, width), 1) // group
    return jnp.where(r == c, 1.0 / group, 0.0).astype(BF16)


def _group_rms(x, gain):
    gm = _group_mean_matrix(x.shape[-1], HEAD_DIM)
    ms = _dot((x * x).astype(BF16), gm)
    return x * lax.rsqrt(ms + RMS_EPS) * gain


def _swap_half_heads(x):
    half = HEAD_DIM // 2
    outs = []
    for s in range(x.shape[-1] // LANES):
        xs = x[:, s * LANES:(s + 1) * LANES]
        lane = lax.broadcasted_iota(jnp.int32, xs.shape, 1)
        up = pltpu.roll(xs, LANES - half, axis=1)
        down = pltpu.roll(xs, half, axis=1)
        outs.append(jnp.where((lane % HEAD_DIM) < half, up, down))
    return jnp.concatenate(outs, axis=-1)


def _ffn_proj_kernel(x_ref, n1_ref, w1_ref, w2_ref, nm_ref, wp_ref, qn_ref, kn_ref, cos_ref, sin_ref,
                     h_ref, sb_ref, s5_ref, gla_ref, gg_ref, dsa1_ref, dsa4_ref, dsa16_ref,
                     acc_ref, slab_ref):
    tm = x_ref.shape[0]
    x = x_ref[...]
    xb = _rms_rows(x, n1_ref[...]).astype(BF16)
    _swiglu_into(acc_ref, xb, w1_ref, w2_ref)
    h = x + 0.5 * acc_ref[...]
    h_ref[...] = h
    ub = _rms_rows(h, nm_ref[...]).astype(BF16)

    dsa = _dot(ub, wp_ref[:, PROJ_DSA[0]:PROJ_DSA[1]])
    cos = jnp.concatenate([cos_ref[...]] * 2, axis=-1)
    sin = jnp.concatenate([sin_ref[...]] * 2, axis=-1)
    parts = []
    for i, gain_ref in enumerate((qn_ref, kn_ref)):
        t = _group_rms(dsa[:, i * 256:(i + 1) * 256], gain_ref[...])
        parts.append(t * cos + _swap_half_heads(t) * sin)
    parts.append(dsa[:, 512:768])
    dsa = jnp.concatenate(parts, axis=-1)
    dsa1_ref[...] = dsa.astype(BF16)
    for s in range(6):
        slab_ref[s] = dsa[:, s * LANES:(s + 1) * LANES]
    for dil, ref in ((4, dsa4_ref), (16, dsa16_ref)):
        for r in range(dil):
            for s in range(6):
                rows = slab_ref[s, pl.ds(r, tm // dil, stride=dil), :]
                ref[0, r, :, s * LANES:(s + 1) * LANES] = rows.astype(BF16)

    s5 = _dot(ub, wp_ref[:, PROJ_S5[0]:PROJ_S5[1]])
    for s in range(2):
        slab_ref[6 + s] = s5[:, s * LANES:(s + 1) * LANES]
    for j in range(S5_CHUNK):
        for s in range(2):
            col = j * 256 + s * LANES
            s5_ref[:, col:col + LANES] = slab_ref[6 + s, pl.ds(j, tm // S5_CHUNK, stride=S5_CHUNK), :]

    sb_ref[...] = _dot(ub, wp_ref[:, PROJ_SB[0]:PROJ_SB[1]]).astype(BF16)
    gla_ref[...] = _dot(ub, wp_ref[:, PROJ_GLA[0]:PROJ_GLA[1]]).astype(BF16)
    gg_ref[...] = _dot(ub, wp_ref[:, PROJ_GG[0]:PROJ_GG[1]])


def _ffn_proj(x2, n1, w1, w2, nm, wp, qn, kn, cos_t, sin_t, bsz, seq, tm, layer):
    tokens = x2.shape[0]
    seq_tiles = seq // tm
    row = lambda i: (i, 0)
    tab = lambda i: (i % seq_tiles, 0)
    res = lambda i: (i // seq_tiles, 0, i % seq_tiles, 0)
    tile = lambda w: pl.BlockSpec((tm, w), row)
    s5_w = S5_CHUNK * 256
    return pl.pallas_call(
        _ffn_proj_kernel,
        grid=(tokens // tm,),
        in_specs=[tile(D_MODEL), _const_spec(n1.shape), _const_spec(w1.shape, layer), _const_spec(w2.shape, layer),
                  _const_spec(nm.shape), _const_spec(wp.shape, layer), _const_spec(qn.shape), _const_spec(kn.shape),
                  pl.BlockSpec((tm, LANES), tab), pl.BlockSpec((tm, LANES), tab)],
        out_specs=[tile(D_MODEL), tile(768), pl.BlockSpec((tm // S5_CHUNK, s5_w), row), tile(768), tile(LANES),
                   tile(768), pl.BlockSpec((1, 4, tm // 4, 768), res), pl.BlockSpec((1, 16, tm // 16, 768), res)],
        out_shape=[jax.ShapeDtypeStruct((tokens, D_MODEL), F32),
                   jax.ShapeDtypeStruct((tokens, 768), BF16),
                   jax.ShapeDtypeStruct((tokens // S5_CHUNK, s5_w), F32),
                   jax.ShapeDtypeStruct((tokens, 768), BF16),
                   jax.ShapeDtypeStruct((tokens, LANES), F32),
                   jax.ShapeDtypeStruct((tokens, 768), BF16),
                   jax.ShapeDtypeStruct((bsz, 4, seq // 4, 768), BF16),
                   jax.ShapeDtypeStruct((bsz, 16, seq // 16, 768), BF16)],
        scratch_shapes=[pltpu.VMEM((tm, D_MODEL), F32), pltpu.VMEM((8, tm, LANES), F32)],
        compiler_params=_cparams(("parallel",)),
        name="ffn_proj",
    )(x2, n1, w1, w2, nm, wp, qn, kn, cos_t, sin_t)


def _merge_ffn_kernel(h_ref, ysb_ref, ys5_ref, ygla_ref, ydsa_ref,
                      nm_ref, wg_ref, wglu_ref, wb_ref, wo_ref, n2_ref, w1_ref, w2_ref,
                      out_ref, acc_ref, slab_ref):
    tm = h_ref.shape[0]
    h = h_ref[...]
    ub = _rms_rows(h, nm_ref[...]).astype(BF16)

    for j in range(S5_CHUNK):
        for s in range(2):
            col = j * 256 + s * LANES
            slab_ref[s, pl.ds(j, tm // S5_CHUNK, stride=S5_CHUNK), :] = ys5_ref[:, col:col + LANES].astype(F32)
    y_s5 = jnp.concatenate([slab_ref[0], slab_ref[1]], axis=-1).astype(BF16)
    glu = _dot(y_s5, wglu_ref[...])
    y_s5 = glu[:, :BRANCH_WIDTH] * jax.nn.sigmoid(glu[:, BRANCH_WIDTH:])

    branches = (ysb_ref[...], y_s5.astype(BF16), ygla_ref[...], ydsa_ref[...])
    mixed = jnp.zeros(h.shape, F32)
    for n in range(N_BRANCH):
        gate = jax.nn.sigmoid(_dot(ub, wg_ref[:, n * D_MODEL:(n + 1) * D_MODEL]))
        mixed = mixed + gate * _dot(branches[n], wb_ref[n])
    h2 = h + _dot(mixed.astype(BF16), wo_ref[...])

    xb = _rms_rows(h2, n2_ref[...]).astype(BF16)
    _swiglu_into(acc_ref, xb, w1_ref, w2_ref)
    out_ref[...] = h2 + 0.5 * acc_ref[...]


def _merge_ffn(h, ysb, ys5, ygla, ydsa, nm, wg, wglu, wb, wo, n2, w1, w2, tm, layer):
    tokens = h.shape[0]
    row = lambda i: (i, 0)
    tile = lambda w: pl.BlockSpec((tm, w), row)
    consts = (nm, wg, wglu, wb, wo, n2, w1, w2)
    return pl.pallas_call(
        _merge_ffn_kernel,
        grid=(tokens // tm,),
        in_specs=[tile(D_MODEL), tile(BRANCH_WIDTH), pl.BlockSpec((tm // S5_CHUNK, S5_CHUNK * 256), row),
                  tile(BRANCH_WIDTH), tile(BRANCH_WIDTH)]
                 + [_const_spec(c.shape, None if c.ndim == 2 and c.shape[0] == 1 else layer) for c in consts],
        out_specs=tile(D_MODEL),
        out_shape=jax.ShapeDtypeStruct((tokens, D_MODEL), F32),
        scratch_shapes=[pltpu.VMEM((tm, D_MODEL), F32), pltpu.VMEM((2, tm, LANES), F32)],
        compiler_params=_cparams(("parallel",)),
        name="merge_ffn",
    )(h, ysb, ys5, ygla, ydsa, *consts)


def _later_key_matrix(n):
    j = lax.broadcasted_iota(jnp.int32, (n, n), 0)
    s = lax.broadcasted_iota(jnp.int32, (n, n), 1)
    return jnp.where(j > s, 1.0, 0.0).astype(BF16)


def _sb_scores(qm, k, keep=None):
    z = _dot_nt(qm, k)
    if keep is not None:
        z = jnp.where(keep, z, NEG_BIG)
    lk = -(jnp.maximum(z, 0.0) + jnp.log(1.0 + jnp.exp(-jnp.abs(z))))
    return z, lk


def _sb_kernel(q_ref, k_ref, v_ref, o_ref):
    seq = q_ref.shape[1]
    u_win = _later_key_matrix(SB_WIN)
    u_blk = _later_key_matrix(SB_Q)
    lane_q = lax.broadcasted_iota(jnp.int32, (SB_Q, LANES), 1)
    n_streams = 2 * SB_GROUP
    key_minus_query = (lax.broadcasted_iota(jnp.int32, (2 * SB_Q, SB_WIN), 1)
                       - lax.broadcasted_iota(jnp.int32, (2 * SB_Q, SB_WIN), 0) % SB_Q)

    def q_group(qg, carry):
        starts, qms, zs, lks, vws = [], [], [], [], []
        for sub in range(SB_GROUP):
            t0 = pl.multiple_of((qg * SB_GROUP + sub) * SB_Q, SB_Q)
            start = pl.multiple_of(jnp.maximum(t0 - (SB_WIN - SB_Q), 0), SB_Q)
            q = q_ref[0, pl.ds(t0, SB_Q), :] * (HEAD_DIM ** -0.5)
            kw = k_ref[0, pl.ds(start, SB_WIN), :]
            vws.append(v_ref[0, pl.ds(start, SB_WIN), :])
            strict = key_minus_query < (t0 - start)
            starts.append(start)
            qm2 = jnp.concatenate(
                [jnp.where((lane_q // HEAD_DIM) == head, q, jnp.zeros_like(q)) for head in range(2)], axis=0)
            z, lk = _sb_scores(qm2, kw, strict)
            qms += [qm2[:SB_Q], qm2[SB_Q:]]
            zs.append(z)
            lks.append(lk)
        lk_all = jnp.concatenate(lks, axis=0)
        later_all = _dot(lk_all.astype(BF16), u_win)
        csums, outs = [], []
        for sub in range(SB_GROUP):
            rows = slice(sub * 2 * SB_Q, (sub + 1) * 2 * SB_Q)
            later = later_all[rows]
            w = jnp.exp(zs[sub] + lks[sub] + later)
            o2 = _dot(w.astype(BF16), vws[sub])
            c2 = later[:, 0:1] + lks[sub][:, 0:1]
            outs += [o2[:SB_Q], o2[SB_Q:]]
            csums += [c2[:SB_Q], c2[SB_Q:]]

        def cond(state):
            step, cmaxes, _, _ = state
            flags = [jnp.logical_and(starts[s // 2] - (step + 1) * SB_Q >= 0, cmaxes[s] > SB_LOG_ZERO)
                     for s in range(n_streams)]
            return functools.reduce(jnp.logical_or, flags)

        def body(state):
            step, _, csums, outs = state
            new_c, new_o = [], []
            for s in range(n_streams):
                nxt = starts[s // 2] - (step + 1) * SB_Q
                in_range = nxt >= 0
                p0 = pl.multiple_of(jnp.maximum(nxt, 0), SB_Q)
                kb = k_ref[0, pl.ds(p0, SB_Q), :]
                vb = v_ref[0, pl.ds(p0, SB_Q), :]
                z, lk = _sb_scores(qms[s], kb)
                later = _dot(lk.astype(BF16), u_blk)
                w = jnp.where(in_range, jnp.exp(z + lk + later + csums[s]), 0.0)
                new_o.append(outs[s] + _dot(w.astype(BF16), vb))
                new_c.append(csums[s] + jnp.where(in_range, later[:, 0:1] + lk[:, 0:1], 0.0))
            return step + 1, tuple(jnp.max(c) for c in new_c), tuple(new_c), tuple(new_o)

        init = (jnp.int32(0), tuple(jnp.max(c) for c in csums), tuple(csums), tuple(outs))
        _, _, _, outs = lax.while_loop(cond, body, init)
        for sub in range(SB_GROUP):
            t0 = pl.multiple_of((qg * SB_GROUP + sub) * SB_Q, SB_Q)
            o_ref[0, pl.ds(t0, SB_Q), :] = jnp.where(
                lane_q < HEAD_DIM, outs[2 * sub], outs[2 * sub + 1]).astype(o_ref.dtype)
        return carry

    lax.fori_loop(0, seq // (SB_Q * SB_GROUP), q_group, 0)


def _stick_breaking(sb3):
    bsz, seq, _ = sb3.shape
    spec = lambda off: pl.BlockSpec((1, seq, LANES), lambda b, hp: (b, 0, off + hp))
    return pl.pallas_call(
        _sb_kernel,
        grid=(bsz, 2),
        in_specs=[spec(0), spec(2), spec(4)],
        out_specs=pl.BlockSpec((1, seq, LANES), lambda b, hp: (b, 0, hp)),
        out_shape=jax.ShapeDtypeStruct((bsz, seq, BRANCH_WIDTH), BF16),
        compiler_params=_cparams(("parallel", "parallel")),
        name="stick_breaking",
    )(sb3, sb3, sb3)


def _s5_kernel(u_ref, m_ref, e_ref, g_ref, pw_ref, d_ref, y_ref, state_ref):
    half = S5_GROUPS * S5_STATE

    @pl.when(pl.program_id(1) == 0)
    def _():
        state_ref[...] = jnp.zeros_like(state_ref)

    u = u_ref[0]
    ub = u.astype(BF16)
    rows = u.shape[0]

    def cmul(ar, ai, xr, xi):
        return ar * xr - ai * xi, ar * xi + ai * xr

    w = _dot(ub, e_ref[...])
    wr, wi = w[:, :half], w[:, half:]
    row = lax.broadcasted_iota(jnp.int32, (rows, half), 0)
    a1r, a1i = pw_ref[0:1, :half], pw_ref[0:1, half:]
    sr, si = state_ref[0:1, :half], state_ref[0:1, half:]
    cr, ci = cmul(a1r, a1i, sr, si)
    wr = jnp.where(row == 0, wr + cr, wr)
    wi = jnp.where(row == 0, wi + ci, wi)
    step = 0
    d = 1
    while d < rows:
        ar, ai = pw_ref[step:step + 1, :half], pw_ref[step:step + 1, half:]
        pr = jnp.where(row >= d, pltpu.roll(wr, d, axis=0), 0.0)
        pi = jnp.where(row >= d, pltpu.roll(wi, d, axis=0), 0.0)
        mr, mi = cmul(ar, ai, pr, pi)
        wr, wi = wr + mr, wi + mi
        d *= 2
        step += 1
    xr = jnp.where(row == 0, sr, pltpu.roll(wr, 1, axis=0))
    xi = jnp.where(row == 0, si, pltpu.roll(wi, 1, axis=0))
    state_ref[0:1, :half] = wr[rows - 1:rows, :]
    state_ref[0:1, half:] = wi[rows - 1:rows, :]

    n_c = S5_GROUPS * S5_GROUP_CH
    slots = [ub[:, j * n_c:(j + 1) * n_c] for j in range(S5_CHUNK)]
    cols = []
    for i in range(S5_CHUNK):
        acc = _dot(slots[0], m_ref[i])
        for j in range(1, i + 1):
            acc = acc + _dot(slots[j], m_ref[i - j])
        cols.append(acc)
    y = jnp.concatenate(cols, axis=1)
    y = y + _dot(xr.astype(BF16), g_ref[:half, :]) + _dot(xi.astype(BF16), g_ref[half:, :])
    y = y + d_ref[...] * u
    y_ref[0] = jax.nn.gelu(y).astype(y_ref.dtype)


def _s5(u3, m, e, g, pw, dskip, layer):
    bsz, rows, width = u3.shape
    tr = min(S5_ROW_TILE, rows)
    blk = pl.BlockSpec((1, tr, width), lambda b, i: (b, i, 0))
    return pl.pallas_call(
        _s5_kernel,
        grid=(bsz, rows // tr),
        in_specs=[blk] + [_const_spec(a.shape, layer) for a in (m, e, g, pw, dskip)],
        out_specs=blk,
        out_shape=jax.ShapeDtypeStruct(u3.shape, BF16),
        scratch_shapes=[pltpu.VMEM((8, width), F32)],
        compiler_params=_cparams(("parallel", "arbitrary")),
        name="s5",
    )(u3, m, e, g, pw, dskip)


def _s5_operators(lam_re, lam_im, log_step, b_re, b_im, c_re, c_im, d_skip, row_tile):
    n_g, n_p, n_h, L = S5_GROUPS, S5_STATE, S5_GROUP_CH, S5_CHUNK
    n_c, n_s = n_g * n_h, n_g * n_p
    dt = jnp.exp(log_step)[:, None]

    def lam_bar_pow(k):
        mag = jnp.exp(k[:, None, None] * (lam_re * dt)[None])
        ang = k[:, None, None] * (lam_im * dt)[None]
        return mag * jnp.cos(ang), mag * jnp.sin(ang)

    pr, pi = lam_bar_pow(jnp.arange(L + 1, dtype=F32))
    den = lam_re * lam_re + lam_im * lam_im
    fr = ((pr[1] - 1.0) * lam_re + pi[1] * lam_im) / den
    fi = (pi[1] * lam_re - (pr[1] - 1.0) * lam_im) / den
    bbr = fr[..., None] * b_re - fi[..., None] * b_im
    bbi = fr[..., None] * b_im + fi[..., None] * b_re
    tr = pr[:L, :, :, None] * bbr[None] - pi[:L, :, :, None] * bbi[None]
    ti = pr[:L, :, :, None] * bbi[None] + pi[:L, :, :, None] * bbr[None]

    def expand(small, rows_per_group, cols_per_group):
        rows = small.shape[-2]
        tiled = jnp.tile(small, (1,) * (small.ndim - 1) + (n_g,))
        r = lax.broadcasted_iota(jnp.int32, (rows, n_g * cols_per_group), 0) // rows_per_group
        c = lax.broadcasted_iota(jnp.int32, (rows, n_g * cols_per_group), 1) // cols_per_group
        return jnp.where(r == c, tiled, 0.0)

    hp = lax.Precision.HIGHEST
    kern = (jnp.einsum('ghp,kgpi->kghi', c_re, tr, precision=hp)
            - jnp.einsum('ghp,kgpi->kghi', c_im, ti, precision=hp))
    m_op = expand(kern.transpose(0, 1, 3, 2).reshape(L, n_c, n_h), n_h, n_h).astype(BF16)

    e_r = expand(tr[::-1].transpose(0, 1, 3, 2).reshape(L, n_c, n_p), n_h, n_p)
    e_i = expand(ti[::-1].transpose(0, 1, 3, 2).reshape(L, n_c, n_p), n_h, n_p)
    e_op = jnp.concatenate([e_r, e_i], axis=2).astype(BF16).reshape(L * n_c, 2 * n_s)

    clr = c_re[None] * pr[1:, :, None, :] - c_im[None] * pi[1:, :, None, :]
    cli = c_re[None] * pi[1:, :, None, :] + c_im[None] * pr[1:, :, None, :]
    g_r = expand(clr.transpose(0, 1, 3, 2).reshape(L, n_s, n_h), n_p, n_h)
    g_i = expand(cli.transpose(0, 1, 3, 2).reshape(L, n_s, n_h), n_p, n_h)
    g_blocks = jnp.concatenate([g_r, -g_i], axis=1).astype(BF16)
    g_op = jnp.concatenate([g_blocks[i] for i in range(L)], axis=1)

    n_steps = max(1, int(math.log2(row_tile)))
    sr, si = lam_bar_pow(L * 2.0 ** jnp.arange(n_steps, dtype=F32))
    pw = jnp.concatenate([sr.reshape(n_steps, n_s), si.reshape(n_steps, n_s)], axis=1)
    pw = jnp.pad(pw, ((0, (-n_steps) % 8), (0, 0)))
    dskip = jnp.tile(d_skip.reshape(1, n_c), (1, L))
    return m_op, e_op, g_op, pw.astype(F32), dskip.astype(F32)


def _gla_constants():
    n = GLA_SUB
    ri = lax.broadcasted_iota(jnp.int32, (n, n), 0)
    ci = lax.broadcasted_iota(jnp.int32, (n, n), 1)
    same = (ri // GLA_BLOCK) == (ci // GLA_BLOCK)
    mats = [same & (ci <= ri), same & (ci > ri)]
    masks = [ri == ci]
    ups, lows = [], []
    for lev in range(GLA_LEVELS):
        half = 1 << lev
        blk = 2 * half
        mid = (ri // blk) * blk + (half - 1)
        later = (ri % blk) >= half
        ups.append(later & (ci > mid) & (ci <= ri))
        lows.append(jnp.logical_not(later) & (ci > ri) & (ci <= mid))
        masks.append(((ri // blk) == (ci // blk)) & later & ((ci % blk) < half))
    stacked = jnp.concatenate([jnp.where(m, 1.0, 0.0).astype(BF16) for m in mats + ups + lows], axis=0)
    return stacked, masks


def _gla_tile(x, gg, state, wgate, bgate, norm, stacked, masks):
    tile = x.shape[0]
    hk = GLA_HEADS * GLA_DK
    hv = GLA_HEADS * GLA_DV
    q = x[:, 0:hk].astype(F32) * (GLA_DK ** -0.5)
    k = x[:, hk:2 * hk].astype(F32)
    vb = x[:, 2 * hk:2 * hk + hv]
    r = x[:, 2 * hk + hv:].astype(F32)

    z = _dot(gg.astype(BF16), wgate) + bgate
    g = -(jnp.maximum(-z, 0.0) + jnp.log(1.0 + jnp.exp(-jnp.abs(z)))) * (1.0 / GLA_GATE_NORM)
    g_hi, g_lo = _hi_lo(g)
    g2 = jnp.concatenate([g_hi, g_lo], axis=1)

    lane_k = lax.broadcasted_iota(jnp.int32, (GLA_SUB, hk), 1) // GLA_DK
    row_blk = lax.broadcasted_iota(jnp.int32, (GLA_SUB, hk), 0) // GLA_BLOCK
    lane_v = lax.broadcasted_iota(jnp.int32, (GLA_SUB, hv), 1) // GLA_DV
    rs = lax.broadcasted_iota(jnp.int32, (hv, hk), 0) // GLA_DV
    cs = lax.broadcasted_iota(jnp.int32, (hv, hk), 1) // GLA_DK
    head_diag = rs == cs
    outs = []
    for st in range(tile // GLA_SUB):
        rows = slice(st * GLA_SUB, (st + 1) * GLA_SUB)
        sums = _dot(stacked, g2[rows])
        sums = sums[:, :hk] + sums[:, hk:]
        part = lambda i: sums[i * GLA_SUB:(i + 1) * GLA_SUB]
        b, tail = part(0), part(1)
        qs, ks, vs = q[rows], k[rows], vb[rows]

        att = [jnp.zeros((GLA_SUB, GLA_SUB), F32) for _ in range(GLA_HEADS)]
        for lev in range(-1, GLA_LEVELS):
            if lev < 0:
                qt, kt = qs.astype(BF16), ks.astype(BF16)
            else:
                qt = (qs * jnp.exp(part(2 + lev))).astype(BF16)
                kt = (ks * jnp.exp(part(2 + GLA_LEVELS + lev))).astype(BF16)
            q_heads = jnp.concatenate(
                [jnp.where(lane_k == h, qt, jnp.zeros_like(qt)) for h in range(GLA_HEADS)], axis=0)
            a = _dot_nt(q_heads, kt)
            for h in range(GLA_HEADS):
                att[h] = jnp.where(masks[lev + 1], a[h * GLA_SUB:(h + 1) * GLA_SUB], att[h])
        att_all = jnp.concatenate([a.astype(BF16) for a in att], axis=1)
        v_heads = jnp.concatenate([jnp.where(lane_v == h, vs, jnp.zeros_like(vs)) for h in range(GLA_HEADS)], axis=0)
        o = _dot(att_all, v_heads)

        q_dec = (qs * jnp.exp(b)).astype(BF16)
        k_dec = (ks * jnp.exp(tail)).astype(BF16)
        b_end = b + tail
        n_blk = GLA_SUB // GLA_BLOCK
        k_blocks = jnp.concatenate(
            [jnp.where(row_blk == n, k_dec, jnp.zeros_like(k_dec)) for n in range(n_blk)], axis=1)
        upd_all = _dot_tn(vs, k_blocks)
        cross = []
        for blk in range(n_blk):
            r0 = blk * GLA_BLOCK
            cross.append(_dot_nt(q_dec[r0:r0 + GLA_BLOCK], state.astype(BF16)))
            upd = upd_all[:, blk * hk:(blk + 1) * hk]
            state = jnp.exp(b_end[r0:r0 + 1, :]) * state + jnp.where(head_diag, upd, 0.0)
        outs.append(o + jnp.concatenate(cross, axis=0))
    o = jnp.concatenate(outs, axis=0)

    o = _group_rms(o, norm)
    return o * (r * jax.nn.sigmoid(r)), state


def _gla_kernel(x_ref, gg_ref, wgate_ref, bgate_ref, norm_ref, y_ref, state_ref):
    @pl.when(pl.program_id(0) == 0)
    def _():
        state_ref[...] = jnp.zeros_like(state_ref)

    stacked, masks = _gla_constants()
    for bi in range(x_ref.shape[0]):
        y, state = _gla_tile(x_ref[bi], gg_ref[bi], state_ref[bi], wgate_ref[...], bgate_ref[...],
                             norm_ref[...], stacked, masks)
        y_ref[bi] = y.astype(y_ref.dtype)
        state_ref[bi] = state


def _gla(x3, gg3, wgate, bgate, norm):
    bsz, seq, _ = x3.shape
    tile = min(GLA_TILE, seq)
    return pl.pallas_call(
        _gla_kernel,
        grid=(seq // tile,),
        in_specs=[pl.BlockSpec((bsz, tile, 768), lambda i: (0, i, 0)),
                  pl.BlockSpec((bsz, tile, LANES), lambda i: (0, i, 0)),
                  _const_spec(wgate.shape), _const_spec(bgate.shape), _const_spec(norm.shape)],
        out_specs=pl.BlockSpec((bsz, tile, BRANCH_WIDTH), lambda i: (0, i, 0)),
        out_shape=jax.ShapeDtypeStruct((bsz, seq, BRANCH_WIDTH), BF16),
        scratch_shapes=[pltpu.VMEM((bsz, GLA_HEADS * GLA_DV, GLA_HEADS * GLA_DK), F32)],
        compiler_params=_cparams(("arbitrary",)),
        name="gla",
    )(x3, gg3, wgate, bgate, norm)


def _dsa_band_bias(offset, n_steps):
    row = lax.broadcasted_iota(jnp.int32, (2 * DSA_BLOCK, 2 * DSA_BLOCK), 0) % DSA_BLOCK
    col = lax.broadcasted_iota(jnp.int32, (2 * DSA_BLOCK, 2 * DSA_BLOCK), 1)
    dist = offset + row - col
    return jnp.where((dist >= 0) & (dist <= n_steps), 0.0, NEG_BIG)


def _dsa_attend(q, kw, vw, bias):
    lane_q = lax.broadcasted_iota(jnp.int32, (DSA_BLOCK, LANES), 1)
    q = q * (HEAD_DIM ** -0.5)
    qm2 = jnp.concatenate(
        [jnp.where((lane_q // HEAD_DIM) == head, q, jnp.zeros_like(q)) for head in range(2)], axis=0)
    s = _dot_nt(qm2, kw) + bias
    m = jnp.max(s, axis=1, keepdims=True)
    p = jnp.exp(s - m)
    l = jnp.sum(p, axis=1, keepdims=True)
    o2 = _dot(p.astype(BF16), vw) / l
    lse2 = jnp.broadcast_to(m + jnp.log(l), (2 * DSA_BLOCK, LANES))
    first = lane_q < HEAD_DIM
    return (jnp.where(first, o2[:DSA_BLOCK], o2[DSA_BLOCK:]),
            jnp.where(first, lse2[:DSA_BLOCK], lse2[DSA_BLOCK:]))


def _dsa_kernel(q1_ref, k1_ref, v1_ref, q4_ref, k4_ref, v4_ref, q16_ref, k16_ref, v16_ref,
                y_ref, o_scr, l_scr):
    seq = q1_ref.shape[1]
    refs = ((q1_ref, k1_ref, v1_ref), (q4_ref, k4_ref, v4_ref), (q16_ref, k16_ref, v16_ref))
    blocks_per_span = DSA_SPAN // DSA_BLOCK
    n_steps = DSA_PATTERNS[0][0] // DSA_PATTERNS[0][1]
    assert all(w // d == n_steps for w, d in DSA_PATTERNS)
    bias_inner = _dsa_band_bias(DSA_BLOCK, n_steps)
    bias_first = _dsa_band_bias(0, n_steps)

    def span(sp, carry):
        for p, (_, dil) in enumerate(DSA_PATTERNS):
            q_ref, k_ref, v_ref = refs[p]
            per_res = blocks_per_span // dil

            def unit(idx, c, p=p, dil=dil, per_res=per_res, q_ref=q_ref, k_ref=k_ref, v_ref=v_ref):
                r = idx // per_res
                jb = idx % per_res
                m0 = pl.multiple_of(sp * (DSA_SPAN // dil) + jb * DSA_BLOCK, DSA_BLOCK)
                start = pl.multiple_of(jnp.maximum(m0 - DSA_BLOCK, 0), DSA_BLOCK)
                if dil == 1:
                    q = q_ref[0, pl.ds(m0, DSA_BLOCK), :]
                    kw = k_ref[0, pl.ds(start, 2 * DSA_BLOCK), :]
                    vw = v_ref[0, pl.ds(start, 2 * DSA_BLOCK), :]
                else:
                    q = q_ref[0, r, pl.ds(m0, DSA_BLOCK), :]
                    kw = k_ref[0, r, pl.ds(start, 2 * DSA_BLOCK), :]
                    vw = v_ref[0, r, pl.ds(start, 2 * DSA_BLOCK), :]
                o, lse = _dsa_attend(q, kw, vw, jnp.where(m0 == 0, bias_first, bias_inner))
                rows = pl.ds(r + dil * jb * DSA_BLOCK, DSA_BLOCK, stride=dil) if dil > 1 else \
                    pl.ds(pl.multiple_of(jb * DSA_BLOCK, DSA_BLOCK), DSA_BLOCK)
                o_scr[p, rows, :] = o
                l_scr[p, rows, :] = lse
                return c

            lax.fori_loop(0, blocks_per_span, unit, 0, unroll=DSA_UNROLL)

        l1, l2, l3 = l_scr[0], l_scr[1], l_scr[2]
        lm = jnp.maximum(jnp.maximum(l1, l2), l3)
        e1, e2, e3 = jnp.exp(l1 - lm), jnp.exp(l2 - lm), jnp.exp(l3 - lm)
        y = (e1 * o_scr[0] + e2 * o_scr[1] + e3 * o_scr[2]) / (e1 + e2 + e3)
        y_ref[0, pl.ds(pl.multiple_of(sp * DSA_SPAN, DSA_SPAN), DSA_SPAN), :] = y.astype(y_ref.dtype)
        return carry

    lax.fori_loop(0, seq // DSA_SPAN, span, 0)


def _dsa(dsa1, dsa4, dsa16):
    bsz, seq, _ = dsa1.shape
    nat = lambda off: pl.BlockSpec((1, seq, LANES), lambda b, hp: (b, 0, off + hp))
    res = lambda dil, off: pl.BlockSpec((1, dil, seq // dil, LANES), lambda b, hp: (b, 0, 0, off + hp))
    return pl.pallas_call(
        _dsa_kernel,
        grid=(bsz, 2),
        in_specs=[nat(0), nat(2), nat(4), res(4, 0), res(4, 2), res(4, 4), res(16, 0), res(16, 2), res(16, 4)],
        out_specs=pl.BlockSpec((1, seq, LANES), lambda b, hp: (b, 0, hp)),
        out_shape=jax.ShapeDtypeStruct((bsz, seq, BRANCH_WIDTH), BF16),
        scratch_shapes=[pltpu.VMEM((3, DSA_SPAN, LANES), F32), pltpu.VMEM((3, DSA_SPAN, LANES), F32)],
        compiler_params=_cparams(("parallel", "parallel")),
        name="dsa",
    )(dsa1, dsa1, dsa1, dsa4, dsa4, dsa4, dsa16, dsa16, dsa16)


def _proj_weights(w_in):
    lead = w_in[..., :1792].astype(BF16)
    gg = jnp.pad(w_in[..., 1792:1808].astype(BF16), ((0, 0), (0, 0), (0, LANES - GLA_GATE_RANK)))
    dsa = w_in[..., 1808:2576].astype(BF16)
    wp = jnp.concatenate([lead, dsa, gg], axis=-1)
    return wp, w_in[..., 2576:].astype(BF16)


def _rope_tables(seq):
    inv = ROPE_THETA ** (-jnp.arange(0, HEAD_DIM, 2, dtype=F32) / HEAD_DIM)
    ang = jnp.arange(seq, dtype=F32)[:, None] * inv[None, :]
    cos, sin = jnp.cos(ang), jnp.sin(ang)
    cos_t = jnp.tile(jnp.concatenate([cos, cos], axis=1), (1, 2))
    sin_t = jnp.tile(jnp.concatenate([-sin, sin], axis=1), (1, 2))
    return cos_t, sin_t


def _pick_tile(n, pref):
    t = min(pref, n)
    while n % t:
        t //= 2
    return t


def kernel(x, ffn1_norm, ffn1_w_in, ffn1_w_out, mix_norm, w_in, s5_lam_re, s5_lam_im, s5_log_step, s5_b_re, s5_b_im, s5_c_re, s5_c_im, s5_d, s5_w_glu, gla_w_gate, gla_b_gate, gla_norm, dsa_q_norm, dsa_k_norm, w_branch, w_out, ffn2_norm, ffn2_w_in, ffn2_w_out):
    bsz, seq, _ = x.shape
    tokens = bsz * seq
    depth = ffn1_norm.shape[0]
    assert seq % DSA_SPAN == 0 and seq // DSA_PATTERNS[-1][1] >= 2 * DSA_BLOCK and seq >= SB_WIN
    tm_a = _pick_tile(seq, 512)
    tm_b = _pick_tile(tokens, 512)
    s5_rows = seq // S5_CHUNK
    s5_tile = min(S5_ROW_TILE, s5_rows)
    cos_t, sin_t = _rope_tables(seq)

    wp_all, wg_all = _proj_weights(w_in)
    w1a_all, w2a_all = ffn1_w_in.astype(BF16), ffn1_w_out.astype(BF16)
    w1b_all, w2b_all = ffn2_w_in.astype(BF16), ffn2_w_out.astype(BF16)
    wglu_all, wb_all, wo_all = s5_w_glu.astype(BF16), w_branch.astype(BF16), w_out.astype(BF16)
    s5_ops = jax.vmap(functools.partial(_s5_operators, row_tile=s5_tile))(
        s5_lam_re, s5_lam_im, s5_log_step, s5_b_re, s5_b_im, s5_c_re, s5_c_im, s5_d)

    h = x.reshape(tokens, D_MODEL)
    for l in range(depth):
        row = lambda p: p[l].reshape(1, -1)
        qn = jnp.tile(row(dsa_q_norm), (1, 4))
        kn = jnp.tile(row(dsa_k_norm), (1, 4))
        h, sb, s5u, gla_in, gg, dsa1, dsa4, dsa16 = _ffn_proj(
            h, row(ffn1_norm), w1a_all, w2a_all, row(mix_norm), wp_all, qn, kn, cos_t, sin_t, bsz, seq, tm_a, l)

        y_sb = _stick_breaking(sb.reshape(bsz, seq, 768)).reshape(tokens, BRANCH_WIDTH)

        y_s5 = _s5(s5u.reshape(bsz, s5_rows, S5_CHUNK * 256), *s5_ops, l).reshape(tokens // S5_CHUNK, S5_CHUNK * 256)

        wgate = jnp.pad(gla_w_gate[l], ((0, LANES - GLA_GATE_RANK), (0, 0))).astype(BF16)
        y_gla = _gla(gla_in.reshape(bsz, seq, 768), gg.reshape(bsz, seq, LANES), wgate,
                     row(gla_b_gate), row(gla_norm)).reshape(tokens, BRANCH_WIDTH)

        y_dsa = _dsa(dsa1.reshape(bsz, seq, 768), dsa4, dsa16).reshape(tokens, BRANCH_WIDTH)

        h = _merge_ffn(h, y_sb, y_s5, y_gla, y_dsa, row(mix_norm), wg_all, wglu_all, wb_all, wo_all,
                       row(ffn2_norm), w1b_all, w2b_all, tm_b, l)
    return h.reshape(bsz, seq, D_MODEL)
```

```python
import functools
import math

import jax
import jax.numpy as jnp
from jax import lax
from jax.experimental import pallas as pl
from jax.experimental.pallas import tpu as pltpu

D_MODEL = 1024
HEAD_DIM = 64
BRANCH_WIDTH = 256
N_BRANCH = 4
SB_HEADS = 4
S5_GROUPS = 16
S5_GROUP_CH = 16
S5_STATE = 64
GLA_HEADS = 4
GLA_DK = 32
GLA_DV = 64
GLA_GATE_RANK = 16
GLA_GATE_NORM = 16.0
DSA_PATTERNS = ((128, 1), (512, 4), (2048, 16))
DSA_BLOCK = 128
ROPE_THETA = 10000.0
D_FF = 2816
RMS_EPS = 1e-6

LANES = 128
VMEM_LIMIT_BYTES = 58 * 1024 * 1024
FF_CHUNK = 256
N_FF_CHUNKS = D_FF // FF_CHUNK
S5_CHUNK = 8
S5_ROW_TILE = 256
GLA_TILE = 256
GLA_BLOCK = 64
GLA_LEVELS = 6
GLA_SUB = 128
SB_Q = 64
SB_WIN = 4 * SB_Q
SB_GROUP = 4
SB_LOG_ZERO = -104.0
NEG_BIG = -1e30
DSA_UNROLL = 8
DSA_SPAN = DSA_PATTERNS[-1][1] * DSA_BLOCK
DSA_MAX_SHIFT = 30.0

PROJ_SB = (0, 768)
PROJ_S5 = (768, 1024)
PROJ_GLA = (1024, 1792)
PROJ_DSA = (1792, 2560)
PROJ_GG = (2560, 2688)
PROJ_WIDTH = 2688

F32 = jnp.float32
BF16 = jnp.bfloat16


def _cparams(sem):
    return pltpu.CompilerParams(dimension_semantics=sem, vmem_limit_bytes=VMEM_LIMIT_BYTES)


def _const_spec(shape, layer=None):
    if layer is None:
        zeros = (0,) * len(shape)
        return pl.BlockSpec(shape, lambda *_: zeros, pipeline_mode=pl.Buffered(1))
    index = (layer,) + (0,) * (len(shape) - 1)
    return pl.BlockSpec((None,) + tuple(shape[1:]), lambda *_: index, pipeline_mode=pl.Buffered(1))


def _rms_rows(x, g):
    ms = jnp.mean(x * x, axis=-1, keepdims=True)
    return x * lax.rsqrt(ms + RMS_EPS) * g


def _hi_lo(x):
    hi = x.astype(BF16)
    lo = (x - hi.astype(F32)).astype(BF16)
    return hi, lo


def _dot(a, b):
    return jnp.dot(a, b, preferred_element_type=F32)


def _dot_nt(a, b):
    return lax.dot_general(a, b, (((1,), (1,)), ((), ())), preferred_element_type=F32)


def _dot_tn(a, b):
    return lax.dot_general(a, b, (((0,), (0,)), ((), ())), preferred_element_type=F32)


def _swiglu_into(acc_ref, xb, w1_ref, w2_ref):
    for c in range(N_FF_CHUNKS):
        lo, hi = c * FF_CHUNK, (c + 1) * FF_CHUNK
        a = _dot(xb, w1_ref[:, lo:hi])
        b = _dot(xb, w1_ref[:, D_FF + lo:D_FF + hi])
        hm = (a * jax.nn.sigmoid(a) * b).astype(BF16)
        part = _dot(hm, w2_ref[lo:hi, :])
        if c == 0:
            acc_ref[...] = part
        else:
            acc_ref[...] += part


def _group_mean_matrix(width, group):
    r = lax.broadcasted_iota(jnp.int32, (width, width), 0) // group
    c = lax.broadcasted_iota(jnp.int32, (width, width), 1) // group
    return jnp.where(r == c, 1.0 / group, 0.0).astype(BF16)


def _group_rms(x, gain):
    gm = _group_mean_matrix(x.shape[-1], HEAD_DIM)
    ms = _dot((x * x).astype(BF16), gm)
    return x * lax.rsqrt(ms + RMS_EPS) * gain


def _swap_half_heads(x):
    half = HEAD_DIM // 2
    outs = []
    for s in range(x.shape[-1] // LANES):
        xs = x[:, s * LANES:(s + 1) * LANES]
        lane = lax.broadcasted_iota(jnp.int32, xs.shape, 1)
        up = pltpu.roll(xs, LANES - half, axis=1)
        down = pltpu.roll(xs, half, axis=1)
        outs.append(jnp.where((lane % HEAD_DIM) < half, up, down))
    return jnp.concatenate(outs, axis=-1)


def _ffn_proj_kernel(x_ref, n1_ref, w1_ref, w2_ref, nm_ref, wp_ref, qn_ref, kn_ref, cos_ref, sin_ref,
                     h_ref, sb_ref, s5_ref, gla_ref, gg_ref, dsa1_ref, dsa4_ref, dsa16_ref,
                     acc_ref, slab_ref):
    tm = x_ref.shape[0]
    x = x_ref[...]
    xb = _rms_rows(x, n1_ref[...]).astype(BF16)
    _swiglu_into(acc_ref, xb, w1_ref, w2_ref)
    h = x + 0.5 * acc_ref[...]
    h_ref[...] = h
    ub = _rms_rows(h, nm_ref[...]).astype(BF16)

    dsa = _dot(ub, wp_ref[:, PROJ_DSA[0]:PROJ_DSA[1]])
    cos = jnp.concatenate([cos_ref[...]] * 2, axis=-1)
    sin = jnp.concatenate([sin_ref[...]] * 2, axis=-1)
    parts = []
    for i, gain_ref in enumerate((qn_ref, kn_ref)):
        t = _group_rms(dsa[:, i * 256:(i + 1) * 256], gain_ref[...])
        parts.append(t * cos + _swap_half_heads(t) * sin)
    parts.append(dsa[:, 512:768])
    dsa = jnp.concatenate(parts, axis=-1)
    dsa1_ref[...] = dsa.astype(BF16)
    for s in range(6):
        slab_ref[s] = dsa[:, s * LANES:(s + 1) * LANES]
    for dil, ref in ((4, dsa4_ref), (16, dsa16_ref)):
        for r in range(dil):
            for s in range(6):
                rows = slab_ref[s, pl.ds(r, tm // dil, stride=dil), :]
                ref[0, r, :, s * LANES:(s + 1) * LANES] = rows.astype(BF16)

    s5 = _dot(ub, wp_ref[:, PROJ_S5[0]:PROJ_S5[1]])
    for s in range(2):
        slab_ref[6 + s] = s5[:, s * LANES:(s + 1) * LANES]
    for j in range(S5_CHUNK):
        for s in range(2):
            col = j * 256 + s * LANES
            s5_ref[:, col:col + LANES] = slab_ref[6 + s, pl.ds(j, tm // S5_CHUNK, stride=S5_CHUNK), :]

    sb_ref[...] = _dot(ub, wp_ref[:, PROJ_SB[0]:PROJ_SB[1]]).astype(BF16)
    gla_ref[...] = _dot(ub, wp_ref[:, PROJ_GLA[0]:PROJ_GLA[1]]).astype(BF16)
    gg_ref[...] = _dot(ub, wp_ref[:, PROJ_GG[0]:PROJ_GG[1]])


def _ffn_proj(x2, n1, w1, w2, nm, wp, qn, kn, cos_t, sin_t, bsz, seq, tm, layer):
    tokens = x2.shape[0]
    seq_tiles = seq // tm
    row = lambda i: (i, 0)
    tab = lambda i: (i % seq_tiles, 0)
    res = lambda i: (i // seq_tiles, 0, i % seq_tiles, 0)
    tile = lambda w: pl.BlockSpec((tm, w), row)
    s5_w = S5_CHUNK * 256
    return pl.pallas_call(
        _ffn_proj_kernel,
        grid=(tokens // tm,),
        in_specs=[tile(D_MODEL), _const_spec(n1.shape), _const_spec(w1.shape, layer), _const_spec(w2.shape, layer),
                  _const_spec(nm.shape), _const_spec(wp.shape, layer), _const_spec(qn.shape), _const_spec(kn.shape),
                  pl.BlockSpec((tm, LANES), tab), pl.BlockSpec((tm, LANES), tab)],
        out_specs=[tile(D_MODEL), tile(768), pl.BlockSpec((tm // S5_CHUNK, s5_w), row), tile(768), tile(LANES),
                   tile(768), pl.BlockSpec((1, 4, tm // 4, 768), res), pl.BlockSpec((1, 16, tm // 16, 768), res)],
        out_shape=[jax.ShapeDtypeStruct((tokens, D_MODEL), F32),
                   jax.ShapeDtypeStruct((tokens, 768), BF16),
                   jax.ShapeDtypeStruct((tokens // S5_CHUNK, s5_w), F32),
                   jax.ShapeDtypeStruct((tokens, 768), BF16),
                   jax.ShapeDtypeStruct((tokens, LANES), F32),
                   jax.ShapeDtypeStruct((tokens, 768), BF16),
                   jax.ShapeDtypeStruct((bsz, 4, seq // 4, 768), BF16),
                   jax.ShapeDtypeStruct((bsz, 16, seq // 16, 768), BF16)],
        scratch_shapes=[pltpu.VMEM((tm, D_MODEL), F32), pltpu.VMEM((8, tm, LANES), F32)],
        compiler_params=_cparams(("parallel",)),
        name="ffn_proj",
    )(x2, n1, w1, w2, nm, wp, qn, kn, cos_t, sin_t)


def _merge_ffn_kernel(h_ref, ysb_ref, ys5_ref, ygla_ref, ydsa_ref,
                      nm_ref, wg_ref, wglu_ref, wb_ref, wo_ref, n2_ref, w1_ref, w2_ref,
                      out_ref, acc_ref, slab_ref):
    tm = h_ref.shape[0]
    h = h_ref[...]
    ub = _rms_rows(h, nm_ref[...]).astype(BF16)

    for j in range(S5_CHUNK):
        for s in range(2):
            col = j * 256 + s * LANES
            slab_ref[s, pl.ds(j, tm // S5_CHUNK, stride=S5_CHUNK), :] = ys5_ref[:, col:col + LANES].astype(F32)
    y_s5 = jnp.concatenate([slab_ref[0], slab_ref[1]], axis=-1).astype(BF16)
    glu = _dot(y_s5, wglu_ref[...])
    y_s5 = glu[:, :BRANCH_WIDTH] * jax.nn.sigmoid(glu[:, BRANCH_WIDTH:])

    branches = (ysb_ref[...], y_s5.astype(BF16), ygla_ref[...], ydsa_ref[...])
    mixed = jnp.zeros(h.shape, F32)
    for n in range(N_BRANCH):
        gate = jax.nn.sigmoid(_dot(ub, wg_ref[:, n * D_MODEL:(n + 1) * D_MODEL]))
        mixed = mixed + gate * _dot(branches[n], wb_ref[n])
    h2 = h + _dot(mixed.astype(BF16), wo_ref[...])

    xb = _rms_rows(h2, n2_ref[...]).astype(BF16)
    _swiglu_into(acc_ref, xb, w1_ref, w2_ref)
    out_ref[...] = h2 + 0.5 * acc_ref[...]


def _merge_ffn(h, ysb, ys5, ygla, ydsa, nm, wg, wglu, wb, wo, n2, w1, w2, tm, layer):
    tokens = h.shape[0]
    row = lambda i: (i, 0)
    tile = lambda w: pl.BlockSpec((tm, w), row)
    consts = (nm, wg, wglu, wb, wo, n2, w1, w2)
    return pl.pallas_call(
        _merge_ffn_kernel,
        grid=(tokens // tm,),
        in_specs=[tile(D_MODEL), tile(BRANCH_WIDTH), pl.BlockSpec((tm // S5_CHUNK, S5_CHUNK * 256), row),
                  tile(BRANCH_WIDTH), tile(BRANCH_WIDTH)]
                 + [_const_spec(c.shape, None if c.ndim == 2 and c.shape[0] == 1 else layer) for c in consts],
        out_specs=tile(D_MODEL),
        out_shape=jax.ShapeDtypeStruct((tokens, D_MODEL), F32),
        scratch_shapes=[pltpu.VMEM((tm, D_MODEL), F32), pltpu.VMEM((2, tm, LANES), F32)],
        compiler_params=_cparams(("parallel",)),
        name="merge_ffn",
    )(h, ysb, ys5, ygla, ydsa, *consts)


def _later_key_matrix(n):
    j = lax.broadcasted_iota(jnp.int32, (n, n), 0)
    s = lax.broadcasted_iota(jnp.int32, (n, n), 1)
    return jnp.where(j > s, 1.0, 0.0).astype(BF16)


def _sb_scores(qm, k, keep=None):
    z = _dot_nt(qm, k)
    if keep is not None:
        z = jnp.where(keep, z, NEG_BIG)
    lk = -(jnp.maximum(z, 0.0) + jnp.log(1.0 + jnp.exp(-jnp.abs(z))))
    return z, lk


def _sb_kernel(q_ref, k_ref, v_ref, o_ref):
    seq = q_ref.shape[1]
    u_win = _later_key_matrix(SB_WIN)
    u_blk = _later_key_matrix(SB_Q)
    lane_q = lax.broadcasted_iota(jnp.int32, (SB_Q, LANES), 1)
    n_streams = 2 * SB_GROUP
    key_minus_query = (lax.broadcasted_iota(jnp.int32, (2 * SB_Q, SB_WIN), 1)
                       - lax.broadcasted_iota(jnp.int32, (2 * SB_Q, SB_WIN), 0) % SB_Q)

    def q_group(qg, carry):
        starts, qms, zs, lks, vws = [], [], [], [], []
        for sub in range(SB_GROUP):
            t0 = pl.multiple_of((qg * SB_GROUP + sub) * SB_Q, SB_Q)
            start = pl.multiple_of(jnp.maximum(t0 - (SB_WIN - SB_Q), 0), SB_Q)
            q = q_ref[0, pl.ds(t0, SB_Q), :] * (HEAD_DIM ** -0.5)
            kw = k_ref[0, pl.ds(start, SB_WIN), :]
            vws.append(v_ref[0, pl.ds(start, SB_WIN), :])
            strict = key_minus_query < (t0 - start)
            starts.append(start)
            qm2 = jnp.concatenate(
                [jnp.where((lane_q // HEAD_DIM) == head, q, jnp.zeros_like(q)) for head in range(2)], axis=0)
            z, lk = _sb_scores(qm2, kw, strict)
            qms += [qm2[:SB_Q], qm2[SB_Q:]]
            zs.append(z)
            lks.append(lk)
        lk_all = jnp.concatenate(lks, axis=0)
        later_all = _dot(lk_all.astype(BF16), u_win)
        csums, outs = [], []
        for sub in range(SB_GROUP):
            rows = slice(sub * 2 * SB_Q, (sub + 1) * 2 * SB_Q)
            later = later_all[rows]
            w = jnp.exp(zs[sub] + lks[sub] + later)
            o2 = _dot(w.astype(BF16), vws[sub])
            c2 = later[:, 0:1] + lks[sub][:, 0:1]
            outs += [o2[:SB_Q], o2[SB_Q:]]
            csums += [c2[:SB_Q], c2[SB_Q:]]

        def cond(state):
            step, cmaxes, _, _ = state
            flags = [jnp.logical_and(starts[s // 2] - (step + 1) * SB_Q >= 0, cmaxes[s] > SB_LOG_ZERO)
                     for s in range(n_streams)]
            return functools.reduce(jnp.logical_or, flags)

        def body(state):
            step, _, csums, outs = state
            new_c, new_o = [], []
            for s in range(n_streams):
                nxt = starts[s // 2] - (step + 1) * SB_Q
                in_range = nxt >= 0
                p0 = pl.multiple_of(jnp.maximum(nxt, 0), SB_Q)
                kb = k_ref[0, pl.ds(p0, SB_Q), :]
                vb = v_ref[0, pl.ds(p0, SB_Q), :]
                z, lk = _sb_scores(qms[s], kb)
                later = _dot(lk.astype(BF16), u_blk)
                w = jnp.where(in_range, jnp.exp(z + lk + later + csums[s]), 0.0)
                new_o.append(outs[s] + _dot(w.astype(BF16), vb))
                new_c.append(csums[s] + jnp.where(in_range, later[:, 0:1] + lk[:, 0:1], 0.0))
            return step + 1, tuple(jnp.max(c) for c in new_c), tuple(new_c), tuple(new_o)

        init = (jnp.int32(0), tuple(jnp.max(c) for c in csums), tuple(csums), tuple(outs))
        _, _, _, outs = lax.while_loop(cond, body, init)
        for sub in range(SB_GROUP):
            t0 = pl.multiple_of((qg * SB_GROUP + sub) * SB_Q, SB_Q)
            o_ref[0, pl.ds(t0, SB_Q), :] = jnp.where(
                lane_q < HEAD_DIM, outs[2 * sub], outs[2 * sub + 1]).astype(o_ref.dtype)
        return carry

    lax.fori_loop(0, seq // (SB_Q * SB_GROUP), q_group, 0)


def _stick_breaking(sb3):
    bsz, seq, _ = sb3.shape
    spec = lambda off: pl.BlockSpec((1, seq, LANES), lambda b, hp: (b, 0, off + hp))
    return pl.pallas_call(
        _sb_kernel,
        grid=(bsz, 2),
        in_specs=[spec(0), spec(2), spec(4)],
        out_specs=pl.BlockSpec((1, seq, LANES), lambda b, hp: (b, 0, hp)),
        out_shape=jax.ShapeDtypeStruct((bsz, seq, BRANCH_WIDTH), BF16),
        compiler_params=_cparams(("parallel", "parallel")),
        name="stick_breaking",
    )(sb3, sb3, sb3)


def _s5_kernel(u_ref, m_ref, e_ref, g_ref, pw_ref, d_ref, y_ref, state_ref):
    half = S5_GROUPS * S5_STATE

    @pl.when(pl.program_id(1) == 0)
    def _():
        state_ref[...] = jnp.zeros_like(state_ref)

    u = u_ref[0]
    ub = u.astype(BF16)
    rows = u.shape[0]

    def cmul(ar, ai, xr, xi):
        return ar * xr - ai * xi, ar * xi + ai * xr

    w = _dot(ub, e_ref[...])
    wr, wi = w[:, :half], w[:, half:]
    row = lax.broadcasted_iota(jnp.int32, (rows, half), 0)
    a1r, a1i = pw_ref[0:1, :half], pw_ref[0:1, half:]
    sr, si = state_ref[0:1, :half], state_ref[0:1, half:]
    cr, ci = cmul(a1r, a1i, sr, si)
    wr = jnp.where(row == 0, wr + cr, wr)
    wi = jnp.where(row == 0, wi + ci, wi)
    step = 0
    d = 1
    while d < rows:
        ar, ai = pw_ref[step:step + 1, :half], pw_ref[step:step + 1, half:]
        pr = jnp.where(row >= d, pltpu.roll(wr, d, axis=0), 0.0)
        pi = jnp.where(row >= d, pltpu.roll(wi, d, axis=0), 0.0)
        mr, mi = cmul(ar, ai, pr, pi)
        wr, wi = wr + mr, wi + mi
        d *= 2
        step += 1
    xr = jnp.where(row == 0, sr, pltpu.roll(wr, 1, axis=0))
    xi = jnp.where(row == 0, si, pltpu.roll(wi, 1, axis=0))
    state_ref[0:1, :half] = wr[rows - 1:rows, :]
    state_ref[0:1, half:] = wi[rows - 1:rows, :]

    n_c = S5_GROUPS * S5_GROUP_CH
    slots = [ub[:, j * n_c:(j + 1) * n_c] for j in range(S5_CHUNK)]
    cols = []
    for i in range(S5_CHUNK):
        acc = _dot(slots[0], m_ref[i])
        for j in range(1, i + 1):
            acc = acc + _dot(slots[j], m_ref[i - j])
        cols.append(acc)
    y = jnp.concatenate(cols, axis=1)
    y = y + _dot(xr.astype(BF16), g_ref[:half, :]) + _dot(xi.astype(BF16), g_ref[half:, :])
    y = y + d_ref[...] * u
    y_ref[0] = jax.nn.gelu(y).astype(y_ref.dtype)


def _s5(u3, m, e, g, pw, dskip, layer):
    bsz, rows, width = u3.shape
    tr = min(S5_ROW_TILE, rows)
    blk = pl.BlockSpec((1, tr, width), lambda b, i: (b, i, 0))
    return pl.pallas_call(
        _s5_kernel,
        grid=(bsz, rows // tr),
        in_specs=[blk] + [_const_spec(a.shape, layer) for a in (m, e, g, pw, dskip)],
        out_specs=blk,
        out_shape=jax.ShapeDtypeStruct(u3.shape, BF16),
        scratch_shapes=[pltpu.VMEM((8, width), F32)],
        compiler_params=_cparams(("parallel", "arbitrary")),
        name="s5",
    )(u3, m, e, g, pw, dskip)


def _s5_operators(lam_re, lam_im, log_step, b_re, b_im, c_re, c_im, d_skip, row_tile):
    n_g, n_p, n_h, L = S5_GROUPS, S5_STATE, S5_GROUP_CH, S5_CHUNK
    n_c, n_s = n_g * n_h, n_g * n_p
    dt = jnp.exp(log_step)[:, None]

    def lam_bar_pow(k):
        mag = jnp.exp(k[:, None, None] * (lam_re * dt)[None])
        ang = k[:, None, None] * (lam_im * dt)[None]
        return mag * jnp.cos(ang), mag * jnp.sin(ang)

    pr, pi = lam_bar_pow(jnp.arange(L + 1, dtype=F32))
    den = lam_re * lam_re + lam_im * lam_im
    fr = ((pr[1] - 1.0) * lam_re + pi[1] * lam_im) / den
    fi = (pi[1] * lam_re - (pr[1] - 1.0) * lam_im) / den
    bbr = fr[..., None] * b_re - fi[..., None] * b_im
    bbi = fr[..., None] * b_im + fi[..., None] * b_re
    tr = pr[:L, :, :, None] * bbr[None] - pi[:L, :, :, None] * bbi[None]
    ti = pr[:L, :, :, None] * bbi[None] + pi[:L, :, :, None] * bbr[None]

    def expand(small, rows_per_group, cols_per_group):
        rows = small.shape[-2]
        src = lax.broadcasted_iota(jnp.int32, (cols_per_group, n_g * cols_per_group), 0)
        dst = lax.broadcasted_iota(jnp.int32, (cols_per_group, n_g * cols_per_group), 1) % cols_per_group
        spread = jnp.where(src == dst, 1.0, 0.0).astype(BF16)
        tiled = jnp.einsum('...rc,cq->...rq', small.astype(BF16), spread, preferred_element_type=F32)
        r = lax.broadcasted_iota(jnp.int32, (rows, n_g * cols_per_group), 0) // rows_per_group
        c = lax.broadcasted_iota(jnp.int32, (rows, n_g * cols_per_group), 1) // cols_per_group
        return jnp.where(r == c, tiled, 0.0)

    hp = lax.Precision.HIGHEST
    kern = (jnp.einsum('ghp,kgpi->kghi', c_re, tr, precision=hp)
            - jnp.einsum('ghp,kgpi->kghi', c_im, ti, precision=hp))
    m_op = expand(kern.transpose(0, 1, 3, 2).reshape(L, n_c, n_h), n_h, n_h).astype(BF16)

    e_r = expand(tr[::-1].transpose(0, 1, 3, 2).reshape(L, n_c, n_p), n_h, n_p)
    e_i = expand(ti[::-1].transpose(0, 1, 3, 2).reshape(L, n_c, n_p), n_h, n_p)
    e_op = jnp.concatenate([e_r, e_i], axis=2).astype(BF16).reshape(L * n_c, 2 * n_s)

    clr = c_re[None] * pr[1:, :, None, :] - c_im[None] * pi[1:, :, None, :]
    cli = c_re[None] * pi[1:, :, None, :] + c_im[None] * pr[1:, :, None, :]
    g_r = expand(clr.transpose(0, 1, 3, 2).reshape(L, n_s, n_h), n_p, n_h)
    g_i = expand(cli.transpose(0, 1, 3, 2).reshape(L, n_s, n_h), n_p, n_h)
    g_blocks = jnp.concatenate([g_r, -g_i], axis=1).astype(BF16)
    g_op = jnp.concatenate([g_blocks[i] for i in range(L)], axis=1)

    n_steps = max(1, int(math.log2(row_tile)))
    sr, si = lam_bar_pow(L * 2.0 ** jnp.arange(n_steps, dtype=F32))
    pw = jnp.concatenate([sr.reshape(n_steps, n_s), si.reshape(n_steps, n_s)], axis=1)
    pw = jnp.pad(pw, ((0, (-n_steps) % 8), (0, 0)))
    dskip = jnp.tile(d_skip.reshape(1, n_c), (1, L))
    return m_op, e_op, g_op, pw.astype(F32), dskip.astype(F32)


def _gla_constants():
    n = GLA_SUB
    ri = lax.broadcasted_iota(jnp.int32, (n, n), 0)
    ci = lax.broadcasted_iota(jnp.int32, (n, n), 1)
    same = (ri // GLA_BLOCK) == (ci // GLA_BLOCK)
    mats = [same & (ci <= ri), same & (ci > ri)]
    masks = [ri == ci]
    ups, lows = [], []
    for lev in range(GLA_LEVELS):
        half = 1 << lev
        blk = 2 * half
        mid = (ri // blk) * blk + (half - 1)
        later = (ri % blk) >= half
        ups.append(later & (ci > mid) & (ci <= ri))
        lows.append(jnp.logical_not(later) & (ci > ri) & (ci <= mid))
        masks.append(((ri // blk) == (ci // blk)) & later & ((ci % blk) < half))
    stacked = jnp.concatenate([jnp.where(m, 1.0, 0.0).astype(BF16) for m in mats + ups + lows], axis=0)
    return stacked, masks


def _gla_tile(x, gg, state, wgate, bgate, norm, stacked, masks):
    tile = x.shape[0]
    hk = GLA_HEADS * GLA_DK
    hv = GLA_HEADS * GLA_DV
    q = x[:, 0:hk].astype(F32) * (GLA_DK ** -0.5)
    k = x[:, hk:2 * hk].astype(F32)
    vb = x[:, 2 * hk:2 * hk + hv]
    r = x[:, 2 * hk + hv:].astype(F32)

    z = _dot(gg.astype(BF16), wgate) + bgate
    g = -(jnp.maximum(-z, 0.0) + jnp.log(1.0 + jnp.exp(-jnp.abs(z)))) * (1.0 / GLA_GATE_NORM)
    g_hi, g_lo = _hi_lo(g)
    g2 = jnp.concatenate([g_hi, g_lo], axis=1)

    lane_k = lax.broadcasted_iota(jnp.int32, (GLA_SUB, hk), 1) // GLA_DK
    row_blk = lax.broadcasted_iota(jnp.int32, (GLA_SUB, hk), 0) // GLA_BLOCK
    lane_v = lax.broadcasted_iota(jnp.int32, (GLA_SUB, hv), 1) // GLA_DV
    rs = lax.broadcasted_iota(jnp.int32, (hv, hk), 0) // GLA_DV
    cs = lax.broadcasted_iota(jnp.int32, (hv, hk), 1) // GLA_DK
    head_diag = rs == cs
    outs = []
    for st in range(tile // GLA_SUB):
        rows = slice(st * GLA_SUB, (st + 1) * GLA_SUB)
        sums = _dot(stacked, g2[rows])
        sums = sums[:, :hk] + sums[:, hk:]
        part = lambda i: sums[i * GLA_SUB:(i + 1) * GLA_SUB]
        b, tail = part(0), part(1)
        qs, ks, vs = q[rows], k[rows], vb[rows]

        att = [jnp.zeros((GLA_SUB, GLA_SUB), F32) for _ in range(GLA_HEADS)]
        for lev in range(-1, GLA_LEVELS):
            if lev < 0:
                qt, kt = qs.astype(BF16), ks.astype(BF16)
            else:
                qt = (qs * jnp.exp(part(2 + lev))).astype(BF16)
                kt = (ks * jnp.exp(part(2 + GLA_LEVELS + lev))).astype(BF16)
            q_heads = jnp.concatenate(
                [jnp.where(lane_k == h, qt, jnp.zeros_like(qt)) for h in range(GLA_HEADS)], axis=0)
            a = _dot_nt(q_heads, kt)
            for h in range(GLA_HEADS):
                att[h] = jnp.where(masks[lev + 1], a[h * GLA_SUB:(h + 1) * GLA_SUB], att[h])
        att_all = jnp.concatenate([a.astype(BF16) for a in att], axis=1)
        v_heads = jnp.concatenate([jnp.where(lane_v == h, vs, jnp.zeros_like(vs)) for h in range(GLA_HEADS)], axis=0)
        o = _dot(att_all, v_heads)

        q_dec = (qs * jnp.exp(b)).astype(BF16)
        k_dec = (ks * jnp.exp(tail)).astype(BF16)
        b_end = b + tail
        n_blk = GLA_SUB // GLA_BLOCK
        k_blocks = jnp.concatenate(
            [jnp.where(row_blk == n, k_dec, jnp.zeros_like(k_dec)) for n in range(n_blk)], axis=1)
        upd_all = _dot_tn(vs, k_blocks)
        cross = []
        for blk in range(n_blk):
            r0 = blk * GLA_BLOCK
            cross.append(_dot_nt(q_dec[r0:r0 + GLA_BLOCK], state.astype(BF16)))
            upd = upd_all[:, blk * hk:(blk + 1) * hk]
            state = jnp.exp(b_end[r0:r0 + 1, :]) * state + jnp.where(head_diag, upd, 0.0)
        outs.append(o + jnp.concatenate(cross, axis=0))
    o = jnp.concatenate(outs, axis=0)

    o = _group_rms(o, norm)
    return o * (r * jax.nn.sigmoid(r)), state


def _gla_kernel(x_ref, gg_ref, wgate_ref, bgate_ref, norm_ref, y_ref, state_ref):
    @pl.when(pl.program_id(0) == 0)
    def _():
        state_ref[...] = jnp.zeros_like(state_ref)

    stacked, masks = _gla_constants()
    for bi in range(x_ref.shape[0]):
        y, state = _gla_tile(x_ref[bi], gg_ref[bi], state_ref[bi], wgate_ref[...], bgate_ref[...],
                             norm_ref[...], stacked, masks)
        y_ref[bi] = y.astype(y_ref.dtype)
        state_ref[bi] = state


def _gla(x3, gg3, wgate, bgate, norm):
    bsz, seq, _ = x3.shape
    tile = min(GLA_TILE, seq)
    return pl.pallas_call(
        _gla_kernel,
        grid=(seq // tile,),
        in_specs=[pl.BlockSpec((bsz, tile, 768), lambda i: (0, i, 0)),
                  pl.BlockSpec((bsz, tile, LANES), lambda i: (0, i, 0)),
                  _const_spec(wgate.shape), _const_spec(bgate.shape), _const_spec(norm.shape)],
        out_specs=pl.BlockSpec((bsz, tile, BRANCH_WIDTH), lambda i: (0, i, 0)),
        out_shape=jax.ShapeDtypeStruct((bsz, seq, BRANCH_WIDTH), BF16),
        scratch_shapes=[pltpu.VMEM((bsz, GLA_HEADS * GLA_DV, GLA_HEADS * GLA_DK), F32)],
        compiler_params=_cparams(("arbitrary",)),
        name="gla",
    )(x3, gg3, wgate, bgate, norm)


def _dsa_band_bias(offset, n_steps):
    row = lax.broadcasted_iota(jnp.int32, (2 * DSA_BLOCK, 2 * DSA_BLOCK), 0) % DSA_BLOCK
    col = lax.broadcasted_iota(jnp.int32, (2 * DSA_BLOCK, 2 * DSA_BLOCK), 1)
    dist = offset + row - col
    return jnp.where((dist >= 0) & (dist <= n_steps), 0.0, NEG_BIG)


def _dsa_attend(q, kw, vw, bias, shift):
    lane_q = lax.broadcasted_iota(jnp.int32, (DSA_BLOCK, LANES), 1)
    q = q * (HEAD_DIM ** -0.5)
    qm2 = jnp.concatenate(
        [jnp.where((lane_q // HEAD_DIM) == head, q, jnp.zeros_like(q)) for head in range(2)], axis=0)
    s = _dot_nt(qm2, kw) + bias
    if shift is None:
        m = jnp.max(s, axis=1, keepdims=True)
        p = jnp.exp(s - m)
        l = jnp.sum(p, axis=1, keepdims=True)
        o2 = _dot(p.astype(BF16), vw) / l
        lse2 = jnp.broadcast_to(m + jnp.log(l), (2 * DSA_BLOCK, LANES))
    else:
        p = jnp.exp(s - shift)
        pv = _dot(p.astype(BF16), jnp.concatenate([vw, jnp.ones_like(vw)], axis=1))
        l = pv[:, LANES:]
        o2 = pv[:, :LANES] / l
        lse2 = shift + jnp.log(l)
    first = lane_q < HEAD_DIM
    return (jnp.where(first, o2[:DSA_BLOCK], o2[DSA_BLOCK:]),
            jnp.where(first, lse2[:DSA_BLOCK], lse2[DSA_BLOCK:]))


def _dsa_kernel(bound_ref, q1_ref, k1_ref, v1_ref, q4_ref, k4_ref, v4_ref, q16_ref, k16_ref, v16_ref,
                y_ref, o_scr, l_scr):
    seq = q1_ref.shape[1]
    refs = ((q1_ref, k1_ref, v1_ref), (q4_ref, k4_ref, v4_ref), (q16_ref, k16_ref, v16_ref))
    blocks_per_span = DSA_SPAN // DSA_BLOCK
    n_steps = DSA_PATTERNS[0][0] // DSA_PATTERNS[0][1]
    assert all(w // d == n_steps for w, d in DSA_PATTERNS)
    bias_inner = _dsa_band_bias(DSA_BLOCK, n_steps)
    bias_first = _dsa_band_bias(0, n_steps)

    def run(shift):
        def span(sp, carry):
            for p, (_, dil) in enumerate(DSA_PATTERNS):
                q_ref, k_ref, v_ref = refs[p]
                per_res = blocks_per_span // dil

                def unit(idx, c, p=p, dil=dil, per_res=per_res, q_ref=q_ref, k_ref=k_ref, v_ref=v_ref):
                    r = idx // per_res
                    jb = idx % per_res
                    m0 = pl.multiple_of(sp * (DSA_SPAN // dil) + jb * DSA_BLOCK, DSA_BLOCK)
                    start = pl.multiple_of(jnp.maximum(m0 - DSA_BLOCK, 0), DSA_BLOCK)
                    if dil == 1:
                        q = q_ref[0, pl.ds(m0, DSA_BLOCK), :]
                        kw = k_ref[0, pl.ds(start, 2 * DSA_BLOCK), :]
                        vw = v_ref[0, pl.ds(start, 2 * DSA_BLOCK), :]
                    else:
                        q = q_ref[0, r, pl.ds(m0, DSA_BLOCK), :]
                        kw = k_ref[0, r, pl.ds(start, 2 * DSA_BLOCK), :]
                        vw = v_ref[0, r, pl.ds(start, 2 * DSA_BLOCK), :]
                    o, lse = _dsa_attend(q, kw, vw, jnp.where(m0 == 0, bias_first, bias_inner), shift)
                    rows = pl.ds(r + dil * jb * DSA_BLOCK, DSA_BLOCK, stride=dil) if dil > 1 else \
                        pl.ds(pl.multiple_of(jb * DSA_BLOCK, DSA_BLOCK), DSA_BLOCK)
                    o_scr[p, rows, :] = o
                    l_scr[p, rows, :] = lse
                    return c

                lax.fori_loop(0, blocks_per_span, unit, 0, unroll=DSA_UNROLL)

            l1, l2, l3 = l_scr[0], l_scr[1], l_scr[2]
            lm = jnp.maximum(jnp.maximum(l1, l2), l3)
            e1, e2, e3 = jnp.exp(l1 - lm), jnp.exp(l2 - lm), jnp.exp(l3 - lm)
            y = (e1 * o_scr[0] + e2 * o_scr[1] + e3 * o_scr[2]) / (e1 + e2 + e3)
            y_ref[0, pl.ds(pl.multiple_of(sp * DSA_SPAN, DSA_SPAN), DSA_SPAN), :] = y.astype(y_ref.dtype)
            return carry

        lax.fori_loop(0, seq // DSA_SPAN, span, 0)

    bound = bound_ref[0]

    @pl.when(bound <= DSA_MAX_SHIFT)
    def _():
        run(bound)

    @pl.when(bound > DSA_MAX_SHIFT)
    def _():
        run(None)


def _dsa(dsa1, dsa4, dsa16, bound):
    bsz, seq, _ = dsa1.shape
    nat = lambda off: pl.BlockSpec((1, seq, LANES), lambda b, hp: (b, 0, off + hp))
    res = lambda dil, off: pl.BlockSpec((1, dil, seq // dil, LANES), lambda b, hp: (b, 0, 0, off + hp))
    return pl.pallas_call(
        _dsa_kernel,
        grid=(bsz, 2),
        in_specs=[pl.BlockSpec(memory_space=pltpu.SMEM),
                  nat(0), nat(2), nat(4), res(4, 0), res(4, 2), res(4, 4), res(16, 0), res(16, 2), res(16, 4)],
        out_specs=pl.BlockSpec((1, seq, LANES), lambda b, hp: (b, 0, hp)),
        out_shape=jax.ShapeDtypeStruct((bsz, seq, BRANCH_WIDTH), BF16),
        scratch_shapes=[pltpu.VMEM((3, DSA_SPAN, LANES), F32), pltpu.VMEM((3, DSA_SPAN, LANES), F32)],
        compiler_params=_cparams(("parallel", "parallel")),
        name="dsa",
    )(bound, dsa1, dsa1, dsa1, dsa4, dsa4, dsa4, dsa16, dsa16, dsa16)


def _proj_weights(w_in):
    lead = w_in[..., :1792].astype(BF16)
    gg = jnp.pad(w_in[..., 1792:1808].astype(BF16), ((0, 0), (0, 0), (0, LANES - GLA_GATE_RANK)))
    dsa = w_in[..., 1808:2576].astype(BF16)
    wp = jnp.concatenate([lead, dsa, gg], axis=-1)
    return wp, w_in[..., 2576:].astype(BF16)


def _rope_tables(seq):
    inv = ROPE_THETA ** (-jnp.arange(0, HEAD_DIM, 2, dtype=F32) / HEAD_DIM)
    ang = jnp.arange(seq, dtype=F32)[:, None] * inv[None, :]
    cos, sin = jnp.cos(ang), jnp.sin(ang)
    cos_t = jnp.tile(jnp.concatenate([cos, cos], axis=1), (1, 2))
    sin_t = jnp.tile(jnp.concatenate([-sin, sin], axis=1), (1, 2))
    return cos_t, sin_t


def _pick_tile(n, pref):
    t = min(pref, n)
    while n % t:
        t //= 2
    return t


def kernel(x, ffn1_norm, ffn1_w_in, ffn1_w_out, mix_norm, w_in, s5_lam_re, s5_lam_im, s5_log_step, s5_b_re, s5_b_im, s5_c_re, s5_c_im, s5_d, s5_w_glu, gla_w_gate, gla_b_gate, gla_norm, dsa_q_norm, dsa_k_norm, w_branch, w_out, ffn2_norm, ffn2_w_in, ffn2_w_out):
    bsz, seq, _ = x.shape
    tokens = bsz * seq
    depth = ffn1_norm.shape[0]
    assert seq % DSA_SPAN == 0 and seq // DSA_PATTERNS[-1][1] >= 2 * DSA_BLOCK and seq >= SB_WIN
    tm_a = _pick_tile(seq, 512)
    tm_b = _pick_tile(tokens, 512)
    s5_rows = seq // S5_CHUNK
    s5_tile = min(S5_ROW_TILE, s5_rows)
    cos_t, sin_t = _rope_tables(seq)

    wp_all, wg_all = _proj_weights(w_in)
    w1a_all, w2a_all = ffn1_w_in.astype(BF16), ffn1_w_out.astype(BF16)
    w1b_all, w2b_all = ffn2_w_in.astype(BF16), ffn2_w_out.astype(BF16)
    wglu_all, wb_all, wo_all = s5_w_glu.astype(BF16), w_branch.astype(BF16), w_out.astype(BF16)
    s5_ops = jax.vmap(functools.partial(_s5_operators, row_tile=s5_tile))(
        s5_lam_re, s5_lam_im, s5_log_step, s5_b_re, s5_b_im, s5_c_re, s5_c_im, s5_d)

    h = x.reshape(tokens, D_MODEL)
    for l in range(depth):
        row = lambda p: p[l].reshape(1, -1)
        qn = jnp.tile(row(dsa_q_norm), (1, 4))
        kn = jnp.tile(row(dsa_k_norm), (1, 4))
        h, sb, s5u, gla_in, gg, dsa1, dsa4, dsa16 = _ffn_proj(
            h, row(ffn1_norm), w1a_all, w2a_all, row(mix_norm), wp_all, qn, kn, cos_t, sin_t, bsz, seq, tm_a, l)

        y_sb = _stick_breaking(sb.reshape(bsz, seq, 768)).reshape(tokens, BRANCH_WIDTH)

        y_s5 = _s5(s5u.reshape(bsz, s5_rows, S5_CHUNK * 256), *s5_ops, l).reshape(tokens // S5_CHUNK, S5_CHUNK * 256)

        wgate = jnp.pad(gla_w_gate[l], ((0, LANES - GLA_GATE_RANK), (0, 0))).astype(BF16)
        y_gla = _gla(gla_in.reshape(bsz, seq, 768), gg.reshape(bsz, seq, LANES), wgate,
                     row(gla_b_gate), row(gla_norm)).reshape(tokens, BRANCH_WIDTH)

        bound = (1.02 * HEAD_DIM ** 0.5 * jnp.max(jnp.abs(dsa_q_norm[l])) * jnp.max(jnp.abs(dsa_k_norm[l]))).reshape(1)
        y_dsa = _dsa(dsa1.reshape(bsz, seq, 768), dsa4, dsa16, bound).reshape(tokens, BRANCH_WIDTH)

        h = _merge_ffn(h, y_sb, y_s5, y_gla, y_dsa, row(mix_norm), wg_all, wglu_all, wb_all, wo_all,
                       row(ffn2_norm), w1b_all, w2b_all, tm_b, l)
    return h.reshape(bsz, seq, D_MODEL)
```

```python
import functools
import math

import jax
import jax.numpy as jnp
from jax import lax
from jax.experimental import pallas as pl
from jax.experimental.pallas import tpu as pltpu

D_MODEL = 1024
HEAD_DIM = 64
BRANCH_WIDTH = 256
N_BRANCH = 4
SB_HEADS = 4
S5_GROUPS = 16
S5_GROUP_CH = 16
S5_STATE = 64
GLA_HEADS = 4
GLA_DK = 32
GLA_DV = 64
GLA_GATE_RANK = 16
GLA_GATE_NORM = 16.0
DSA_PATTERNS = ((128, 1), (512, 4), (2048, 16))
DSA_BLOCK = 128
ROPE_THETA = 10000.0
D_FF = 2816
RMS_EPS = 1e-6

LANES = 128
VMEM_LIMIT_BYTES = 58 * 1024 * 1024
FF_CHUNK = 256
N_FF_CHUNKS = D_FF // FF_CHUNK
S5_CHUNK = 8
S5_ROW_TILE = 256
GLA_TILE = 256
GLA_BLOCK = 64
GLA_LEVELS = 6
GLA_SUB = 128
SB_Q = 64
SB_WIN = 4 * SB_Q
SB_GROUP = 8
SB_LOG_ZERO = -104.0
NEG_BIG = -1e30
DSA_UNROLL = 8
DSA_SPAN = DSA_PATTERNS[-1][1] * DSA_BLOCK
DSA_MAX_SHIFT = 30.0

PROJ_SB = (0, 768)
PROJ_S5 = (768, 1024)
PROJ_GLA = (1024, 1792)
PROJ_DSA = (1792, 2560)
PROJ_GG = (2560, 2688)
PROJ_WIDTH = 2688

F32 = jnp.float32
BF16 = jnp.bfloat16


def _cparams(sem):
    return pltpu.CompilerParams(dimension_semantics=sem, vmem_limit_bytes=VMEM_LIMIT_BYTES)


def _const_spec(shape, layer=None):
    if layer is None:
        zeros = (0,) * len(shape)
        return pl.BlockSpec(shape, lambda *_: zeros, pipeline_mode=pl.Buffered(1))
    index = (layer,) + (0,) * (len(shape) - 1)
    return pl.BlockSpec((None,) + tuple(shape[1:]), lambda *_: index, pipeline_mode=pl.Buffered(1))


def _rms_rows(x, g):
    ms = jnp.mean(x * x, axis=-1, keepdims=True)
    return x * lax.rsqrt(ms + RMS_EPS) * g


def _hi_lo(x):
    hi = x.astype(BF16)
    lo = (x - hi.astype(F32)).astype(BF16)
    return hi, lo


def _dot(a, b):
    return jnp.dot(a, b, preferred_element_type=F32)


def _dot_nt(a, b):
    return lax.dot_general(a, b, (((1,), (1,)), ((), ())), preferred_element_type=F32)


def _dot_tn(a, b):
    return lax.dot_general(a, b, (((0,), (0,)), ((), ())), preferred_element_type=F32)


def _swiglu_into(acc_ref, xb, w1_ref, w2_ref):
    for c in range(N_FF_CHUNKS):
        lo, hi = c * FF_CHUNK, (c + 1) * FF_CHUNK
        a = _dot(xb, w1_ref[:, lo:hi])
        b = _dot(xb, w1_ref[:, D_FF + lo:D_FF + hi])
        hm = (a * jax.nn.sigmoid(a) * b).astype(BF16)
        part = _dot(hm, w2_ref[lo:hi, :])
        if c == 0:
            acc_ref[...] = part
        else:
            acc_ref[...] += part


def _group_mean_matrix(width, group):
    r = lax.broadcasted_iota(jnp.int32, (width, width), 0) // group
    c = lax.broadcasted_iota(jnp.int32, (width, width), 1) // group
    return jnp.where(r == c, 1.0 / group, 0.0).astype(BF16)


def _group_rms(x, gain):
    gm = _group_mean_matrix(x.shape[-1], HEAD_DIM)
    ms = _dot((x * x).astype(BF16), gm)
    return x * lax.rsqrt(ms + RMS_EPS) * gain


def _swap_half_heads(x):
    half = HEAD_DIM // 2
    outs = []
    for s in range(x.shape[-1] // LANES):
        xs = x[:, s * LANES:(s + 1) * LANES]
        lane = lax.broadcasted_iota(jnp.int32, xs.shape, 1)
        up = pltpu.roll(xs, LANES - half, axis=1)
        down = pltpu.roll(xs, half, axis=1)
        outs.append(jnp.where((lane % HEAD_DIM) < half, up, down))
    return jnp.concatenate(outs, axis=-1)


def _ffn_proj_kernel(x_ref, n1_ref, w1_ref, w2_ref, nm_ref, wp_ref, qn_ref, kn_ref, cos_ref, sin_ref,
                     h_ref, sb_ref, s5_ref, gla_ref, gg_ref, dsa1_ref, dsa4_ref, dsa16_ref,
                     acc_ref, slab_ref):
    tm = x_ref.shape[0]
    x = x_ref[...]
    xb = _rms_rows(x, n1_ref[...]).astype(BF16)
    _swiglu_into(acc_ref, xb, w1_ref, w2_ref)
    h = x + 0.5 * acc_ref[...]
    h_ref[...] = h
    ub = _rms_rows(h, nm_ref[...]).astype(BF16)

    dsa = _dot(ub, wp_ref[:, PROJ_DSA[0]:PROJ_DSA[1]])
    cos = jnp.concatenate([cos_ref[...]] * 2, axis=-1)
    sin = jnp.concatenate([sin_ref[...]] * 2, axis=-1)
    parts = []
    for i, gain_ref in enumerate((qn_ref, kn_ref)):
        t = _group_rms(dsa[:, i * 256:(i + 1) * 256], gain_ref[...])
        parts.append(t * cos + _swap_half_heads(t) * sin)
    parts.append(dsa[:, 512:768])
    dsa = jnp.concatenate(parts, axis=-1)
    dsa1_ref[...] = dsa.astype(BF16)
    for s in range(6):
        slab_ref[s] = dsa[:, s * LANES:(s + 1) * LANES]
    for dil, ref in ((4, dsa4_ref), (16, dsa16_ref)):
        for r in range(dil):
            for s in range(6):
                rows = slab_ref[s, pl.ds(r, tm // dil, stride=dil), :]
                ref[0, r, :, s * LANES:(s + 1) * LANES] = rows.astype(BF16)

    s5 = _dot(ub, wp_ref[:, PROJ_S5[0]:PROJ_S5[1]])
    for s in range(2):
        slab_ref[6 + s] = s5[:, s * LANES:(s + 1) * LANES]
    for j in range(S5_CHUNK):
        for s in range(2):
            col = j * 256 + s * LANES
            s5_ref[:, col:col + LANES] = slab_ref[6 + s, pl.ds(j, tm // S5_CHUNK, stride=S5_CHUNK), :]

    sb_ref[...] = _dot(ub, wp_ref[:, PROJ_SB[0]:PROJ_SB[1]]).astype(BF16)
    gla_ref[...] = _dot(ub, wp_ref[:, PROJ_GLA[0]:PROJ_GLA[1]]).astype(BF16)
    gg_ref[...] = _dot(ub, wp_ref[:, PROJ_GG[0]:PROJ_GG[1]])


def _ffn_proj(x2, n1, w1, w2, nm, wp, qn, kn, cos_t, sin_t, bsz, seq, tm, layer):
    tokens = x2.shape[0]
    seq_tiles = seq // tm
    row = lambda i: (i, 0)
    tab = lambda i: (i % seq_tiles, 0)
    res = lambda i: (i // seq_tiles, 0, i % seq_tiles, 0)
    tile = lambda w: pl.BlockSpec((tm, w), row)
    s5_w = S5_CHUNK * 256
    return pl.pallas_call(
        _ffn_proj_kernel,
        grid=(tokens // tm,),
        in_specs=[tile(D_MODEL), _const_spec(n1.shape), _const_spec(w1.shape, layer), _const_spec(w2.shape, layer),
                  _const_spec(nm.shape), _const_spec(wp.shape, layer), _const_spec(qn.shape), _const_spec(kn.shape),
                  pl.BlockSpec((tm, LANES), tab), pl.BlockSpec((tm, LANES), tab)],
        out_specs=[tile(D_MODEL), tile(768), pl.BlockSpec((tm // S5_CHUNK, s5_w), row), tile(768), tile(LANES),
                   tile(768), pl.BlockSpec((1, 4, tm // 4, 768), res), pl.BlockSpec((1, 16, tm // 16, 768), res)],
        out_shape=[jax.ShapeDtypeStruct((tokens, D_MODEL), F32),
                   jax.ShapeDtypeStruct((tokens, 768), BF16),
                   jax.ShapeDtypeStruct((tokens // S5_CHUNK, s5_w), F32),
                   jax.ShapeDtypeStruct((tokens, 768), BF16),
                   jax.ShapeDtypeStruct((tokens, LANES), F32),
                   jax.ShapeDtypeStruct((tokens, 768), BF16),
                   jax.ShapeDtypeStruct((bsz, 4, seq // 4, 768), BF16),
                   jax.ShapeDtypeStruct((bsz, 16, seq // 16, 768), BF16)],
        scratch_shapes=[pltpu.VMEM((tm, D_MODEL), F32), pltpu.VMEM((8, tm, LANES), F32)],
        compiler_params=_cparams(("parallel",)),
        name="ffn_proj",
    )(x2, n1, w1, w2, nm, wp, qn, kn, cos_t, sin_t)


def _merge_ffn_kernel(h_ref, ysb_ref, ys5_ref, ygla_ref, ydsa_ref,
                      nm_ref, wg_ref, wglu_ref, wb_ref, wo_ref, n2_ref, w1_ref, w2_ref,
                      out_ref, acc_ref, slab_ref):
    tm = h_ref.shape[0]
    proj = {n: _dot(ref[...], wb_ref[n]) for n, ref in ((0, ysb_ref), (2, ygla_ref), (3, ydsa_ref))}

    for j in range(S5_CHUNK):
        for s in range(2):
            col = j * 256 + s * LANES
            slab_ref[s, pl.ds(j, tm // S5_CHUNK, stride=S5_CHUNK), :] = ys5_ref[:, col:col + LANES].astype(F32)
    y_s5 = jnp.concatenate([slab_ref[0], slab_ref[1]], axis=-1).astype(BF16)
    glu = _dot(y_s5, wglu_ref[...])
    y_s5 = glu[:, :BRANCH_WIDTH] * jax.nn.sigmoid(glu[:, BRANCH_WIDTH:])
    proj[1] = _dot(y_s5.astype(BF16), wb_ref[1])

    h = h_ref[...]
    ub = _rms_rows(h, nm_ref[...]).astype(BF16)
    mixed = jnp.zeros(h.shape, F32)
    for n in range(N_BRANCH):
        gate = jax.nn.sigmoid(_dot(ub, wg_ref[:, n * D_MODEL:(n + 1) * D_MODEL]))
        mixed = mixed + gate * proj[n]
    h2 = h + _dot(mixed.astype(BF16), wo_ref[...])

    xb = _rms_rows(h2, n2_ref[...]).astype(BF16)
    _swiglu_into(acc_ref, xb, w1_ref, w2_ref)
    out_ref[...] = h2 + 0.5 * acc_ref[...]


def _merge_ffn(h, ysb, ys5, ygla, ydsa, nm, wg, wglu, wb, wo, n2, w1, w2, tm, layer):
    tokens = h.shape[0]
    row = lambda i: (i, 0)
    tile = lambda w: pl.BlockSpec((tm, w), row)
    consts = (nm, wg, wglu, wb, wo, n2, w1, w2)
    return pl.pallas_call(
        _merge_ffn_kernel,
        grid=(tokens // tm,),
        in_specs=[tile(D_MODEL), tile(BRANCH_WIDTH), pl.BlockSpec((tm // S5_CHUNK, S5_CHUNK * 256), row),
                  tile(BRANCH_WIDTH), tile(BRANCH_WIDTH)]
                 + [_const_spec(c.shape, None if c.ndim == 2 and c.shape[0] == 1 else layer) for c in consts],
        out_specs=tile(D_MODEL),
        out_shape=jax.ShapeDtypeStruct((tokens, D_MODEL), F32),
        scratch_shapes=[pltpu.VMEM((tm, D_MODEL), F32), pltpu.VMEM((2, tm, LANES), F32)],
        compiler_params=_cparams(("parallel",)),
        name="merge_ffn",
    )(h, ysb, ys5, ygla, ydsa, *consts)


def _later_key_matrix(n):
    j = lax.broadcasted_iota(jnp.int32, (n, n), 0)
    s = lax.broadcasted_iota(jnp.int32, (n, n), 1)
    return jnp.where(j > s, 1.0, 0.0).astype(BF16)


def _sb_scores(qm, k, keep=None):
    z = _dot_nt(qm, k)
    if keep is not None:
        z = jnp.where(keep, z, NEG_BIG)
    lk = -(jnp.maximum(z, 0.0) + jnp.log(1.0 + jnp.exp(-jnp.abs(z))))
    return z, lk


def _sb_kernel(q_ref, k_ref, v_ref, o_ref):
    seq = q_ref.shape[1]
    u_win = _later_key_matrix(SB_WIN)
    u_blk = _later_key_matrix(SB_Q)
    lane_q = lax.broadcasted_iota(jnp.int32, (SB_Q, LANES), 1)
    n_streams = 2 * SB_GROUP
    key_minus_query = (lax.broadcasted_iota(jnp.int32, (2 * SB_Q, SB_WIN), 1)
                       - lax.broadcasted_iota(jnp.int32, (2 * SB_Q, SB_WIN), 0) % SB_Q)

    def q_group(qg, carry):
        starts, qms, zs, lks, vws = [], [], [], [], []
        for sub in range(SB_GROUP):
            t0 = pl.multiple_of((qg * SB_GROUP + sub) * SB_Q, SB_Q)
            start = pl.multiple_of(jnp.maximum(t0 - (SB_WIN - SB_Q), 0), SB_Q)
            q = q_ref[0, pl.ds(t0, SB_Q), :] * (HEAD_DIM ** -0.5)
            kw = k_ref[0, pl.ds(start, SB_WIN), :]
            vws.append(v_ref[0, pl.ds(start, SB_WIN), :])
            strict = key_minus_query < (t0 - start)
            starts.append(start)
            qm2 = jnp.concatenate(
                [jnp.where((lane_q // HEAD_DIM) == head, q, jnp.zeros_like(q)) for head in range(2)], axis=0)
            z, lk = _sb_scores(qm2, kw, strict)
            qms += [qm2[:SB_Q], qm2[SB_Q:]]
            zs.append(z)
            lks.append(lk)
        lk_all = jnp.concatenate(lks, axis=0)
        later_all = _dot(lk_all.astype(BF16), u_win)
        csums, outs = [], []
        for sub in range(SB_GROUP):
            rows = slice(sub * 2 * SB_Q, (sub + 1) * 2 * SB_Q)
            later = later_all[rows]
            w = jnp.exp(zs[sub] + lks[sub] + later)
            o2 = _dot(w.astype(BF16), vws[sub])
            c2 = later[:, 0:1] + lks[sub][:, 0:1]
            outs += [o2[:SB_Q], o2[SB_Q:]]
            csums += [c2[:SB_Q], c2[SB_Q:]]

        def cond(state):
            step, cmaxes, _, _ = state
            flags = [jnp.logical_and(starts[s // 2] - (step + 1) * SB_Q >= 0, cmaxes[s] > SB_LOG_ZERO)
                     for s in range(n_streams)]
            return functools.reduce(jnp.logical_or, flags)

        def body(state):
            step, _, csums, outs = state
            new_c, new_o = [], []
            for s in range(n_streams):
                nxt = starts[s // 2] - (step + 1) * SB_Q
                in_range = nxt >= 0
                p0 = pl.multiple_of(jnp.maximum(nxt, 0), SB_Q)
                kb = k_ref[0, pl.ds(p0, SB_Q), :]
                vb = v_ref[0, pl.ds(p0, SB_Q), :]
                z, lk = _sb_scores(qms[s], kb)
                later = _dot(lk.astype(BF16), u_blk)
                w = jnp.where(in_range, jnp.exp(z + lk + later + csums[s]), 0.0)
                new_o.append(outs[s] + _dot(w.astype(BF16), vb))
                new_c.append(csums[s] + jnp.where(in_range, later[:, 0:1] + lk[:, 0:1], 0.0))
            return step + 1, tuple(jnp.max(c) for c in new_c), tuple(new_c), tuple(new_o)

        init = (jnp.int32(0), tuple(jnp.max(c) for c in csums), tuple(csums), tuple(outs))
        _, _, _, outs = lax.while_loop(cond, body, init)
        for sub in range(SB_GROUP):
            t0 = pl.multiple_of((qg * SB_GROUP + sub) * SB_Q, SB_Q)
            o_ref[0, pl.ds(t0, SB_Q), :] = jnp.where(
                lane_q < HEAD_DIM, outs[2 * sub], outs[2 * sub + 1]).astype(o_ref.dtype)
        return carry

    lax.fori_loop(0, seq // (SB_Q * SB_GROUP), q_group, 0)


def _stick_breaking(sb3):
    bsz, seq, _ = sb3.shape
    spec = lambda off: pl.BlockSpec((1, seq, LANES), lambda b, hp: (b, 0, off + hp))
    return pl.pallas_call(
        _sb_kernel,
        grid=(bsz, 2),
        in_specs=[spec(0), spec(2), spec(4)],
        out_specs=pl.BlockSpec((1, seq, LANES), lambda b, hp: (b, 0, hp)),
        out_shape=jax.ShapeDtypeStruct((bsz, seq, BRANCH_WIDTH), BF16),
        compiler_params=_cparams(("parallel", "parallel")),
        name="stick_breaking",
    )(sb3, sb3, sb3)


def _s5_kernel(u_ref, m_ref, e_ref, g_ref, pw_ref, d_ref, y_ref, state_ref):
    half = S5_GROUPS * S5_STATE

    @pl.when(pl.program_id(1) == 0)
    def _():
        state_ref[...] = jnp.zeros_like(state_ref)

    u = u_ref[0]
    ub = u.astype(BF16)
    rows = u.shape[0]

    def cmul(ar, ai, xr, xi):
        return ar * xr - ai * xi, ar * xi + ai * xr

    w = _dot(ub, e_ref[...])
    wr, wi = w[:, :half], w[:, half:]
    row = lax.broadcasted_iota(jnp.int32, (rows, half), 0)
    a1r, a1i = pw_ref[0:1, :half], pw_ref[0:1, half:]
    sr, si = state_ref[0:1, :half], state_ref[0:1, half:]
    cr, ci = cmul(a1r, a1i, sr, si)
    wr = jnp.where(row == 0, wr + cr, wr)
    wi = jnp.where(row == 0, wi + ci, wi)
    step = 0
    d = 1
    while d < rows:
        ar, ai = pw_ref[step:step + 1, :half], pw_ref[step:step + 1, half:]
        pr = jnp.where(row >= d, pltpu.roll(wr, d, axis=0), 0.0)
        pi = jnp.where(row >= d, pltpu.roll(wi, d, axis=0), 0.0)
        mr, mi = cmul(ar, ai, pr, pi)
        wr, wi = wr + mr, wi + mi
        d *= 2
        step += 1
    xr = jnp.where(row == 0, sr, pltpu.roll(wr, 1, axis=0))
    xi = jnp.where(row == 0, si, pltpu.roll(wi, 1, axis=0))
    state_ref[0:1, :half] = wr[rows - 1:rows, :]
    state_ref[0:1, half:] = wi[rows - 1:rows, :]

    n_c = S5_GROUPS * S5_GROUP_CH
    slots = [ub[:, j * n_c:(j + 1) * n_c] for j in range(S5_CHUNK)]
    cols = []
    for i in range(S5_CHUNK):
        acc = _dot(slots[0], m_ref[i])
        for j in range(1, i + 1):
            acc = acc + _dot(slots[j], m_ref[i - j])
        cols.append(acc)
    y = jnp.concatenate(cols, axis=1)
    y = y + _dot(xr.astype(BF16), g_ref[:half, :]) + _dot(xi.astype(BF16), g_ref[half:, :])
    y = y + d_ref[...] * u
    y_ref[0] = jax.nn.gelu(y).astype(y_ref.dtype)


def _s5(u3, m, e, g, pw, dskip, layer):
    bsz, rows, width = u3.shape
    tr = min(S5_ROW_TILE, rows)
    blk = pl.BlockSpec((1, tr, width), lambda b, i: (b, i, 0))
    return pl.pallas_call(
        _s5_kernel,
        grid=(bsz, rows // tr),
        in_specs=[blk] + [_const_spec(a.shape, layer) for a in (m, e, g, pw, dskip)],
        out_specs=blk,
        out_shape=jax.ShapeDtypeStruct(u3.shape, BF16),
        scratch_shapes=[pltpu.VMEM((8, width), F32)],
        compiler_params=_cparams(("parallel", "arbitrary")),
        name="s5",
    )(u3, m, e, g, pw, dskip)


def _s5_operators(lam_re, lam_im, log_step, b_re, b_im, c_re, c_im, d_skip, row_tile):
    n_g, n_p, n_h, L = S5_GROUPS, S5_STATE, S5_GROUP_CH, S5_CHUNK
    n_c, n_s = n_g * n_h, n_g * n_p
    dt = jnp.exp(log_step)[:, None]

    def lam_bar_pow(k):
        mag = jnp.exp(k[:, None, None] * (lam_re * dt)[None])
        ang = k[:, None, None] * (lam_im * dt)[None]
        return mag * jnp.cos(ang), mag * jnp.sin(ang)

    pr, pi = lam_bar_pow(jnp.arange(L + 1, dtype=F32))
    den = lam_re * lam_re + lam_im * lam_im
    fr = ((pr[1] - 1.0) * lam_re + pi[1] * lam_im) / den
    fi = (pi[1] * lam_re - (pr[1] - 1.0) * lam_im) / den
    bbr = fr[..., None] * b_re - fi[..., None] * b_im
    bbi = fr[..., None] * b_im + fi[..., None] * b_re
    tr = pr[:L, :, :, None] * bbr[None] - pi[:L, :, :, None] * bbi[None]
    ti = pr[:L, :, :, None] * bbi[None] + pi[:L, :, :, None] * bbr[None]

    def expand(small, rows_per_group, cols_per_group):
        rows = small.shape[-2]
        src = lax.broadcasted_iota(jnp.int32, (cols_per_group, n_g * cols_per_group), 0)
        dst = lax.broadcasted_iota(jnp.int32, (cols_per_group, n_g * cols_per_group), 1) % cols_per_group
        spread = jnp.where(src == dst, 1.0, 0.0).astype(BF16)
        tiled = jnp.einsum('...rc,cq->...rq', small.astype(BF16), spread, preferred_element_type=F32)
        r = lax.broadcasted_iota(jnp.int32, (rows, n_g * cols_per_group), 0) // rows_per_group
        c = lax.broadcasted_iota(jnp.int32, (rows, n_g * cols_per_group), 1) // cols_per_group
        return jnp.where(r == c, tiled, 0.0)

    hp = lax.Precision.HIGHEST
    kern = (jnp.einsum('ghp,kgpi->kghi', c_re, tr, precision=hp)
            - jnp.einsum('ghp,kgpi->kghi', c_im, ti, precision=hp))
    m_op = expand(kern.transpose(0, 1, 3, 2).reshape(L, n_c, n_h), n_h, n_h).astype(BF16)

    e_r = expand(tr[::-1].transpose(0, 1, 3, 2).reshape(L, n_c, n_p), n_h, n_p)
    e_i = expand(ti[::-1].transpose(0, 1, 3, 2).reshape(L, n_c, n_p), n_h, n_p)
    e_op = jnp.concatenate([e_r, e_i], axis=2).astype(BF16).reshape(L * n_c, 2 * n_s)

    clr = c_re[None] * pr[1:, :, None, :] - c_im[None] * pi[1:, :, None, :]
    cli = c_re[None] * pi[1:, :, None, :] + c_im[None] * pr[1:, :, None, :]
    g_r = expand(clr.transpose(0, 1, 3, 2).reshape(L, n_s, n_h), n_p, n_h)
    g_i = expand(cli.transpose(0, 1, 3, 2).reshape(L, n_s, n_h), n_p, n_h)
    g_blocks = jnp.concatenate([g_r, -g_i], axis=1).astype(BF16)
    g_op = jnp.concatenate([g_blocks[i] for i in range(L)], axis=1)

    n_steps = max(1, int(math.log2(row_tile)))
    sr, si = lam_bar_pow(L * 2.0 ** jnp.arange(n_steps, dtype=F32))
    pw = jnp.concatenate([sr.reshape(n_steps, n_s), si.reshape(n_steps, n_s)], axis=1)
    pw = jnp.pad(pw, ((0, (-n_steps) % 8), (0, 0)))
    dskip = jnp.tile(d_skip.reshape(1, n_c), (1, L))
    return m_op, e_op, g_op, pw.astype(F32), dskip.astype(F32)


def _gla_constants():
    n = GLA_SUB
    ri = lax.broadcasted_iota(jnp.int32, (n, n), 0)
    ci = lax.broadcasted_iota(jnp.int32, (n, n), 1)
    same = (ri // GLA_BLOCK) == (ci // GLA_BLOCK)
    mats = [same & (ci <= ri), same & (ci > ri)]
    masks = [ri == ci]
    ups, lows = [], []
    for lev in range(GLA_LEVELS):
        half = 1 << lev
        blk = 2 * half
        mid = (ri // blk) * blk + (half - 1)
        later = (ri % blk) >= half
        ups.append(later & (ci > mid) & (ci <= ri))
        lows.append(jnp.logical_not(later) & (ci > ri) & (ci <= mid))
        masks.append(((ri // blk) == (ci // blk)) & later & ((ci % blk) < half))
    stacked = jnp.concatenate([jnp.where(m, 1.0, 0.0).astype(BF16) for m in mats + ups + lows], axis=0)
    return stacked, masks


def _gla_tile(x, gg, state, wgate, bgate, norm, stacked, masks):
    tile = x.shape[0]
    hk = GLA_HEADS * GLA_DK
    hv = GLA_HEADS * GLA_DV
    q = x[:, 0:hk].astype(F32) * (GLA_DK ** -0.5)
    k = x[:, hk:2 * hk].astype(F32)
    vb = x[:, 2 * hk:2 * hk + hv]
    r = x[:, 2 * hk + hv:].astype(F32)

    z = _dot(gg.astype(BF16), wgate) + bgate
    g = -(jnp.maximum(-z, 0.0) + jnp.log(1.0 + jnp.exp(-jnp.abs(z)))) * (1.0 / GLA_GATE_NORM)
    g_hi, g_lo = _hi_lo(g)
    g2 = jnp.concatenate([g_hi, g_lo], axis=1)

    lane_k = lax.broadcasted_iota(jnp.int32, (GLA_SUB, hk), 1) // GLA_DK
    row_blk = lax.broadcasted_iota(jnp.int32, (GLA_SUB, hk), 0) // GLA_BLOCK
    lane_v = lax.broadcasted_iota(jnp.int32, (GLA_SUB, hv), 1) // GLA_DV
    rs = lax.broadcasted_iota(jnp.int32, (hv, hk), 0) // GLA_DV
    cs = lax.broadcasted_iota(jnp.int32, (hv, hk), 1) // GLA_DK
    head_diag = rs == cs
    outs = []
    for st in range(tile // GLA_SUB):
        rows = slice(st * GLA_SUB, (st + 1) * GLA_SUB)
        sums = _dot(stacked, g2[rows])
        sums = sums[:, :hk] + sums[:, hk:]
        part = lambda i: sums[i * GLA_SUB:(i + 1) * GLA_SUB]
        b, tail = part(0), part(1)
        qs, ks, vs = q[rows], k[rows], vb[rows]

        att = [jnp.zeros((GLA_SUB, GLA_SUB), F32) for _ in range(GLA_HEADS)]
        for lev in range(-1, GLA_LEVELS):
            if lev < 0:
                qt, kt = qs.astype(BF16), ks.astype(BF16)
            else:
                qt = (qs * jnp.exp(part(2 + lev))).astype(BF16)
                kt = (ks * jnp.exp(part(2 + GLA_LEVELS + lev))).astype(BF16)
            q_heads = jnp.concatenate(
                [jnp.where(lane_k == h, qt, jnp.zeros_like(qt)) for h in range(GLA_HEADS)], axis=0)
            a = _dot_nt(q_heads, kt)
            for h in range(GLA_HEADS):
                att[h] = jnp.where(masks[lev + 1], a[h * GLA_SUB:(h + 1) * GLA_SUB], att[h])
        att_all = jnp.concatenate([a.astype(BF16) for a in att], axis=1)
        v_heads = jnp.concatenate([jnp.where(lane_v == h, vs, jnp.zeros_like(vs)) for h in range(GLA_HEADS)], axis=0)
        o = _dot(att_all, v_heads)

        q_dec = (qs * jnp.exp(b)).astype(BF16)
        k_dec = (ks * jnp.exp(tail)).astype(BF16)
        b_end = b + tail
        n_blk = GLA_SUB // GLA_BLOCK
        k_blocks = jnp.concatenate(
            [jnp.where(row_blk == n, k_dec, jnp.zeros_like(k_dec)) for n in range(n_blk)], axis=1)
        upd_all = _dot_tn(vs, k_blocks)
        cross = []
        for blk in range(n_blk):
            r0 = blk * GLA_BLOCK
            cross.append(_dot_nt(q_dec[r0:r0 + GLA_BLOCK], state.astype(BF16)))
            upd = upd_all[:, blk * hk:(blk + 1) * hk]
            state = jnp.exp(b_end[r0:r0 + 1, :]) * state + jnp.where(head_diag, upd, 0.0)
        outs.append(o + jnp.concatenate(cross, axis=0))
    o = jnp.concatenate(outs, axis=0)

    o = _group_rms(o, norm)
    return o * (r * jax.nn.sigmoid(r)), state


def _gla_kernel(x_ref, gg_ref, wgate_ref, bgate_ref, norm_ref, y_ref, state_ref):
    @pl.when(pl.program_id(0) == 0)
    def _():
        state_ref[...] = jnp.zeros_like(state_ref)

    stacked, masks = _gla_constants()
    for bi in range(x_ref.shape[0]):
        y, state = _gla_tile(x_ref[bi], gg_ref[bi], state_ref[bi], wgate_ref[...], bgate_ref[...],
                             norm_ref[...], stacked, masks)
        y_ref[bi] = y.astype(y_ref.dtype)
        state_ref[bi] = state


def _gla(x3, gg3, wgate, bgate, norm):
    bsz, seq, _ = x3.shape
    tile = min(GLA_TILE, seq)
    return pl.pallas_call(
        _gla_kernel,
        grid=(seq // tile,),
        in_specs=[pl.BlockSpec((bsz, tile, 768), lambda i: (0, i, 0)),
                  pl.BlockSpec((bsz, tile, LANES), lambda i: (0, i, 0)),
                  _const_spec(wgate.shape), _const_spec(bgate.shape), _const_spec(norm.shape)],
        out_specs=pl.BlockSpec((bsz, tile, BRANCH_WIDTH), lambda i: (0, i, 0)),
        out_shape=jax.ShapeDtypeStruct((bsz, seq, BRANCH_WIDTH), BF16),
        scratch_shapes=[pltpu.VMEM((bsz, GLA_HEADS * GLA_DV, GLA_HEADS * GLA_DK), F32)],
        compiler_params=_cparams(("arbitrary",)),
        name="gla",
    )(x3, gg3, wgate, bgate, norm)


def _dsa_band_bias(offset, n_steps):
    row = lax.broadcasted_iota(jnp.int32, (2 * DSA_BLOCK, 2 * DSA_BLOCK), 0) % DSA_BLOCK
    col = lax.broadcasted_iota(jnp.int32, (2 * DSA_BLOCK, 2 * DSA_BLOCK), 1)
    dist = offset + row - col
    return jnp.where((dist >= 0) & (dist <= n_steps), 0.0, NEG_BIG)


def _dsa_attend(q, kw, vw, bias, shift):
    lane_q = lax.broadcasted_iota(jnp.int32, (DSA_BLOCK, LANES), 1)
    q = q * (HEAD_DIM ** -0.5)
    qm2 = jnp.concatenate(
        [jnp.where((lane_q // HEAD_DIM) == head, q, jnp.zeros_like(q)) for head in range(2)], axis=0)
    s = _dot_nt(qm2, kw) + bias
    if shift is None:
        m = jnp.max(s, axis=1, keepdims=True)
        p = jnp.exp(s - m)
        l = jnp.sum(p, axis=1, keepdims=True)
        o2 = _dot(p.astype(BF16), vw) / l
        lse2 = jnp.broadcast_to(m + jnp.log(l), (2 * DSA_BLOCK, LANES))
    else:
        p = jnp.exp(s - shift)
        pv = _dot(p.astype(BF16), jnp.concatenate([vw, jnp.ones_like(vw)], axis=1))
        l = pv[:, LANES:]
        o2 = pv[:, :LANES] / l
        lse2 = shift + jnp.log(l)
    first = lane_q < HEAD_DIM
    return (jnp.where(first, o2[:DSA_BLOCK], o2[DSA_BLOCK:]),
            jnp.where(first, lse2[:DSA_BLOCK], lse2[DSA_BLOCK:]))


def _dsa_kernel(bound_ref, q1_ref, k1_ref, v1_ref, q4_ref, k4_ref, v4_ref, q16_ref, k16_ref, v16_ref,
                y_ref, o_scr, l_scr):
    seq = q1_ref.shape[1]
    refs = ((q1_ref, k1_ref, v1_ref), (q4_ref, k4_ref, v4_ref), (q16_ref, k16_ref, v16_ref))
    blocks_per_span = DSA_SPAN // DSA_BLOCK
    n_steps = DSA_PATTERNS[0][0] // DSA_PATTERNS[0][1]
    assert all(w // d == n_steps for w, d in DSA_PATTERNS)
    bias_inner = _dsa_band_bias(DSA_BLOCK, n_steps)
    bias_first = _dsa_band_bias(0, n_steps)

    def run(shift):
        def span(sp, carry):
            for p, (_, dil) in enumerate(DSA_PATTERNS):
                q_ref, k_ref, v_ref = refs[p]
                per_res = blocks_per_span // dil

                def unit(idx, c, p=p, dil=dil, per_res=per_res, q_ref=q_ref, k_ref=k_ref, v_ref=v_ref):
                    r = idx // per_res
                    jb = idx % per_res
                    m0 = pl.multiple_of(sp * (DSA_SPAN // dil) + jb * DSA_BLOCK, DSA_BLOCK)
                    start = pl.multiple_of(jnp.maximum(m0 - DSA_BLOCK, 0), DSA_BLOCK)
                    if dil == 1:
                        q = q_ref[0, pl.ds(m0, DSA_BLOCK), :]
                        kw = k_ref[0, pl.ds(start, 2 * DSA_BLOCK), :]
                        vw = v_ref[0, pl.ds(start, 2 * DSA_BLOCK), :]
                    else:
                        q = q_ref[0, r, pl.ds(m0, DSA_BLOCK), :]
                        kw = k_ref[0, r, pl.ds(start, 2 * DSA_BLOCK), :]
                        vw = v_ref[0, r, pl.ds(start, 2 * DSA_BLOCK), :]
                    o, lse = _dsa_attend(q, kw, vw, jnp.where(m0 == 0, bias_first, bias_inner), shift)
                    rows = pl.ds(r + dil * jb * DSA_BLOCK, DSA_BLOCK, stride=dil) if dil > 1 else \
                        pl.ds(pl.multiple_of(jb * DSA_BLOCK, DSA_BLOCK), DSA_BLOCK)
                    o_scr[p, rows, :] = o
                    l_scr[p, rows, :] = lse
                    return c

                lax.fori_loop(0, blocks_per_span, unit, 0, unroll=DSA_UNROLL)

            l1, l2, l3 = l_scr[0], l_scr[1], l_scr[2]
            lm = jnp.maximum(jnp.maximum(l1, l2), l3)
            e1, e2, e3 = jnp.exp(l1 - lm), jnp.exp(l2 - lm), jnp.exp(l3 - lm)
            y = (e1 * o_scr[0] + e2 * o_scr[1] + e3 * o_scr[2]) / (e1 + e2 + e3)
            y_ref[0, pl.ds(pl.multiple_of(sp * DSA_SPAN, DSA_SPAN), DSA_SPAN), :] = y.astype(y_ref.dtype)
            return carry

        lax.fori_loop(0, seq // DSA_SPAN, span, 0)

    bound = bound_ref[0]

    @pl.when(bound <= DSA_MAX_SHIFT)
    def _():
        run(bound)

    @pl.when(bound > DSA_MAX_SHIFT)
    def _():
        run(None)


def _dsa(dsa1, dsa4, dsa16, bound):
    bsz, seq, _ = dsa1.shape
    nat = lambda off: pl.BlockSpec((1, seq, LANES), lambda b, hp: (b, 0, off + hp))
    res = lambda dil, off: pl.BlockSpec((1, dil, seq // dil, LANES), lambda b, hp: (b, 0, 0, off + hp))
    return pl.pallas_call(
        _dsa_kernel,
        grid=(bsz, 2),
        in_specs=[pl.BlockSpec(memory_space=pltpu.SMEM),
                  nat(0), nat(2), nat(4), res(4, 0), res(4, 2), res(4, 4), res(16, 0), res(16, 2), res(16, 4)],
        out_specs=pl.BlockSpec((1, seq, LANES), lambda b, hp: (b, 0, hp)),
        out_shape=jax.ShapeDtypeStruct((bsz, seq, BRANCH_WIDTH), BF16),
        scratch_shapes=[pltpu.VMEM((3, DSA_SPAN, LANES), F32), pltpu.VMEM((3, DSA_SPAN, LANES), F32)],
        compiler_params=_cparams(("parallel", "parallel")),
        name="dsa",
    )(bound, dsa1, dsa1, dsa1, dsa4, dsa4, dsa4, dsa16, dsa16, dsa16)


def _proj_weights(w_in):
    lead = w_in[..., :1792].astype(BF16)
    gg = jnp.pad(w_in[..., 1792:1808].astype(BF16), ((0, 0), (0, 0), (0, LANES - GLA_GATE_RANK)))
    dsa = w_in[..., 1808:2576].astype(BF16)
    wp = jnp.concatenate([lead, dsa, gg], axis=-1)
    return wp, w_in[..., 2576:].astype(BF16)


def _rope_tables(seq):
    inv = ROPE_THETA ** (-jnp.arange(0, HEAD_DIM, 2, dtype=F32) / HEAD_DIM)
    ang = jnp.arange(seq, dtype=F32)[:, None] * inv[None, :]
    cos, sin = jnp.cos(ang), jnp.sin(ang)
    cos_t = jnp.tile(jnp.concatenate([cos, cos], axis=1), (1, 2))
    sin_t = jnp.tile(jnp.concatenate([-sin, sin], axis=1), (1, 2))
    return cos_t, sin_t


def _pick_tile(n, pref):
    t = min(pref, n)
    while n % t:
        t //= 2
    return t


def kernel(x, ffn1_norm, ffn1_w_in, ffn1_w_out, mix_norm, w_in, s5_lam_re, s5_lam_im, s5_log_step, s5_b_re, s5_b_im, s5_c_re, s5_c_im, s5_d, s5_w_glu, gla_w_gate, gla_b_gate, gla_norm, dsa_q_norm, dsa_k_norm, w_branch, w_out, ffn2_norm, ffn2_w_in, ffn2_w_out):
    bsz, seq, _ = x.shape
    tokens = bsz * seq
    depth = ffn1_norm.shape[0]
    assert seq % DSA_SPAN == 0 and seq // DSA_PATTERNS[-1][1] >= 2 * DSA_BLOCK and seq >= SB_WIN
    tm_a = _pick_tile(seq, 512)
    tm_b = _pick_tile(tokens, 512)
    s5_rows = seq // S5_CHUNK
    s5_tile = min(S5_ROW_TILE, s5_rows)
    cos_t, sin_t = _rope_tables(seq)

    wp_all, wg_all = _proj_weights(w_in)
    w1a_all, w2a_all = ffn1_w_in.astype(BF16), ffn1_w_out.astype(BF16)
    w1b_all, w2b_all = ffn2_w_in.astype(BF16), ffn2_w_out.astype(BF16)
    wglu_all, wb_all, wo_all = s5_w_glu.astype(BF16), w_branch.astype(BF16), w_out.astype(BF16)
    s5_ops = jax.vmap(functools.partial(_s5_operators, row_tile=s5_tile))(
        s5_lam_re, s5_lam_im, s5_log_step, s5_b_re, s5_b_im, s5_c_re, s5_c_im, s5_d)

    h = x.reshape(tokens, D_MODEL)
    for l in range(depth):
        row = lambda p: p[l].reshape(1, -1)
        qn = jnp.tile(row(dsa_q_norm), (1, 4))
        kn = jnp.tile(row(dsa_k_norm), (1, 4))
        h, sb, s5u, gla_in, gg, dsa1, dsa4, dsa16 = _ffn_proj(
            h, row(ffn1_norm), w1a_all, w2a_all, row(mix_norm), wp_all, qn, kn, cos_t, sin_t, bsz, seq, tm_a, l)

        y_sb = _stick_breaking(sb.reshape(bsz, seq, 768)).reshape(tokens, BRANCH_WIDTH)

        y_s5 = _s5(s5u.reshape(bsz, s5_rows, S5_CHUNK * 256), *s5_ops, l).reshape(tokens // S5_CHUNK, S5_CHUNK * 256)

        wgate = jnp.pad(gla_w_gate[l], ((0, LANES - GLA_GATE_RANK), (0, 0))).astype(BF16)
        y_gla = _gla(gla_in.reshape(bsz, seq, 768), gg.reshape(bsz, seq, LANES), wgate,
                     row(gla_b_gate), row(gla_norm)).reshape(tokens, BRANCH_WIDTH)

        bound = (1.02 * HEAD_DIM ** 0.5 * jnp.max(jnp.abs(dsa_q_norm[l])) * jnp.max(jnp.abs(dsa_k_norm[l]))).reshape(1)
        y_dsa = _dsa(dsa1.reshape(bsz, seq, 768), dsa4, dsa16, bound).reshape(tokens, BRANCH_WIDTH)

        h = _merge_ffn(h, y_sb, y_s5, y_gla, y_dsa, row(mix_norm), wg_all, wglu_all, wb_all, wo_all,
                       row(ffn2_norm), w1b_all, w2b_all, tm_b, l)
    return h.reshape(bsz, seq, D_MODEL)
```

```python
import functools
import math

import jax
import jax.numpy as jnp
from jax import lax
from jax.experimental import pallas as pl
from jax.experimental.pallas import tpu as pltpu

D_MODEL = 1024
HEAD_DIM = 64
BRANCH_WIDTH = 256
N_BRANCH = 4
SB_HEADS = 4
S5_GROUPS = 16
S5_GROUP_CH = 16
S5_STATE = 64
GLA_HEADS = 4
GLA_DK = 32
GLA_DV = 64
GLA_GATE_RANK = 16
GLA_GATE_NORM = 16.0
DSA_PATTERNS = ((128, 1), (512, 4), (2048, 16))
DSA_BLOCK = 128
ROPE_THETA = 10000.0
D_FF = 2816
RMS_EPS = 1e-6

LANES = 128
VMEM_LIMIT_BYTES = 58 * 1024 * 1024
FF_CHUNK = 256
N_FF_CHUNKS = D_FF // FF_CHUNK
S5_CHUNK = 8
S5_ROW_TILE = 256
GLA_BLOCK = 64
GLA_LEVELS = 6
GLA_SUB = 128
SB_Q = 64
SB_WIN = 4 * SB_Q
SB_GROUP = 8
SB_LOG_ZERO = -104.0
NEG_BIG = -1e30
DSA_UNROLL = 8
DSA_SPAN = DSA_PATTERNS[-1][1] * DSA_BLOCK
DSA_MAX_SHIFT = 30.0

PROJ_SB = (0, 768)
PROJ_S5 = (768, 1024)
PROJ_GLA = (1024, 1792)
PROJ_DSA = (1792, 2560)
PROJ_GG = (2560, 2688)
PROJ_WIDTH = 2688

F32 = jnp.float32
BF16 = jnp.bfloat16


def _cparams(sem):
    return pltpu.CompilerParams(dimension_semantics=sem, vmem_limit_bytes=VMEM_LIMIT_BYTES)


def _const_spec(shape, layer=None):
    if layer is None:
        zeros = (0,) * len(shape)
        return pl.BlockSpec(shape, lambda *_: zeros, pipeline_mode=pl.Buffered(1))
    index = (layer,) + (0,) * (len(shape) - 1)
    return pl.BlockSpec((None,) + tuple(shape[1:]), lambda *_: index, pipeline_mode=pl.Buffered(1))


def _rms_rows(x, g):
    ms = jnp.mean(x * x, axis=-1, keepdims=True)
    return x * lax.rsqrt(ms + RMS_EPS) * g


def _dot(a, b):
    return jnp.dot(a, b, preferred_element_type=F32)


def _dot_nt(a, b):
    return lax.dot_general(a, b, (((1,), (1,)), ((), ())), preferred_element_type=F32)


def _dot_tn(a, b):
    return lax.dot_general(a, b, (((0,), (0,)), ((), ())), preferred_element_type=F32)


def _swiglu_into(acc_ref, xb, w1_ref, w2_ref, between=None):
    for c in range(N_FF_CHUNKS):
        lo, hi = c * FF_CHUNK, (c + 1) * FF_CHUNK
        a = _dot(xb, w1_ref[:, lo:hi])
        b = _dot(xb, w1_ref[:, D_FF + lo:D_FF + hi])
        hm = (a * jax.nn.sigmoid(a) * b).astype(BF16)
        part = _dot(hm, w2_ref[lo:hi, :])
        if c == 0:
            acc_ref[...] = part
        else:
            acc_ref[...] += part
        if between is not None:
            next(between, None)


def _group_mean_matrix(width, group):
    r = lax.broadcasted_iota(jnp.int32, (width, width), 0) // group
    c = lax.broadcasted_iota(jnp.int32, (width, width), 1) // group
    return jnp.where(r == c, 1.0 / group, 0.0).astype(BF16)


def _group_rms(x, gain):
    gm = _group_mean_matrix(x.shape[-1], HEAD_DIM)
    ms = _dot((x * x).astype(BF16), gm)
    return x * lax.rsqrt(ms + RMS_EPS) * gain


def _swap_half_heads(x):
    half = HEAD_DIM // 2
    outs = []
    for s in range(x.shape[-1] // LANES):
        xs = x[:, s * LANES:(s + 1) * LANES]
        lane = lax.broadcasted_iota(jnp.int32, xs.shape, 1)
        up = pltpu.roll(xs, LANES - half, axis=1)
        down = pltpu.roll(xs, half, axis=1)
        outs.append(jnp.where((lane % HEAD_DIM) < half, up, down))
    return jnp.concatenate(outs, axis=-1)


def _ffn_proj_kernel(x_ref, n1_ref, w1_ref, w2_ref, nm_ref, wp_ref, qn_ref, kn_ref, cos_ref, sin_ref,
                     wgate_ref, bgate_ref, gnorm_ref,
                     h_ref, sb_ref, s5_ref, ygla_ref, dsa1_ref, dsa4_ref, dsa16_ref,
                     acc_ref, slab_ref, glax_ref, glag_ref, glas_ref, *, seq_tiles):
    tm = x_ref.shape[0]
    step = pl.program_id(0)

    @pl.when(step == 0)
    def _():
        glax_ref[...] = jnp.zeros_like(glax_ref)
        glag_ref[...] = jnp.zeros_like(glag_ref)
        glas_ref[...] = jnp.zeros_like(glas_ref)

    starts_sequence = lax.rem(step - 1, seq_tiles) == 0
    state = jnp.where(starts_sequence, 0.0, glas_ref[...])
    stacked, masks = _gla_constants()
    gla = {}
    gla_stages = _gla_stages(glax_ref[...], glag_ref[...], state, wgate_ref[...], bgate_ref[...], gnorm_ref[...],
                             stacked, masks, gla)

    x = x_ref[...]
    xb = _rms_rows(x, n1_ref[...]).astype(BF16)
    _swiglu_into(acc_ref, xb, w1_ref, w2_ref, between=gla_stages)
    for _ in gla_stages:
        pass
    ygla_ref[...] = gla["y"].astype(ygla_ref.dtype)
    glas_ref[...] = gla["state"]
    h = x + 0.5 * acc_ref[...]
    h_ref[...] = h
    ub = _rms_rows(h, nm_ref[...]).astype(BF16)

    dsa = _dot(ub, wp_ref[:, PROJ_DSA[0]:PROJ_DSA[1]])
    cos = jnp.concatenate([cos_ref[...]] * 2, axis=-1)
    sin = jnp.concatenate([sin_ref[...]] * 2, axis=-1)
    parts = []
    for i, gain_ref in enumerate((qn_ref, kn_ref)):
        t = _group_rms(dsa[:, i * 256:(i + 1) * 256], gain_ref[...])
        parts.append(t * cos + _swap_half_heads(t) * sin)
    parts.append(dsa[:, 512:768])
    dsa = jnp.concatenate(parts, axis=-1)
    dsa1_ref[...] = dsa.astype(BF16)
    for s in range(6):
        slab_ref[s] = dsa[:, s * LANES:(s + 1) * LANES]
    for dil, ref in ((4, dsa4_ref), (16, dsa16_ref)):
        for r in range(dil):
            for s in range(6):
                rows = slab_ref[s, pl.ds(r, tm // dil, stride=dil), :]
                ref[0, r, :, s * LANES:(s + 1) * LANES] = rows.astype(BF16)

    s5 = _dot(ub, wp_ref[:, PROJ_S5[0]:PROJ_S5[1]])
    for s in range(2):
        slab_ref[6 + s] = s5[:, s * LANES:(s + 1) * LANES]
    for j in range(S5_CHUNK):
        for s in range(2):
            col = j * 256 + s * LANES
            s5_ref[:, col:col + LANES] = slab_ref[6 + s, pl.ds(j, tm // S5_CHUNK, stride=S5_CHUNK), :]

    sb_ref[...] = _dot(ub, wp_ref[:, PROJ_SB[0]:PROJ_SB[1]]).astype(BF16)
    glax_ref[...] = _dot(ub, wp_ref[:, PROJ_GLA[0]:PROJ_GLA[1]]).astype(BF16)
    glag_ref[...] = _dot(ub, wp_ref[:, PROJ_GG[0]:PROJ_GG[1]])


def _ffn_proj(x2, n1, w1, w2, nm, wp, qn, kn, cos_t, sin_t, wgate, bgate, gnorm, bsz, seq, tm, layer):
    tokens = x2.shape[0]
    seq_tiles = seq // tm
    n_tiles = tokens // tm
    cur = lambda i: jnp.minimum(i, n_tiles - 1)
    row = lambda i: (cur(i), 0)
    lag = lambda i: (jnp.maximum(i - 1, 0), 0)
    tab = lambda i: (cur(i) % seq_tiles, 0)
    res = lambda i: (cur(i) // seq_tiles, 0, cur(i) % seq_tiles, 0)
    tile = lambda w: pl.BlockSpec((tm, w), row)
    s5_w = S5_CHUNK * 256
    return pl.pallas_call(
        functools.partial(_ffn_proj_kernel, seq_tiles=seq_tiles),
        grid=(n_tiles + 1,),
        in_specs=[tile(D_MODEL), _const_spec(n1.shape), _const_spec(w1.shape, layer), _const_spec(w2.shape, layer),
                  _const_spec(nm.shape), _const_spec(wp.shape, layer), _const_spec(qn.shape), _const_spec(kn.shape),
                  pl.BlockSpec((tm, LANES), tab), pl.BlockSpec((tm, LANES), tab),
                  _const_spec(wgate.shape), _const_spec(bgate.shape), _const_spec(gnorm.shape)],
        out_specs=[tile(D_MODEL), tile(768), pl.BlockSpec((tm // S5_CHUNK, s5_w), row),
                   pl.BlockSpec((tm, BRANCH_WIDTH), lag),
                   tile(768), pl.BlockSpec((1, 4, tm // 4, 768), res), pl.BlockSpec((1, 16, tm // 16, 768), res)],
        out_shape=[jax.ShapeDtypeStruct((tokens, D_MODEL), F32),
                   jax.ShapeDtypeStruct((tokens, 768), BF16),
                   jax.ShapeDtypeStruct((tokens // S5_CHUNK, s5_w), F32),
                   jax.ShapeDtypeStruct((tokens, BRANCH_WIDTH), BF16),
                   jax.ShapeDtypeStruct((tokens, 768), BF16),
                   jax.ShapeDtypeStruct((bsz, 4, seq // 4, 768), BF16),
                   jax.ShapeDtypeStruct((bsz, 16, seq // 16, 768), BF16)],
        scratch_shapes=[pltpu.VMEM((tm, D_MODEL), F32), pltpu.VMEM((8, tm, LANES), F32),
                        pltpu.VMEM((tm, 768), BF16), pltpu.VMEM((tm, LANES), F32),
                        pltpu.VMEM((GLA_HEADS * GLA_DV, GLA_HEADS * GLA_DK), F32)],
        compiler_params=_cparams(("arbitrary",)),
        name="ffn_proj",
    )(x2, n1, w1, w2, nm, wp, qn, kn, cos_t, sin_t, wgate, bgate, gnorm)


def _merge_ffn_kernel(h_ref, ysb_ref, ys5_ref, ygla_ref, ydsa_ref,
                      nm_ref, wg_ref, wglu_ref, wb_ref, wo_ref, n2_ref, w1_ref, w2_ref,
                      out_ref, acc_ref, slab_ref):
    tm = h_ref.shape[0]
    proj = {n: _dot(ref[...], wb_ref[n]) for n, ref in ((0, ysb_ref), (2, ygla_ref), (3, ydsa_ref))}

    for j in range(S5_CHUNK):
        for s in range(2):
            col = j * 256 + s * LANES
            slab_ref[s, pl.ds(j, tm // S5_CHUNK, stride=S5_CHUNK), :] = ys5_ref[:, col:col + LANES].astype(F32)
    y_s5 = jnp.concatenate([slab_ref[0], slab_ref[1]], axis=-1).astype(BF16)
    glu = _dot(y_s5, wglu_ref[...])
    y_s5 = glu[:, :BRANCH_WIDTH] * jax.nn.sigmoid(glu[:, BRANCH_WIDTH:])
    proj[1] = _dot(y_s5.astype(BF16), wb_ref[1])

    h = h_ref[...]
    ub = _rms_rows(h, nm_ref[...]).astype(BF16)
    mixed = jnp.zeros(h.shape, F32)
    for n in range(N_BRANCH):
        gate = jax.nn.sigmoid(_dot(ub, wg_ref[:, n * D_MODEL:(n + 1) * D_MODEL]))
        mixed = mixed + gate * proj[n]
    h2 = h + _dot(mixed.astype(BF16), wo_ref[...])

    xb = _rms_rows(h2, n2_ref[...]).astype(BF16)
    _swiglu_into(acc_ref, xb, w1_ref, w2_ref)
    out_ref[...] = h2 + 0.5 * acc_ref[...]


def _merge_ffn(h, ysb, ys5, ygla, ydsa, nm, wg, wglu, wb, wo, n2, w1, w2, tm, layer):
    tokens = h.shape[0]
    row = lambda i: (i, 0)
    tile = lambda w: pl.BlockSpec((tm, w), row)
    consts = (nm, wg, wglu, wb, wo, n2, w1, w2)
    return pl.pallas_call(
        _merge_ffn_kernel,
        grid=(tokens // tm,),
        in_specs=[tile(D_MODEL), tile(BRANCH_WIDTH), pl.BlockSpec((tm // S5_CHUNK, S5_CHUNK * 256), row),
                  tile(BRANCH_WIDTH), tile(BRANCH_WIDTH)]
                 + [_const_spec(c.shape, None if c.ndim == 2 and c.shape[0] == 1 else layer) for c in consts],
        out_specs=tile(D_MODEL),
        out_shape=jax.ShapeDtypeStruct((tokens, D_MODEL), F32),
        scratch_shapes=[pltpu.VMEM((tm, D_MODEL), F32), pltpu.VMEM((2, tm, LANES), F32)],
        compiler_params=_cparams(("parallel",)),
        name="merge_ffn",
    )(h, ysb, ys5, ygla, ydsa, *consts)


def _later_key_matrix(n):
    j = lax.broadcasted_iota(jnp.int32, (n, n), 0)
    s = lax.broadcasted_iota(jnp.int32, (n, n), 1)
    return jnp.where(j > s, 1.0, 0.0).astype(BF16)


def _sb_scores(qm, k, keep=None):
    z = _dot_nt(qm, k)
    if keep is not None:
        z = jnp.where(keep, z, NEG_BIG)
    lk = -(jnp.maximum(z, 0.0) + jnp.log(1.0 + jnp.exp(-jnp.abs(z))))
    return z, lk


def _sb_kernel(q_ref, k_ref, v_ref, o_ref):
    seq = q_ref.shape[1]
    u_win = _later_key_matrix(SB_WIN)
    u_blk = _later_key_matrix(SB_Q)
    lane_q = lax.broadcasted_iota(jnp.int32, (SB_Q, LANES), 1)
    n_streams = 2 * SB_GROUP
    key_minus_query = (lax.broadcasted_iota(jnp.int32, (2 * SB_Q, SB_WIN), 1)
                       - lax.broadcasted_iota(jnp.int32, (2 * SB_Q, SB_WIN), 0) % SB_Q)

    def q_group(qg, carry):
        starts, qms, zs, lks, vws = [], [], [], [], []
        for sub in range(SB_GROUP):
            t0 = pl.multiple_of((qg * SB_GROUP + sub) * SB_Q, SB_Q)
            start = pl.multiple_of(jnp.maximum(t0 - (SB_WIN - SB_Q), 0), SB_Q)
            q = q_ref[0, pl.ds(t0, SB_Q), :] * (HEAD_DIM ** -0.5)
            kw = k_ref[0, pl.ds(start, SB_WIN), :]
            vws.append(v_ref[0, pl.ds(start, SB_WIN), :])
            strict = key_minus_query < (t0 - start)
            starts.append(start)
            qm2 = jnp.concatenate(
                [jnp.where((lane_q // HEAD_DIM) == head, q, jnp.zeros_like(q)) for head in range(2)], axis=0)
            z, lk = _sb_scores(qm2, kw, strict)
            qms += [qm2[:SB_Q], qm2[SB_Q:]]
            zs.append(z)
            lks.append(lk)
        lk_all = jnp.concatenate(lks, axis=0)
        later_all = _dot(lk_all.astype(BF16), u_win)
        csums, outs = [], []
        for sub in range(SB_GROUP):
            rows = slice(sub * 2 * SB_Q, (sub + 1) * 2 * SB_Q)
            later = later_all[rows]
            w = jnp.exp(zs[sub] + lks[sub] + later)
            o2 = _dot(w.astype(BF16), vws[sub])
            c2 = later[:, 0:1] + lks[sub][:, 0:1]
            outs += [o2[:SB_Q], o2[SB_Q:]]
            csums += [c2[:SB_Q], c2[SB_Q:]]

        def cond(state):
            step, cmaxes, _, _ = state
            flags = [jnp.logical_and(starts[s // 2] - (step + 1) * SB_Q >= 0, cmaxes[s] > SB_LOG_ZERO)
                     for s in range(n_streams)]
            return functools.reduce(jnp.logical_or, flags)

        def body(state):
            step, _, csums, outs = state
            new_c, new_o = [], []
            for s in range(n_streams):
                nxt = starts[s // 2] - (step + 1) * SB_Q
                in_range = nxt >= 0
                p0 = pl.multiple_of(jnp.maximum(nxt, 0), SB_Q)
                kb = k_ref[0, pl.ds(p0, SB_Q), :]
                vb = v_ref[0, pl.ds(p0, SB_Q), :]
                z, lk = _sb_scores(qms[s], kb)
                later = _dot(lk.astype(BF16), u_blk)
                w = jnp.where(in_range, jnp.exp(z + lk + later + csums[s]), 0.0)
                new_o.append(outs[s] + _dot(w.astype(BF16), vb))
                new_c.append(csums[s] + jnp.where(in_range, later[:, 0:1] + lk[:, 0:1], 0.0))
            return step + 1, tuple(jnp.max(c) for c in new_c), tuple(new_c), tuple(new_o)

        init = (jnp.int32(0), tuple(jnp.max(c) for c in csums), tuple(csums), tuple(outs))
        _, _, _, outs = lax.while_loop(cond, body, init)
        for sub in range(SB_GROUP):
            t0 = pl.multiple_of((qg * SB_GROUP + sub) * SB_Q, SB_Q)
            o_ref[0, pl.ds(t0, SB_Q), :] = jnp.where(
                lane_q < HEAD_DIM, outs[2 * sub], outs[2 * sub + 1]).astype(o_ref.dtype)
        return carry

    lax.fori_loop(0, seq // (SB_Q * SB_GROUP), q_group, 0)


def _stick_breaking(sb3):
    bsz, seq, _ = sb3.shape
    spec = lambda off: pl.BlockSpec((1, seq, LANES), lambda b, hp: (b, 0, off + hp))
    return pl.pallas_call(
        _sb_kernel,
        grid=(bsz, 2),
        in_specs=[spec(0), spec(2), spec(4)],
        out_specs=pl.BlockSpec((1, seq, LANES), lambda b, hp: (b, 0, hp)),
        out_shape=jax.ShapeDtypeStruct((bsz, seq, BRANCH_WIDTH), BF16),
        compiler_params=_cparams(("parallel", "parallel")),
        name="stick_breaking",
    )(sb3, sb3, sb3)


def _s5_kernel(u_ref, m_ref, e_ref, g_ref, pw_ref, d_ref, y_ref, state_ref):
    half = S5_GROUPS * S5_STATE

    @pl.when(pl.program_id(1) == 0)
    def _():
        state_ref[...] = jnp.zeros_like(state_ref)

    u = u_ref[0]
    ub = u.astype(BF16)
    rows = u.shape[0]

    def cmul(ar, ai, xr, xi):
        return ar * xr - ai * xi, ar * xi + ai * xr

    w = _dot(ub, e_ref[...])
    wr, wi = w[:, :half], w[:, half:]
    row = lax.broadcasted_iota(jnp.int32, (rows, half), 0)
    a1r, a1i = pw_ref[0:1, :half], pw_ref[0:1, half:]
    sr, si = state_ref[0:1, :half], state_ref[0:1, half:]
    cr, ci = cmul(a1r, a1i, sr, si)
    wr = jnp.where(row == 0, wr + cr, wr)
    wi = jnp.where(row == 0, wi + ci, wi)
    step = 0
    d = 1
    while d < rows:
        ar, ai = pw_ref[step:step + 1, :half], pw_ref[step:step + 1, half:]
        pr = jnp.where(row >= d, pltpu.roll(wr, d, axis=0), 0.0)
        pi = jnp.where(row >= d, pltpu.roll(wi, d, axis=0), 0.0)
        mr, mi = cmul(ar, ai, pr, pi)
        wr, wi = wr + mr, wi + mi
        d *= 2
        step += 1
    xr = jnp.where(row == 0, sr, pltpu.roll(wr, 1, axis=0))
    xi = jnp.where(row == 0, si, pltpu.roll(wi, 1, axis=0))
    state_ref[0:1, :half] = wr[rows - 1:rows, :]
    state_ref[0:1, half:] = wi[rows - 1:rows, :]

    n_c = S5_GROUPS * S5_GROUP_CH
    slots = [ub[:, j * n_c:(j + 1) * n_c] for j in range(S5_CHUNK)]
    cols = []
    for i in range(S5_CHUNK):
        acc = _dot(slots[0], m_ref[i])
        for j in range(1, i + 1):
            acc = acc + _dot(slots[j], m_ref[i - j])
        cols.append(acc)
    y = jnp.concatenate(cols, axis=1)
    y = y + _dot(xr.astype(BF16), g_ref[:half, :]) + _dot(xi.astype(BF16), g_ref[half:, :])
    y = y + d_ref[...] * u
    y_ref[0] = jax.nn.gelu(y).astype(y_ref.dtype)


def _s5(u3, m, e, g, pw, dskip, layer):
    bsz, rows, width = u3.shape
    tr = min(S5_ROW_TILE, rows)
    blk = pl.BlockSpec((1, tr, width), lambda b, i: (b, i, 0))
    return pl.pallas_call(
        _s5_kernel,
        grid=(bsz, rows // tr),
        in_specs=[blk] + [_const_spec(a.shape, layer) for a in (m, e, g, pw, dskip)],
        out_specs=blk,
        out_shape=jax.ShapeDtypeStruct(u3.shape, BF16),
        scratch_shapes=[pltpu.VMEM((8, width), F32)],
        compiler_params=_cparams(("parallel", "arbitrary")),
        name="s5",
    )(u3, m, e, g, pw, dskip)


def _s5_operators(lam_re, lam_im, log_step, b_re, b_im, c_re, c_im, d_skip, row_tile):
    n_g, n_p, n_h, L = S5_GROUPS, S5_STATE, S5_GROUP_CH, S5_CHUNK
    n_c, n_s = n_g * n_h, n_g * n_p
    dt = jnp.exp(log_step)[:, None]

    def lam_bar_pow(k):
        mag = jnp.exp(k[:, None, None] * (lam_re * dt)[None])
        ang = k[:, None, None] * (lam_im * dt)[None]
        return mag * jnp.cos(ang), mag * jnp.sin(ang)

    pr, pi = lam_bar_pow(jnp.arange(L + 1, dtype=F32))
    den = lam_re * lam_re + lam_im * lam_im
    fr = ((pr[1] - 1.0) * lam_re + pi[1] * lam_im) / den
    fi = (pi[1] * lam_re - (pr[1] - 1.0) * lam_im) / den
    bbr = fr[..., None] * b_re - fi[..., None] * b_im
    bbi = fr[..., None] * b_im + fi[..., None] * b_re
    tr = pr[:L, :, :, None] * bbr[None] - pi[:L, :, :, None] * bbi[None]
    ti = pr[:L, :, :, None] * bbi[None] + pi[:L, :, :, None] * bbr[None]

    def expand(small, rows_per_group, cols_per_group):
        rows = small.shape[-2]
        src = lax.broadcasted_iota(jnp.int32, (cols_per_group, n_g * cols_per_group), 0)
        dst = lax.broadcasted_iota(jnp.int32, (cols_per_group, n_g * cols_per_group), 1) % cols_per_group
        spread = jnp.where(src == dst, 1.0, 0.0).astype(BF16)
        tiled = jnp.einsum('...rc,cq->...rq', small.astype(BF16), spread, preferred_element_type=F32)
        r = lax.broadcasted_iota(jnp.int32, (rows, n_g * cols_per_group), 0) // rows_per_group
        c = lax.broadcasted_iota(jnp.int32, (rows, n_g * cols_per_group), 1) // cols_per_group
        return jnp.where(r == c, tiled, 0.0)

    hp = lax.Precision.HIGHEST
    kern = (jnp.einsum('ghp,kgpi->kghi', c_re, tr, precision=hp)
            - jnp.einsum('ghp,kgpi->kghi', c_im, ti, precision=hp))
    m_op = expand(kern.transpose(0, 1, 3, 2).reshape(L, n_c, n_h), n_h, n_h).astype(BF16)

    e_r = expand(tr[::-1].transpose(0, 1, 3, 2).reshape(L, n_c, n_p), n_h, n_p)
    e_i = expand(ti[::-1].transpose(0, 1, 3, 2).reshape(L, n_c, n_p), n_h, n_p)
    e_op = jnp.concatenate([e_r, e_i], axis=2).astype(BF16).reshape(L * n_c, 2 * n_s)

    clr = c_re[None] * pr[1:, :, None, :] - c_im[None] * pi[1:, :, None, :]
    cli = c_re[None] * pi[1:, :, None, :] + c_im[None] * pr[1:, :, None, :]
    g_r = expand(clr.transpose(0, 1, 3, 2).reshape(L, n_s, n_h), n_p, n_h)
    g_i = expand(cli.transpose(0, 1, 3, 2).reshape(L, n_s, n_h), n_p, n_h)
    g_blocks = jnp.concatenate([g_r, -g_i], axis=1).astype(BF16)
    g_op = jnp.concatenate([g_blocks[i] for i in range(L)], axis=1)

    n_steps = max(1, int(math.log2(row_tile)))
    sr, si = lam_bar_pow(L * 2.0 ** jnp.arange(n_steps, dtype=F32))
    pw = jnp.concatenate([sr.reshape(n_steps, n_s), si.reshape(n_steps, n_s)], axis=1)
    pw = jnp.pad(pw, ((0, (-n_steps) % 8), (0, 0)))
    dskip = jnp.tile(d_skip.reshape(1, n_c), (1, L))
    return m_op, e_op, g_op, pw.astype(F32), dskip.astype(F32)


def _gla_constants():
    n = GLA_SUB
    ri = lax.broadcasted_iota(jnp.int32, (n, n), 0)
    ci = lax.broadcasted_iota(jnp.int32, (n, n), 1)
    same = (ri // GLA_BLOCK) == (ci // GLA_BLOCK)
    mats = [same & (ci <= ri), same & (ci > ri)]
    masks = [ri == ci]
    ups, lows = [], []
    for lev in range(GLA_LEVELS):
        half = 1 << lev
        blk = 2 * half
        mid = (ri // blk) * blk + (half - 1)
        later = (ri % blk) >= half
        ups.append(later & (ci > mid) & (ci <= ri))
        lows.append(jnp.logical_not(later) & (ci > ri) & (ci <= mid))
        masks.append(((ri // blk) == (ci // blk)) & later & ((ci % blk) < half))
    stacked = jnp.concatenate([jnp.where(m, 1.0, 0.0).astype(BF16) for m in mats + ups + lows], axis=0)
    return stacked, masks


def _gla_stages(x, gg, state, wgate, bgate, norm, stacked, masks, result):
    n_rows = x.shape[0]
    n_sub = n_rows // GLA_SUB
    hk = GLA_HEADS * GLA_DK
    hv = GLA_HEADS * GLA_DV
    q = x[:, 0:hk].astype(F32) * (GLA_DK ** -0.5)
    k = x[:, hk:2 * hk].astype(F32)
    vb = x[:, 2 * hk:2 * hk + hv]
    r = x[:, 2 * hk + hv:].astype(F32)
    sub = lambda t, st: t[st * GLA_SUB:(st + 1) * GLA_SUB]

    z = _dot(gg.astype(BF16), wgate) + bgate
    g = -(jnp.maximum(-z, 0.0) + jnp.log(1.0 + jnp.exp(-jnp.abs(z)))) * (1.0 / GLA_GATE_NORM)
    gb = g.astype(BF16)
    yield

    sums = [_dot(stacked, sub(gb, st)) for st in range(n_sub)]
    part = lambda st, i: sums[st][i * GLA_SUB:(i + 1) * GLA_SUB]
    yield

    lane_k = lax.broadcasted_iota(jnp.int32, (GLA_SUB, hk), 1) // GLA_DK
    att = [[jnp.zeros((GLA_SUB, GLA_SUB), F32) for _ in range(GLA_HEADS)] for _ in range(n_sub)]
    for lev in range(-1, GLA_LEVELS):
        for st in range(n_sub):
            qs, ks = sub(q, st), sub(k, st)
            if lev < 0:
                qt, kt = qs.astype(BF16), ks.astype(BF16)
            else:
                qt = (qs * jnp.exp(part(st, 2 + lev))).astype(BF16)
                kt = (ks * jnp.exp(part(st, 2 + GLA_LEVELS + lev))).astype(BF16)
            q_heads = jnp.concatenate(
                [jnp.where(lane_k == h, qt, jnp.zeros_like(qt)) for h in range(GLA_HEADS)], axis=0)
            a = _dot_nt(q_heads, kt)
            for h in range(GLA_HEADS):
                att[st][h] = jnp.where(masks[lev + 1], a[h * GLA_SUB:(h + 1) * GLA_SUB], att[st][h])
        yield

    lane_v = lax.broadcasted_iota(jnp.int32, (GLA_SUB, hv), 1) // GLA_DV
    inner = []
    for st in range(n_sub):
        vs = sub(vb, st)
        att_all = jnp.concatenate([a.astype(BF16) for a in att[st]], axis=1)
        v_heads = jnp.concatenate([jnp.where(lane_v == h, vs, jnp.zeros_like(vs)) for h in range(GLA_HEADS)], axis=0)
        inner.append(_dot(att_all, v_heads))
    yield

    row_blk = lax.broadcasted_iota(jnp.int32, (GLA_SUB, hk), 0) // GLA_BLOCK
    rs = lax.broadcasted_iota(jnp.int32, (hv, hk), 0) // GLA_DV
    cs = lax.broadcasted_iota(jnp.int32, (hv, hk), 1) // GLA_DK
    head_diag = rs == cs
    n_blk = GLA_SUB // GLA_BLOCK
    outs = []
    for st in range(n_sub):
        b, tail = part(st, 0), part(st, 1)
        q_dec = (sub(q, st) * jnp.exp(b)).astype(BF16)
        k_dec = (sub(k, st) * jnp.exp(tail)).astype(BF16)
        b_end = b + tail
        k_blocks = jnp.concatenate(
            [jnp.where(row_blk == n, k_dec, jnp.zeros_like(k_dec)) for n in range(n_blk)], axis=1)
        upd_all = _dot_tn(sub(vb, st), k_blocks)
        cross = []
        for blk in range(n_blk):
            r0 = blk * GLA_BLOCK
            cross.append(_dot_nt(q_dec[r0:r0 + GLA_BLOCK], state.astype(BF16)))
            upd = upd_all[:, blk * hk:(blk + 1) * hk]
            state = jnp.exp(b_end[r0:r0 + 1, :]) * state + jnp.where(head_diag, upd, 0.0)
        outs.append(inner[st] + jnp.concatenate(cross, axis=0))
    yield

    o = _group_rms(jnp.concatenate(outs, axis=0), norm)
    result["y"] = o * (r * jax.nn.sigmoid(r))
    result["state"] = state


def _dsa_band_bias(offset, n_steps):
    row = lax.broadcasted_iota(jnp.int32, (2 * DSA_BLOCK, 2 * DSA_BLOCK), 0) % DSA_BLOCK
    col = lax.broadcasted_iota(jnp.int32, (2 * DSA_BLOCK, 2 * DSA_BLOCK), 1)
    dist = offset + row - col
    return jnp.where((dist >= 0) & (dist <= n_steps), 0.0, NEG_BIG)


def _dsa_attend(q, kw, vw, bias, shift):
    lane_q = lax.broadcasted_iota(jnp.int32, (DSA_BLOCK, LANES), 1)
    q = q * (HEAD_DIM ** -0.5)
    qm2 = jnp.concatenate(
        [jnp.where((lane_q // HEAD_DIM) == head, q, jnp.zeros_like(q)) for head in range(2)], axis=0)
    s = _dot_nt(qm2, kw) + bias
    if shift is None:
        m = jnp.max(s, axis=1, keepdims=True)
        p = jnp.exp(s - m)
        l = jnp.sum(p, axis=1, keepdims=True)
        o2 = _dot(p.astype(BF16), vw) / l
        lse2 = jnp.broadcast_to(m + jnp.log(l), (2 * DSA_BLOCK, LANES))
    else:
        p = jnp.exp(s - shift)
        pv = _dot(p.astype(BF16), jnp.concatenate([vw, jnp.ones_like(vw)], axis=1))
        l = pv[:, LANES:]
        o2 = pv[:, :LANES] / l
        lse2 = shift + jnp.log(l)
    first = lane_q < HEAD_DIM
    return (jnp.where(first, o2[:DSA_BLOCK], o2[DSA_BLOCK:]),
            jnp.where(first, lse2[:DSA_BLOCK], lse2[DSA_BLOCK:]))


def _dsa_kernel(bound_ref, q1_ref, k1_ref, v1_ref, q4_ref, k4_ref, v4_ref, q16_ref, k16_ref, v16_ref,
                y_ref, o_scr, l_scr):
    seq = q1_ref.shape[1]
    refs = ((q1_ref, k1_ref, v1_ref), (q4_ref, k4_ref, v4_ref), (q16_ref, k16_ref, v16_ref))
    blocks_per_span = DSA_SPAN // DSA_BLOCK
    n_steps = DSA_PATTERNS[0][0] // DSA_PATTERNS[0][1]
    assert all(w // d == n_steps for w, d in DSA_PATTERNS)
    bias_inner = _dsa_band_bias(DSA_BLOCK, n_steps)
    bias_first = _dsa_band_bias(0, n_steps)

    def run(shift):
        def span(sp, carry):
            for p, (_, dil) in enumerate(DSA_PATTERNS):
                q_ref, k_ref, v_ref = refs[p]
                per_res = blocks_per_span // dil

                def unit(idx, c, p=p, dil=dil, per_res=per_res, q_ref=q_ref, k_ref=k_ref, v_ref=v_ref):
                    r = idx // per_res
                    jb = idx % per_res
                    m0 = pl.multiple_of(sp * (DSA_SPAN // dil) + jb * DSA_BLOCK, DSA_BLOCK)
                    start = pl.multiple_of(jnp.maximum(m0 - DSA_BLOCK, 0), DSA_BLOCK)
                    if dil == 1:
                        q = q_ref[0, pl.ds(m0, DSA_BLOCK), :]
                        kw = k_ref[0, pl.ds(start, 2 * DSA_BLOCK), :]
                        vw = v_ref[0, pl.ds(start, 2 * DSA_BLOCK), :]
                    else:
                        q = q_ref[0, r, pl.ds(m0, DSA_BLOCK), :]
                        kw = k_ref[0, r, pl.ds(start, 2 * DSA_BLOCK), :]
                        vw = v_ref[0, r, pl.ds(start, 2 * DSA_BLOCK), :]
                    o, lse = _dsa_attend(q, kw, vw, jnp.where(m0 == 0, bias_first, bias_inner), shift)
                    rows = pl.ds(r + dil * jb * DSA_BLOCK, DSA_BLOCK, stride=dil) if dil > 1 else \
                        pl.ds(pl.multiple_of(jb * DSA_BLOCK, DSA_BLOCK), DSA_BLOCK)
                    o_scr[p, rows, :] = o
                    l_scr[p, rows, :] = lse
                    return c

                lax.fori_loop(0, blocks_per_span, unit, 0, unroll=DSA_UNROLL)

            l1, l2, l3 = l_scr[0], l_scr[1], l_scr[2]
            lm = jnp.maximum(jnp.maximum(l1, l2), l3)
            e1, e2, e3 = jnp.exp(l1 - lm), jnp.exp(l2 - lm), jnp.exp(l3 - lm)
            y = (e1 * o_scr[0] + e2 * o_scr[1] + e3 * o_scr[2]) / (e1 + e2 + e3)
            y_ref[0, pl.ds(pl.multiple_of(sp * DSA_SPAN, DSA_SPAN), DSA_SPAN), :] = y.astype(y_ref.dtype)
            return carry

        lax.fori_loop(0, seq // DSA_SPAN, span, 0)

    bound = bound_ref[0]

    @pl.when(bound <= DSA_MAX_SHIFT)
    def _():
        run(bound)

    @pl.when(bound > DSA_MAX_SHIFT)
    def _():
        run(None)


def _dsa(dsa1, dsa4, dsa16, bound):
    bsz, seq, _ = dsa1.shape
    nat = lambda off: pl.BlockSpec((1, seq, LANES), lambda b, hp: (b, 0, off + hp))
    res = lambda dil, off: pl.BlockSpec((1, dil, seq // dil, LANES), lambda b, hp: (b, 0, 0, off + hp))
    return pl.pallas_call(
        _dsa_kernel,
        grid=(bsz, 2),
        in_specs=[pl.BlockSpec(memory_space=pltpu.SMEM),
                  nat(0), nat(2), nat(4), res(4, 0), res(4, 2), res(4, 4), res(16, 0), res(16, 2), res(16, 4)],
        out_specs=pl.BlockSpec((1, seq, LANES), lambda b, hp: (b, 0, hp)),
        out_shape=jax.ShapeDtypeStruct((bsz, seq, BRANCH_WIDTH), BF16),
        scratch_shapes=[pltpu.VMEM((3, DSA_SPAN, LANES), F32), pltpu.VMEM((3, DSA_SPAN, LANES), F32)],
        compiler_params=_cparams(("parallel", "parallel")),
        name="dsa",
    )(bound, dsa1, dsa1, dsa1, dsa4, dsa4, dsa4, dsa16, dsa16, dsa16)


def _proj_weights(w_in):
    lead = w_in[..., :1792].astype(BF16)
    gg = jnp.pad(w_in[..., 1792:1808].astype(BF16), ((0, 0), (0, 0), (0, LANES - GLA_GATE_RANK)))
    dsa = w_in[..., 1808:2576].astype(BF16)
    wp = jnp.concatenate([lead, dsa, gg], axis=-1)
    return wp, w_in[..., 2576:].astype(BF16)


def _rope_tables(seq):
    inv = ROPE_THETA ** (-jnp.arange(0, HEAD_DIM, 2, dtype=F32) / HEAD_DIM)
    ang = jnp.arange(seq, dtype=F32)[:, None] * inv[None, :]
    cos, sin = jnp.cos(ang), jnp.sin(ang)
    cos_t = jnp.tile(jnp.concatenate([cos, cos], axis=1), (1, 2))
    sin_t = jnp.tile(jnp.concatenate([-sin, sin], axis=1), (1, 2))
    return cos_t, sin_t


def _pick_tile(n, pref):
    t = min(pref, n)
    while n % t:
        t //= 2
    return t


def kernel(x, ffn1_norm, ffn1_w_in, ffn1_w_out, mix_norm, w_in, s5_lam_re, s5_lam_im, s5_log_step, s5_b_re, s5_b_im, s5_c_re, s5_c_im, s5_d, s5_w_glu, gla_w_gate, gla_b_gate, gla_norm, dsa_q_norm, dsa_k_norm, w_branch, w_out, ffn2_norm, ffn2_w_in, ffn2_w_out):
    bsz, seq, _ = x.shape
    tokens = bsz * seq
    depth = ffn1_norm.shape[0]
    assert seq % DSA_SPAN == 0 and seq // DSA_PATTERNS[-1][1] >= 2 * DSA_BLOCK and seq >= SB_WIN
    tm_a = _pick_tile(seq, 512)
    tm_b = _pick_tile(tokens, 512)
    s5_rows = seq // S5_CHUNK
    s5_tile = min(S5_ROW_TILE, s5_rows)
    cos_t, sin_t = _rope_tables(seq)

    wp_all, wg_all = _proj_weights(w_in)
    w1a_all, w2a_all = ffn1_w_in.astype(BF16), ffn1_w_out.astype(BF16)
    w1b_all, w2b_all = ffn2_w_in.astype(BF16), ffn2_w_out.astype(BF16)
    wglu_all, wb_all, wo_all = s5_w_glu.astype(BF16), w_branch.astype(BF16), w_out.astype(BF16)
    s5_ops = jax.vmap(functools.partial(_s5_operators, row_tile=s5_tile))(
        s5_lam_re, s5_lam_im, s5_log_step, s5_b_re, s5_b_im, s5_c_re, s5_c_im, s5_d)

    h = x.reshape(tokens, D_MODEL)
    for l in range(depth):
        row = lambda p: p[l].reshape(1, -1)
        qn = jnp.tile(row(dsa_q_norm), (1, 4))
        kn = jnp.tile(row(dsa_k_norm), (1, 4))
        wgate = jnp.pad(gla_w_gate[l], ((0, LANES - GLA_GATE_RANK), (0, 0))).astype(BF16)
        h, sb, s5u, y_gla, dsa1, dsa4, dsa16 = _ffn_proj(
            h, row(ffn1_norm), w1a_all, w2a_all, row(mix_norm), wp_all, qn, kn, cos_t, sin_t,
            wgate, row(gla_b_gate), row(gla_norm), bsz, seq, tm_a, l)

        y_sb = _stick_breaking(sb.reshape(bsz, seq, 768)).reshape(tokens, BRANCH_WIDTH)

        y_s5 = _s5(s5u.reshape(bsz, s5_rows, S5_CHUNK * 256), *s5_ops, l).reshape(tokens // S5_CHUNK, S5_CHUNK * 256)

        bound = (1.02 * HEAD_DIM ** 0.5 * jnp.max(jnp.abs(dsa_q_norm[l])) * jnp.max(jnp.abs(dsa_k_norm[l]))).reshape(1)
        y_dsa = _dsa(dsa1.reshape(bsz, seq, 768), dsa4, dsa16, bound).reshape(tokens, BRANCH_WIDTH)

        h = _merge_ffn(h, y_sb, y_s5, y_gla, y_dsa, row(mix_norm), wg_all, wglu_all, wb_all, wo_all,
                       row(ffn2_norm), w1b_all, w2b_all, tm_b, l)
    return h.reshape(bsz, seq, D_MODEL)
```

```python
import functools
import math

import jax
import jax.numpy as jnp
from jax import lax
from jax.experimental import pallas as pl
from jax.experimental.pallas import tpu as pltpu

D_MODEL = 1024
HEAD_DIM = 64
BRANCH_WIDTH = 256
N_BRANCH = 4
SB_HEADS = 4
S5_GROUPS = 16
S5_GROUP_CH = 16
S5_STATE = 64
GLA_HEADS = 4
GLA_DK = 32
GLA_DV = 64
GLA_GATE_RANK = 16
GLA_GATE_NORM = 16.0
DSA_PATTERNS = ((128, 1), (512, 4), (2048, 16))
DSA_BLOCK = 128
ROPE_THETA = 10000.0
D_FF = 2816
RMS_EPS = 1e-6

LANES = 128
VMEM_LIMIT_BYTES = 58 * 1024 * 1024
FF_CHUNK = 256
N_FF_CHUNKS = D_FF // FF_CHUNK
S5_CHUNK = 8
S5_ROW_TILE = 256
GLA_BLOCK = 64
GLA_LEVELS = 6
GLA_SUB = 128
SB_Q = 64
SB_WIN = 4 * SB_Q
SB_GROUP = 8
SB_LOG_ZERO = -104.0
NEG_BIG = -1e30
DSA_GROUP = 8
DSA_SPAN = DSA_PATTERNS[-1][1] * DSA_BLOCK
DSA_MAX_SHIFT = 30.0

PROJ_SB = (0, 768)
PROJ_S5 = (768, 1024)
PROJ_GLA = (1024, 1792)
PROJ_DSA = (1792, 2560)
PROJ_GG = (2560, 2688)
PROJ_WIDTH = 2688

F32 = jnp.float32
BF16 = jnp.bfloat16


def _cparams(sem):
    return pltpu.CompilerParams(dimension_semantics=sem, vmem_limit_bytes=VMEM_LIMIT_BYTES)


def _const_spec(shape, layer=None):
    if layer is None:
        zeros = (0,) * len(shape)
        return pl.BlockSpec(shape, lambda *_: zeros, pipeline_mode=pl.Buffered(1))
    index = (layer,) + (0,) * (len(shape) - 1)
    return pl.BlockSpec((None,) + tuple(shape[1:]), lambda *_: index, pipeline_mode=pl.Buffered(1))


def _rms_rows(x, g):
    ms = jnp.mean(x * x, axis=-1, keepdims=True)
    return x * lax.rsqrt(ms + RMS_EPS) * g


def _dot(a, b):
    return jnp.dot(a, b, preferred_element_type=F32)


def _dot_nt(a, b):
    return lax.dot_general(a, b, (((1,), (1,)), ((), ())), preferred_element_type=F32)


def _dot_tn(a, b):
    return lax.dot_general(a, b, (((0,), (0,)), ((), ())), preferred_element_type=F32)


def _swiglu_into(acc_ref, xb, w1_ref, w2_ref, between=None):
    for c in range(N_FF_CHUNKS):
        lo, hi = c * FF_CHUNK, (c + 1) * FF_CHUNK
        a = _dot(xb, w1_ref[:, lo:hi])
        b = _dot(xb, w1_ref[:, D_FF + lo:D_FF + hi])
        hm = (a * jax.nn.sigmoid(a) * b).astype(BF16)
        part = _dot(hm, w2_ref[lo:hi, :])
        if c == 0:
            acc_ref[...] = part
        else:
            acc_ref[...] += part
        if between is not None:
            next(between, None)


def _group_mean_matrix(width, group):
    r = lax.broadcasted_iota(jnp.int32, (width, width), 0) // group
    c = lax.broadcasted_iota(jnp.int32, (width, width), 1) // group
    return jnp.where(r == c, 1.0 / group, 0.0).astype(BF16)


def _group_rms(x, gain):
    gm = _group_mean_matrix(x.shape[-1], HEAD_DIM)
    ms = _dot((x * x).astype(BF16), gm)
    return x * lax.rsqrt(ms + RMS_EPS) * gain


def _swap_half_heads(x):
    half = HEAD_DIM // 2
    outs = []
    for s in range(x.shape[-1] // LANES):
        xs = x[:, s * LANES:(s + 1) * LANES]
        lane = lax.broadcasted_iota(jnp.int32, xs.shape, 1)
        up = pltpu.roll(xs, LANES - half, axis=1)
        down = pltpu.roll(xs, half, axis=1)
        outs.append(jnp.where((lane % HEAD_DIM) < half, up, down))
    return jnp.concatenate(outs, axis=-1)


def _ffn_proj_tile(x_ref, n1_ref, w1_ref, w2_ref, nm_ref, wp_ref, qn_ref, kn_ref, cos_ref, sin_ref,
                   h_ref, sb_ref, s5_ref, dsa1_ref, dsa4_ref, dsa16_ref, acc_ref, slab_ref, glax_ref, glag_ref,
                   between, after_ffn):
    tm = x_ref.shape[0]
    x = x_ref[...]
    xb = _rms_rows(x, n1_ref[...]).astype(BF16)
    _swiglu_into(acc_ref, xb, w1_ref, w2_ref, between=between)
    after_ffn()
    h = x + 0.5 * acc_ref[...]
    h_ref[...] = h
    ub = _rms_rows(h, nm_ref[...]).astype(BF16)

    dsa = _dot(ub, wp_ref[:, PROJ_DSA[0]:PROJ_DSA[1]])
    cos = jnp.concatenate([cos_ref[...]] * 2, axis=-1)
    sin = jnp.concatenate([sin_ref[...]] * 2, axis=-1)
    parts = []
    for i, gain_ref in enumerate((qn_ref, kn_ref)):
        t = _group_rms(dsa[:, i * 256:(i + 1) * 256], gain_ref[...])
        parts.append(t * cos + _swap_half_heads(t) * sin)
    parts.append(dsa[:, 512:768])
    dsa = jnp.concatenate(parts, axis=-1)
    dsa1_ref[...] = dsa.astype(BF16)
    for s in range(6):
        slab_ref[s] = dsa[:, s * LANES:(s + 1) * LANES]
    for dil, ref in ((4, dsa4_ref), (16, dsa16_ref)):
        for r in range(dil):
            for s in range(6):
                rows = slab_ref[s, pl.ds(r, tm // dil, stride=dil), :]
                ref[0, r, :, s * LANES:(s + 1) * LANES] = rows.astype(BF16)

    s5 = _dot(ub, wp_ref[:, PROJ_S5[0]:PROJ_S5[1]])
    for s in range(2):
        slab_ref[6 + s] = s5[:, s * LANES:(s + 1) * LANES]
    for j in range(S5_CHUNK):
        for s in range(2):
            col = j * 256 + s * LANES
            s5_ref[:, col:col + LANES] = slab_ref[6 + s, pl.ds(j, tm // S5_CHUNK, stride=S5_CHUNK), :]

    sb_ref[...] = _dot(ub, wp_ref[:, PROJ_SB[0]:PROJ_SB[1]]).astype(BF16)
    glax_ref[...] = _dot(ub, wp_ref[:, PROJ_GLA[0]:PROJ_GLA[1]]).astype(BF16)
    glag_ref[...] = _dot(ub, wp_ref[:, PROJ_GG[0]:PROJ_GG[1]])


def _ffn_proj_kernel(x_ref, n1_ref, w1_ref, w2_ref, nm_ref, wp_ref, qn_ref, kn_ref, cos_ref, sin_ref,
                     wgate_ref, bgate_ref, gnorm_ref,
                     h_ref, sb_ref, s5_ref, ygla_ref, dsa1_ref, dsa4_ref, dsa16_ref,
                     acc_ref, slab_ref, glax_ref, glag_ref, glas_ref, *, seq_tiles):
    step = pl.program_id(0)
    n_tiles = pl.num_programs(0) - 1

    @pl.when(step == 0)
    def _():
        glax_ref[...] = jnp.zeros_like(glax_ref)
        glag_ref[...] = jnp.zeros_like(glag_ref)
        glas_ref[...] = jnp.zeros_like(glas_ref)

    def start_gla():
        starts_sequence = lax.rem(step - 1, seq_tiles) == 0
        state = jnp.where(starts_sequence, 0.0, glas_ref[...])
        stacked, masks = _gla_constants()
        gla = {}
        stages = _gla_stages(glax_ref[...], glag_ref[...], state, wgate_ref[...], bgate_ref[...], gnorm_ref[...],
                             stacked, masks, gla)
        return stages, gla

    def finish_gla(stages, gla):
        for _ in stages:
            pass
        ygla_ref[...] = gla["y"].astype(ygla_ref.dtype)
        glas_ref[...] = gla["state"]

    @pl.when(step < n_tiles)
    def _():
        stages, gla = start_gla()
        _ffn_proj_tile(x_ref, n1_ref, w1_ref, w2_ref, nm_ref, wp_ref, qn_ref, kn_ref, cos_ref, sin_ref,
                       h_ref, sb_ref, s5_ref, dsa1_ref, dsa4_ref, dsa16_ref, acc_ref, slab_ref, glax_ref, glag_ref,
                       between=stages, after_ffn=lambda: finish_gla(stages, gla))

    @pl.when(step == n_tiles)
    def _():
        finish_gla(*start_gla())


def _ffn_proj(x2, n1, w1, w2, nm, wp, qn, kn, cos_t, sin_t, wgate, bgate, gnorm, bsz, seq, tm, layer):
    tokens = x2.shape[0]
    seq_tiles = seq // tm
    n_tiles = tokens // tm
    cur = lambda i: jnp.minimum(i, n_tiles - 1)
    row = lambda i: (cur(i), 0)
    lag = lambda i: (jnp.maximum(i - 1, 0), 0)
    tab = lambda i: (cur(i) % seq_tiles, 0)
    res = lambda i: (cur(i) // seq_tiles, 0, cur(i) % seq_tiles, 0)
    tile = lambda w: pl.BlockSpec((tm, w), row)
    s5_w = S5_CHUNK * 256
    return pl.pallas_call(
        functools.partial(_ffn_proj_kernel, seq_tiles=seq_tiles),
        grid=(n_tiles + 1,),
        in_specs=[tile(D_MODEL), _const_spec(n1.shape), _const_spec(w1.shape, layer), _const_spec(w2.shape, layer),
                  _const_spec(nm.shape), _const_spec(wp.shape, layer), _const_spec(qn.shape), _const_spec(kn.shape),
                  pl.BlockSpec((tm, LANES), tab), pl.BlockSpec((tm, LANES), tab),
                  _const_spec(wgate.shape), _const_spec(bgate.shape), _const_spec(gnorm.shape)],
        out_specs=[tile(D_MODEL), tile(768), pl.BlockSpec((tm // S5_CHUNK, s5_w), row),
                   pl.BlockSpec((tm, BRANCH_WIDTH), lag),
                   tile(768), pl.BlockSpec((1, 4, tm // 4, 768), res), pl.BlockSpec((1, 16, tm // 16, 768), res)],
        out_shape=[jax.ShapeDtypeStruct((tokens, D_MODEL), F32),
                   jax.ShapeDtypeStruct((tokens, 768), BF16),
                   jax.ShapeDtypeStruct((tokens // S5_CHUNK, s5_w), F32),
                   jax.ShapeDtypeStruct((tokens, BRANCH_WIDTH), BF16),
                   jax.ShapeDtypeStruct((tokens, 768), BF16),
                   jax.ShapeDtypeStruct((bsz, 4, seq // 4, 768), BF16),
                   jax.ShapeDtypeStruct((bsz, 16, seq // 16, 768), BF16)],
        scratch_shapes=[pltpu.VMEM((tm, D_MODEL), F32), pltpu.VMEM((8, tm, LANES), F32),
                        pltpu.VMEM((tm, 768), BF16), pltpu.VMEM((tm, LANES), F32),
                        pltpu.VMEM((GLA_HEADS * GLA_DV, GLA_HEADS * GLA_DK), F32)],
        compiler_params=_cparams(("arbitrary",)),
        name="ffn_proj",
    )(x2, n1, w1, w2, nm, wp, qn, kn, cos_t, sin_t, wgate, bgate, gnorm)


def _merge_ffn_kernel(h_ref, ysb_ref, ys5_ref, ygla_ref, ydsa_ref,
                      nm_ref, wg_ref, wglu_ref, wb_ref, wo_ref, n2_ref, w1_ref, w2_ref,
                      out_ref, acc_ref, slab_ref):
    tm = h_ref.shape[0]
    proj = {n: _dot(ref[...], wb_ref[n]) for n, ref in ((0, ysb_ref), (2, ygla_ref), (3, ydsa_ref))}

    for j in range(S5_CHUNK):
        for s in range(2):
            col = j * 256 + s * LANES
            slab_ref[s, pl.ds(j, tm // S5_CHUNK, stride=S5_CHUNK), :] = ys5_ref[:, col:col + LANES].astype(F32)
    y_s5 = jnp.concatenate([slab_ref[0], slab_ref[1]], axis=-1).astype(BF16)
    glu = _dot(y_s5, wglu_ref[...])
    y_s5 = glu[:, :BRANCH_WIDTH] * jax.nn.sigmoid(glu[:, BRANCH_WIDTH:])
    proj[1] = _dot(y_s5.astype(BF16), wb_ref[1])

    h = h_ref[...]
    ub = _rms_rows(h, nm_ref[...]).astype(BF16)
    mixed = jnp.zeros(h.shape, F32)
    for n in range(N_BRANCH):
        gate = jax.nn.sigmoid(_dot(ub, wg_ref[:, n * D_MODEL:(n + 1) * D_MODEL]))
        mixed = mixed + gate * proj[n]
    h2 = h + _dot(mixed.astype(BF16), wo_ref[...])

    xb = _rms_rows(h2, n2_ref[...]).astype(BF16)
    _swiglu_into(acc_ref, xb, w1_ref, w2_ref)
    out_ref[...] = h2 + 0.5 * acc_ref[...]


def _merge_ffn(h, ysb, ys5, ygla, ydsa, nm, wg, wglu, wb, wo, n2, w1, w2, tm, layer):
    tokens = h.shape[0]
    row = lambda i: (i, 0)
    tile = lambda w: pl.BlockSpec((tm, w), row)
    consts = (nm, wg, wglu, wb, wo, n2, w1, w2)
    return pl.pallas_call(
        _merge_ffn_kernel,
        grid=(tokens // tm,),
        in_specs=[tile(D_MODEL), tile(BRANCH_WIDTH), pl.BlockSpec((tm // S5_CHUNK, S5_CHUNK * 256), row),
                  tile(BRANCH_WIDTH), tile(BRANCH_WIDTH)]
                 + [_const_spec(c.shape, None if c.ndim == 2 and c.shape[0] == 1 else layer) for c in consts],
        out_specs=tile(D_MODEL),
        out_shape=jax.ShapeDtypeStruct((tokens, D_MODEL), F32),
        scratch_shapes=[pltpu.VMEM((tm, D_MODEL), F32), pltpu.VMEM((2, tm, LANES), F32)],
        compiler_params=_cparams(("parallel",)),
        name="merge_ffn",
    )(h, ysb, ys5, ygla, ydsa, *consts)


def _later_key_matrix(n):
    j = lax.broadcasted_iota(jnp.int32, (n, n), 0)
    s = lax.broadcasted_iota(jnp.int32, (n, n), 1)
    return jnp.where(j > s, 1.0, 0.0).astype(BF16)


def _sb_scores(qm, k, keep=None):
    z = _dot_nt(qm, k)
    if keep is not None:
        z = jnp.where(keep, z, NEG_BIG)
    lk = -(jnp.maximum(z, 0.0) + jnp.log(1.0 + jnp.exp(-jnp.abs(z))))
    return z, lk


def _sb_kernel(q_ref, k_ref, v_ref, o_ref):
    seq = q_ref.shape[1]
    u_win = _later_key_matrix(SB_WIN)
    u_blk = _later_key_matrix(SB_Q)
    lane_q = lax.broadcasted_iota(jnp.int32, (SB_Q, LANES), 1)
    n_streams = 2 * SB_GROUP
    key_minus_query = (lax.broadcasted_iota(jnp.int32, (2 * SB_Q, SB_WIN), 1)
                       - lax.broadcasted_iota(jnp.int32, (2 * SB_Q, SB_WIN), 0) % SB_Q)

    def q_group(qg, carry):
        starts, qms, zs, lks, vws = [], [], [], [], []
        for sub in range(SB_GROUP):
            t0 = pl.multiple_of((qg * SB_GROUP + sub) * SB_Q, SB_Q)
            start = pl.multiple_of(jnp.maximum(t0 - (SB_WIN - SB_Q), 0), SB_Q)
            q = q_ref[0, pl.ds(t0, SB_Q), :] * (HEAD_DIM ** -0.5)
            kw = k_ref[0, pl.ds(start, SB_WIN), :]
            vws.append(v_ref[0, pl.ds(start, SB_WIN), :])
            strict = key_minus_query < (t0 - start)
            starts.append(start)
            qm2 = jnp.concatenate(
                [jnp.where((lane_q // HEAD_DIM) == head, q, jnp.zeros_like(q)) for head in range(2)], axis=0)
            z, lk = _sb_scores(qm2, kw, strict)
            qms += [qm2[:SB_Q], qm2[SB_Q:]]
            zs.append(z)
            lks.append(lk)
        lk_all = jnp.concatenate(lks, axis=0)
        later_all = _dot(lk_all.astype(BF16), u_win)
        csums, outs = [], []
        for sub in range(SB_GROUP):
            rows = slice(sub * 2 * SB_Q, (sub + 1) * 2 * SB_Q)
            later = later_all[rows]
            w = jnp.exp(zs[sub] + lks[sub] + later)
            o2 = _dot(w.astype(BF16), vws[sub])
            c2 = later[:, 0:1] + lks[sub][:, 0:1]
            outs += [o2[:SB_Q], o2[SB_Q:]]
            csums += [c2[:SB_Q], c2[SB_Q:]]

        def cond(state):
            step, cmaxes, _, _ = state
            flags = [jnp.logical_and(starts[s // 2] - (step + 1) * SB_Q >= 0, cmaxes[s] > SB_LOG_ZERO)
                     for s in range(n_streams)]
            return functools.reduce(jnp.logical_or, flags)

        def body(state):
            step, _, csums, outs = state
            new_c, new_o = [], []
            for s in range(n_streams):
                nxt = starts[s // 2] - (step + 1) * SB_Q
                in_range = nxt >= 0
                p0 = pl.multiple_of(jnp.maximum(nxt, 0), SB_Q)
                kb = k_ref[0, pl.ds(p0, SB_Q), :]
                vb = v_ref[0, pl.ds(p0, SB_Q), :]
                z, lk = _sb_scores(qms[s], kb)
                later = _dot(lk.astype(BF16), u_blk)
                w = jnp.where(in_range, jnp.exp(z + lk + later + csums[s]), 0.0)
                new_o.append(outs[s] + _dot(w.astype(BF16), vb))
                new_c.append(csums[s] + jnp.where(in_range, later[:, 0:1] + lk[:, 0:1], 0.0))
            return step + 1, tuple(jnp.max(c) for c in new_c), tuple(new_c), tuple(new_o)

        init = (jnp.int32(0), tuple(jnp.max(c) for c in csums), tuple(csums), tuple(outs))
        _, _, _, outs = lax.while_loop(cond, body, init)
        for sub in range(SB_GROUP):
            t0 = pl.multiple_of((qg * SB_GROUP + sub) * SB_Q, SB_Q)
            o_ref[0, pl.ds(t0, SB_Q), :] = jnp.where(
                lane_q < HEAD_DIM, outs[2 * sub], outs[2 * sub + 1]).astype(o_ref.dtype)
        return carry

    lax.fori_loop(0, seq // (SB_Q * SB_GROUP), q_group, 0)


def _stick_breaking(sb3):
    bsz, seq, _ = sb3.shape
    spec = lambda off: pl.BlockSpec((1, seq, LANES), lambda b, hp: (b, 0, off + hp))
    return pl.pallas_call(
        _sb_kernel,
        grid=(bsz, 2),
        in_specs=[spec(0), spec(2), spec(4)],
        out_specs=pl.BlockSpec((1, seq, LANES), lambda b, hp: (b, 0, hp)),
        out_shape=jax.ShapeDtypeStruct((bsz, seq, BRANCH_WIDTH), BF16),
        compiler_params=_cparams(("parallel", "parallel")),
        name="stick_breaking",
    )(sb3, sb3, sb3)


def _s5_kernel(u_ref, m_ref, e_ref, g_ref, pw_ref, d_ref, y_ref, state_ref):
    half = S5_GROUPS * S5_STATE

    @pl.when(pl.program_id(1) == 0)
    def _():
        state_ref[...] = jnp.zeros_like(state_ref)

    u = u_ref[0]
    ub = u.astype(BF16)
    rows = u.shape[0]

    def cmul(ar, ai, xr, xi):
        return ar * xr - ai * xi, ar * xi + ai * xr

    w = _dot(ub, e_ref[...])
    wr, wi = w[:, :half], w[:, half:]

    n_c = S5_GROUPS * S5_GROUP_CH
    slots = [ub[:, j * n_c:(j + 1) * n_c] for j in range(S5_CHUNK)]
    cols = []
    for i in range(S5_CHUNK):
        acc = _dot(slots[0], m_ref[i])
        for j in range(1, i + 1):
            acc = acc + _dot(slots[j], m_ref[i - j])
        cols.append(acc)
    y = jnp.concatenate(cols, axis=1)

    row = lax.broadcasted_iota(jnp.int32, (rows, half), 0)
    a1r, a1i = pw_ref[0:1, :half], pw_ref[0:1, half:]
    sr, si = state_ref[0:1, :half], state_ref[0:1, half:]
    cr, ci = cmul(a1r, a1i, sr, si)
    wr = jnp.where(row == 0, wr + cr, wr)
    wi = jnp.where(row == 0, wi + ci, wi)
    step = 0
    d = 1
    while d < rows:
        ar, ai = pw_ref[step:step + 1, :half], pw_ref[step:step + 1, half:]
        pr = jnp.where(row >= d, pltpu.roll(wr, d, axis=0), 0.0)
        pi = jnp.where(row >= d, pltpu.roll(wi, d, axis=0), 0.0)
        mr, mi = cmul(ar, ai, pr, pi)
        wr, wi = wr + mr, wi + mi
        d *= 2
        step += 1
    xr = jnp.where(row == 0, sr, pltpu.roll(wr, 1, axis=0))
    xi = jnp.where(row == 0, si, pltpu.roll(wi, 1, axis=0))
    state_ref[0:1, :half] = wr[rows - 1:rows, :]
    state_ref[0:1, half:] = wi[rows - 1:rows, :]

    y = y + _dot(xr.astype(BF16), g_ref[:half, :]) + _dot(xi.astype(BF16), g_ref[half:, :])
    y = y + d_ref[...] * u
    y_ref[0] = jax.nn.gelu(y).astype(y_ref.dtype)


def _s5(u3, m, e, g, pw, dskip, layer):
    bsz, rows, width = u3.shape
    tr = min(S5_ROW_TILE, rows)
    blk = pl.BlockSpec((1, tr, width), lambda b, i: (b, i, 0))
    return pl.pallas_call(
        _s5_kernel,
        grid=(bsz, rows // tr),
        in_specs=[blk] + [_const_spec(a.shape, layer) for a in (m, e, g, pw, dskip)],
        out_specs=blk,
        out_shape=jax.ShapeDtypeStruct(u3.shape, BF16),
        scratch_shapes=[pltpu.VMEM((8, width), F32)],
        compiler_params=_cparams(("parallel", "arbitrary")),
        name="s5",
    )(u3, m, e, g, pw, dskip)


def _s5_operators(lam_re, lam_im, log_step, b_re, b_im, c_re, c_im, d_skip, row_tile):
    n_g, n_p, n_h, L = S5_GROUPS, S5_STATE, S5_GROUP_CH, S5_CHUNK
    n_c, n_s = n_g * n_h, n_g * n_p
    dt = jnp.exp(log_step)[:, None]

    def lam_bar_pow(k):
        mag = jnp.exp(k[:, None, None] * (lam_re * dt)[None])
        ang = k[:, None, None] * (lam_im * dt)[None]
        return mag * jnp.cos(ang), mag * jnp.sin(ang)

    pr, pi = lam_bar_pow(jnp.arange(L + 1, dtype=F32))
    den = lam_re * lam_re + lam_im * lam_im
    fr = ((pr[1] - 1.0) * lam_re + pi[1] * lam_im) / den
    fi = (pi[1] * lam_re - (pr[1] - 1.0) * lam_im) / den
    bbr = fr[..., None] * b_re - fi[..., None] * b_im
    bbi = fr[..., None] * b_im + fi[..., None] * b_re
    tr = pr[:L, :, :, None] * bbr[None] - pi[:L, :, :, None] * bbi[None]
    ti = pr[:L, :, :, None] * bbi[None] + pi[:L, :, :, None] * bbr[None]

    def expand(small, rows_per_group, cols_per_group):
        rows = small.shape[-2]
        src = lax.broadcasted_iota(jnp.int32, (cols_per_group, n_g * cols_per_group), 0)
        dst = lax.broadcasted_iota(jnp.int32, (cols_per_group, n_g * cols_per_group), 1) % cols_per_group
        spread = jnp.where(src == dst, 1.0, 0.0).astype(BF16)
        tiled = jnp.einsum('...rc,cq->...rq', small.astype(BF16), spread, preferred_element_type=F32)
        r = lax.broadcasted_iota(jnp.int32, (rows, n_g * cols_per_group), 0) // rows_per_group
        c = lax.broadcasted_iota(jnp.int32, (rows, n_g * cols_per_group), 1) // cols_per_group
        return jnp.where(r == c, tiled, 0.0)

    hp = lax.Precision.HIGHEST
    kern = (jnp.einsum('ghp,kgpi->kghi', c_re, tr, precision=hp)
            - jnp.einsum('ghp,kgpi->kghi', c_im, ti, precision=hp))
    m_op = expand(kern.transpose(0, 1, 3, 2).reshape(L, n_c, n_h), n_h, n_h).astype(BF16)

    e_r = expand(tr[::-1].transpose(0, 1, 3, 2).reshape(L, n_c, n_p), n_h, n_p)
    e_i = expand(ti[::-1].transpose(0, 1, 3, 2).reshape(L, n_c, n_p), n_h, n_p)
    e_op = jnp.concatenate([e_r, e_i], axis=2).astype(BF16).reshape(L * n_c, 2 * n_s)

    clr = c_re[None] * pr[1:, :, None, :] - c_im[None] * pi[1:, :, None, :]
    cli = c_re[None] * pi[1:, :, None, :] + c_im[None] * pr[1:, :, None, :]
    g_r = expand(clr.transpose(0, 1, 3, 2).reshape(L, n_s, n_h), n_p, n_h)
    g_i = expand(cli.transpose(0, 1, 3, 2).reshape(L, n_s, n_h), n_p, n_h)
    g_blocks = jnp.concatenate([g_r, -g_i], axis=1).astype(BF16)
    g_op = jnp.concatenate([g_blocks[i] for i in range(L)], axis=1)

    n_steps = max(1, int(math.log2(row_tile)))
    sr, si = lam_bar_pow(L * 2.0 ** jnp.arange(n_steps, dtype=F32))
    pw = jnp.concatenate([sr.reshape(n_steps, n_s), si.reshape(n_steps, n_s)], axis=1)
    pw = jnp.pad(pw, ((0, (-n_steps) % 8), (0, 0)))
    dskip = jnp.tile(d_skip.reshape(1, n_c), (1, L))
    return m_op, e_op, g_op, pw.astype(F32), dskip.astype(F32)


def _gla_constants():
    n = GLA_SUB
    ri = lax.broadcasted_iota(jnp.int32, (n, n), 0)
    ci = lax.broadcasted_iota(jnp.int32, (n, n), 1)
    same = (ri // GLA_BLOCK) == (ci // GLA_BLOCK)
    mats = [same & (ci <= ri), same & (ci > ri)]
    masks = [ri == ci]
    ups, lows = [], []
    for lev in range(GLA_LEVELS):
        half = 1 << lev
        blk = 2 * half
        mid = (ri // blk) * blk + (half - 1)
        later = (ri % blk) >= half
        ups.append(later & (ci > mid) & (ci <= ri))
        lows.append(jnp.logical_not(later) & (ci > ri) & (ci <= mid))
        masks.append(((ri // blk) == (ci // blk)) & later & ((ci % blk) < half))
    stacked = jnp.concatenate([jnp.where(m, 1.0, 0.0).astype(BF16) for m in mats + ups + lows], axis=0)
    return stacked, masks


def _gla_stages(x, gg, state, wgate, bgate, norm, stacked, masks, result):
    n_rows = x.shape[0]
    n_sub = n_rows // GLA_SUB
    hk = GLA_HEADS * GLA_DK
    hv = GLA_HEADS * GLA_DV
    q = x[:, 0:hk].astype(F32) * (GLA_DK ** -0.5)
    k = x[:, hk:2 * hk].astype(F32)
    vb = x[:, 2 * hk:2 * hk + hv]
    r = x[:, 2 * hk + hv:].astype(F32)
    sub = lambda t, st: t[st * GLA_SUB:(st + 1) * GLA_SUB]

    z = _dot(gg.astype(BF16), wgate) + bgate
    g = -(jnp.maximum(-z, 0.0) + jnp.log(1.0 + jnp.exp(-jnp.abs(z)))) * (1.0 / GLA_GATE_NORM)
    gb = g.astype(BF16)
    yield

    sums = [_dot(stacked, sub(gb, st)) for st in range(n_sub)]
    part = lambda st, i: sums[st][i * GLA_SUB:(i + 1) * GLA_SUB]
    yield

    lane_k = lax.broadcasted_iota(jnp.int32, (GLA_SUB, hk), 1) // GLA_DK
    att = [[jnp.zeros((GLA_SUB, GLA_SUB), F32) for _ in range(GLA_HEADS)] for _ in range(n_sub)]
    for lev in range(-1, GLA_LEVELS):
        for st in range(n_sub):
            qs, ks = sub(q, st), sub(k, st)
            if lev < 0:
                qt, kt = qs.astype(BF16), ks.astype(BF16)
            else:
                qt = (qs * jnp.exp(part(st, 2 + lev))).astype(BF16)
                kt = (ks * jnp.exp(part(st, 2 + GLA_LEVELS + lev))).astype(BF16)
            q_heads = jnp.concatenate(
                [jnp.where(lane_k == h, qt, jnp.zeros_like(qt)) for h in range(GLA_HEADS)], axis=0)
            a = _dot_nt(q_heads, kt)
            for h in range(GLA_HEADS):
                att[st][h] = jnp.where(masks[lev + 1], a[h * GLA_SUB:(h + 1) * GLA_SUB], att[st][h])
        yield

    lane_v = lax.broadcasted_iota(jnp.int32, (GLA_SUB, hv), 1) // GLA_DV
    inner = []
    for st in range(n_sub):
        vs = sub(vb, st)
        att_all = jnp.concatenate([a.astype(BF16) for a in att[st]], axis=1)
        v_heads = jnp.concatenate([jnp.where(lane_v == h, vs, jnp.zeros_like(vs)) for h in range(GLA_HEADS)], axis=0)
        inner.append(_dot(att_all, v_heads))
    yield

    row_blk = lax.broadcasted_iota(jnp.int32, (GLA_SUB, hk), 0) // GLA_BLOCK
    rs = lax.broadcasted_iota(jnp.int32, (hv, hk), 0) // GLA_DV
    cs = lax.broadcasted_iota(jnp.int32, (hv, hk), 1) // GLA_DK
    head_diag = rs == cs
    n_blk = GLA_SUB // GLA_BLOCK
    outs = []
    for st in range(n_sub):
        b, tail = part(st, 0), part(st, 1)
        q_dec = (sub(q, st) * jnp.exp(b)).astype(BF16)
        k_dec = (sub(k, st) * jnp.exp(tail)).astype(BF16)
        b_end = b + tail
        k_blocks = jnp.concatenate(
            [jnp.where(row_blk == n, k_dec, jnp.zeros_like(k_dec)) for n in range(n_blk)], axis=1)
        upd_all = _dot_tn(sub(vb, st), k_blocks)
        cross = []
        for blk in range(n_blk):
            r0 = blk * GLA_BLOCK
            cross.append(_dot_nt(q_dec[r0:r0 + GLA_BLOCK], state.astype(BF16)))
            upd = upd_all[:, blk * hk:(blk + 1) * hk]
            state = jnp.exp(b_end[r0:r0 + 1, :]) * state + jnp.where(head_diag, upd, 0.0)
        outs.append(inner[st] + jnp.concatenate(cross, axis=0))
    yield

    o = _group_rms(jnp.concatenate(outs, axis=0), norm)
    result["y"] = o * (r * jax.nn.sigmoid(r))
    result["state"] = state


def _dsa_band_bias(offset, n_steps):
    row = lax.broadcasted_iota(jnp.int32, (2 * DSA_BLOCK, 2 * DSA_BLOCK), 0) % DSA_BLOCK
    col = lax.broadcasted_iota(jnp.int32, (2 * DSA_BLOCK, 2 * DSA_BLOCK), 1)
    dist = offset + row - col
    return jnp.where((dist >= 0) & (dist <= n_steps), 0.0, NEG_BIG)


def _dsa_logits(q, kw, bias):
    lane_q = lax.broadcasted_iota(jnp.int32, (DSA_BLOCK, LANES), 1)
    q = q * (HEAD_DIM ** -0.5)
    qm2 = jnp.concatenate(
        [jnp.where((lane_q // HEAD_DIM) == head, q, jnp.zeros_like(q)) for head in range(2)], axis=0)
    return _dot_nt(qm2, kw) + bias


def _dsa_softmax_pv(s, vw, shift):
    if shift is None:
        m = jnp.max(s, axis=1, keepdims=True)
        p = jnp.exp(s - m)
        l = jnp.sum(p, axis=1, keepdims=True)
        o2 = _dot(p.astype(BF16), vw) / l
        lse2 = jnp.broadcast_to(m + jnp.log(l), (2 * DSA_BLOCK, LANES))
    else:
        p = jnp.exp(s)
        pv = _dot(p.astype(BF16), jnp.concatenate([vw, jnp.ones_like(vw)], axis=1))
        l = pv[:, LANES:]
        o2 = pv[:, :LANES] / l
        lse2 = shift + jnp.log(l)
    first = lax.broadcasted_iota(jnp.int32, (DSA_BLOCK, LANES), 1) < HEAD_DIM
    return (jnp.where(first, o2[:DSA_BLOCK], o2[DSA_BLOCK:]),
            jnp.where(first, lse2[:DSA_BLOCK], lse2[DSA_BLOCK:]))


def _dsa_kernel(bound_ref, q1_ref, k1_ref, v1_ref, q4_ref, k4_ref, v4_ref, q16_ref, k16_ref, v16_ref,
                y_ref, o_scr, l_scr):
    seq = q1_ref.shape[1]
    refs = ((q1_ref, k1_ref, v1_ref), (q4_ref, k4_ref, v4_ref), (q16_ref, k16_ref, v16_ref))
    blocks_per_span = DSA_SPAN // DSA_BLOCK
    n_steps = DSA_PATTERNS[0][0] // DSA_PATTERNS[0][1]
    assert all(w // d == n_steps for w, d in DSA_PATTERNS)
    bias_inner = _dsa_band_bias(DSA_BLOCK, n_steps)
    bias_first = _dsa_band_bias(0, n_steps)

    def run(shift):
        inner, first = (bias_inner, bias_first) if shift is None else (bias_inner - shift, bias_first - shift)

        def span(sp, carry):
            for p, (_, dil) in enumerate(DSA_PATTERNS):
                q_ref, k_ref, v_ref = refs[p]
                per_res = blocks_per_span // dil

                def group(gi, c, p=p, dil=dil, per_res=per_res, q_ref=q_ref, k_ref=k_ref, v_ref=v_ref):
                    units = []
                    for u in range(DSA_GROUP):
                        idx = gi * DSA_GROUP + u
                        r = idx // per_res
                        jb = idx % per_res
                        m0 = pl.multiple_of(sp * (DSA_SPAN // dil) + jb * DSA_BLOCK, DSA_BLOCK)
                        start = pl.multiple_of(jnp.maximum(m0 - DSA_BLOCK, 0), DSA_BLOCK)
                        if dil == 1:
                            q = q_ref[0, pl.ds(m0, DSA_BLOCK), :]
                            kw = k_ref[0, pl.ds(start, 2 * DSA_BLOCK), :]
                            vw = v_ref[0, pl.ds(start, 2 * DSA_BLOCK), :]
                        else:
                            q = q_ref[0, r, pl.ds(m0, DSA_BLOCK), :]
                            kw = k_ref[0, r, pl.ds(start, 2 * DSA_BLOCK), :]
                            vw = v_ref[0, r, pl.ds(start, 2 * DSA_BLOCK), :]
                        may_start_sequence = u % per_res == 0
                        s = _dsa_logits(q, kw, jnp.where(m0 == 0, first, inner) if may_start_sequence else inner)
                        rows = pl.ds(r + dil * jb * DSA_BLOCK, DSA_BLOCK, stride=dil) if dil > 1 else \
                            pl.ds(pl.multiple_of(jb * DSA_BLOCK, DSA_BLOCK), DSA_BLOCK)
                        units.append((s, vw, rows))
                    results = [_dsa_softmax_pv(s, vw, shift) for s, vw, _ in units]
                    for (o, lse), (_, _, rows) in zip(results, units):
                        o_scr[p, rows, :] = o
                        l_scr[p, rows, :] = lse
                    return c

                lax.fori_loop(0, blocks_per_span // DSA_GROUP, group, 0)

            l1, l2, l3 = l_scr[0], l_scr[1], l_scr[2]
            lm = jnp.maximum(jnp.maximum(l1, l2), l3)
            e1, e2, e3 = jnp.exp(l1 - lm), jnp.exp(l2 - lm), jnp.exp(l3 - lm)
            y = (e1 * o_scr[0] + e2 * o_scr[1] + e3 * o_scr[2]) / (e1 + e2 + e3)
            y_ref[0, pl.ds(pl.multiple_of(sp * DSA_SPAN, DSA_SPAN), DSA_SPAN), :] = y.astype(y_ref.dtype)
            return carry

        lax.fori_loop(0, seq // DSA_SPAN, span, 0)

    bound = bound_ref[0]

    @pl.when(bound <= DSA_MAX_SHIFT)
    def _():
        run(bound)

    @pl.when(bound > DSA_MAX_SHIFT)
    def _():
        run(None)


def _dsa(dsa1, dsa4, dsa16, bound):
    bsz, seq, _ = dsa1.shape
    nat = lambda off: pl.BlockSpec((1, seq, LANES), lambda b, hp: (b, 0, off + hp))
    res = lambda dil, off: pl.BlockSpec((1, dil, seq // dil, LANES), lambda b, hp: (b, 0, 0, off + hp))
    return pl.pallas_call(
        _dsa_kernel,
        grid=(bsz, 2),
        in_specs=[pl.BlockSpec(memory_space=pltpu.SMEM),
                  nat(0), nat(2), nat(4), res(4, 0), res(4, 2), res(4, 4), res(16, 0), res(16, 2), res(16, 4)],
        out_specs=pl.BlockSpec((1, seq, LANES), lambda b, hp: (b, 0, hp)),
        out_shape=jax.ShapeDtypeStruct((bsz, seq, BRANCH_WIDTH), BF16),
        scratch_shapes=[pltpu.VMEM((3, DSA_SPAN, LANES), F32), pltpu.VMEM((3, DSA_SPAN, LANES), F32)],
        compiler_params=_cparams(("parallel", "parallel")),
        name="dsa",
    )(bound, dsa1, dsa1, dsa1, dsa4, dsa4, dsa4, dsa16, dsa16, dsa16)


def _proj_weights(w_in):
    lead = w_in[..., :1792].astype(BF16)
    gg = jnp.pad(w_in[..., 1792:1808].astype(BF16), ((0, 0), (0, 0), (0, LANES - GLA_GATE_RANK)))
    dsa = w_in[..., 1808:2576].astype(BF16)
    wp = jnp.concatenate([lead, dsa, gg], axis=-1)
    return wp, w_in[..., 2576:].astype(BF16)


def _rope_tables(seq):
    inv = ROPE_THETA ** (-jnp.arange(0, HEAD_DIM, 2, dtype=F32) / HEAD_DIM)
    ang = jnp.arange(seq, dtype=F32)[:, None] * inv[None, :]
    cos, sin = jnp.cos(ang), jnp.sin(ang)
    cos_t = jnp.tile(jnp.concatenate([cos, cos], axis=1), (1, 2))
    sin_t = jnp.tile(jnp.concatenate([-sin, sin], axis=1), (1, 2))
    return cos_t, sin_t


def _pick_tile(n, pref):
    t = min(pref, n)
    while n % t:
        t //= 2
    return t


def kernel(x, ffn1_norm, ffn1_w_in, ffn1_w_out, mix_norm, w_in, s5_lam_re, s5_lam_im, s5_log_step, s5_b_re, s5_b_im, s5_c_re, s5_c_im, s5_d, s5_w_glu, gla_w_gate, gla_b_gate, gla_norm, dsa_q_norm, dsa_k_norm, w_branch, w_out, ffn2_norm, ffn2_w_in, ffn2_w_out):
    bsz, seq, _ = x.shape
    tokens = bsz * seq
    depth = ffn1_norm.shape[0]
    assert seq % DSA_SPAN == 0 and seq // DSA_PATTERNS[-1][1] >= 2 * DSA_BLOCK and seq >= SB_WIN
    tm_a = _pick_tile(seq, 512)
    tm_b = _pick_tile(tokens, 512)
    s5_rows = seq // S5_CHUNK
    s5_tile = min(S5_ROW_TILE, s5_rows)
    cos_t, sin_t = _rope_tables(seq)

    wp_all, wg_all = _proj_weights(w_in)
    w1a_all, w2a_all = ffn1_w_in.astype(BF16), ffn1_w_out.astype(BF16)
    w1b_all, w2b_all = ffn2_w_in.astype(BF16), ffn2_w_out.astype(BF16)
    wglu_all, wb_all, wo_all = s5_w_glu.astype(BF16), w_branch.astype(BF16), w_out.astype(BF16)
    s5_ops = jax.vmap(functools.partial(_s5_operators, row_tile=s5_tile))(
        s5_lam_re, s5_lam_im, s5_log_step, s5_b_re, s5_b_im, s5_c_re, s5_c_im, s5_d)

    h = x.reshape(tokens, D_MODEL)
    for l in range(depth):
        row = lambda p: p[l].reshape(1, -1)
        qn = jnp.tile(row(dsa_q_norm), (1, 4))
        kn = jnp.tile(row(dsa_k_norm), (1, 4))
        wgate = jnp.pad(gla_w_gate[l], ((0, LANES - GLA_GATE_RANK), (0, 0))).astype(BF16)
        h, sb, s5u, y_gla, dsa1, dsa4, dsa16 = _ffn_proj(
            h, row(ffn1_norm), w1a_all, w2a_all, row(mix_norm), wp_all, qn, kn, cos_t, sin_t,
            wgate, row(gla_b_gate), row(gla_norm), bsz, seq, tm_a, l)

        y_sb = _stick_breaking(sb.reshape(bsz, seq, 768)).reshape(tokens, BRANCH_WIDTH)

        y_s5 = _s5(s5u.reshape(bsz, s5_rows, S5_CHUNK * 256), *s5_ops, l).reshape(tokens // S5_CHUNK, S5_CHUNK * 256)

        bound = (1.02 * HEAD_DIM ** 0.5 * jnp.max(jnp.abs(dsa_q_norm[l])) * jnp.max(jnp.abs(dsa_k_norm[l]))).reshape(1)
        y_dsa = _dsa(dsa1.reshape(bsz, seq, 768), dsa4, dsa16, bound).reshape(tokens, BRANCH_WIDTH)

        h = _merge_ffn(h, y_sb, y_s5, y_gla, y_dsa, row(mix_norm), wg_all, wglu_all, wb_all, wo_all,
                       row(ffn2_norm), w1b_all, w2b_all, tm_b, l)
    return h.reshape(bsz, seq, D_MODEL)
```

```python
import functools
import math

import jax
import jax.numpy as jnp
from jax import lax
from jax.experimental import pallas as pl
from jax.experimental.pallas import tpu as pltpu

D_MODEL = 1024
HEAD_DIM = 64
BRANCH_WIDTH = 256
N_BRANCH = 4
S5_GROUPS = 16
S5_GROUP_CH = 16
S5_STATE = 64
GLA_HEADS = 4
GLA_DK = 32
GLA_DV = 64
GLA_GATE_RANK = 16
GLA_GATE_NORM = 16.0
DSA_PATTERNS = ((128, 1), (512, 4), (2048, 16))
DSA_BLOCK = 128
ROPE_THETA = 10000.0
D_FF = 2816
RMS_EPS = 1e-6

LANES = 128
VMEM_LIMIT_BYTES = 58 * 1024 * 1024
FF_CHUNK = 256
N_FF_CHUNKS = D_FF // FF_CHUNK
S5_CHUNK = 8
S5_ROW_TILE = 256
GLA_BLOCK = 64
GLA_LEVELS = 6
GLA_SUB = 128
SB_Q = 64
SB_WIN = 4 * SB_Q
SB_GROUP = 8
SB_TAIL = SB_WIN - SB_Q
SB_LOG_ZERO = -104.0
NEG_BIG = -1e30
DSA_GROUP = 8
DSA_SPAN = DSA_PATTERNS[-1][1] * DSA_BLOCK
DSA_MAX_SHIFT = 30.0

PROJ_SB = (0, 768)
PROJ_S5 = (768, 1024)
PROJ_GLA = (1024, 1792)
PROJ_DSA = (1792, 2560)
PROJ_GG = (2560, 2688)

F32 = jnp.float32
BF16 = jnp.bfloat16


def _cparams(sem):
    return pltpu.CompilerParams(dimension_semantics=sem, vmem_limit_bytes=VMEM_LIMIT_BYTES)


def _const_spec(shape, layer=None):
    if layer is None:
        zeros = (0,) * len(shape)
        return pl.BlockSpec(shape, lambda *_: zeros, pipeline_mode=pl.Buffered(1))
    index = (layer,) + (0,) * (len(shape) - 1)
    return pl.BlockSpec((None,) + tuple(shape[1:]), lambda *_: index, pipeline_mode=pl.Buffered(1))


def _rms_rows(x, g):
    ms = jnp.mean(x * x, axis=-1, keepdims=True)
    return x * lax.rsqrt(ms + RMS_EPS) * g


def _dot(a, b):
    return jnp.dot(a, b, preferred_element_type=F32)


def _dot_nt(a, b):
    return lax.dot_general(a, b, (((1,), (1,)), ((), ())), preferred_element_type=F32)


def _dot_tn(a, b):
    return lax.dot_general(a, b, (((0,), (0,)), ((), ())), preferred_element_type=F32)


def _round_robin(*stage_generators):
    live = list(stage_generators)
    while live:
        live = [g for g in live if next(g, StopIteration) is not StopIteration]
        yield


def _swiglu_into(acc_ref, xb, w1_ref, w2_ref, between=None):
    for c in range(N_FF_CHUNKS):
        lo, hi = c * FF_CHUNK, (c + 1) * FF_CHUNK
        a = _dot(xb, w1_ref[:, lo:hi])
        b = _dot(xb, w1_ref[:, D_FF + lo:D_FF + hi])
        hm = (a * jax.nn.sigmoid(a) * b).astype(BF16)
        part = _dot(hm, w2_ref[lo:hi, :])
        if c == 0:
            acc_ref[...] = part
        else:
            acc_ref[...] += part
        if between is not None:
            next(between, None)


def _group_mean_matrix(width, group):
    r = lax.broadcasted_iota(jnp.int32, (width, width), 0) // group
    c = lax.broadcasted_iota(jnp.int32, (width, width), 1) // group
    return jnp.where(r == c, 1.0 / group, 0.0).astype(BF16)


def _group_rms(x, gain):
    gm = _group_mean_matrix(x.shape[-1], HEAD_DIM)
    ms = _dot((x * x).astype(BF16), gm)
    return x * lax.rsqrt(ms + RMS_EPS) * gain


def _swap_half_heads(x):
    half = HEAD_DIM // 2
    outs = []
    for s in range(x.shape[-1] // LANES):
        xs = x[:, s * LANES:(s + 1) * LANES]
        lane = lax.broadcasted_iota(jnp.int32, xs.shape, 1)
        up = pltpu.roll(xs, LANES - half, axis=1)
        down = pltpu.roll(xs, half, axis=1)
        outs.append(jnp.where((lane % HEAD_DIM) < half, up, down))
    return jnp.concatenate(outs, axis=-1)


def _ffn_proj_tile(x_ref, n1_ref, w1_ref, w2_ref, nm_ref, wp_ref, qn_ref, kn_ref, cos_ref, sin_ref,
                   h_ref, sb_ref, s5_ref, dsa1_ref, dsa4_ref, dsa16_ref, acc_ref, slab_ref, glax_ref, glag_ref,
                   sbh_ref, between, after_ffn):
    tm = x_ref.shape[0]
    x = x_ref[...]
    xb = _rms_rows(x, n1_ref[...]).astype(BF16)
    _swiglu_into(acc_ref, xb, w1_ref, w2_ref, between=between)
    after_ffn()
    h = x + 0.5 * acc_ref[...]
    h_ref[...] = h
    ub = _rms_rows(h, nm_ref[...]).astype(BF16)

    dsa = _dot(ub, wp_ref[:, PROJ_DSA[0]:PROJ_DSA[1]])
    cos = jnp.concatenate([cos_ref[...]] * 2, axis=-1)
    sin = jnp.concatenate([sin_ref[...]] * 2, axis=-1)
    parts = []
    for i, gain_ref in enumerate((qn_ref, kn_ref)):
        t = _group_rms(dsa[:, i * 256:(i + 1) * 256], gain_ref[...])
        parts.append(t * cos + _swap_half_heads(t) * sin)
    parts.append(dsa[:, 512:768])
    dsa = jnp.concatenate(parts, axis=-1)
    dsa1_ref[...] = dsa.astype(BF16)
    for s in range(6):
        slab_ref[s] = dsa[:, s * LANES:(s + 1) * LANES]
    for dil, ref in ((4, dsa4_ref), (16, dsa16_ref)):
        for r in range(dil):
            for s in range(6):
                rows = slab_ref[s, pl.ds(r, tm // dil, stride=dil), :]
                ref[0, r, :, s * LANES:(s + 1) * LANES] = rows.astype(BF16)

    s5 = _dot(ub, wp_ref[:, PROJ_S5[0]:PROJ_S5[1]])
    for s in range(2):
        slab_ref[6 + s] = s5[:, s * LANES:(s + 1) * LANES]
    for j in range(S5_CHUNK):
        for s in range(2):
            col = j * 256 + s * LANES
            s5_ref[:, col:col + LANES] = slab_ref[6 + s, pl.ds(j, tm // S5_CHUNK, stride=S5_CHUNK), :]

    sb = _dot(ub, wp_ref[:, PROJ_SB[0]:PROJ_SB[1]]).astype(BF16)
    sb_ref[...] = sb
    tail = sbh_ref[tm:tm + SB_TAIL, :]
    sbh_ref[0:SB_TAIL, :] = tail
    sbh_ref[SB_TAIL:, :] = sb
    glax_ref[...] = _dot(ub, wp_ref[:, PROJ_GLA[0]:PROJ_GLA[1]]).astype(BF16)
    glag_ref[...] = _dot(ub, wp_ref[:, PROJ_GG[0]:PROJ_GG[1]])


def _ffn_proj_kernel(x_ref, n1_ref, w1_ref, w2_ref, nm_ref, wp_ref, qn_ref, kn_ref, cos_ref, sin_ref,
                     wgate_ref, bgate_ref, gnorm_ref,
                     h_ref, sb_ref, s5_ref, ygla_ref, ysb_ref, sbf_ref, dsa1_ref, dsa4_ref, dsa16_ref,
                     acc_ref, slab_ref, glax_ref, glag_ref, glas_ref, sbh_ref, *, seq_tiles):
    step = pl.program_id(0)
    n_tiles = pl.num_programs(0) - 1

    @pl.when(step == 0)
    def _():
        glax_ref[...] = jnp.zeros_like(glax_ref)
        glag_ref[...] = jnp.zeros_like(glag_ref)
        glas_ref[...] = jnp.zeros_like(glas_ref)
        sbh_ref[...] = jnp.zeros_like(sbh_ref)

    def start_mixers():
        starts_sequence = lax.rem(step - 1, seq_tiles) == 0
        state = jnp.where(starts_sequence, 0.0, glas_ref[...])
        stacked, masks = _gla_constants()
        gla, sbw = {}, {}
        stages = _round_robin(
            _gla_stages(glax_ref[...], glag_ref[...], state, wgate_ref[...], bgate_ref[...], gnorm_ref[...],
                        stacked, masks, gla),
            _sb_window_stages(sbh_ref, jnp.logical_not(starts_sequence), sbw))
        return stages, gla, sbw

    def finish_mixers(stages, gla, sbw):
        for _ in stages:
            pass
        ygla_ref[...] = gla["y"].astype(ygla_ref.dtype)
        glas_ref[...] = gla["state"]
        ysb_ref[...] = sbw["y"].astype(ysb_ref.dtype)
        sbf_ref[...] = jnp.full(sbf_ref.shape, sbw["cmax"], F32)

    @pl.when(step < n_tiles)
    def _():
        stages, gla, sbw = start_mixers()
        _ffn_proj_tile(x_ref, n1_ref, w1_ref, w2_ref, nm_ref, wp_ref, qn_ref, kn_ref, cos_ref, sin_ref,
                       h_ref, sb_ref, s5_ref, dsa1_ref, dsa4_ref, dsa16_ref, acc_ref, slab_ref, glax_ref, glag_ref,
                       sbh_ref, between=stages, after_ffn=lambda: finish_mixers(stages, gla, sbw))

    @pl.when(step == n_tiles)
    def _():
        finish_mixers(*start_mixers())


def _ffn_proj(x2, n1, w1, w2, nm, wp, qn, kn, cos_t, sin_t, wgate, bgate, gnorm, bsz, seq, tm, layer):
    tokens = x2.shape[0]
    seq_tiles = seq // tm
    n_tiles = tokens // tm
    cur = lambda i: jnp.minimum(i, n_tiles - 1)
    row = lambda i: (cur(i), 0)
    lag = lambda i: (jnp.maximum(i - 1, 0), 0)
    tab = lambda i: (cur(i) % seq_tiles, 0)
    res = lambda i: (cur(i) // seq_tiles, 0, cur(i) % seq_tiles, 0)
    tile = lambda w: pl.BlockSpec((tm, w), row)
    s5_w = S5_CHUNK * 256
    return pl.pallas_call(
        functools.partial(_ffn_proj_kernel, seq_tiles=seq_tiles),
        grid=(n_tiles + 1,),
        in_specs=[tile(D_MODEL), _const_spec(n1.shape), _const_spec(w1.shape, layer), _const_spec(w2.shape, layer),
                  _const_spec(nm.shape), _const_spec(wp.shape, layer), _const_spec(qn.shape), _const_spec(kn.shape),
                  pl.BlockSpec((tm, LANES), tab), pl.BlockSpec((tm, LANES), tab),
                  _const_spec(wgate.shape), _const_spec(bgate.shape), _const_spec(gnorm.shape)],
        out_specs=[tile(D_MODEL), tile(768), pl.BlockSpec((tm // S5_CHUNK, s5_w), row),
                   pl.BlockSpec((tm, BRANCH_WIDTH), lag), pl.BlockSpec((tm, BRANCH_WIDTH), lag),
                   pl.BlockSpec((8, LANES), lag),
                   tile(768), pl.BlockSpec((1, 4, tm // 4, 768), res), pl.BlockSpec((1, 16, tm // 16, 768), res)],
        out_shape=[jax.ShapeDtypeStruct((tokens, D_MODEL), F32),
                   jax.ShapeDtypeStruct((tokens, 768), BF16),
                   jax.ShapeDtypeStruct((tokens // S5_CHUNK, s5_w), F32),
                   jax.ShapeDtypeStruct((tokens, BRANCH_WIDTH), BF16),
                   jax.ShapeDtypeStruct((tokens, BRANCH_WIDTH), BF16),
                   jax.ShapeDtypeStruct((n_tiles * 8, LANES), F32),
                   jax.ShapeDtypeStruct((tokens, 768), BF16),
                   jax.ShapeDtypeStruct((bsz, 4, seq // 4, 768), BF16),
                   jax.ShapeDtypeStruct((bsz, 16, seq // 16, 768), BF16)],
        scratch_shapes=[pltpu.VMEM((tm, D_MODEL), F32), pltpu.VMEM((8, tm, LANES), F32),
                        pltpu.VMEM((tm, 768), BF16), pltpu.VMEM((tm, LANES), F32),
                        pltpu.VMEM((GLA_HEADS * GLA_DV, GLA_HEADS * GLA_DK), F32),
                        pltpu.VMEM((SB_TAIL + tm, 768), BF16)],
        compiler_params=_cparams(("arbitrary",)),
        name="ffn_proj",
    )(x2, n1, w1, w2, nm, wp, qn, kn, cos_t, sin_t, wgate, bgate, gnorm)


def _merge_ffn_kernel(h_ref, ysb_ref, ys5_ref, ygla_ref, ydsa_ref,
                      nm_ref, wg_ref, wglu_ref, wb_ref, wo_ref, n2_ref, w1_ref, w2_ref,
                      out_ref, acc_ref, slab_ref):
    tm = h_ref.shape[0]
    proj = {n: _dot(ref[...], wb_ref[n]) for n, ref in ((0, ysb_ref), (2, ygla_ref), (3, ydsa_ref))}

    for j in range(S5_CHUNK):
        for s in range(2):
            col = j * 256 + s * LANES
            slab_ref[s, pl.ds(j, tm // S5_CHUNK, stride=S5_CHUNK), :] = ys5_ref[:, col:col + LANES].astype(F32)
    y_s5 = jnp.concatenate([slab_ref[0], slab_ref[1]], axis=-1).astype(BF16)
    glu = _dot(y_s5, wglu_ref[...])
    y_s5 = glu[:, :BRANCH_WIDTH] * jax.nn.sigmoid(glu[:, BRANCH_WIDTH:])
    proj[1] = _dot(y_s5.astype(BF16), wb_ref[1])

    h = h_ref[...]
    ub = _rms_rows(h, nm_ref[...]).astype(BF16)
    mixed = jnp.zeros(h.shape, F32)
    for n in range(N_BRANCH):
        gate = jax.nn.sigmoid(_dot(ub, wg_ref[:, n * D_MODEL:(n + 1) * D_MODEL]))
        mixed = mixed + gate * proj[n]
    h2 = h + _dot(mixed.astype(BF16), wo_ref[...])

    xb = _rms_rows(h2, n2_ref[...]).astype(BF16)
    _swiglu_into(acc_ref, xb, w1_ref, w2_ref)
    out_ref[...] = h2 + 0.5 * acc_ref[...]


def _merge_ffn(h, ysb, ys5, ygla, ydsa, nm, wg, wglu, wb, wo, n2, w1, w2, tm, layer):
    tokens = h.shape[0]
    row = lambda i: (i, 0)
    tile = lambda w: pl.BlockSpec((tm, w), row)
    consts = (nm, wg, wglu, wb, wo, n2, w1, w2)
    return pl.pallas_call(
        _merge_ffn_kernel,
        grid=(tokens // tm,),
        in_specs=[tile(D_MODEL), tile(BRANCH_WIDTH), pl.BlockSpec((tm // S5_CHUNK, S5_CHUNK * 256), row),
                  tile(BRANCH_WIDTH), tile(BRANCH_WIDTH)]
                 + [_const_spec(c.shape, None if c.ndim == 2 and c.shape[0] == 1 else layer) for c in consts],
        out_specs=tile(D_MODEL),
        out_shape=jax.ShapeDtypeStruct((tokens, D_MODEL), F32),
        scratch_shapes=[pltpu.VMEM((tm, D_MODEL), F32), pltpu.VMEM((2, tm, LANES), F32)],
        compiler_params=_cparams(("parallel",)),
        name="merge_ffn",
    )(h, ysb, ys5, ygla, ydsa, *consts)


def _later_key_matrix(n):
    j = lax.broadcasted_iota(jnp.int32, (n, n), 0)
    s = lax.broadcasted_iota(jnp.int32, (n, n), 1)
    return jnp.where(j > s, 1.0, 0.0).astype(BF16)


def _sb_scores(qm, k, keep=None):
    z = _dot_nt(qm, k)
    if keep is not None:
        z = jnp.where(keep, z, NEG_BIG)
    lk = -(jnp.maximum(z, 0.0) + jnp.log(1.0 + jnp.exp(-jnp.abs(z))))
    return z, lk


def _sb_window_stages(hist_ref, not_first, result):
    rows = hist_ref.shape[0] - SB_TAIL
    n_blk = rows // SB_Q
    u_win = _later_key_matrix(SB_WIN)
    lane_q = lax.broadcasted_iota(jnp.int32, (SB_Q, LANES), 1)
    col = lax.broadcasted_iota(jnp.int32, (2 * SB_Q, SB_WIN), 1)
    key_minus_query = col - lax.broadcasted_iota(jnp.int32, (2 * SB_Q, SB_WIN), 0) % SB_Q
    strict = key_minus_query < SB_TAIL
    first = lane_q < HEAD_DIM
    outs = [[None] * n_blk for _ in range(2)]
    cmax = jnp.float32(-jnp.inf)
    for pair in range(2):
        q_cols = slice(pair * LANES, (pair + 1) * LANES)
        k_cols = slice(256 + pair * LANES, 256 + (pair + 1) * LANES)
        v_cols = slice(512 + pair * LANES, 512 + (pair + 1) * LANES)
        zs, lks = [], []
        for qb in range(n_blk):
            qs = hist_ref[SB_TAIL + qb * SB_Q:SB_TAIL + (qb + 1) * SB_Q, q_cols] * (HEAD_DIM ** -0.5)
            qm2 = jnp.concatenate([jnp.where(first, qs, jnp.zeros_like(qs)),
                                   jnp.where(first, jnp.zeros_like(qs), qs)], axis=0)
            keep = strict
            if qb * SB_Q < SB_TAIL:
                first_col = jnp.where(not_first, 0, SB_TAIL - qb * SB_Q)
                keep = strict & (col >= first_col)
            z, lk = _sb_scores(qm2, hist_ref[qb * SB_Q:qb * SB_Q + SB_WIN, k_cols], keep)
            zs.append(z)
            lks.append(lk)
        yield
        later_all = _dot(jnp.concatenate(lks, axis=0).astype(BF16), u_win)
        yield
        for qb in range(n_blk):
            later = later_all[qb * 2 * SB_Q:(qb + 1) * 2 * SB_Q]
            w = jnp.exp(zs[qb] + lks[qb] + later)
            o2 = _dot(w.astype(BF16), hist_ref[qb * SB_Q:qb * SB_Q + SB_WIN, v_cols])
            outs[pair][qb] = jnp.where(first, o2[:SB_Q], o2[SB_Q:])
            left = jnp.max(later[:, 0:1] + lks[qb][:, 0:1])
            has_older = True if qb * SB_Q > SB_TAIL else not_first
            cmax = jnp.maximum(cmax, jnp.where(has_older, left, -jnp.inf))
        yield
    result["y"] = jnp.concatenate([jnp.concatenate(outs[pair], axis=0) for pair in range(2)], axis=1)
    result["cmax"] = cmax


def _sb_kernel(q_ref, k_ref, v_ref, o_ref):
    seq = q_ref.shape[1]
    u_win = _later_key_matrix(SB_WIN)
    u_blk = _later_key_matrix(SB_Q)
    lane_q = lax.broadcasted_iota(jnp.int32, (SB_Q, LANES), 1)
    n_streams = 2 * SB_GROUP
    key_minus_query = (lax.broadcasted_iota(jnp.int32, (2 * SB_Q, SB_WIN), 1)
                       - lax.broadcasted_iota(jnp.int32, (2 * SB_Q, SB_WIN), 0) % SB_Q)

    def q_group(qg, carry):
        starts, qms, zs, lks, vws = [], [], [], [], []
        for sub in range(SB_GROUP):
            t0 = pl.multiple_of((qg * SB_GROUP + sub) * SB_Q, SB_Q)
            start = pl.multiple_of(jnp.maximum(t0 - (SB_WIN - SB_Q), 0), SB_Q)
            q = q_ref[0, pl.ds(t0, SB_Q), :] * (HEAD_DIM ** -0.5)
            kw = k_ref[0, pl.ds(start, SB_WIN), :]
            vws.append(v_ref[0, pl.ds(start, SB_WIN), :])
            strict = key_minus_query < (t0 - start)
            starts.append(start)
            qm2 = jnp.concatenate(
                [jnp.where((lane_q // HEAD_DIM) == head, q, jnp.zeros_like(q)) for head in range(2)], axis=0)
            z, lk = _sb_scores(qm2, kw, strict)
            qms += [qm2[:SB_Q], qm2[SB_Q:]]
            zs.append(z)
            lks.append(lk)
        lk_all = jnp.concatenate(lks, axis=0)
        later_all = _dot(lk_all.astype(BF16), u_win)
        csums, outs = [], []
        for sub in range(SB_GROUP):
            rows = slice(sub * 2 * SB_Q, (sub + 1) * 2 * SB_Q)
            later = later_all[rows]
            w = jnp.exp(zs[sub] + lks[sub] + later)
            o2 = _dot(w.astype(BF16), vws[sub])
            c2 = later[:, 0:1] + lks[sub][:, 0:1]
            outs += [o2[:SB_Q], o2[SB_Q:]]
            csums += [c2[:SB_Q], c2[SB_Q:]]

        def cond(state):
            step, cmaxes, _, _ = state
            flags = [jnp.logical_and(starts[s // 2] - (step + 1) * SB_Q >= 0, cmaxes[s] > SB_LOG_ZERO)
                     for s in range(n_streams)]
            return functools.reduce(jnp.logical_or, flags)

        def body(state):
            step, _, csums, outs = state
            new_c, new_o = [], []
            for s in range(n_streams):
                nxt = starts[s // 2] - (step + 1) * SB_Q
                in_range = nxt >= 0
                p0 = pl.multiple_of(jnp.maximum(nxt, 0), SB_Q)
                kb = k_ref[0, pl.ds(p0, SB_Q), :]
                vb = v_ref[0, pl.ds(p0, SB_Q), :]
                z, lk = _sb_scores(qms[s], kb)
                later = _dot(lk.astype(BF16), u_blk)
                w = jnp.where(in_range, jnp.exp(z + lk + later + csums[s]), 0.0)
                new_o.append(outs[s] + _dot(w.astype(BF16), vb))
                new_c.append(csums[s] + jnp.where(in_range, later[:, 0:1] + lk[:, 0:1], 0.0))
            return step + 1, tuple(jnp.max(c) for c in new_c), tuple(new_c), tuple(new_o)

        init = (jnp.int32(0), tuple(jnp.max(c) for c in csums), tuple(csums), tuple(outs))
        _, _, _, outs = lax.while_loop(cond, body, init)
        for sub in range(SB_GROUP):
            t0 = pl.multiple_of((qg * SB_GROUP + sub) * SB_Q, SB_Q)
            o_ref[0, pl.ds(t0, SB_Q), :] = jnp.where(
                lane_q < HEAD_DIM, outs[2 * sub], outs[2 * sub + 1]).astype(o_ref.dtype)
        return carry

    lax.fori_loop(0, seq // (SB_Q * SB_GROUP), q_group, 0)


def _stick_breaking(sb3):
    bsz, seq, _ = sb3.shape
    spec = lambda off: pl.BlockSpec((1, seq, LANES), lambda b, hp: (b, 0, off + hp))
    return pl.pallas_call(
        _sb_kernel,
        grid=(bsz, 2),
        in_specs=[spec(0), spec(2), spec(4)],
        out_specs=pl.BlockSpec((1, seq, LANES), lambda b, hp: (b, 0, hp)),
        out_shape=jax.ShapeDtypeStruct((bsz, seq, BRANCH_WIDTH), BF16),
        compiler_params=_cparams(("parallel", "parallel")),
        name="stick_breaking",
    )(sb3, sb3, sb3)


def _s5_kernel(u_ref, m_ref, e_ref, g_ref, pw_ref, d_ref, y_ref, state_ref):
    half = S5_GROUPS * S5_STATE

    @pl.when(pl.program_id(1) == 0)
    def _():
        state_ref[...] = jnp.zeros_like(state_ref)

    u = u_ref[0]
    ub = u.astype(BF16)
    rows = u.shape[0]

    def cmul(ar, ai, xr, xi):
        return ar * xr - ai * xi, ar * xi + ai * xr

    w = _dot(ub, e_ref[...])
    wr, wi = w[:, :half], w[:, half:]

    n_c = S5_GROUPS * S5_GROUP_CH
    slots = [ub[:, j * n_c:(j + 1) * n_c] for j in range(S5_CHUNK)]
    cols = []
    for i in range(S5_CHUNK):
        acc = _dot(slots[0], m_ref[i])
        for j in range(1, i + 1):
            acc = acc + _dot(slots[j], m_ref[i - j])
        cols.append(acc)
    y = jnp.concatenate(cols, axis=1)

    row = lax.broadcasted_iota(jnp.int32, (rows, LANES), 0)
    xrs, xis = [], []
    for t in range(half // LANES):
        re, im = slice(t * LANES, (t + 1) * LANES), slice(half + t * LANES, half + (t + 1) * LANES)
        tr, ti = wr[:, re], wi[:, re]
        sr, si = state_ref[0:1, re], state_ref[0:1, im]
        cr, ci = cmul(pw_ref[0:1, re], pw_ref[0:1, im], sr, si)
        tr = jnp.where(row == 0, tr + cr, tr)
        ti = jnp.where(row == 0, ti + ci, ti)
        step = 0
        d = 1
        while d < rows:
            pr = jnp.where(row >= d, pltpu.roll(tr, d, axis=0), 0.0)
            pi = jnp.where(row >= d, pltpu.roll(ti, d, axis=0), 0.0)
            mr, mi = cmul(pw_ref[step:step + 1, re], pw_ref[step:step + 1, im], pr, pi)
            tr, ti = tr + mr, ti + mi
            d *= 2
            step += 1
        xrs.append(jnp.where(row == 0, sr, pltpu.roll(tr, 1, axis=0)))
        xis.append(jnp.where(row == 0, si, pltpu.roll(ti, 1, axis=0)))
        state_ref[0:1, re] = tr[rows - 1:rows, :]
        state_ref[0:1, im] = ti[rows - 1:rows, :]
    xr, xi = jnp.concatenate(xrs, axis=1), jnp.concatenate(xis, axis=1)

    y = y + _dot(xr.astype(BF16), g_ref[:half, :]) + _dot(xi.astype(BF16), g_ref[half:, :])
    y = y + d_ref[...] * u
    y_ref[0] = jax.nn.gelu(y).astype(y_ref.dtype)


def _s5(u3, m, e, g, pw, dskip, layer):
    bsz, rows, width = u3.shape
    tr = min(S5_ROW_TILE, rows)
    blk = pl.BlockSpec((1, tr, width), lambda b, i: (b, i, 0))
    return pl.pallas_call(
        _s5_kernel,
        grid=(bsz, rows // tr),
        in_specs=[blk] + [_const_spec(a.shape, layer) for a in (m, e, g, pw, dskip)],
        out_specs=blk,
        out_shape=jax.ShapeDtypeStruct(u3.shape, BF16),
        scratch_shapes=[pltpu.VMEM((8, width), F32)],
        compiler_params=_cparams(("parallel", "arbitrary")),
        name="s5",
    )(u3, m, e, g, pw, dskip)


def _s5_operators(lam_re, lam_im, log_step, b_re, b_im, c_re, c_im, d_skip, row_tile):
    n_g, n_p, n_h, L = S5_GROUPS, S5_STATE, S5_GROUP_CH, S5_CHUNK
    n_c, n_s = n_g * n_h, n_g * n_p
    dt = jnp.exp(log_step)[:, None]

    def lam_bar_pow(k):
        mag = jnp.exp(k[:, None, None] * (lam_re * dt)[None])
        ang = k[:, None, None] * (lam_im * dt)[None]
        return mag * jnp.cos(ang), mag * jnp.sin(ang)

    pr, pi = lam_bar_pow(jnp.arange(L + 1, dtype=F32))
    den = lam_re * lam_re + lam_im * lam_im
    fr = ((pr[1] - 1.0) * lam_re + pi[1] * lam_im) / den
    fi = (pi[1] * lam_re - (pr[1] - 1.0) * lam_im) / den
    bbr = fr[..., None] * b_re - fi[..., None] * b_im
    bbi = fr[..., None] * b_im + fi[..., None] * b_re
    tr = pr[:L, :, :, None] * bbr[None] - pi[:L, :, :, None] * bbi[None]
    ti = pr[:L, :, :, None] * bbi[None] + pi[:L, :, :, None] * bbr[None]

    def expand(small, rows_per_group, cols_per_group):
        rows = small.shape[-2]
        src = lax.broadcasted_iota(jnp.int32, (cols_per_group, n_g * cols_per_group), 0)
        dst = lax.broadcasted_iota(jnp.int32, (cols_per_group, n_g * cols_per_group), 1) % cols_per_group
        spread = jnp.where(src == dst, 1.0, 0.0).astype(BF16)
        tiled = jnp.einsum('...rc,cq->...rq', small.astype(BF16), spread, preferred_element_type=F32)
        r = lax.broadcasted_iota(jnp.int32, (rows, n_g * cols_per_group), 0) // rows_per_group
        c = lax.broadcasted_iota(jnp.int32, (rows, n_g * cols_per_group), 1) // cols_per_group
        return jnp.where(r == c, tiled, 0.0)

    hp = lax.Precision.HIGHEST
    kern = (jnp.einsum('ghp,kgpi->kghi', c_re, tr, precision=hp)
            - jnp.einsum('ghp,kgpi->kghi', c_im, ti, precision=hp))
    m_op = expand(kern.transpose(0, 1, 3, 2).reshape(L, n_c, n_h), n_h, n_h).astype(BF16)

    e_r = expand(tr[::-1].transpose(0, 1, 3, 2).reshape(L, n_c, n_p), n_h, n_p)
    e_i = expand(ti[::-1].transpose(0, 1, 3, 2).reshape(L, n_c, n_p), n_h, n_p)
    e_op = jnp.concatenate([e_r, e_i], axis=2).astype(BF16).reshape(L * n_c, 2 * n_s)

    clr = c_re[None] * pr[1:, :, None, :] - c_im[None] * pi[1:, :, None, :]
    cli = c_re[None] * pi[1:, :, None, :] + c_im[None] * pr[1:, :, None, :]
    g_r = expand(clr.transpose(0, 1, 3, 2).reshape(L, n_s, n_h), n_p, n_h)
    g_i = expand(cli.transpose(0, 1, 3, 2).reshape(L, n_s, n_h), n_p, n_h)
    g_blocks = jnp.concatenate([g_r, -g_i], axis=1).astype(BF16)
    g_op = jnp.concatenate([g_blocks[i] for i in range(L)], axis=1)

    n_steps = max(1, int(math.log2(row_tile)))
    sr, si = lam_bar_pow(L * 2.0 ** jnp.arange(n_steps, dtype=F32))
    pw = jnp.concatenate([sr.reshape(n_steps, n_s), si.reshape(n_steps, n_s)], axis=1)
    pw = jnp.pad(pw, ((0, (-n_steps) % 8), (0, 0)))
    dskip = jnp.tile(d_skip.reshape(1, n_c), (1, L))
    return m_op, e_op, g_op, pw.astype(F32), dskip.astype(F32)


def _gla_constants():
    n = GLA_SUB
    ri = lax.broadcasted_iota(jnp.int32, (n, n), 0)
    ci = lax.broadcasted_iota(jnp.int32, (n, n), 1)
    same = (ri // GLA_BLOCK) == (ci // GLA_BLOCK)
    mats = [same & (ci <= ri), same & (ci > ri)]
    masks = [ri == ci]
    ups, lows = [], []
    for lev in range(GLA_LEVELS):
        half = 1 << lev
        blk = 2 * half
        mid = (ri // blk) * blk + (half - 1)
        later = (ri % blk) >= half
        ups.append(later & (ci > mid) & (ci <= ri))
        lows.append(jnp.logical_not(later) & (ci > ri) & (ci <= mid))
        masks.append(((ri // blk) == (ci // blk)) & later & ((ci % blk) < half))
    stacked = jnp.concatenate([jnp.where(m, 1.0, 0.0).astype(BF16) for m in mats + ups + lows], axis=0)
    return stacked, masks


def _gla_stages(x, gg, state, wgate, bgate, norm, stacked, masks, result):
    n_rows = x.shape[0]
    n_sub = n_rows // GLA_SUB
    hk = GLA_HEADS * GLA_DK
    hv = GLA_HEADS * GLA_DV
    q = x[:, 0:hk].astype(F32) * (GLA_DK ** -0.5)
    k = x[:, hk:2 * hk].astype(F32)
    vb = x[:, 2 * hk:2 * hk + hv]
    r = x[:, 2 * hk + hv:].astype(F32)
    sub = lambda t, st: t[st * GLA_SUB:(st + 1) * GLA_SUB]

    z = _dot(gg.astype(BF16), wgate) + bgate
    g = -(jnp.maximum(-z, 0.0) + jnp.log(1.0 + jnp.exp(-jnp.abs(z)))) * (1.0 / GLA_GATE_NORM)
    gb = g.astype(BF16)
    yield

    def decay_sums(st, blocks):
        picked = jnp.concatenate([stacked[i * GLA_SUB:(i + 1) * GLA_SUB] for i in blocks], axis=0)
        both = _dot(picked, sub(gb, st))
        return [both[n * GLA_SUB:(n + 1) * GLA_SUB] for n in range(len(blocks))]

    def level_blocks(lev):
        return 2 + lev, 2 + GLA_LEVELS + lev

    lane_k = lax.broadcasted_iota(jnp.int32, (GLA_SUB, hk), 1) // GLA_DK
    att = [[jnp.zeros((GLA_SUB, GLA_SUB), F32) for _ in range(GLA_HEADS)] for _ in range(n_sub)]
    ahead = [decay_sums(st, level_blocks(0)) for st in range(n_sub)]
    for lev in range(-1, GLA_LEVELS):
        now, ahead = ahead, None
        for st in range(n_sub):
            qs, ks = sub(q, st), sub(k, st)
            if lev < 0:
                qt, kt = qs.astype(BF16), ks.astype(BF16)
            else:
                qt = (qs * jnp.exp(now[st][0])).astype(BF16)
                kt = (ks * jnp.exp(now[st][1])).astype(BF16)
            q_heads = jnp.concatenate(
                [jnp.where(lane_k == h, qt, jnp.zeros_like(qt)) for h in range(GLA_HEADS)], axis=0)
            a = _dot_nt(q_heads, kt)
            for h in range(GLA_HEADS):
                att[st][h] = jnp.where(masks[lev + 1], a[h * GLA_SUB:(h + 1) * GLA_SUB], att[st][h])
        if lev < 0:
            ahead = now
        elif lev + 1 < GLA_LEVELS:
            ahead = [decay_sums(st, level_blocks(lev + 1)) for st in range(n_sub)]
        else:
            ahead = [decay_sums(st, (0, 1)) for st in range(n_sub)]
        yield
    part = lambda st, i: ahead[st][i]

    lane_v = lax.broadcasted_iota(jnp.int32, (GLA_SUB, hv), 1) // GLA_DV
    inner = []
    for st in range(n_sub):
        vs = sub(vb, st)
        att_all = jnp.concatenate([a.astype(BF16) for a in att[st]], axis=1)
        v_heads = jnp.concatenate([jnp.where(lane_v == h, vs, jnp.zeros_like(vs)) for h in range(GLA_HEADS)], axis=0)
        inner.append(_dot(att_all, v_heads))
    yield

    row_blk = lax.broadcasted_iota(jnp.int32, (GLA_SUB, hk), 0) // GLA_BLOCK
    rs = lax.broadcasted_iota(jnp.int32, (hv, hk), 0) // GLA_DV
    cs = lax.broadcasted_iota(jnp.int32, (hv, hk), 1) // GLA_DK
    head_diag = rs == cs
    n_blk = GLA_SUB // GLA_BLOCK
    outs = []
    for st in range(n_sub):
        b, tail = part(st, 0), part(st, 1)
        q_dec = (sub(q, st) * jnp.exp(b)).astype(BF16)
        k_dec = (sub(k, st) * jnp.exp(tail)).astype(BF16)
        b_end = b + tail
        k_blocks = jnp.concatenate(
            [jnp.where(row_blk == n, k_dec, jnp.zeros_like(k_dec)) for n in range(n_blk)], axis=1)
        upd_all = _dot_tn(sub(vb, st), k_blocks)
        cross = []
        for blk in range(n_blk):
            r0 = blk * GLA_BLOCK
            cross.append(_dot_nt(q_dec[r0:r0 + GLA_BLOCK], state.astype(BF16)))
            upd = upd_all[:, blk * hk:(blk + 1) * hk]
            state = jnp.exp(b_end[r0:r0 + 1, :]) * state + jnp.where(head_diag, upd, 0.0)
        outs.append(inner[st] + jnp.concatenate(cross, axis=0))
    yield

    o = _group_rms(jnp.concatenate(outs, axis=0), norm)
    result["y"] = o * (r * jax.nn.sigmoid(r))
    result["state"] = state


def _dsa_band_bias(offset, n_steps):
    row = lax.broadcasted_iota(jnp.int32, (2 * DSA_BLOCK, 2 * DSA_BLOCK), 0) % DSA_BLOCK
    col = lax.broadcasted_iota(jnp.int32, (2 * DSA_BLOCK, 2 * DSA_BLOCK), 1)
    dist = offset + row - col
    return jnp.where((dist >= 0) & (dist <= n_steps), 0.0, NEG_BIG)


def _dsa_logits(q, kw, bias):
    lane_q = lax.broadcasted_iota(jnp.int32, (DSA_BLOCK, LANES), 1)
    q = q * (HEAD_DIM ** -0.5)
    qm2 = jnp.concatenate(
        [jnp.where((lane_q // HEAD_DIM) == head, q, jnp.zeros_like(q)) for head in range(2)], axis=0)
    return _dot_nt(qm2, kw) + bias


def _dsa_softmax_pv(s, vw, shift):
    if shift is None:
        m = jnp.max(s, axis=1, keepdims=True)
        p = jnp.exp(s - m)
        l = jnp.sum(p, axis=1, keepdims=True)
        o2 = _dot(p.astype(BF16), vw) / l
        lse2 = jnp.broadcast_to(m + jnp.log(l), (2 * DSA_BLOCK, LANES))
    else:
        p = jnp.exp(s)
        pv = _dot(p.astype(BF16), jnp.concatenate([vw, jnp.ones_like(vw)], axis=1))
        l = pv[:, LANES:]
        o2 = pv[:, :LANES] / l
        lse2 = shift + jnp.log(l)
    first = lax.broadcasted_iota(jnp.int32, (DSA_BLOCK, LANES), 1) < HEAD_DIM
    return (jnp.where(first, o2[:DSA_BLOCK], o2[DSA_BLOCK:]),
            jnp.where(first, lse2[:DSA_BLOCK], lse2[DSA_BLOCK:]))


def _dsa_kernel(bound_ref, q1_ref, k1_ref, v1_ref, q4_ref, k4_ref, v4_ref, q16_ref, k16_ref, v16_ref,
                y_ref, o_scr, l_scr):
    seq = q1_ref.shape[1]
    refs = ((q1_ref, k1_ref, v1_ref), (q4_ref, k4_ref, v4_ref), (q16_ref, k16_ref, v16_ref))
    blocks_per_span = DSA_SPAN // DSA_BLOCK
    n_steps = DSA_PATTERNS[0][0] // DSA_PATTERNS[0][1]
    assert all(w // d == n_steps for w, d in DSA_PATTERNS)
    bias_inner = _dsa_band_bias(DSA_BLOCK, n_steps)
    bias_first = _dsa_band_bias(0, n_steps)

    def run(shift):
        inner, first = (bias_inner, bias_first) if shift is None else (bias_inner - shift, bias_first - shift)

        def span(sp, carry):
            for p, (_, dil) in enumerate(DSA_PATTERNS):
                q_ref, k_ref, v_ref = refs[p]
                per_res = blocks_per_span // dil

                def group(gi, c, p=p, dil=dil, per_res=per_res, q_ref=q_ref, k_ref=k_ref, v_ref=v_ref):
                    units = []
                    for u in range(DSA_GROUP):
                        idx = gi * DSA_GROUP + u
                        r = idx // per_res
                        jb = idx % per_res
                        m0 = pl.multiple_of(sp * (DSA_SPAN // dil) + jb * DSA_BLOCK, DSA_BLOCK)
                        start = pl.multiple_of(jnp.maximum(m0 - DSA_BLOCK, 0), DSA_BLOCK)
                        if dil == 1:
                            q = q_ref[0, pl.ds(m0, DSA_BLOCK), :]
                            kw = k_ref[0, pl.ds(start, 2 * DSA_BLOCK), :]
                            vw = v_ref[0, pl.ds(start, 2 * DSA_BLOCK), :]
                        else:
                            q = q_ref[0, r, pl.ds(m0, DSA_BLOCK), :]
                            kw = k_ref[0, r, pl.ds(start, 2 * DSA_BLOCK), :]
                            vw = v_ref[0, r, pl.ds(start, 2 * DSA_BLOCK), :]
                        may_start_sequence = u % per_res == 0
                        s = _dsa_logits(q, kw, jnp.where(m0 == 0, first, inner) if may_start_sequence else inner)
                        rows = pl.ds(r + dil * jb * DSA_BLOCK, DSA_BLOCK, stride=dil) if dil > 1 else \
                            pl.ds(pl.multiple_of(jb * DSA_BLOCK, DSA_BLOCK), DSA_BLOCK)
                        units.append((s, vw, rows))
                    results = [_dsa_softmax_pv(s, vw, shift) for s, vw, _ in units]
                    for (o, lse), (_, _, rows) in zip(results, units):
                        o_scr[p, rows, :] = o
                        l_scr[p, rows, :] = lse
                    return c

                lax.fori_loop(0, blocks_per_span // DSA_GROUP, group, 0)

            l1, l2, l3 = l_scr[0], l_scr[1], l_scr[2]
            lm = jnp.maximum(jnp.maximum(l1, l2), l3)
            e1, e2, e3 = jnp.exp(l1 - lm), jnp.exp(l2 - lm), jnp.exp(l3 - lm)
            y = (e1 * o_scr[0] + e2 * o_scr[1] + e3 * o_scr[2]) / (e1 + e2 + e3)
            y_ref[0, pl.ds(pl.multiple_of(sp * DSA_SPAN, DSA_SPAN), DSA_SPAN), :] = y.astype(y_ref.dtype)
            return carry

        lax.fori_loop(0, seq // DSA_SPAN, span, 0)

    bound = bound_ref[0]

    @pl.when(bound <= DSA_MAX_SHIFT)
    def _():
        run(bound)

    @pl.when(bound > DSA_MAX_SHIFT)
    def _():
        run(None)


def _dsa(dsa1, dsa4, dsa16, bound):
    bsz, seq, _ = dsa1.shape
    nat = lambda off: pl.BlockSpec((1, seq, LANES), lambda b, hp: (b, 0, off + hp))
    res = lambda dil, off: pl.BlockSpec((1, dil, seq // dil, LANES), lambda b, hp: (b, 0, 0, off + hp))
    return pl.pallas_call(
        _dsa_kernel,
        grid=(bsz, 2),
        in_specs=[pl.BlockSpec(memory_space=pltpu.SMEM),
                  nat(0), nat(2), nat(4), res(4, 0), res(4, 2), res(4, 4), res(16, 0), res(16, 2), res(16, 4)],
        out_specs=pl.BlockSpec((1, seq, LANES), lambda b, hp: (b, 0, hp)),
        out_shape=jax.ShapeDtypeStruct((bsz, seq, BRANCH_WIDTH), BF16),
        scratch_shapes=[pltpu.VMEM((3, DSA_SPAN, LANES), F32), pltpu.VMEM((3, DSA_SPAN, LANES), F32)],
        compiler_params=_cparams(("parallel", "parallel")),
        name="dsa",
    )(bound, dsa1, dsa1, dsa1, dsa4, dsa4, dsa4, dsa16, dsa16, dsa16)


def _proj_weights(w_in):
    lead = w_in[..., :1792].astype(BF16)
    gg = jnp.pad(w_in[..., 1792:1808].astype(BF16), ((0, 0), (0, 0), (0, LANES - GLA_GATE_RANK)))
    dsa = w_in[..., 1808:2576].astype(BF16)
    wp = jnp.concatenate([lead, dsa, gg], axis=-1)
    return wp, w_in[..., 2576:].astype(BF16)


def _rope_tables(seq):
    inv = ROPE_THETA ** (-jnp.arange(0, HEAD_DIM, 2, dtype=F32) / HEAD_DIM)
    ang = jnp.arange(seq, dtype=F32)[:, None] * inv[None, :]
    cos, sin = jnp.cos(ang), jnp.sin(ang)
    cos_t = jnp.tile(jnp.concatenate([cos, cos], axis=1), (1, 2))
    sin_t = jnp.tile(jnp.concatenate([-sin, sin], axis=1), (1, 2))
    return cos_t, sin_t


def _pick_tile(n, pref):
    t = min(pref, n)
    while n % t:
        t //= 2
    return t


def kernel(x, ffn1_norm, ffn1_w_in, ffn1_w_out, mix_norm, w_in, s5_lam_re, s5_lam_im, s5_log_step, s5_b_re, s5_b_im, s5_c_re, s5_c_im, s5_d, s5_w_glu, gla_w_gate, gla_b_gate, gla_norm, dsa_q_norm, dsa_k_norm, w_branch, w_out, ffn2_norm, ffn2_w_in, ffn2_w_out):
    bsz, seq, _ = x.shape
    tokens = bsz * seq
    depth = ffn1_norm.shape[0]
    assert seq % DSA_SPAN == 0 and seq // DSA_PATTERNS[-1][1] >= 2 * DSA_BLOCK and seq >= SB_WIN
    tm_a = _pick_tile(seq, 512)
    tm_b = _pick_tile(tokens, 512)
    s5_rows = seq // S5_CHUNK
    s5_tile = min(S5_ROW_TILE, s5_rows)
    cos_t, sin_t = _rope_tables(seq)

    wp_all, wg_all = _proj_weights(w_in)
    w1a_all, w2a_all = ffn1_w_in.astype(BF16), ffn1_w_out.astype(BF16)
    w1b_all, w2b_all = ffn2_w_in.astype(BF16), ffn2_w_out.astype(BF16)
    wglu_all, wb_all, wo_all = s5_w_glu.astype(BF16), w_branch.astype(BF16), w_out.astype(BF16)
    s5_ops = jax.vmap(functools.partial(_s5_operators, row_tile=s5_tile))(
        s5_lam_re, s5_lam_im, s5_log_step, s5_b_re, s5_b_im, s5_c_re, s5_c_im, s5_d)

    h = x.reshape(tokens, D_MODEL)
    for l in range(depth):
        row = lambda p: p[l].reshape(1, -1)
        qn = jnp.tile(row(dsa_q_norm), (1, 4))
        kn = jnp.tile(row(dsa_k_norm), (1, 4))
        wgate = jnp.pad(gla_w_gate[l], ((0, LANES - GLA_GATE_RANK), (0, 0))).astype(BF16)
        h, sb, s5u, y_gla, y_sb_window, sb_left, dsa1, dsa4, dsa16 = _ffn_proj(
            h, row(ffn1_norm), w1a_all, w2a_all, row(mix_norm), wp_all, qn, kn, cos_t, sin_t,
            wgate, row(gla_b_gate), row(gla_norm), bsz, seq, tm_a, l)

        y_sb = lax.cond(jnp.max(sb_left) > SB_LOG_ZERO,
                        lambda: _stick_breaking(sb.reshape(bsz, seq, 768)).reshape(tokens, BRANCH_WIDTH),
                        lambda: y_sb_window)

        y_s5 = _s5(s5u.reshape(bsz, s5_rows, S5_CHUNK * 256), *s5_ops, l).reshape(tokens // S5_CHUNK, S5_CHUNK * 256)

        bound = (1.02 * HEAD_DIM ** 0.5 * jnp.max(jnp.abs(dsa_q_norm[l])) * jnp.max(jnp.abs(dsa_k_norm[l]))).reshape(1)
        y_dsa = _dsa(dsa1.reshape(bsz, seq, 768), dsa4, dsa16, bound).reshape(tokens, BRANCH_WIDTH)

        h = _merge_ffn(h, y_sb, y_s5, y_gla, y_dsa, row(mix_norm), wg_all, wglu_all, wb_all, wo_all,
                       row(ffn2_norm), w1b_all, w2b_all, tm_b, l)
    return h.reshape(bsz, seq, D_MODEL)
```

```python
import functools
import math

import jax
import jax.numpy as jnp
from jax import lax
from jax.experimental import pallas as pl
from jax.experimental.pallas import tpu as pltpu

D_MODEL = 1024
HEAD_DIM = 64
BRANCH_WIDTH = 256
N_BRANCH = 4
S5_GROUPS = 16
S5_GROUP_CH = 16
S5_STATE = 64
GLA_HEADS = 4
GLA_DK = 32
GLA_DV = 64
GLA_GATE_RANK = 16
GLA_GATE_NORM = 16.0
DSA_PATTERNS = ((128, 1), (512, 4), (2048, 16))
DSA_BLOCK = 128
ROPE_THETA = 10000.0
D_FF = 2816
RMS_EPS = 1e-6

LANES = 128
VMEM_LIMIT_BYTES = 58 * 1024 * 1024
FF_CHUNK = 256
N_FF_CHUNKS = D_FF // FF_CHUNK
S5_CHUNK = 8
S5_ROW_TILE = 256
GLA_BLOCK = 64
GLA_LEVELS = 6
GLA_SUB = 128
SB_Q = 64
SB_WIN = 4 * SB_Q
SB_GROUP = 8
SB_TAIL = SB_WIN - SB_Q
SB_LOG_ZERO = -104.0
NEG_BIG = -1e30
DSA_GROUP = 16
DSA_SPAN = DSA_PATTERNS[-1][1] * DSA_BLOCK
DSA_MAX_SHIFT = 30.0

PROJ_SB = (0, 768)
PROJ_S5 = (768, 1024)
PROJ_GLA = (1024, 1792)
PROJ_DSA = (1792, 2560)
PROJ_GG = (2560, 2688)

F32 = jnp.float32
BF16 = jnp.bfloat16


def _cparams(sem):
    return pltpu.CompilerParams(dimension_semantics=sem, vmem_limit_bytes=VMEM_LIMIT_BYTES)


def _const_spec(shape, layer=None):
    if layer is None:
        zeros = (0,) * len(shape)
        return pl.BlockSpec(shape, lambda *_: zeros, pipeline_mode=pl.Buffered(1))
    index = (layer,) + (0,) * (len(shape) - 1)
    return pl.BlockSpec((None,) + tuple(shape[1:]), lambda *_: index, pipeline_mode=pl.Buffered(1))


def _rms_rows(x, g):
    ms = jnp.mean(x * x, axis=-1, keepdims=True)
    return x * lax.rsqrt(ms + RMS_EPS) * g


def _dot(a, b):
    return jnp.dot(a, b, preferred_element_type=F32)


def _dot_nt(a, b):
    return lax.dot_general(a, b, (((1,), (1,)), ((), ())), preferred_element_type=F32)


def _dot_tn(a, b):
    return lax.dot_general(a, b, (((0,), (0,)), ((), ())), preferred_element_type=F32)


def _round_robin(*stage_generators):
    live = list(stage_generators)
    while live:
        live = [g for g in live if next(g, StopIteration) is not StopIteration]
        yield


def _swiglu_into(acc_ref, xb, w1_ref, w2_ref, between=None):
    for c in range(N_FF_CHUNKS):
        lo, hi = c * FF_CHUNK, (c + 1) * FF_CHUNK
        a = _dot(xb, w1_ref[:, lo:hi])
        b = _dot(xb, w1_ref[:, D_FF + lo:D_FF + hi])
        hm = (a * jax.nn.sigmoid(a) * b).astype(BF16)
        part = _dot(hm, w2_ref[lo:hi, :])
        if c == 0:
            acc_ref[...] = part
        else:
            acc_ref[...] += part
        if between is not None:
            next(between, None)


def _group_mean_matrix(width, group):
    r = lax.broadcasted_iota(jnp.int32, (width, width), 0) // group
    c = lax.broadcasted_iota(jnp.int32, (width, width), 1) // group
    return jnp.where(r == c, 1.0 / group, 0.0).astype(BF16)


def _group_rms(x, gain):
    gm = _group_mean_matrix(x.shape[-1], HEAD_DIM)
    ms = _dot((x * x).astype(BF16), gm)
    return x * lax.rsqrt(ms + RMS_EPS) * gain


def _swap_half_heads(x):
    half = HEAD_DIM // 2
    outs = []
    for s in range(x.shape[-1] // LANES):
        xs = x[:, s * LANES:(s + 1) * LANES]
        lane = lax.broadcasted_iota(jnp.int32, xs.shape, 1)
        up = pltpu.roll(xs, LANES - half, axis=1)
        down = pltpu.roll(xs, half, axis=1)
        outs.append(jnp.where((lane % HEAD_DIM) < half, up, down))
    return jnp.concatenate(outs, axis=-1)


def _ffn_proj_tile(x_ref, n1_ref, w1_ref, w2_ref, nm_ref, wp_ref, qn_ref, kn_ref, cos_ref, sin_ref,
                   h_ref, sb_ref, s5_ref, dsa1_ref, dsa4_ref, dsa16_ref, acc_ref, slab_ref, glax_ref, glag_ref,
                   sbh_ref, between, after_ffn):
    tm = x_ref.shape[0]
    x = x_ref[...]
    xb = _rms_rows(x, n1_ref[...]).astype(BF16)
    _swiglu_into(acc_ref, xb, w1_ref, w2_ref, between=between)
    after_ffn()
    h = x + 0.5 * acc_ref[...]
    h_ref[...] = h
    ub = _rms_rows(h, nm_ref[...]).astype(BF16)

    dsa = _dot(ub, wp_ref[:, PROJ_DSA[0]:PROJ_DSA[1]])
    cos = jnp.concatenate([cos_ref[...]] * 2, axis=-1)
    sin = jnp.concatenate([sin_ref[...]] * 2, axis=-1)
    parts = []
    for i, gain_ref in enumerate((qn_ref, kn_ref)):
        t = _group_rms(dsa[:, i * 256:(i + 1) * 256], gain_ref[...])
        parts.append(t * cos + _swap_half_heads(t) * sin)
    parts.append(dsa[:, 512:768])
    dsa = jnp.concatenate(parts, axis=-1)
    dsa1_ref[...] = dsa.astype(BF16)
    for s in range(6):
        slab_ref[s] = dsa[:, s * LANES:(s + 1) * LANES]
    for dil, ref in ((4, dsa4_ref), (16, dsa16_ref)):
        for r in range(dil):
            for s in range(6):
                rows = slab_ref[s, pl.ds(r, tm // dil, stride=dil), :]
                ref[0, r, :, s * LANES:(s + 1) * LANES] = rows.astype(BF16)

    s5 = _dot(ub, wp_ref[:, PROJ_S5[0]:PROJ_S5[1]])
    for s in range(2):
        slab_ref[6 + s] = s5[:, s * LANES:(s + 1) * LANES]
    for j in range(S5_CHUNK):
        for s in range(2):
            col = j * 256 + s * LANES
            s5_ref[:, col:col + LANES] = slab_ref[6 + s, pl.ds(j, tm // S5_CHUNK, stride=S5_CHUNK), :]

    sb = _dot(ub, wp_ref[:, PROJ_SB[0]:PROJ_SB[1]]).astype(BF16)
    sb_ref[...] = sb
    tail = sbh_ref[tm:tm + SB_TAIL, :]
    sbh_ref[0:SB_TAIL, :] = tail
    sbh_ref[SB_TAIL:, :] = sb
    glax_ref[...] = _dot(ub, wp_ref[:, PROJ_GLA[0]:PROJ_GLA[1]]).astype(BF16)
    glag_ref[...] = _dot(ub, wp_ref[:, PROJ_GG[0]:PROJ_GG[1]])


def _ffn_proj_kernel(x_ref, n1_ref, w1_ref, w2_ref, nm_ref, wp_ref, qn_ref, kn_ref, cos_ref, sin_ref,
                     wgate_ref, bgate_ref, gnorm_ref,
                     h_ref, sb_ref, s5_ref, ygla_ref, ysb_ref, sbf_ref, dsa1_ref, dsa4_ref, dsa16_ref,
                     acc_ref, slab_ref, glax_ref, glag_ref, glas_ref, sbh_ref, *, seq_tiles):
    step = pl.program_id(0)
    n_tiles = pl.num_programs(0) - 1

    @pl.when(step == 0)
    def _():
        glax_ref[...] = jnp.zeros_like(glax_ref)
        glag_ref[...] = jnp.zeros_like(glag_ref)
        glas_ref[...] = jnp.zeros_like(glas_ref)
        sbh_ref[...] = jnp.zeros_like(sbh_ref)

    def start_mixers():
        starts_sequence = lax.rem(step - 1, seq_tiles) == 0
        state = jnp.where(starts_sequence, 0.0, glas_ref[...])
        stacked, masks = _gla_constants()
        gla, sbw = {}, {}
        stages = _round_robin(
            _gla_stages(glax_ref[...], glag_ref[...], state, wgate_ref[...], bgate_ref[...], gnorm_ref[...],
                        stacked, masks, gla),
            _sb_window_stages(sbh_ref, jnp.logical_not(starts_sequence), sbw))
        return stages, gla, sbw

    def finish_mixers(stages, gla, sbw):
        for _ in stages:
            pass
        ygla_ref[...] = gla["y"].astype(ygla_ref.dtype)
        glas_ref[...] = gla["state"]
        ysb_ref[...] = sbw["y"].astype(ysb_ref.dtype)
        sbf_ref[...] = jnp.full(sbf_ref.shape, sbw["cmax"], F32)

    @pl.when(step < n_tiles)
    def _():
        stages, gla, sbw = start_mixers()
        _ffn_proj_tile(x_ref, n1_ref, w1_ref, w2_ref, nm_ref, wp_ref, qn_ref, kn_ref, cos_ref, sin_ref,
                       h_ref, sb_ref, s5_ref, dsa1_ref, dsa4_ref, dsa16_ref, acc_ref, slab_ref, glax_ref, glag_ref,
                       sbh_ref, between=stages, after_ffn=lambda: finish_mixers(stages, gla, sbw))

    @pl.when(step == n_tiles)
    def _():
        finish_mixers(*start_mixers())


def _ffn_proj(x2, n1, w1, w2, nm, wp, qn, kn, cos_t, sin_t, wgate, bgate, gnorm, bsz, seq, tm, layer):
    tokens = x2.shape[0]
    seq_tiles = seq // tm
    n_tiles = tokens // tm
    cur = lambda i: jnp.minimum(i, n_tiles - 1)
    row = lambda i: (cur(i), 0)
    lag = lambda i: (jnp.maximum(i - 1, 0), 0)
    tab = lambda i: (cur(i) % seq_tiles, 0)
    res = lambda i: (cur(i) // seq_tiles, 0, cur(i) % seq_tiles, 0)
    tile = lambda w: pl.BlockSpec((tm, w), row)
    s5_w = S5_CHUNK * 256
    return pl.pallas_call(
        functools.partial(_ffn_proj_kernel, seq_tiles=seq_tiles),
        grid=(n_tiles + 1,),
        in_specs=[tile(D_MODEL), _const_spec(n1.shape), _const_spec(w1.shape, layer), _const_spec(w2.shape, layer),
                  _const_spec(nm.shape), _const_spec(wp.shape, layer), _const_spec(qn.shape), _const_spec(kn.shape),
                  pl.BlockSpec((tm, LANES), tab), pl.BlockSpec((tm, LANES), tab),
                  _const_spec(wgate.shape), _const_spec(bgate.shape), _const_spec(gnorm.shape)],
        out_specs=[tile(D_MODEL), tile(768), pl.BlockSpec((tm // S5_CHUNK, s5_w), row),
                   pl.BlockSpec((tm, BRANCH_WIDTH), lag), pl.BlockSpec((tm, BRANCH_WIDTH), lag),
                   pl.BlockSpec((8, LANES), lag),
                   tile(768), pl.BlockSpec((1, 4, tm // 4, 768), res), pl.BlockSpec((1, 16, tm // 16, 768), res)],
        out_shape=[jax.ShapeDtypeStruct((tokens, D_MODEL), F32),
                   jax.ShapeDtypeStruct((tokens, 768), BF16),
                   jax.ShapeDtypeStruct((tokens // S5_CHUNK, s5_w), F32),
                   jax.ShapeDtypeStruct((tokens, BRANCH_WIDTH), BF16),
                   jax.ShapeDtypeStruct((tokens, BRANCH_WIDTH), BF16),
                   jax.ShapeDtypeStruct((n_tiles * 8, LANES), F32),
                   jax.ShapeDtypeStruct((tokens, 768), BF16),
                   jax.ShapeDtypeStruct((bsz, 4, seq // 4, 768), BF16),
                   jax.ShapeDtypeStruct((bsz, 16, seq // 16, 768), BF16)],
        scratch_shapes=[pltpu.VMEM((tm, D_MODEL), F32), pltpu.VMEM((8, tm, LANES), F32),
                        pltpu.VMEM((tm, 768), BF16), pltpu.VMEM((tm, LANES), F32),
                        pltpu.VMEM((GLA_HEADS * GLA_DV, GLA_HEADS * GLA_DK), F32),
                        pltpu.VMEM((SB_TAIL + tm, 768), BF16)],
        compiler_params=_cparams(("arbitrary",)),
        name="ffn_proj",
    )(x2, n1, w1, w2, nm, wp, qn, kn, cos_t, sin_t, wgate, bgate, gnorm)


def _merge_ffn_kernel(h_ref, ysb_ref, ys5_ref, ygla_ref, ydsa_ref,
                      nm_ref, wg_ref, wglu_ref, wb_ref, wo_ref, n2_ref, w1_ref, w2_ref,
                      out_ref, acc_ref, slab_ref):
    tm = h_ref.shape[0]
    proj = {n: _dot(ref[...], wb_ref[n]) for n, ref in ((0, ysb_ref), (2, ygla_ref), (3, ydsa_ref))}

    for j in range(S5_CHUNK):
        for s in range(2):
            col = j * 256 + s * LANES
            slab_ref[s, pl.ds(j, tm // S5_CHUNK, stride=S5_CHUNK), :] = ys5_ref[:, col:col + LANES].astype(F32)
    y_s5 = jnp.concatenate([slab_ref[0], slab_ref[1]], axis=-1).astype(BF16)
    glu = _dot(y_s5, wglu_ref[...])
    y_s5 = glu[:, :BRANCH_WIDTH] * jax.nn.sigmoid(glu[:, BRANCH_WIDTH:])
    proj[1] = _dot(y_s5.astype(BF16), wb_ref[1])

    h = h_ref[...]
    ub = _rms_rows(h, nm_ref[...]).astype(BF16)
    mixed = jnp.zeros(h.shape, F32)
    for n in range(N_BRANCH):
        gate = jax.nn.sigmoid(_dot(ub, wg_ref[:, n * D_MODEL:(n + 1) * D_MODEL]))
        mixed = mixed + gate * proj[n]
    h2 = h + _dot(mixed.astype(BF16), wo_ref[...])

    xb = _rms_rows(h2, n2_ref[...]).astype(BF16)
    _swiglu_into(acc_ref, xb, w1_ref, w2_ref)
    out_ref[...] = h2 + 0.5 * acc_ref[...]


def _merge_ffn(h, ysb, ys5, ygla, ydsa, nm, wg, wglu, wb, wo, n2, w1, w2, tm, layer):
    tokens = h.shape[0]
    row = lambda i: (i, 0)
    tile = lambda w: pl.BlockSpec((tm, w), row)
    consts = (nm, wg, wglu, wb, wo, n2, w1, w2)
    return pl.pallas_call(
        _merge_ffn_kernel,
        grid=(tokens // tm,),
        in_specs=[tile(D_MODEL), tile(BRANCH_WIDTH), pl.BlockSpec((tm // S5_CHUNK, S5_CHUNK * 256), row),
                  tile(BRANCH_WIDTH), tile(BRANCH_WIDTH)]
                 + [_const_spec(c.shape, None if c.ndim == 2 and c.shape[0] == 1 else layer) for c in consts],
        out_specs=tile(D_MODEL),
        out_shape=jax.ShapeDtypeStruct((tokens, D_MODEL), F32),
        scratch_shapes=[pltpu.VMEM((tm, D_MODEL), F32), pltpu.VMEM((2, tm, LANES), F32)],
        compiler_params=_cparams(("parallel",)),
        name="merge_ffn",
    )(h, ysb, ys5, ygla, ydsa, *consts)


def _later_key_matrix(n):
    j = lax.broadcasted_iota(jnp.int32, (n, n), 0)
    s = lax.broadcasted_iota(jnp.int32, (n, n), 1)
    return jnp.where(j > s, 1.0, 0.0).astype(BF16)


def _sb_scores(qm, k, keep=None):
    z = _dot_nt(qm, k)
    if keep is not None:
        z = jnp.where(keep, z, NEG_BIG)
    lk = -(jnp.maximum(z, 0.0) + jnp.log(1.0 + jnp.exp(-jnp.abs(z))))
    return z, lk


def _sb_window_stages(hist_ref, not_first, result):
    rows = hist_ref.shape[0] - SB_TAIL
    n_blk = rows // SB_Q
    u_win = _later_key_matrix(SB_WIN)
    lane_q = lax.broadcasted_iota(jnp.int32, (SB_Q, LANES), 1)
    col = lax.broadcasted_iota(jnp.int32, (2 * SB_Q, SB_WIN), 1)
    key_minus_query = col - lax.broadcasted_iota(jnp.int32, (2 * SB_Q, SB_WIN), 0) % SB_Q
    strict = key_minus_query < SB_TAIL
    first = lane_q < HEAD_DIM
    outs = [[None] * n_blk for _ in range(2)]
    cmax = jnp.float32(-jnp.inf)
    for pair in range(2):
        q_cols = slice(pair * LANES, (pair + 1) * LANES)
        k_cols = slice(256 + pair * LANES, 256 + (pair + 1) * LANES)
        v_cols = slice(512 + pair * LANES, 512 + (pair + 1) * LANES)
        zs, lks = [], []
        for qb in range(n_blk):
            qs = hist_ref[SB_TAIL + qb * SB_Q:SB_TAIL + (qb + 1) * SB_Q, q_cols] * (HEAD_DIM ** -0.5)
            qm2 = jnp.concatenate([jnp.where(first, qs, jnp.zeros_like(qs)),
                                   jnp.where(first, jnp.zeros_like(qs), qs)], axis=0)
            keep = strict
            if qb * SB_Q < SB_TAIL:
                first_col = jnp.where(not_first, 0, SB_TAIL - qb * SB_Q)
                keep = strict & (col >= first_col)
            z, lk = _sb_scores(qm2, hist_ref[qb * SB_Q:qb * SB_Q + SB_WIN, k_cols], keep)
            zs.append(z)
            lks.append(lk)
        yield
        later_all = _dot(jnp.concatenate(lks, axis=0).astype(BF16), u_win)
        yield
        for qb in range(n_blk):
            later = later_all[qb * 2 * SB_Q:(qb + 1) * 2 * SB_Q]
            w = jnp.exp(zs[qb] + lks[qb] + later)
            o2 = _dot(w.astype(BF16), hist_ref[qb * SB_Q:qb * SB_Q + SB_WIN, v_cols])
            outs[pair][qb] = jnp.where(first, o2[:SB_Q], o2[SB_Q:])
            left = jnp.max(later[:, 0:1] + lks[qb][:, 0:1])
            has_older = True if qb * SB_Q > SB_TAIL else not_first
            cmax = jnp.maximum(cmax, jnp.where(has_older, left, -jnp.inf))
        yield
    result["y"] = jnp.concatenate([jnp.concatenate(outs[pair], axis=0) for pair in range(2)], axis=1)
    result["cmax"] = cmax


def _sb_kernel(q_ref, k_ref, v_ref, o_ref):
    seq = q_ref.shape[1]
    u_win = _later_key_matrix(SB_WIN)
    u_blk = _later_key_matrix(SB_Q)
    lane_q = lax.broadcasted_iota(jnp.int32, (SB_Q, LANES), 1)
    n_streams = 2 * SB_GROUP
    key_minus_query = (lax.broadcasted_iota(jnp.int32, (2 * SB_Q, SB_WIN), 1)
                       - lax.broadcasted_iota(jnp.int32, (2 * SB_Q, SB_WIN), 0) % SB_Q)

    def q_group(qg, carry):
        starts, qms, zs, lks, vws = [], [], [], [], []
        for sub in range(SB_GROUP):
            t0 = pl.multiple_of((qg * SB_GROUP + sub) * SB_Q, SB_Q)
            start = pl.multiple_of(jnp.maximum(t0 - (SB_WIN - SB_Q), 0), SB_Q)
            q = q_ref[0, pl.ds(t0, SB_Q), :] * (HEAD_DIM ** -0.5)
            kw = k_ref[0, pl.ds(start, SB_WIN), :]
            vws.append(v_ref[0, pl.ds(start, SB_WIN), :])
            strict = key_minus_query < (t0 - start)
            starts.append(start)
            qm2 = jnp.concatenate(
                [jnp.where((lane_q // HEAD_DIM) == head, q, jnp.zeros_like(q)) for head in range(2)], axis=0)
            z, lk = _sb_scores(qm2, kw, strict)
            qms += [qm2[:SB_Q], qm2[SB_Q:]]
            zs.append(z)
            lks.append(lk)
        lk_all = jnp.concatenate(lks, axis=0)
        later_all = _dot(lk_all.astype(BF16), u_win)
        csums, outs = [], []
        for sub in range(SB_GROUP):
            rows = slice(sub * 2 * SB_Q, (sub + 1) * 2 * SB_Q)
            later = later_all[rows]
            w = jnp.exp(zs[sub] + lks[sub] + later)
            o2 = _dot(w.astype(BF16), vws[sub])
            c2 = later[:, 0:1] + lks[sub][:, 0:1]
            outs += [o2[:SB_Q], o2[SB_Q:]]
            csums += [c2[:SB_Q], c2[SB_Q:]]

        def cond(state):
            step, cmaxes, _, _ = state
            flags = [jnp.logical_and(starts[s // 2] - (step + 1) * SB_Q >= 0, cmaxes[s] > SB_LOG_ZERO)
                     for s in range(n_streams)]
            return functools.reduce(jnp.logical_or, flags)

        def body(state):
            step, _, csums, outs = state
            new_c, new_o = [], []
            for s in range(n_streams):
                nxt = starts[s // 2] - (step + 1) * SB_Q
                in_range = nxt >= 0
                p0 = pl.multiple_of(jnp.maximum(nxt, 0), SB_Q)
                kb = k_ref[0, pl.ds(p0, SB_Q), :]
                vb = v_ref[0, pl.ds(p0, SB_Q), :]
                z, lk = _sb_scores(qms[s], kb)
                later = _dot(lk.astype(BF16), u_blk)
                w = jnp.where(in_range, jnp.exp(z + lk + later + csums[s]), 0.0)
                new_o.append(outs[s] + _dot(w.astype(BF16), vb))
                new_c.append(csums[s] + jnp.where(in_range, later[:, 0:1] + lk[:, 0:1], 0.0))
            return step + 1, tuple(jnp.max(c) for c in new_c), tuple(new_c), tuple(new_o)

        init = (jnp.int32(0), tuple(jnp.max(c) for c in csums), tuple(csums), tuple(outs))
        _, _, _, outs = lax.while_loop(cond, body, init)
        for sub in range(SB_GROUP):
            t0 = pl.multiple_of((qg * SB_GROUP + sub) * SB_Q, SB_Q)
            o_ref[0, pl.ds(t0, SB_Q), :] = jnp.where(
                lane_q < HEAD_DIM, outs[2 * sub], outs[2 * sub + 1]).astype(o_ref.dtype)
        return carry

    lax.fori_loop(0, seq // (SB_Q * SB_GROUP), q_group, 0)


def _stick_breaking(sb3):
    bsz, seq, _ = sb3.shape
    spec = lambda off: pl.BlockSpec((1, seq, LANES), lambda b, hp: (b, 0, off + hp))
    return pl.pallas_call(
        _sb_kernel,
        grid=(bsz, 2),
        in_specs=[spec(0), spec(2), spec(4)],
        out_specs=pl.BlockSpec((1, seq, LANES), lambda b, hp: (b, 0, hp)),
        out_shape=jax.ShapeDtypeStruct((bsz, seq, BRANCH_WIDTH), BF16),
        compiler_params=_cparams(("parallel", "parallel")),
        name="stick_breaking",
    )(sb3, sb3, sb3)


def _s5_kernel(u_ref, m_ref, e_ref, g_ref, pw_ref, d_ref, y_ref, state_ref):
    half = S5_GROUPS * S5_STATE

    @pl.when(pl.program_id(1) == 0)
    def _():
        state_ref[...] = jnp.zeros_like(state_ref)

    u = u_ref[0]
    ub = u.astype(BF16)
    rows = u.shape[0]

    def cmul(ar, ai, xr, xi):
        return ar * xr - ai * xi, ar * xi + ai * xr

    w = _dot(ub, e_ref[...])
    wr, wi = w[:, :half], w[:, half:]

    n_c = S5_GROUPS * S5_GROUP_CH
    slots = [ub[:, j * n_c:(j + 1) * n_c] for j in range(S5_CHUNK)]
    cols = []
    for i in range(S5_CHUNK):
        acc = _dot(slots[0], m_ref[i])
        for j in range(1, i + 1):
            acc = acc + _dot(slots[j], m_ref[i - j])
        cols.append(acc)
    y = jnp.concatenate(cols, axis=1)

    row = lax.broadcasted_iota(jnp.int32, (rows, LANES), 0)
    xrs, xis = [], []
    for t in range(half // LANES):
        re, im = slice(t * LANES, (t + 1) * LANES), slice(half + t * LANES, half + (t + 1) * LANES)
        tr, ti = wr[:, re], wi[:, re]
        sr, si = state_ref[0:1, re], state_ref[0:1, im]
        cr, ci = cmul(pw_ref[0:1, re], pw_ref[0:1, im], sr, si)
        tr = jnp.where(row == 0, tr + cr, tr)
        ti = jnp.where(row == 0, ti + ci, ti)
        step = 0
        d = 1
        while d < rows:
            pr = jnp.where(row >= d, pltpu.roll(tr, d, axis=0), 0.0)
            pi = jnp.where(row >= d, pltpu.roll(ti, d, axis=0), 0.0)
            mr, mi = cmul(pw_ref[step:step + 1, re], pw_ref[step:step + 1, im], pr, pi)
            tr, ti = tr + mr, ti + mi
            d *= 2
            step += 1
        xrs.append(jnp.where(row == 0, sr, pltpu.roll(tr, 1, axis=0)))
        xis.append(jnp.where(row == 0, si, pltpu.roll(ti, 1, axis=0)))
        state_ref[0:1, re] = tr[rows - 1:rows, :]
        state_ref[0:1, im] = ti[rows - 1:rows, :]
    xr, xi = jnp.concatenate(xrs, axis=1), jnp.concatenate(xis, axis=1)

    y = y + _dot(xr.astype(BF16), g_ref[:half, :]) + _dot(xi.astype(BF16), g_ref[half:, :])
    y = y + d_ref[...] * u
    y_ref[0] = jax.nn.gelu(y).astype(y_ref.dtype)


def _s5(u3, m, e, g, pw, dskip, layer):
    bsz, rows, width = u3.shape
    tr = min(S5_ROW_TILE, rows)
    blk = pl.BlockSpec((1, tr, width), lambda b, i: (b, i, 0))
    return pl.pallas_call(
        _s5_kernel,
        grid=(bsz, rows // tr),
        in_specs=[blk] + [_const_spec(a.shape, layer) for a in (m, e, g, pw, dskip)],
        out_specs=blk,
        out_shape=jax.ShapeDtypeStruct(u3.shape, BF16),
        scratch_shapes=[pltpu.VMEM((8, width), F32)],
        compiler_params=_cparams(("parallel", "arbitrary")),
        name="s5",
    )(u3, m, e, g, pw, dskip)


def _s5_operators(lam_re, lam_im, log_step, b_re, b_im, c_re, c_im, d_skip, row_tile):
    n_g, n_p, n_h, L = S5_GROUPS, S5_STATE, S5_GROUP_CH, S5_CHUNK
    n_c, n_s = n_g * n_h, n_g * n_p
    dt = jnp.exp(log_step)[:, None]

    def lam_bar_pow(k):
        mag = jnp.exp(k[:, None, None] * (lam_re * dt)[None])
        ang = k[:, None, None] * (lam_im * dt)[None]
        return mag * jnp.cos(ang), mag * jnp.sin(ang)

    pr, pi = lam_bar_pow(jnp.arange(L + 1, dtype=F32))
    den = lam_re * lam_re + lam_im * lam_im
    fr = ((pr[1] - 1.0) * lam_re + pi[1] * lam_im) / den
    fi = (pi[1] * lam_re - (pr[1] - 1.0) * lam_im) / den
    bbr = fr[..., None] * b_re - fi[..., None] * b_im
    bbi = fr[..., None] * b_im + fi[..., None] * b_re
    tr = pr[:L, :, :, None] * bbr[None] - pi[:L, :, :, None] * bbi[None]
    ti = pr[:L, :, :, None] * bbi[None] + pi[:L, :, :, None] * bbr[None]

    def expand(small, rows_per_group, cols_per_group):
        rows = small.shape[-2]
        src = lax.broadcasted_iota(jnp.int32, (cols_per_group, n_g * cols_per_group), 0)
        dst = lax.broadcasted_iota(jnp.int32, (cols_per_group, n_g * cols_per_group), 1) % cols_per_group
        spread = jnp.where(src == dst, 1.0, 0.0).astype(BF16)
        tiled = jnp.einsum('...rc,cq->...rq', small.astype(BF16), spread, preferred_element_type=F32)
        r = lax.broadcasted_iota(jnp.int32, (rows, n_g * cols_per_group), 0) // rows_per_group
        c = lax.broadcasted_iota(jnp.int32, (rows, n_g * cols_per_group), 1) // cols_per_group
        return jnp.where(r == c, tiled, 0.0)

    hp = lax.Precision.HIGHEST
    kern = (jnp.einsum('ghp,kgpi->kghi', c_re, tr, precision=hp)
            - jnp.einsum('ghp,kgpi->kghi', c_im, ti, precision=hp))
    m_op = expand(kern.transpose(0, 1, 3, 2).reshape(L, n_c, n_h), n_h, n_h).astype(BF16)

    e_r = expand(tr[::-1].transpose(0, 1, 3, 2).reshape(L, n_c, n_p), n_h, n_p)
    e_i = expand(ti[::-1].transpose(0, 1, 3, 2).reshape(L, n_c, n_p), n_h, n_p)
    e_op = jnp.concatenate([e_r, e_i], axis=2).astype(BF16).reshape(L * n_c, 2 * n_s)

    clr = c_re[None] * pr[1:, :, None, :] - c_im[None] * pi[1:, :, None, :]
    cli = c_re[None] * pi[1:, :, None, :] + c_im[None] * pr[1:, :, None, :]
    g_r = expand(clr.transpose(0, 1, 3, 2).reshape(L, n_s, n_h), n_p, n_h)
    g_i = expand(cli.transpose(0, 1, 3, 2).reshape(L, n_s, n_h), n_p, n_h)
    g_blocks = jnp.concatenate([g_r, -g_i], axis=1).astype(BF16)
    g_op = jnp.concatenate([g_blocks[i] for i in range(L)], axis=1)

    n_steps = max(1, int(math.log2(row_tile)))
    sr, si = lam_bar_pow(L * 2.0 ** jnp.arange(n_steps, dtype=F32))
    pw = jnp.concatenate([sr.reshape(n_steps, n_s), si.reshape(n_steps, n_s)], axis=1)
    pw = jnp.pad(pw, ((0, (-n_steps) % 8), (0, 0)))
    dskip = jnp.tile(d_skip.reshape(1, n_c), (1, L))
    return m_op, e_op, g_op, pw.astype(F32), dskip.astype(F32)


def _gla_constants():
    n = GLA_SUB
    ri = lax.broadcasted_iota(jnp.int32, (n, n), 0)
    ci = lax.broadcasted_iota(jnp.int32, (n, n), 1)
    same = (ri // GLA_BLOCK) == (ci // GLA_BLOCK)
    mats = [same & (ci <= ri), same & (ci > ri)]
    masks = [ri == ci]
    ups, lows = [], []
    for lev in range(GLA_LEVELS):
        half = 1 << lev
        blk = 2 * half
        mid = (ri // blk) * blk + (half - 1)
        later = (ri % blk) >= half
        ups.append(later & (ci > mid) & (ci <= ri))
        lows.append(jnp.logical_not(later) & (ci > ri) & (ci <= mid))
        masks.append(((ri // blk) == (ci // blk)) & later & ((ci % blk) < half))
    stacked = jnp.concatenate([jnp.where(m, 1.0, 0.0).astype(BF16) for m in mats + ups + lows], axis=0)
    return stacked, masks


def _gla_stages(x, gg, state, wgate, bgate, norm, stacked, masks, result):
    n_rows = x.shape[0]
    n_sub = n_rows // GLA_SUB
    hk = GLA_HEADS * GLA_DK
    hv = GLA_HEADS * GLA_DV
    q = x[:, 0:hk].astype(F32) * (GLA_DK ** -0.5)
    k = x[:, hk:2 * hk].astype(F32)
    vb = x[:, 2 * hk:2 * hk + hv]
    r = x[:, 2 * hk + hv:].astype(F32)
    sub = lambda t, st: t[st * GLA_SUB:(st + 1) * GLA_SUB]

    z = _dot(gg.astype(BF16), wgate) + bgate
    g = -(jnp.maximum(-z, 0.0) + jnp.log(1.0 + jnp.exp(-jnp.abs(z)))) * (1.0 / GLA_GATE_NORM)
    gb = g.astype(BF16)
    yield

    def decay_sums(st, blocks):
        picked = jnp.concatenate([stacked[i * GLA_SUB:(i + 1) * GLA_SUB] for i in blocks], axis=0)
        both = _dot(picked, sub(gb, st))
        return [both[n * GLA_SUB:(n + 1) * GLA_SUB] for n in range(len(blocks))]

    def level_blocks(lev):
        return 2 + lev, 2 + GLA_LEVELS + lev

    lane_k = lax.broadcasted_iota(jnp.int32, (GLA_SUB, hk), 1) // GLA_DK
    att = [[jnp.zeros((GLA_SUB, GLA_SUB), F32) for _ in range(GLA_HEADS)] for _ in range(n_sub)]
    ahead = [decay_sums(st, level_blocks(0)) for st in range(n_sub)]
    for lev in range(-1, GLA_LEVELS):
        now, ahead = ahead, None
        for st in range(n_sub):
            qs, ks = sub(q, st), sub(k, st)
            if lev < 0:
                qt, kt = qs.astype(BF16), ks.astype(BF16)
            else:
                qt = (qs * jnp.exp(now[st][0])).astype(BF16)
                kt = (ks * jnp.exp(now[st][1])).astype(BF16)
            q_heads = jnp.concatenate(
                [jnp.where(lane_k == h, qt, jnp.zeros_like(qt)) for h in range(GLA_HEADS)], axis=0)
            a = _dot_nt(q_heads, kt)
            for h in range(GLA_HEADS):
                att[st][h] = jnp.where(masks[lev + 1], a[h * GLA_SUB:(h + 1) * GLA_SUB], att[st][h])
        if lev < 0:
            ahead = now
        elif lev + 1 < GLA_LEVELS:
            ahead = [decay_sums(st, level_blocks(lev + 1)) for st in range(n_sub)]
        else:
            ahead = [decay_sums(st, (0, 1)) for st in range(n_sub)]
        yield
    part = lambda st, i: ahead[st][i]

    lane_v = lax.broadcasted_iota(jnp.int32, (GLA_SUB, hv), 1) // GLA_DV
    inner = []
    for st in range(n_sub):
        vs = sub(vb, st)
        att_all = jnp.concatenate([a.astype(BF16) for a in att[st]], axis=1)
        v_heads = jnp.concatenate([jnp.where(lane_v == h, vs, jnp.zeros_like(vs)) for h in range(GLA_HEADS)], axis=0)
        inner.append(_dot(att_all, v_heads))
    yield

    row_blk = lax.broadcasted_iota(jnp.int32, (GLA_SUB, hk), 0) // GLA_BLOCK
    rs = lax.broadcasted_iota(jnp.int32, (hv, hk), 0) // GLA_DV
    cs = lax.broadcasted_iota(jnp.int32, (hv, hk), 1) // GLA_DK
    head_diag = rs == cs
    n_blk = GLA_SUB // GLA_BLOCK
    outs = []
    for st in range(n_sub):
        b, tail = part(st, 0), part(st, 1)
        q_dec = (sub(q, st) * jnp.exp(b)).astype(BF16)
        k_dec = (sub(k, st) * jnp.exp(tail)).astype(BF16)
        b_end = b + tail
        k_blocks = jnp.concatenate(
            [jnp.where(row_blk == n, k_dec, jnp.zeros_like(k_dec)) for n in range(n_blk)], axis=1)
        upd_all = _dot_tn(sub(vb, st), k_blocks)
        cross = []
        for blk in range(n_blk):
            r0 = blk * GLA_BLOCK
            cross.append(_dot_nt(q_dec[r0:r0 + GLA_BLOCK], state.astype(BF16)))
            upd = upd_all[:, blk * hk:(blk + 1) * hk]
            state = jnp.exp(b_end[r0:r0 + 1, :]) * state + jnp.where(head_diag, upd, 0.0)
        outs.append(inner[st] + jnp.concatenate(cross, axis=0))
    yield

    o = _group_rms(jnp.concatenate(outs, axis=0), norm)
    result["y"] = o * (r * jax.nn.sigmoid(r))
    result["state"] = state


def _dsa_band_bias(offset, n_steps):
    row = lax.broadcasted_iota(jnp.int32, (2 * DSA_BLOCK, 2 * DSA_BLOCK), 0) % DSA_BLOCK
    col = lax.broadcasted_iota(jnp.int32, (2 * DSA_BLOCK, 2 * DSA_BLOCK), 1)
    dist = offset + row - col
    return jnp.where((dist >= 0) & (dist <= n_steps), 0.0, NEG_BIG)


def _dsa_logits(q, kw, bias):
    lane_q = lax.broadcasted_iota(jnp.int32, (DSA_BLOCK, LANES), 1)
    q = q * (HEAD_DIM ** -0.5)
    qm2 = jnp.concatenate(
        [jnp.where((lane_q // HEAD_DIM) == head, q, jnp.zeros_like(q)) for head in range(2)], axis=0)
    return _dot_nt(qm2, kw) + bias


def _dsa_softmax_pv(s, vw, shift):
    if shift is None:
        m = jnp.max(s, axis=1, keepdims=True)
        p = jnp.exp(s - m)
        l = jnp.sum(p, axis=1, keepdims=True)
        o2 = _dot(p.astype(BF16), vw) / l
        lse2 = jnp.broadcast_to(m + jnp.log(l), (2 * DSA_BLOCK, LANES))
    else:
        p = jnp.exp(s)
        pv = _dot(p.astype(BF16), jnp.concatenate([vw, jnp.ones_like(vw)], axis=1))
        l = pv[:, LANES:]
        o2 = pv[:, :LANES] / l
        lse2 = shift + jnp.log(l)
    first = lax.broadcasted_iota(jnp.int32, (DSA_BLOCK, LANES), 1) < HEAD_DIM
    return (jnp.where(first, o2[:DSA_BLOCK], o2[DSA_BLOCK:]),
            jnp.where(first, lse2[:DSA_BLOCK], lse2[DSA_BLOCK:]))


def _dsa_kernel(bound_ref, q1_ref, k1_ref, v1_ref, q4_ref, k4_ref, v4_ref, q16_ref, k16_ref, v16_ref,
                y_ref, o_scr, l_scr):
    seq = q1_ref.shape[1]
    refs = ((q1_ref, k1_ref, v1_ref), (q4_ref, k4_ref, v4_ref), (q16_ref, k16_ref, v16_ref))
    blocks_per_span = DSA_SPAN // DSA_BLOCK
    n_steps = DSA_PATTERNS[0][0] // DSA_PATTERNS[0][1]
    assert all(w // d == n_steps for w, d in DSA_PATTERNS)
    bias_inner = _dsa_band_bias(DSA_BLOCK, n_steps)
    bias_first = _dsa_band_bias(0, n_steps)

    def run(shift):
        inner, first = (bias_inner, bias_first) if shift is None else (bias_inner - shift, bias_first - shift)

        def span(sp, carry):
            for p, (_, dil) in enumerate(DSA_PATTERNS):
                q_ref, k_ref, v_ref = refs[p]
                per_res = blocks_per_span // dil

                def group(gi, c, p=p, dil=dil, per_res=per_res, q_ref=q_ref, k_ref=k_ref, v_ref=v_ref):
                    units = []
                    for u in range(DSA_GROUP):
                        idx = gi * DSA_GROUP + u
                        r = idx // per_res
                        jb = idx % per_res
                        m0 = pl.multiple_of(sp * (DSA_SPAN // dil) + jb * DSA_BLOCK, DSA_BLOCK)
                        start = pl.multiple_of(jnp.maximum(m0 - DSA_BLOCK, 0), DSA_BLOCK)
                        if dil == 1:
                            q = q_ref[0, pl.ds(m0, DSA_BLOCK), :]
                            kw = k_ref[0, pl.ds(start, 2 * DSA_BLOCK), :]
                            vw = v_ref[0, pl.ds(start, 2 * DSA_BLOCK), :]
                        else:
                            q = q_ref[0, r, pl.ds(m0, DSA_BLOCK), :]
                            kw = k_ref[0, r, pl.ds(start, 2 * DSA_BLOCK), :]
                            vw = v_ref[0, r, pl.ds(start, 2 * DSA_BLOCK), :]
                        may_start_sequence = u % per_res == 0
                        s = _dsa_logits(q, kw, jnp.where(m0 == 0, first, inner) if may_start_sequence else inner)
                        rows = pl.ds(r + dil * jb * DSA_BLOCK, DSA_BLOCK, stride=dil) if dil > 1 else \
                            pl.ds(pl.multiple_of(jb * DSA_BLOCK, DSA_BLOCK), DSA_BLOCK)
                        units.append((s, vw, rows))
                    results = [_dsa_softmax_pv(s, vw, shift) for s, vw, _ in units]
                    for (o, lse), (_, _, rows) in zip(results, units):
                        o_scr[p, rows, :] = o
                        l_scr[p, rows, :] = lse
                    return c

                lax.fori_loop(0, blocks_per_span // DSA_GROUP, group, 0)

            l1, l2, l3 = l_scr[0], l_scr[1], l_scr[2]
            lm = jnp.maximum(jnp.maximum(l1, l2), l3)
            e1, e2, e3 = jnp.exp(l1 - lm), jnp.exp(l2 - lm), jnp.exp(l3 - lm)
            y = (e1 * o_scr[0] + e2 * o_scr[1] + e3 * o_scr[2]) / (e1 + e2 + e3)
            y_ref[0, pl.ds(pl.multiple_of(sp * DSA_SPAN, DSA_SPAN), DSA_SPAN), :] = y.astype(y_ref.dtype)
            return carry

        lax.fori_loop(0, seq // DSA_SPAN, span, 0)

    bound = bound_ref[0]

    @pl.when(bound <= DSA_MAX_SHIFT)
    def _():
        run(bound)

    @pl.when(bound > DSA_MAX_SHIFT)
    def _():
        run(None)


def _dsa(dsa1, dsa4, dsa16, bound):
    bsz, seq, _ = dsa1.shape
    nat = lambda off: pl.BlockSpec((1, seq, LANES), lambda b, hp: (b, 0, off + hp))
    res = lambda dil, off: pl.BlockSpec((1, dil, seq // dil, LANES), lambda b, hp: (b, 0, 0, off + hp))
    return pl.pallas_call(
        _dsa_kernel,
        grid=(bsz, 2),
        in_specs=[pl.BlockSpec(memory_space=pltpu.SMEM),
                  nat(0), nat(2), nat(4), res(4, 0), res(4, 2), res(4, 4), res(16, 0), res(16, 2), res(16, 4)],
        out_specs=pl.BlockSpec((1, seq, LANES), lambda b, hp: (b, 0, hp)),
        out_shape=jax.ShapeDtypeStruct((bsz, seq, BRANCH_WIDTH), BF16),
        scratch_shapes=[pltpu.VMEM((3, DSA_SPAN, LANES), F32), pltpu.VMEM((3, DSA_SPAN, LANES), F32)],
        compiler_params=_cparams(("parallel", "parallel")),
        name="dsa",
    )(bound, dsa1, dsa1, dsa1, dsa4, dsa4, dsa4, dsa16, dsa16, dsa16)


def _proj_weights(w_in):
    lead = w_in[..., :1792].astype(BF16)
    gg = jnp.pad(w_in[..., 1792:1808].astype(BF16), ((0, 0), (0, 0), (0, LANES - GLA_GATE_RANK)))
    dsa = w_in[..., 1808:2576].astype(BF16)
    wp = jnp.concatenate([lead, dsa, gg], axis=-1)
    return wp, w_in[..., 2576:].astype(BF16)


def _rope_tables(seq):
    inv = ROPE_THETA ** (-jnp.arange(0, HEAD_DIM, 2, dtype=F32) / HEAD_DIM)
    ang = jnp.arange(seq, dtype=F32)[:, None] * inv[None, :]
    cos, sin = jnp.cos(ang), jnp.sin(ang)
    cos_t = jnp.tile(jnp.concatenate([cos, cos], axis=1), (1, 2))
    sin_t = jnp.tile(jnp.concatenate([-sin, sin], axis=1), (1, 2))
    return cos_t, sin_t


def _pick_tile(n, pref):
    t = min(pref, n)
    while n % t:
        t //= 2
    return t


def kernel(x, ffn1_norm, ffn1_w_in, ffn1_w_out, mix_norm, w_in, s5_lam_re, s5_lam_im, s5_log_step, s5_b_re, s5_b_im, s5_c_re, s5_c_im, s5_d, s5_w_glu, gla_w_gate, gla_b_gate, gla_norm, dsa_q_norm, dsa_k_norm, w_branch, w_out, ffn2_norm, ffn2_w_in, ffn2_w_out):
    bsz, seq, _ = x.shape
    tokens = bsz * seq
    depth = ffn1_norm.shape[0]
    assert seq % DSA_SPAN == 0 and seq // DSA_PATTERNS[-1][1] >= 2 * DSA_BLOCK and seq >= SB_WIN
    tm_a = _pick_tile(seq, 512)
    tm_b = _pick_tile(tokens, 512)
    s5_rows = seq // S5_CHUNK
    s5_tile = min(S5_ROW_TILE, s5_rows)
    cos_t, sin_t = _rope_tables(seq)

    wp_all, wg_all = _proj_weights(w_in)
    w1a_all, w2a_all = ffn1_w_in.astype(BF16), ffn1_w_out.astype(BF16)
    w1b_all, w2b_all = ffn2_w_in.astype(BF16), ffn2_w_out.astype(BF16)
    wglu_all, wb_all, wo_all = s5_w_glu.astype(BF16), w_branch.astype(BF16), w_out.astype(BF16)
    s5_ops = jax.vmap(functools.partial(_s5_operators, row_tile=s5_tile))(
        s5_lam_re, s5_lam_im, s5_log_step, s5_b_re, s5_b_im, s5_c_re, s5_c_im, s5_d)

    h = x.reshape(tokens, D_MODEL)
    for l in range(depth):
        row = lambda p: p[l].reshape(1, -1)
        qn = jnp.tile(row(dsa_q_norm), (1, 4))
        kn = jnp.tile(row(dsa_k_norm), (1, 4))
        wgate = jnp.pad(gla_w_gate[l], ((0, LANES - GLA_GATE_RANK), (0, 0))).astype(BF16)
        h, sb, s5u, y_gla, y_sb_window, sb_left, dsa1, dsa4, dsa16 = _ffn_proj(
            h, row(ffn1_norm), w1a_all, w2a_all, row(mix_norm), wp_all, qn, kn, cos_t, sin_t,
            wgate, row(gla_b_gate), row(gla_norm), bsz, seq, tm_a, l)

        y_sb = lax.cond(jnp.max(sb_left) > SB_LOG_ZERO,
                        lambda: _stick_breaking(sb.reshape(bsz, seq, 768)).reshape(tokens, BRANCH_WIDTH),
                        lambda: y_sb_window)

        y_s5 = _s5(s5u.reshape(bsz, s5_rows, S5_CHUNK * 256), *s5_ops, l).reshape(tokens // S5_CHUNK, S5_CHUNK * 256)

        bound = (1.02 * HEAD_DIM ** 0.5 * jnp.max(jnp.abs(dsa_q_norm[l])) * jnp.max(jnp.abs(dsa_k_norm[l]))).reshape(1)
        y_dsa = _dsa(dsa1.reshape(bsz, seq, 768), dsa4, dsa16, bound).reshape(tokens, BRANCH_WIDTH)

        h = _merge_ffn(h, y_sb, y_s5, y_gla, y_dsa, row(mix_norm), wg_all, wglu_all, wb_all, wo_all,
                       row(ffn2_norm), w1b_all, w2b_all, tm_b, l)
    return h.reshape(bsz, seq, D_MODEL)
```

```python
import functools
import math

import jax
import jax.numpy as jnp
from jax import lax
from jax.experimental import pallas as pl
from jax.experimental.pallas import tpu as pltpu

D_MODEL = 1024
HEAD_DIM = 64
BRANCH_WIDTH = 256
N_BRANCH = 4
S5_GROUPS = 16
S5_GROUP_CH = 16
S5_STATE = 64
GLA_HEADS = 4
GLA_DK = 32
GLA_DV = 64
GLA_GATE_RANK = 16
GLA_GATE_NORM = 16.0
DSA_PATTERNS = ((128, 1), (512, 4), (2048, 16))
DSA_BLOCK = 128
ROPE_THETA = 10000.0
D_FF = 2816
RMS_EPS = 1e-6

LANES = 128
VMEM_LIMIT_BYTES = 58 * 1024 * 1024
FF_CHUNK = 256
N_FF_CHUNKS = D_FF // FF_CHUNK
S5_CHUNK = 8
S5_ROW_TILE = 256
GLA_BLOCK = 64
GLA_LEVELS = 6
GLA_SUB = 128
SB_Q = 64
SB_WIN = 4 * SB_Q
SB_GROUP = 8
SB_TAIL = SB_WIN - SB_Q
SB_LOG_ZERO = -104.0
NEG_BIG = -1e30
DSA_GROUP = 16
DSA_SPAN = DSA_PATTERNS[-1][1] * DSA_BLOCK
DSA_MAX_SHIFT = 30.0

PROJ_SB = (0, 768)
PROJ_S5 = (768, 1024)
PROJ_GLA = (1024, 1792)
PROJ_DSA = (1792, 2560)
PROJ_GG = (2560, 2688)

F32 = jnp.float32
BF16 = jnp.bfloat16


def _cparams(sem):
    return pltpu.CompilerParams(dimension_semantics=sem, vmem_limit_bytes=VMEM_LIMIT_BYTES)


def _const_spec(shape, layer=None):
    if layer is None:
        zeros = (0,) * len(shape)
        return pl.BlockSpec(shape, lambda *_: zeros, pipeline_mode=pl.Buffered(1))
    index = (layer,) + (0,) * (len(shape) - 1)
    return pl.BlockSpec((None,) + tuple(shape[1:]), lambda *_: index, pipeline_mode=pl.Buffered(1))


def _rms_rows(x, g):
    ms = jnp.mean(x * x, axis=-1, keepdims=True)
    return x * lax.rsqrt(ms + RMS_EPS) * g


def _dot(a, b):
    return jnp.dot(a, b, preferred_element_type=F32)


def _dot_nt(a, b):
    return lax.dot_general(a, b, (((1,), (1,)), ((), ())), preferred_element_type=F32)


def _dot_tn(a, b):
    return lax.dot_general(a, b, (((0,), (0,)), ((), ())), preferred_element_type=F32)


def _round_robin(*stage_generators):
    live = list(stage_generators)
    while live:
        live = [g for g in live if next(g, StopIteration) is not StopIteration]
        yield


def _swiglu_into(acc_ref, xb, w1_ref, w2_ref, between=None):
    for c in range(N_FF_CHUNKS):
        lo, hi = c * FF_CHUNK, (c + 1) * FF_CHUNK
        a = _dot(xb, w1_ref[:, lo:hi])
        b = _dot(xb, w1_ref[:, D_FF + lo:D_FF + hi])
        hm = (a * jax.nn.sigmoid(a) * b).astype(BF16)
        part = _dot(hm, w2_ref[lo:hi, :])
        if c == 0:
            acc_ref[...] = part
        else:
            acc_ref[...] += part
        if between is not None:
            next(between, None)


def _group_mean_matrix(width, group):
    r = lax.broadcasted_iota(jnp.int32, (width, width), 0) // group
    c = lax.broadcasted_iota(jnp.int32, (width, width), 1) // group
    return jnp.where(r == c, 1.0 / group, 0.0).astype(BF16)


def _group_rms(x, gain):
    gm = _group_mean_matrix(x.shape[-1], HEAD_DIM)
    ms = _dot((x * x).astype(BF16), gm)
    return x * lax.rsqrt(ms + RMS_EPS) * gain


def _swap_half_heads(x):
    half = HEAD_DIM // 2
    outs = []
    for s in range(x.shape[-1] // LANES):
        xs = x[:, s * LANES:(s + 1) * LANES]
        lane = lax.broadcasted_iota(jnp.int32, xs.shape, 1)
        up = pltpu.roll(xs, LANES - half, axis=1)
        down = pltpu.roll(xs, half, axis=1)
        outs.append(jnp.where((lane % HEAD_DIM) < half, up, down))
    return jnp.concatenate(outs, axis=-1)


def _ffn_proj_tile(x_ref, n1_ref, w1_ref, w2_ref, nm_ref, wp_ref, qn_ref, kn_ref, cos_ref, sin_ref,
                   h_ref, sb_ref, s5_ref, dsa1_ref, dsa4_ref, dsa16_ref, acc_ref, slab_ref, glax_ref, glag_ref,
                   sbh_ref, between, after_ffn):
    tm = x_ref.shape[0]
    x = x_ref[...]
    xb = _rms_rows(x, n1_ref[...]).astype(BF16)
    _swiglu_into(acc_ref, xb, w1_ref, w2_ref, between=between)
    after_ffn()
    h = x + 0.5 * acc_ref[...]
    h_ref[...] = h
    ub = _rms_rows(h, nm_ref[...]).astype(BF16)

    dsa = _dot_nt(ub, wp_ref[PROJ_DSA[0]:PROJ_DSA[1], :])
    cos = jnp.concatenate([cos_ref[...]] * 2, axis=-1)
    sin = jnp.concatenate([sin_ref[...]] * 2, axis=-1)
    parts = []
    for i, gain_ref in enumerate((qn_ref, kn_ref)):
        t = _group_rms(dsa[:, i * 256:(i + 1) * 256], gain_ref[...])
        parts.append(t * cos + _swap_half_heads(t) * sin)
    parts.append(dsa[:, 512:768])
    dsa = jnp.concatenate(parts, axis=-1)
    dsa1_ref[...] = dsa.astype(BF16)
    for s in range(6):
        slab_ref[s] = dsa[:, s * LANES:(s + 1) * LANES]
    for dil, ref in ((4, dsa4_ref), (16, dsa16_ref)):
        for r in range(dil):
            for s in range(6):
                rows = slab_ref[s, pl.ds(r, tm // dil, stride=dil), :]
                ref[0, r, :, s * LANES:(s + 1) * LANES] = rows.astype(BF16)

    s5 = _dot_nt(ub, wp_ref[PROJ_S5[0]:PROJ_S5[1], :])
    for s in range(2):
        slab_ref[6 + s] = s5[:, s * LANES:(s + 1) * LANES]
    for j in range(S5_CHUNK):
        for s in range(2):
            col = j * 256 + s * LANES
            s5_ref[:, col:col + LANES] = slab_ref[6 + s, pl.ds(j, tm // S5_CHUNK, stride=S5_CHUNK), :]

    sb = _dot_nt(ub, wp_ref[PROJ_SB[0]:PROJ_SB[1], :]).astype(BF16)
    sb_ref[...] = sb
    tail = sbh_ref[tm:tm + SB_TAIL, :]
    sbh_ref[0:SB_TAIL, :] = tail
    sbh_ref[SB_TAIL:, :] = sb
    glax_ref[...] = _dot_nt(ub, wp_ref[PROJ_GLA[0]:PROJ_GLA[1], :]).astype(BF16)
    glag_ref[...] = _dot_nt(ub, wp_ref[PROJ_GG[0]:PROJ_GG[1], :])


def _ffn_proj_kernel(x_ref, n1_ref, w1_ref, w2_ref, nm_ref, wp_ref, qn_ref, kn_ref, cos_ref, sin_ref,
                     wgate_ref, bgate_ref, gnorm_ref,
                     h_ref, sb_ref, s5_ref, ygla_ref, ysb_ref, sbf_ref, dsa1_ref, dsa4_ref, dsa16_ref,
                     acc_ref, slab_ref, glax_ref, glag_ref, glas_ref, sbh_ref, *, seq_tiles):
    step = pl.program_id(0)
    n_tiles = pl.num_programs(0) - 1

    @pl.when(step == 0)
    def _():
        glax_ref[...] = jnp.zeros_like(glax_ref)
        glag_ref[...] = jnp.zeros_like(glag_ref)
        glas_ref[...] = jnp.zeros_like(glas_ref)
        sbh_ref[...] = jnp.zeros_like(sbh_ref)

    def start_mixers():
        starts_sequence = lax.rem(step - 1, seq_tiles) == 0
        state = jnp.where(starts_sequence, 0.0, glas_ref[...])
        stacked, masks = _gla_constants()
        gla, sbw = {}, {}
        stages = _round_robin(
            _gla_stages(glax_ref[...], glag_ref[...], state, wgate_ref[...], bgate_ref[...], gnorm_ref[...],
                        stacked, masks, gla),
            _sb_window_stages(sbh_ref, jnp.logical_not(starts_sequence), sbw))
        return stages, gla, sbw

    def finish_mixers(stages, gla, sbw):
        for _ in stages:
            pass
        ygla_ref[...] = gla["y"].astype(ygla_ref.dtype)
        glas_ref[...] = gla["state"]
        ysb_ref[...] = sbw["y"].astype(ysb_ref.dtype)
        sbf_ref[...] = jnp.full(sbf_ref.shape, sbw["cmax"], F32)

    @pl.when(step < n_tiles)
    def _():
        stages, gla, sbw = start_mixers()
        _ffn_proj_tile(x_ref, n1_ref, w1_ref, w2_ref, nm_ref, wp_ref, qn_ref, kn_ref, cos_ref, sin_ref,
                       h_ref, sb_ref, s5_ref, dsa1_ref, dsa4_ref, dsa16_ref, acc_ref, slab_ref, glax_ref, glag_ref,
                       sbh_ref, between=stages, after_ffn=lambda: finish_mixers(stages, gla, sbw))

    @pl.when(step == n_tiles)
    def _():
        finish_mixers(*start_mixers())


def _ffn_proj(x2, n1, w1, w2, nm, wp, qn, kn, cos_t, sin_t, wgate, bgate, gnorm, bsz, seq, tm, layer):
    tokens = x2.shape[0]
    seq_tiles = seq // tm
    n_tiles = tokens // tm
    cur = lambda i: jnp.minimum(i, n_tiles - 1)
    row = lambda i: (cur(i), 0)
    lag = lambda i: (jnp.maximum(i - 1, 0), 0)
    tab = lambda i: (cur(i) % seq_tiles, 0)
    res = lambda i: (cur(i) // seq_tiles, 0, cur(i) % seq_tiles, 0)
    tile = lambda w: pl.BlockSpec((tm, w), row)
    s5_w = S5_CHUNK * 256
    return pl.pallas_call(
        functools.partial(_ffn_proj_kernel, seq_tiles=seq_tiles),
        grid=(n_tiles + 1,),
        in_specs=[tile(D_MODEL), _const_spec(n1.shape), _const_spec(w1.shape, layer), _const_spec(w2.shape, layer),
                  _const_spec(nm.shape), _const_spec(wp.shape, layer), _const_spec(qn.shape), _const_spec(kn.shape),
                  pl.BlockSpec((tm, LANES), tab), pl.BlockSpec((tm, LANES), tab),
                  _const_spec(wgate.shape), _const_spec(bgate.shape), _const_spec(gnorm.shape)],
        out_specs=[tile(D_MODEL), tile(768), pl.BlockSpec((tm // S5_CHUNK, s5_w), row),
                   pl.BlockSpec((tm, BRANCH_WIDTH), lag), pl.BlockSpec((tm, BRANCH_WIDTH), lag),
                   pl.BlockSpec((8, LANES), lag),
                   tile(768), pl.BlockSpec((1, 4, tm // 4, 768), res), pl.BlockSpec((1, 16, tm // 16, 768), res)],
        out_shape=[jax.ShapeDtypeStruct((tokens, D_MODEL), F32),
                   jax.ShapeDtypeStruct((tokens, 768), BF16),
                   jax.ShapeDtypeStruct((tokens // S5_CHUNK, s5_w), F32),
                   jax.ShapeDtypeStruct((tokens, BRANCH_WIDTH), BF16),
                   jax.ShapeDtypeStruct((tokens, BRANCH_WIDTH), BF16),
                   jax.ShapeDtypeStruct((n_tiles * 8, LANES), F32),
                   jax.ShapeDtypeStruct((tokens, 768), BF16),
                   jax.ShapeDtypeStruct((bsz, 4, seq // 4, 768), BF16),
                   jax.ShapeDtypeStruct((bsz, 16, seq // 16, 768), BF16)],
        scratch_shapes=[pltpu.VMEM((tm, D_MODEL), F32), pltpu.VMEM((8, tm, LANES), F32),
                        pltpu.VMEM((tm, 768), BF16), pltpu.VMEM((tm, LANES), F32),
                        pltpu.VMEM((GLA_HEADS * GLA_DV, GLA_HEADS * GLA_DK), F32),
                        pltpu.VMEM((SB_TAIL + tm, 768), BF16)],
        compiler_params=_cparams(("arbitrary",)),
        name="ffn_proj",
    )(x2, n1, w1, w2, nm, wp, qn, kn, cos_t, sin_t, wgate, bgate, gnorm)


def _merge_ffn_kernel(h_ref, ysb_ref, ys5_ref, ygla_ref, ydsa_ref,
                      nm_ref, wg_ref, wglu_ref, wb_ref, wo_ref, n2_ref, w1_ref, w2_ref,
                      out_ref, acc_ref, slab_ref):
    tm = h_ref.shape[0]
    proj = {n: _dot(ref[...], wb_ref[n]) for n, ref in ((0, ysb_ref), (2, ygla_ref), (3, ydsa_ref))}

    for j in range(S5_CHUNK):
        for s in range(2):
            col = j * 256 + s * LANES
            slab_ref[s, pl.ds(j, tm // S5_CHUNK, stride=S5_CHUNK), :] = ys5_ref[:, col:col + LANES].astype(F32)
    y_s5 = jnp.concatenate([slab_ref[0], slab_ref[1]], axis=-1).astype(BF16)
    glu = _dot(y_s5, wglu_ref[...])
    y_s5 = glu[:, :BRANCH_WIDTH] * jax.nn.sigmoid(glu[:, BRANCH_WIDTH:])
    proj[1] = _dot(y_s5.astype(BF16), wb_ref[1])

    h = h_ref[...]
    ub = _rms_rows(h, nm_ref[...]).astype(BF16)
    mixed = jnp.zeros(h.shape, F32)
    for n in range(N_BRANCH):
        gate = jax.nn.sigmoid(_dot_nt(ub, wg_ref[n * D_MODEL:(n + 1) * D_MODEL, :]))
        mixed = mixed + gate * proj[n]
    h2 = h + _dot(mixed.astype(BF16), wo_ref[...])

    xb = _rms_rows(h2, n2_ref[...]).astype(BF16)
    _swiglu_into(acc_ref, xb, w1_ref, w2_ref)
    out_ref[...] = h2 + 0.5 * acc_ref[...]


def _merge_ffn(h, ysb, ys5, ygla, ydsa, nm, wg, wglu, wb, wo, n2, w1, w2, tm, layer):
    tokens = h.shape[0]
    row = lambda i: (i, 0)
    tile = lambda w: pl.BlockSpec((tm, w), row)
    consts = (nm, wg, wglu, wb, wo, n2, w1, w2)
    return pl.pallas_call(
        _merge_ffn_kernel,
        grid=(tokens // tm,),
        in_specs=[tile(D_MODEL), tile(BRANCH_WIDTH), pl.BlockSpec((tm // S5_CHUNK, S5_CHUNK * 256), row),
                  tile(BRANCH_WIDTH), tile(BRANCH_WIDTH)]
                 + [_const_spec(c.shape, None if c.ndim == 2 and c.shape[0] == 1 else layer) for c in consts],
        out_specs=tile(D_MODEL),
        out_shape=jax.ShapeDtypeStruct((tokens, D_MODEL), F32),
        scratch_shapes=[pltpu.VMEM((tm, D_MODEL), F32), pltpu.VMEM((2, tm, LANES), F32)],
        compiler_params=_cparams(("parallel",)),
        name="merge_ffn",
    )(h, ysb, ys5, ygla, ydsa, *consts)


def _later_key_matrix(n):
    j = lax.broadcasted_iota(jnp.int32, (n, n), 0)
    s = lax.broadcasted_iota(jnp.int32, (n, n), 1)
    return jnp.where(j > s, 1.0, 0.0).astype(BF16)


def _sb_scores(qm, k, keep=None):
    z = _dot_nt(qm, k)
    if keep is not None:
        z = jnp.where(keep, z, NEG_BIG)
    lk = -(jnp.maximum(z, 0.0) + jnp.log(1.0 + jnp.exp(-jnp.abs(z))))
    return z, lk


def _sb_window_stages(hist_ref, not_first, result):
    rows = hist_ref.shape[0] - SB_TAIL
    n_blk = rows // SB_Q
    u_win = _later_key_matrix(SB_WIN)
    lane_q = lax.broadcasted_iota(jnp.int32, (SB_Q, LANES), 1)
    col = lax.broadcasted_iota(jnp.int32, (2 * SB_Q, SB_WIN), 1)
    key_minus_query = col - lax.broadcasted_iota(jnp.int32, (2 * SB_Q, SB_WIN), 0) % SB_Q
    strict = key_minus_query < SB_TAIL
    first = lane_q < HEAD_DIM
    outs = [[None] * n_blk for _ in range(2)]
    cmax = jnp.float32(-jnp.inf)
    for pair in range(2):
        q_cols = slice(pair * LANES, (pair + 1) * LANES)
        k_cols = slice(256 + pair * LANES, 256 + (pair + 1) * LANES)
        v_cols = slice(512 + pair * LANES, 512 + (pair + 1) * LANES)
        zs, lks = [], []
        for qb in range(n_blk):
            qs = hist_ref[SB_TAIL + qb * SB_Q:SB_TAIL + (qb + 1) * SB_Q, q_cols] * (HEAD_DIM ** -0.5)
            qm2 = jnp.concatenate([jnp.where(first, qs, jnp.zeros_like(qs)),
                                   jnp.where(first, jnp.zeros_like(qs), qs)], axis=0)
            keep = strict
            if qb * SB_Q < SB_TAIL:
                first_col = jnp.where(not_first, 0, SB_TAIL - qb * SB_Q)
                keep = strict & (col >= first_col)
            z, lk = _sb_scores(qm2, hist_ref[qb * SB_Q:qb * SB_Q + SB_WIN, k_cols], keep)
            zs.append(z)
            lks.append(lk)
        yield
        later_all = _dot(jnp.concatenate(lks, axis=0).astype(BF16), u_win)
        yield
        for qb in range(n_blk):
            later = later_all[qb * 2 * SB_Q:(qb + 1) * 2 * SB_Q]
            w = jnp.exp(zs[qb] + lks[qb] + later)
            o2 = _dot(w.astype(BF16), hist_ref[qb * SB_Q:qb * SB_Q + SB_WIN, v_cols])
            outs[pair][qb] = jnp.where(first, o2[:SB_Q], o2[SB_Q:])
            left = jnp.max(later[:, 0:1] + lks[qb][:, 0:1])
            has_older = True if qb * SB_Q > SB_TAIL else not_first
            cmax = jnp.maximum(cmax, jnp.where(has_older, left, -jnp.inf))
        yield
    result["y"] = jnp.concatenate([jnp.concatenate(outs[pair], axis=0) for pair in range(2)], axis=1)
    result["cmax"] = cmax


def _sb_kernel(q_ref, k_ref, v_ref, o_ref):
    seq = q_ref.shape[1]
    u_win = _later_key_matrix(SB_WIN)
    u_blk = _later_key_matrix(SB_Q)
    lane_q = lax.broadcasted_iota(jnp.int32, (SB_Q, LANES), 1)
    n_streams = 2 * SB_GROUP
    key_minus_query = (lax.broadcasted_iota(jnp.int32, (2 * SB_Q, SB_WIN), 1)
                       - lax.broadcasted_iota(jnp.int32, (2 * SB_Q, SB_WIN), 0) % SB_Q)

    def q_group(qg, carry):
        starts, qms, zs, lks, vws = [], [], [], [], []
        for sub in range(SB_GROUP):
            t0 = pl.multiple_of((qg * SB_GROUP + sub) * SB_Q, SB_Q)
            start = pl.multiple_of(jnp.maximum(t0 - (SB_WIN - SB_Q), 0), SB_Q)
            q = q_ref[0, pl.ds(t0, SB_Q), :] * (HEAD_DIM ** -0.5)
            kw = k_ref[0, pl.ds(start, SB_WIN), :]
            vws.append(v_ref[0, pl.ds(start, SB_WIN), :])
            strict = key_minus_query < (t0 - start)
            starts.append(start)
            qm2 = jnp.concatenate(
                [jnp.where((lane_q // HEAD_DIM) == head, q, jnp.zeros_like(q)) for head in range(2)], axis=0)
            z, lk = _sb_scores(qm2, kw, strict)
            qms += [qm2[:SB_Q], qm2[SB_Q:]]
            zs.append(z)
            lks.append(lk)
        lk_all = jnp.concatenate(lks, axis=0)
        later_all = _dot(lk_all.astype(BF16), u_win)
        csums, outs = [], []
        for sub in range(SB_GROUP):
            rows = slice(sub * 2 * SB_Q, (sub + 1) * 2 * SB_Q)
            later = later_all[rows]
            w = jnp.exp(zs[sub] + lks[sub] + later)
            o2 = _dot(w.astype(BF16), vws[sub])
            c2 = later[:, 0:1] + lks[sub][:, 0:1]
            outs += [o2[:SB_Q], o2[SB_Q:]]
            csums += [c2[:SB_Q], c2[SB_Q:]]

        def cond(state):
            step, cmaxes, _, _ = state
            flags = [jnp.logical_and(starts[s // 2] - (step + 1) * SB_Q >= 0, cmaxes[s] > SB_LOG_ZERO)
                     for s in range(n_streams)]
            return functools.reduce(jnp.logical_or, flags)

        def body(state):
            step, _, csums, outs = state
            new_c, new_o = [], []
            for s in range(n_streams):
                nxt = starts[s // 2] - (step + 1) * SB_Q
                in_range = nxt >= 0
                p0 = pl.multiple_of(jnp.maximum(nxt, 0), SB_Q)
                kb = k_ref[0, pl.ds(p0, SB_Q), :]
                vb = v_ref[0, pl.ds(p0, SB_Q), :]
                z, lk = _sb_scores(qms[s], kb)
                later = _dot(lk.astype(BF16), u_blk)
                w = jnp.where(in_range, jnp.exp(z + lk + later + csums[s]), 0.0)
                new_o.append(outs[s] + _dot(w.astype(BF16), vb))
                new_c.append(csums[s] + jnp.where(in_range, later[:, 0:1] + lk[:, 0:1], 0.0))
            return step + 1, tuple(jnp.max(c) for c in new_c), tuple(new_c), tuple(new_o)

        init = (jnp.int32(0), tuple(jnp.max(c) for c in csums), tuple(csums), tuple(outs))
        _, _, _, outs = lax.while_loop(cond, body, init)
        for sub in range(SB_GROUP):
            t0 = pl.multiple_of((qg * SB_GROUP + sub) * SB_Q, SB_Q)
            o_ref[0, pl.ds(t0, SB_Q), :] = jnp.where(
                lane_q < HEAD_DIM, outs[2 * sub], outs[2 * sub + 1]).astype(o_ref.dtype)
        return carry

    lax.fori_loop(0, seq // (SB_Q * SB_GROUP), q_group, 0)


def _stick_breaking(sb3):
    bsz, seq, _ = sb3.shape
    spec = lambda off: pl.BlockSpec((1, seq, LANES), lambda b, hp: (b, 0, off + hp))
    return pl.pallas_call(
        _sb_kernel,
        grid=(bsz, 2),
        in_specs=[spec(0), spec(2), spec(4)],
        out_specs=pl.BlockSpec((1, seq, LANES), lambda b, hp: (b, 0, hp)),
        out_shape=jax.ShapeDtypeStruct((bsz, seq, BRANCH_WIDTH), BF16),
        compiler_params=_cparams(("parallel", "parallel")),
        name="stick_breaking",
    )(sb3, sb3, sb3)


def _s5_kernel(u_ref, m_ref, e_ref, g_ref, pw_ref, d_ref, y_ref, state_ref):
    half = S5_GROUPS * S5_STATE

    @pl.when(pl.program_id(1) == 0)
    def _():
        state_ref[...] = jnp.zeros_like(state_ref)

    u = u_ref[0]
    ub = u.astype(BF16)
    rows = u.shape[0]

    def cmul(ar, ai, xr, xi):
        return ar * xr - ai * xi, ar * xi + ai * xr

    w = _dot(ub, e_ref[...])
    wr, wi = w[:, :half], w[:, half:]

    n_c = S5_GROUPS * S5_GROUP_CH
    slots = [ub[:, j * n_c:(j + 1) * n_c] for j in range(S5_CHUNK)]
    cols = []
    for i in range(S5_CHUNK):
        acc = _dot(slots[0], m_ref[i])
        for j in range(1, i + 1):
            acc = acc + _dot(slots[j], m_ref[i - j])
        cols.append(acc)
    y = jnp.concatenate(cols, axis=1)

    row = lax.broadcasted_iota(jnp.int32, (rows, LANES), 0)
    xrs, xis = [], []
    for t in range(half // LANES):
        re, im = slice(t * LANES, (t + 1) * LANES), slice(half + t * LANES, half + (t + 1) * LANES)
        tr, ti = wr[:, re], wi[:, re]
        sr, si = state_ref[0:1, re], state_ref[0:1, im]
        cr, ci = cmul(pw_ref[0:1, re], pw_ref[0:1, im], sr, si)
        tr = jnp.where(row == 0, tr + cr, tr)
        ti = jnp.where(row == 0, ti + ci, ti)
        step = 0
        d = 1
        while d < rows:
            pr = jnp.where(row >= d, pltpu.roll(tr, d, axis=0), 0.0)
            pi = jnp.where(row >= d, pltpu.roll(ti, d, axis=0), 0.0)
            mr, mi = cmul(pw_ref[step:step + 1, re], pw_ref[step:step + 1, im], pr, pi)
            tr, ti = tr + mr, ti + mi
            d *= 2
            step += 1
        xrs.append(jnp.where(row == 0, sr, pltpu.roll(tr, 1, axis=0)))
        xis.append(jnp.where(row == 0, si, pltpu.roll(ti, 1, axis=0)))
        state_ref[0:1, re] = tr[rows - 1:rows, :]
        state_ref[0:1, im] = ti[rows - 1:rows, :]
    xr, xi = jnp.concatenate(xrs, axis=1), jnp.concatenate(xis, axis=1)

    y = y + _dot(xr.astype(BF16), g_ref[:half, :]) + _dot(xi.astype(BF16), g_ref[half:, :])
    y = y + d_ref[...] * u
    y_ref[0] = jax.nn.gelu(y).astype(y_ref.dtype)


def _s5(u3, m, e, g, pw, dskip, layer):
    bsz, rows, width = u3.shape
    tr = min(S5_ROW_TILE, rows)
    blk = pl.BlockSpec((1, tr, width), lambda b, i: (b, i, 0))
    return pl.pallas_call(
        _s5_kernel,
        grid=(bsz, rows // tr),
        in_specs=[blk] + [_const_spec(a.shape, layer) for a in (m, e, g, pw, dskip)],
        out_specs=blk,
        out_shape=jax.ShapeDtypeStruct(u3.shape, BF16),
        scratch_shapes=[pltpu.VMEM((8, width), F32)],
        compiler_params=_cparams(("parallel", "arbitrary")),
        name="s5",
    )(u3, m, e, g, pw, dskip)


def _s5_operators(lam_re, lam_im, log_step, b_re, b_im, c_re, c_im, d_skip, row_tile):
    n_g, n_p, n_h, L = S5_GROUPS, S5_STATE, S5_GROUP_CH, S5_CHUNK
    n_c, n_s = n_g * n_h, n_g * n_p
    dt = jnp.exp(log_step)[:, None]

    def lam_bar_pow(k):
        mag = jnp.exp(k[:, None, None] * (lam_re * dt)[None])
        ang = k[:, None, None] * (lam_im * dt)[None]
        return mag * jnp.cos(ang), mag * jnp.sin(ang)

    pr, pi = lam_bar_pow(jnp.arange(L + 1, dtype=F32))
    den = lam_re * lam_re + lam_im * lam_im
    fr = ((pr[1] - 1.0) * lam_re + pi[1] * lam_im) / den
    fi = (pi[1] * lam_re - (pr[1] - 1.0) * lam_im) / den
    bbr = fr[..., None] * b_re - fi[..., None] * b_im
    bbi = fr[..., None] * b_im + fi[..., None] * b_re
    tr = pr[:L, :, :, None] * bbr[None] - pi[:L, :, :, None] * bbi[None]
    ti = pr[:L, :, :, None] * bbi[None] + pi[:L, :, :, None] * bbr[None]

    def expand(small, rows_per_group, cols_per_group):
        rows = small.shape[-2]
        src = lax.broadcasted_iota(jnp.int32, (cols_per_group, n_g * cols_per_group), 0)
        dst = lax.broadcasted_iota(jnp.int32, (cols_per_group, n_g * cols_per_group), 1) % cols_per_group
        spread = jnp.where(src == dst, 1.0, 0.0).astype(BF16)
        tiled = jnp.einsum('...rc,cq->...rq', small.astype(BF16), spread, preferred_element_type=F32)
        r = lax.broadcasted_iota(jnp.int32, (rows, n_g * cols_per_group), 0) // rows_per_group
        c = lax.broadcasted_iota(jnp.int32, (rows, n_g * cols_per_group), 1) // cols_per_group
        return jnp.where(r == c, tiled, 0.0)

    hp = lax.Precision.HIGHEST
    kern = (jnp.einsum('ghp,kgpi->kghi', c_re, tr, precision=hp)
            - jnp.einsum('ghp,kgpi->kghi', c_im, ti, precision=hp))
    m_op = expand(kern.transpose(0, 1, 3, 2).reshape(L, n_c, n_h), n_h, n_h).astype(BF16)

    e_r = expand(tr[::-1].transpose(0, 1, 3, 2).reshape(L, n_c, n_p), n_h, n_p)
    e_i = expand(ti[::-1].transpose(0, 1, 3, 2).reshape(L, n_c, n_p), n_h, n_p)
    e_op = jnp.concatenate([e_r, e_i], axis=2).astype(BF16).reshape(L * n_c, 2 * n_s)

    clr = c_re[None] * pr[1:, :, None, :] - c_im[None] * pi[1:, :, None, :]
    cli = c_re[None] * pi[1:, :, None, :] + c_im[None] * pr[1:, :, None, :]
    g_r = expand(clr.transpose(0, 1, 3, 2).reshape(L, n_s, n_h), n_p, n_h)
    g_i = expand(cli.transpose(0, 1, 3, 2).reshape(L, n_s, n_h), n_p, n_h)
    g_blocks = jnp.concatenate([g_r, -g_i], axis=1).astype(BF16)
    g_op = jnp.concatenate([g_blocks[i] for i in range(L)], axis=1)

    n_steps = max(1, int(math.log2(row_tile)))
    sr, si = lam_bar_pow(L * 2.0 ** jnp.arange(n_steps, dtype=F32))
    pw = jnp.concatenate([sr.reshape(n_steps, n_s), si.reshape(n_steps, n_s)], axis=1)
    pw = jnp.pad(pw, ((0, (-n_steps) % 8), (0, 0)))
    dskip = jnp.tile(d_skip.reshape(1, n_c), (1, L))
    return m_op, e_op, g_op, pw.astype(F32), dskip.astype(F32)


def _gla_constants():
    n = GLA_SUB
    ri = lax.broadcasted_iota(jnp.int32, (n, n), 0)
    ci = lax.broadcasted_iota(jnp.int32, (n, n), 1)
    same = (ri // GLA_BLOCK) == (ci // GLA_BLOCK)
    mats = [same & (ci <= ri), same & (ci > ri)]
    masks = [ri == ci]
    ups, lows = [], []
    for lev in range(GLA_LEVELS):
        half = 1 << lev
        blk = 2 * half
        mid = (ri // blk) * blk + (half - 1)
        later = (ri % blk) >= half
        ups.append(later & (ci > mid) & (ci <= ri))
        lows.append(jnp.logical_not(later) & (ci > ri) & (ci <= mid))
        masks.append(((ri // blk) == (ci // blk)) & later & ((ci % blk) < half))
    stacked = jnp.concatenate([jnp.where(m, 1.0, 0.0).astype(BF16) for m in mats + ups + lows], axis=0)
    return stacked, masks


def _gla_stages(x, gg, state, wgate, bgate, norm, stacked, masks, result):
    n_rows = x.shape[0]
    n_sub = n_rows // GLA_SUB
    hk = GLA_HEADS * GLA_DK
    hv = GLA_HEADS * GLA_DV
    q = x[:, 0:hk].astype(F32) * (GLA_DK ** -0.5)
    k = x[:, hk:2 * hk].astype(F32)
    vb = x[:, 2 * hk:2 * hk + hv]
    r = x[:, 2 * hk + hv:].astype(F32)
    sub = lambda t, st: t[st * GLA_SUB:(st + 1) * GLA_SUB]

    z = _dot(gg.astype(BF16), wgate) + bgate
    g = -(jnp.maximum(-z, 0.0) + jnp.log(1.0 + jnp.exp(-jnp.abs(z)))) * (1.0 / GLA_GATE_NORM)
    gb = g.astype(BF16)
    yield

    def decay_sums(st, blocks):
        picked = jnp.concatenate([stacked[i * GLA_SUB:(i + 1) * GLA_SUB] for i in blocks], axis=0)
        both = _dot(picked, sub(gb, st))
        return [both[n * GLA_SUB:(n + 1) * GLA_SUB] for n in range(len(blocks))]

    def level_blocks(lev):
        return 2 + lev, 2 + GLA_LEVELS + lev

    lane_k = lax.broadcasted_iota(jnp.int32, (GLA_SUB, hk), 1) // GLA_DK
    att = [[jnp.zeros((GLA_SUB, GLA_SUB), F32) for _ in range(GLA_HEADS)] for _ in range(n_sub)]
    ahead = [decay_sums(st, level_blocks(0)) for st in range(n_sub)]
    for lev in range(-1, GLA_LEVELS):
        now, ahead = ahead, None
        for st in range(n_sub):
            qs, ks = sub(q, st), sub(k, st)
            if lev < 0:
                qt, kt = qs.astype(BF16), ks.astype(BF16)
            else:
                qt = (qs * jnp.exp(now[st][0])).astype(BF16)
                kt = (ks * jnp.exp(now[st][1])).astype(BF16)
            q_heads = jnp.concatenate(
                [jnp.where(lane_k == h, qt, jnp.zeros_like(qt)) for h in range(GLA_HEADS)], axis=0)
            a = _dot_nt(q_heads, kt)
            for h in range(GLA_HEADS):
                att[st][h] = jnp.where(masks[lev + 1], a[h * GLA_SUB:(h + 1) * GLA_SUB], att[st][h])
        if lev < 0:
            ahead = now
        elif lev + 1 < GLA_LEVELS:
            ahead = [decay_sums(st, level_blocks(lev + 1)) for st in range(n_sub)]
        else:
            ahead = [decay_sums(st, (0, 1)) for st in range(n_sub)]
        yield
    part = lambda st, i: ahead[st][i]

    lane_v = lax.broadcasted_iota(jnp.int32, (GLA_SUB, hv), 1) // GLA_DV
    inner = []
    for st in range(n_sub):
        vs = sub(vb, st)
        att_all = jnp.concatenate([a.astype(BF16) for a in att[st]], axis=1)
        v_heads = jnp.concatenate([jnp.where(lane_v == h, vs, jnp.zeros_like(vs)) for h in range(GLA_HEADS)], axis=0)
        inner.append(_dot(att_all, v_heads))
    yield

    row_blk = lax.broadcasted_iota(jnp.int32, (GLA_SUB, hk), 0) // GLA_BLOCK
    rs = lax.broadcasted_iota(jnp.int32, (hv, hk), 0) // GLA_DV
    cs = lax.broadcasted_iota(jnp.int32, (hv, hk), 1) // GLA_DK
    head_diag = rs == cs
    n_blk = GLA_SUB // GLA_BLOCK
    outs = []
    for st in range(n_sub):
        b, tail = part(st, 0), part(st, 1)
        q_dec = (sub(q, st) * jnp.exp(b)).astype(BF16)
        k_dec = (sub(k, st) * jnp.exp(tail)).astype(BF16)
        b_end = b + tail
        k_blocks = jnp.concatenate(
            [jnp.where(row_blk == n, k_dec, jnp.zeros_like(k_dec)) for n in range(n_blk)], axis=1)
        upd_all = _dot_tn(sub(vb, st), k_blocks)
        cross = []
        for blk in range(n_blk):
            r0 = blk * GLA_BLOCK
            cross.append(_dot_nt(q_dec[r0:r0 + GLA_BLOCK], state.astype(BF16)))
            upd = upd_all[:, blk * hk:(blk + 1) * hk]
            state = jnp.exp(b_end[r0:r0 + 1, :]) * state + jnp.where(head_diag, upd, 0.0)
        outs.append(inner[st] + jnp.concatenate(cross, axis=0))
    yield

    o = _group_rms(jnp.concatenate(outs, axis=0), norm)
    result["y"] = o * (r * jax.nn.sigmoid(r))
    result["state"] = state


def _dsa_band_bias(offset, n_steps):
    row = lax.broadcasted_iota(jnp.int32, (2 * DSA_BLOCK, 2 * DSA_BLOCK), 0) % DSA_BLOCK
    col = lax.broadcasted_iota(jnp.int32, (2 * DSA_BLOCK, 2 * DSA_BLOCK), 1)
    dist = offset + row - col
    return jnp.where((dist >= 0) & (dist <= n_steps), 0.0, NEG_BIG)


def _dsa_logits(q, kw, bias):
    lane_q = lax.broadcasted_iota(jnp.int32, (DSA_BLOCK, LANES), 1)
    q = q * (HEAD_DIM ** -0.5)
    qm2 = jnp.concatenate(
        [jnp.where((lane_q // HEAD_DIM) == head, q, jnp.zeros_like(q)) for head in range(2)], axis=0)
    return _dot_nt(qm2, kw) + bias


def _dsa_softmax_pv(s, vw, shift):
    if shift is None:
        m = jnp.max(s, axis=1, keepdims=True)
        p = jnp.exp(s - m)
        l = jnp.sum(p, axis=1, keepdims=True)
        o2 = _dot(p.astype(BF16), vw) / l
        lse2 = jnp.broadcast_to(m + jnp.log(l), (2 * DSA_BLOCK, LANES))
    else:
        p = jnp.exp(s)
        pv = _dot(p.astype(BF16), jnp.concatenate([vw, jnp.ones_like(vw)], axis=1))
        l = pv[:, LANES:]
        o2 = pv[:, :LANES] / l
        lse2 = shift + jnp.log(l)
    first = lax.broadcasted_iota(jnp.int32, (DSA_BLOCK, LANES), 1) < HEAD_DIM
    return (jnp.where(first, o2[:DSA_BLOCK], o2[DSA_BLOCK:]),
            jnp.where(first, lse2[:DSA_BLOCK], lse2[DSA_BLOCK:]))


def _dsa_kernel(bound_ref, q1_ref, k1_ref, v1_ref, q4_ref, k4_ref, v4_ref, q16_ref, k16_ref, v16_ref,
                y_ref, o_scr, l_scr):
    seq = q1_ref.shape[1]
    refs = ((q1_ref, k1_ref, v1_ref), (q4_ref, k4_ref, v4_ref), (q16_ref, k16_ref, v16_ref))
    blocks_per_span = DSA_SPAN // DSA_BLOCK
    n_steps = DSA_PATTERNS[0][0] // DSA_PATTERNS[0][1]
    assert all(w // d == n_steps for w, d in DSA_PATTERNS)
    bias_inner = _dsa_band_bias(DSA_BLOCK, n_steps)
    bias_first = _dsa_band_bias(0, n_steps)

    def run(shift):
        inner, first = (bias_inner, bias_first) if shift is None else (bias_inner - shift, bias_first - shift)

        def span(sp, carry):
            for p, (_, dil) in enumerate(DSA_PATTERNS):
                q_ref, k_ref, v_ref = refs[p]
                per_res = blocks_per_span // dil

                def group(gi, c, p=p, dil=dil, per_res=per_res, q_ref=q_ref, k_ref=k_ref, v_ref=v_ref):
                    units = []
                    for u in range(DSA_GROUP):
                        idx = gi * DSA_GROUP + u
                        r = idx // per_res
                        jb = idx % per_res
                        m0 = pl.multiple_of(sp * (DSA_SPAN // dil) + jb * DSA_BLOCK, DSA_BLOCK)
                        start = pl.multiple_of(jnp.maximum(m0 - DSA_BLOCK, 0), DSA_BLOCK)
                        if dil == 1:
                            q = q_ref[0, pl.ds(m0, DSA_BLOCK), :]
                            kw = k_ref[0, pl.ds(start, 2 * DSA_BLOCK), :]
                            vw = v_ref[0, pl.ds(start, 2 * DSA_BLOCK), :]
                        else:
                            q = q_ref[0, r, pl.ds(m0, DSA_BLOCK), :]
                            kw = k_ref[0, r, pl.ds(start, 2 * DSA_BLOCK), :]
                            vw = v_ref[0, r, pl.ds(start, 2 * DSA_BLOCK), :]
                        may_start_sequence = u % per_res == 0
                        s = _dsa_logits(q, kw, jnp.where(m0 == 0, first, inner) if may_start_sequence else inner)
                        rows = pl.ds(r + dil * jb * DSA_BLOCK, DSA_BLOCK, stride=dil) if dil > 1 else \
                            pl.ds(pl.multiple_of(jb * DSA_BLOCK, DSA_BLOCK), DSA_BLOCK)
                        units.append((s, vw, rows))
                    results = [_dsa_softmax_pv(s, vw, shift) for s, vw, _ in units]
                    for (o, lse), (_, _, rows) in zip(results, units):
                        o_scr[p, rows, :] = o
                        l_scr[p, rows, :] = lse
                    return c

                lax.fori_loop(0, blocks_per_span // DSA_GROUP, group, 0)

            l1, l2, l3 = l_scr[0], l_scr[1], l_scr[2]
            lm = jnp.maximum(jnp.maximum(l1, l2), l3)
            e1, e2, e3 = jnp.exp(l1 - lm), jnp.exp(l2 - lm), jnp.exp(l3 - lm)
            y = (e1 * o_scr[0] + e2 * o_scr[1] + e3 * o_scr[2]) / (e1 + e2 + e3)
            y_ref[0, pl.ds(pl.multiple_of(sp * DSA_SPAN, DSA_SPAN), DSA_SPAN), :] = y.astype(y_ref.dtype)
            return carry

        lax.fori_loop(0, seq // DSA_SPAN, span, 0)

    bound = bound_ref[0]

    @pl.when(bound <= DSA_MAX_SHIFT)
    def _():
        run(bound)

    @pl.when(bound > DSA_MAX_SHIFT)
    def _():
        run(None)


def _dsa(dsa1, dsa4, dsa16, bound):
    bsz, seq, _ = dsa1.shape
    nat = lambda off: pl.BlockSpec((1, seq, LANES), lambda b, hp: (b, 0, off + hp))
    res = lambda dil, off: pl.BlockSpec((1, dil, seq // dil, LANES), lambda b, hp: (b, 0, 0, off + hp))
    return pl.pallas_call(
        _dsa_kernel,
        grid=(bsz, 2),
        in_specs=[pl.BlockSpec(memory_space=pltpu.SMEM),
                  nat(0), nat(2), nat(4), res(4, 0), res(4, 2), res(4, 4), res(16, 0), res(16, 2), res(16, 4)],
        out_specs=pl.BlockSpec((1, seq, LANES), lambda b, hp: (b, 0, hp)),
        out_shape=jax.ShapeDtypeStruct((bsz, seq, BRANCH_WIDTH), BF16),
        scratch_shapes=[pltpu.VMEM((3, DSA_SPAN, LANES), F32), pltpu.VMEM((3, DSA_SPAN, LANES), F32)],
        compiler_params=_cparams(("parallel", "parallel")),
        name="dsa",
    )(bound, dsa1, dsa1, dsa1, dsa4, dsa4, dsa4, dsa16, dsa16, dsa16)


def _proj_weights(w_in):
    w_t = jnp.swapaxes(w_in, 1, 2)
    lead = w_t[:, :1792].astype(BF16)
    gg = jnp.pad(w_t[:, 1792:1808].astype(BF16), ((0, 0), (0, LANES - GLA_GATE_RANK), (0, 0)))
    dsa = w_t[:, 1808:2576].astype(BF16)
    wp = jnp.concatenate([lead, dsa, gg], axis=1)
    return wp, w_t[:, 2576:].astype(BF16)


def _rope_tables(seq):
    inv = ROPE_THETA ** (-jnp.arange(0, HEAD_DIM, 2, dtype=F32) / HEAD_DIM)
    ang = jnp.arange(seq, dtype=F32)[:, None] * inv[None, :]
    cos, sin = jnp.cos(ang), jnp.sin(ang)
    cos_t = jnp.tile(jnp.concatenate([cos, cos], axis=1), (1, 2))
    sin_t = jnp.tile(jnp.concatenate([-sin, sin], axis=1), (1, 2))
    return cos_t, sin_t


def _pick_tile(n, pref):
    t = min(pref, n)
    while n % t:
        t //= 2
    return t


def kernel(x, ffn1_norm, ffn1_w_in, ffn1_w_out, mix_norm, w_in, s5_lam_re, s5_lam_im, s5_log_step, s5_b_re, s5_b_im, s5_c_re, s5_c_im, s5_d, s5_w_glu, gla_w_gate, gla_b_gate, gla_norm, dsa_q_norm, dsa_k_norm, w_branch, w_out, ffn2_norm, ffn2_w_in, ffn2_w_out):
    bsz, seq, _ = x.shape
    tokens = bsz * seq
    depth = ffn1_norm.shape[0]
    assert seq % DSA_SPAN == 0 and seq // DSA_PATTERNS[-1][1] >= 2 * DSA_BLOCK and seq >= SB_WIN
    tm_a = _pick_tile(seq, 512)
    tm_b = _pick_tile(tokens, 512)
    s5_rows = seq // S5_CHUNK
    s5_tile = min(S5_ROW_TILE, s5_rows)
    cos_t, sin_t = _rope_tables(seq)

    wp_all, wg_all = _proj_weights(w_in)
    w1a_all, w2a_all = ffn1_w_in.astype(BF16), ffn1_w_out.astype(BF16)
    w1b_all, w2b_all = ffn2_w_in.astype(BF16), ffn2_w_out.astype(BF16)
    wglu_all, wb_all, wo_all = s5_w_glu.astype(BF16), w_branch.astype(BF16), w_out.astype(BF16)
    s5_ops = jax.vmap(functools.partial(_s5_operators, row_tile=s5_tile))(
        s5_lam_re, s5_lam_im, s5_log_step, s5_b_re, s5_b_im, s5_c_re, s5_c_im, s5_d)

    h = x.reshape(tokens, D_MODEL)
    for l in range(depth):
        row = lambda p: p[l].reshape(1, -1)
        qn = jnp.tile(row(dsa_q_norm), (1, 4))
        kn = jnp.tile(row(dsa_k_norm), (1, 4))
        wgate = jnp.pad(gla_w_gate[l], ((0, LANES - GLA_GATE_RANK), (0, 0))).astype(BF16)
        h, sb, s5u, y_gla, y_sb_window, sb_left, dsa1, dsa4, dsa16 = _ffn_proj(
            h, row(ffn1_norm), w1a_all, w2a_all, row(mix_norm), wp_all, qn, kn, cos_t, sin_t,
            wgate, row(gla_b_gate), row(gla_norm), bsz, seq, tm_a, l)

        y_sb = lax.cond(jnp.max(sb_left) > SB_LOG_ZERO,
                        lambda: _stick_breaking(sb.reshape(bsz, seq, 768)).reshape(tokens, BRANCH_WIDTH),
                        lambda: y_sb_window)

        y_s5 = _s5(s5u.reshape(bsz, s5_rows, S5_CHUNK * 256), *s5_ops, l).reshape(tokens // S5_CHUNK, S5_CHUNK * 256)

        bound = (1.02 * HEAD_DIM ** 0.5 * jnp.max(jnp.abs(dsa_q_norm[l])) * jnp.max(jnp.abs(dsa_k_norm[l]))).reshape(1)
        y_dsa = _dsa(dsa1.reshape(bsz, seq, 768), dsa4, dsa16, bound).reshape(tokens, BRANCH_WIDTH)

        h = _merge_ffn(h, y_sb, y_s5, y_gla, y_dsa, row(mix_norm), wg_all, wglu_all, wb_all, wo_all,
                       row(ffn2_norm), w1b_all, w2b_all, tm_b, l)
    return h.reshape(bsz, seq, D_MODEL)
```

```python
import functools
import math

import jax
import jax.numpy as jnp
from jax import lax
from jax.experimental import pallas as pl
from jax.experimental.pallas import tpu as pltpu

D_MODEL = 1024
HEAD_DIM = 64
BRANCH_WIDTH = 256
N_BRANCH = 4
S5_GROUPS = 16
S5_GROUP_CH = 16
S5_STATE = 64
GLA_HEADS = 4
GLA_DK = 32
GLA_DV = 64
GLA_GATE_RANK = 16
GLA_GATE_NORM = 16.0
DSA_PATTERNS = ((128, 1), (512, 4), (2048, 16))
DSA_BLOCK = 128
ROPE_THETA = 10000.0
D_FF = 2816
RMS_EPS = 1e-6

LANES = 128
VMEM_LIMIT_BYTES = 58 * 1024 * 1024
FF_CHUNK = 256
N_FF_CHUNKS = D_FF // FF_CHUNK
S5_CHUNK = 8
S5_ROW_TILE = 256
GLA_BLOCK = 64
GLA_LEVELS = 6
GLA_SUB = 128
SB_Q = 64
SB_WIN = 4 * SB_Q
SB_GROUP = 8
SB_TAIL = SB_WIN - SB_Q
SB_LOG_ZERO = -104.0
NEG_BIG = -1e30
DSA_GROUP = 16
DSA_SPAN = DSA_PATTERNS[-1][1] * DSA_BLOCK
DSA_MAX_SHIFT = 30.0

PROJ_SB = (0, 768)
PROJ_S5 = (768, 1024)
PROJ_GLA = (1024, 1792)
PROJ_GG = (1792, 1792 + LANES)
PROJ_DSA = (1808, 2576)
PROJ_ROWS = 2576
PROJ_GATES = 2576

F32 = jnp.float32
BF16 = jnp.bfloat16


def _cparams(sem):
    return pltpu.CompilerParams(dimension_semantics=sem, vmem_limit_bytes=VMEM_LIMIT_BYTES)


def _const_spec(shape, layer=None):
    if layer is None:
        zeros = (0,) * len(shape)
        return pl.BlockSpec(shape, lambda *_: zeros, pipeline_mode=pl.Buffered(1))
    index = (layer,) + (0,) * (len(shape) - 1)
    return pl.BlockSpec((None,) + tuple(shape[1:]), lambda *_: index, pipeline_mode=pl.Buffered(1))


def _rms_rows(x, g):
    ms = jnp.mean(x * x, axis=-1, keepdims=True)
    return x * lax.rsqrt(ms + RMS_EPS) * g


def _dot(a, b):
    return jnp.dot(a, b, preferred_element_type=F32)


def _dot_nt(a, b):
    return lax.dot_general(a, b, (((1,), (1,)), ((), ())), preferred_element_type=F32)


def _dot_tn(a, b):
    return lax.dot_general(a, b, (((0,), (0,)), ((), ())), preferred_element_type=F32)


def _round_robin(*stage_generators):
    live = list(stage_generators)
    while live:
        live = [g for g in live if next(g, StopIteration) is not StopIteration]
        yield


def _swiglu_into(acc_ref, xb, w1_ref, w2_ref, between=None):
    for c in range(N_FF_CHUNKS):
        lo, hi = c * FF_CHUNK, (c + 1) * FF_CHUNK
        a = _dot(xb, w1_ref[:, lo:hi])
        b = _dot(xb, w1_ref[:, D_FF + lo:D_FF + hi])
        hm = (a * jax.nn.sigmoid(a) * b).astype(BF16)
        part = _dot(hm, w2_ref[lo:hi, :])
        if c == 0:
            acc_ref[...] = part
        else:
            acc_ref[...] += part
        if between is not None:
            next(between, None)


def _group_mean_matrix(width, group):
    r = lax.broadcasted_iota(jnp.int32, (width, width), 0) // group
    c = lax.broadcasted_iota(jnp.int32, (width, width), 1) // group
    return jnp.where(r == c, 1.0 / group, 0.0).astype(BF16)


def _group_rms(x, gain):
    gm = _group_mean_matrix(x.shape[-1], HEAD_DIM)
    ms = _dot((x * x).astype(BF16), gm)
    return x * lax.rsqrt(ms + RMS_EPS) * gain


def _swap_half_heads(x):
    half = HEAD_DIM // 2
    outs = []
    for s in range(x.shape[-1] // LANES):
        xs = x[:, s * LANES:(s + 1) * LANES]
        lane = lax.broadcasted_iota(jnp.int32, xs.shape, 1)
        up = pltpu.roll(xs, LANES - half, axis=1)
        down = pltpu.roll(xs, half, axis=1)
        outs.append(jnp.where((lane % HEAD_DIM) < half, up, down))
    return jnp.concatenate(outs, axis=-1)


def _ffn_proj_tile(x_ref, n1_ref, w1_ref, w2_ref, nm_ref, wp_ref, qn_ref, kn_ref, cos_ref, sin_ref,
                   h_ref, sb_ref, s5_ref, dsa1_ref, dsa4_ref, dsa16_ref, acc_ref, slab_ref, glax_ref, glag_ref,
                   sbh_ref, between, after_ffn):
    tm = x_ref.shape[0]
    x = x_ref[...]
    xb = _rms_rows(x, n1_ref[...]).astype(BF16)
    _swiglu_into(acc_ref, xb, w1_ref, w2_ref, between=between)
    after_ffn()
    h = x + 0.5 * acc_ref[...]
    h_ref[...] = h
    ub = _rms_rows(h, nm_ref[...]).astype(BF16)

    dsa = _dot_nt(ub, wp_ref[PROJ_DSA[0]:PROJ_DSA[1], :])
    cos = jnp.concatenate([cos_ref[...]] * 2, axis=-1)
    sin = jnp.concatenate([sin_ref[...]] * 2, axis=-1)
    parts = []
    for i, gain_ref in enumerate((qn_ref, kn_ref)):
        t = _group_rms(dsa[:, i * 256:(i + 1) * 256], gain_ref[...])
        parts.append(t * cos + _swap_half_heads(t) * sin)
    parts.append(dsa[:, 512:768])
    dsa = jnp.concatenate(parts, axis=-1)
    dsa1_ref[...] = dsa.astype(BF16)
    for s in range(6):
        slab_ref[s] = dsa[:, s * LANES:(s + 1) * LANES]
    for dil, ref in ((4, dsa4_ref), (16, dsa16_ref)):
        for r in range(dil):
            for s in range(6):
                rows = slab_ref[s, pl.ds(r, tm // dil, stride=dil), :]
                ref[0, r, :, s * LANES:(s + 1) * LANES] = rows.astype(BF16)

    s5 = _dot_nt(ub, wp_ref[PROJ_S5[0]:PROJ_S5[1], :])
    for s in range(2):
        slab_ref[6 + s] = s5[:, s * LANES:(s + 1) * LANES]
    for j in range(S5_CHUNK):
        for s in range(2):
            col = j * 256 + s * LANES
            s5_ref[:, col:col + LANES] = slab_ref[6 + s, pl.ds(j, tm // S5_CHUNK, stride=S5_CHUNK), :]

    sb = _dot_nt(ub, wp_ref[PROJ_SB[0]:PROJ_SB[1], :]).astype(BF16)
    sb_ref[...] = sb
    tail = sbh_ref[tm:tm + SB_TAIL, :]
    sbh_ref[0:SB_TAIL, :] = tail
    sbh_ref[SB_TAIL:, :] = sb
    glax_ref[...] = _dot_nt(ub, wp_ref[PROJ_GLA[0]:PROJ_GLA[1], :]).astype(BF16)
    gg = _dot_nt(ub, wp_ref[PROJ_GG[0]:PROJ_GG[1], :])
    lane = lax.broadcasted_iota(jnp.int32, gg.shape, 1)
    glag_ref[...] = jnp.where(lane < GLA_GATE_RANK, gg, 0.0)


def _ffn_proj_kernel(x_ref, n1_ref, w1_ref, w2_ref, nm_ref, wp_ref, qn_ref, kn_ref, cos_ref, sin_ref,
                     wgate_ref, bgate_ref, gnorm_ref,
                     h_ref, sb_ref, s5_ref, ygla_ref, ysb_ref, sbf_ref, dsa1_ref, dsa4_ref, dsa16_ref,
                     acc_ref, slab_ref, glax_ref, glag_ref, glas_ref, sbh_ref, *, seq_tiles):
    step = pl.program_id(0)
    n_tiles = pl.num_programs(0) - 1

    @pl.when(step == 0)
    def _():
        glax_ref[...] = jnp.zeros_like(glax_ref)
        glag_ref[...] = jnp.zeros_like(glag_ref)
        glas_ref[...] = jnp.zeros_like(glas_ref)
        sbh_ref[...] = jnp.zeros_like(sbh_ref)

    def start_mixers():
        starts_sequence = lax.rem(step - 1, seq_tiles) == 0
        state = jnp.where(starts_sequence, 0.0, glas_ref[...])
        stacked, masks = _gla_constants()
        gla, sbw = {}, {}
        stages = _round_robin(
            _gla_stages(glax_ref[...], glag_ref[...], state, wgate_ref[...], bgate_ref[...], gnorm_ref[...],
                        stacked, masks, gla),
            _sb_window_stages(sbh_ref, jnp.logical_not(starts_sequence), sbw))
        return stages, gla, sbw

    def finish_mixers(stages, gla, sbw):
        for _ in stages:
            pass
        ygla_ref[...] = gla["y"].astype(ygla_ref.dtype)
        glas_ref[...] = gla["state"]
        ysb_ref[...] = sbw["y"].astype(ysb_ref.dtype)
        sbf_ref[...] = jnp.full(sbf_ref.shape, sbw["cmax"], F32)

    @pl.when(step < n_tiles)
    def _():
        stages, gla, sbw = start_mixers()
        _ffn_proj_tile(x_ref, n1_ref, w1_ref, w2_ref, nm_ref, wp_ref, qn_ref, kn_ref, cos_ref, sin_ref,
                       h_ref, sb_ref, s5_ref, dsa1_ref, dsa4_ref, dsa16_ref, acc_ref, slab_ref, glax_ref, glag_ref,
                       sbh_ref, between=stages, after_ffn=lambda: finish_mixers(stages, gla, sbw))

    @pl.when(step == n_tiles)
    def _():
        finish_mixers(*start_mixers())


def _ffn_proj(x2, n1, w1, w2, nm, wp, qn, kn, cos_t, sin_t, wgate, bgate, gnorm, bsz, seq, tm, layer):
    tokens = x2.shape[0]
    seq_tiles = seq // tm
    n_tiles = tokens // tm
    cur = lambda i: jnp.minimum(i, n_tiles - 1)
    row = lambda i: (cur(i), 0)
    lag = lambda i: (jnp.maximum(i - 1, 0), 0)
    tab = lambda i: (cur(i) % seq_tiles, 0)
    res = lambda i: (cur(i) // seq_tiles, 0, cur(i) % seq_tiles, 0)
    tile = lambda w: pl.BlockSpec((tm, w), row)
    s5_w = S5_CHUNK * 256
    return pl.pallas_call(
        functools.partial(_ffn_proj_kernel, seq_tiles=seq_tiles),
        grid=(n_tiles + 1,),
        in_specs=[tile(D_MODEL), _const_spec(n1.shape), _const_spec(w1.shape, layer), _const_spec(w2.shape, layer),
                  _const_spec(nm.shape), _const_spec((wp.shape[0], PROJ_ROWS, D_MODEL), layer),
                  _const_spec(qn.shape), _const_spec(kn.shape),
                  pl.BlockSpec((tm, LANES), tab), pl.BlockSpec((tm, LANES), tab),
                  _const_spec(wgate.shape), _const_spec(bgate.shape), _const_spec(gnorm.shape)],
        out_specs=[tile(D_MODEL), tile(768), pl.BlockSpec((tm // S5_CHUNK, s5_w), row),
                   pl.BlockSpec((tm, BRANCH_WIDTH), lag), pl.BlockSpec((tm, BRANCH_WIDTH), lag),
                   pl.BlockSpec((8, LANES), lag),
                   tile(768), pl.BlockSpec((1, 4, tm // 4, 768), res), pl.BlockSpec((1, 16, tm // 16, 768), res)],
        out_shape=[jax.ShapeDtypeStruct((tokens, D_MODEL), F32),
                   jax.ShapeDtypeStruct((tokens, 768), BF16),
                   jax.ShapeDtypeStruct((tokens // S5_CHUNK, s5_w), F32),
                   jax.ShapeDtypeStruct((tokens, BRANCH_WIDTH), BF16),
                   jax.ShapeDtypeStruct((tokens, BRANCH_WIDTH), BF16),
                   jax.ShapeDtypeStruct((n_tiles * 8, LANES), F32),
                   jax.ShapeDtypeStruct((tokens, 768), BF16),
                   jax.ShapeDtypeStruct((bsz, 4, seq // 4, 768), BF16),
                   jax.ShapeDtypeStruct((bsz, 16, seq // 16, 768), BF16)],
        scratch_shapes=[pltpu.VMEM((tm, D_MODEL), F32), pltpu.VMEM((8, tm, LANES), F32),
                        pltpu.VMEM((tm, 768), BF16), pltpu.VMEM((tm, LANES), F32),
                        pltpu.VMEM((GLA_HEADS * GLA_DV, GLA_HEADS * GLA_DK), F32),
                        pltpu.VMEM((SB_TAIL + tm, 768), BF16)],
        compiler_params=_cparams(("arbitrary",)),
        name="ffn_proj",
    )(x2, n1, w1, w2, nm, wp, qn, kn, cos_t, sin_t, wgate, bgate, gnorm)


def _merge_ffn_kernel(h_ref, ysb_ref, ys5_ref, ygla_ref, ydsa_ref,
                      nm_ref, wg_ref, wglu_ref, wb_ref, wo_ref, n2_ref, w1_ref, w2_ref,
                      out_ref, acc_ref, slab_ref):
    tm = h_ref.shape[0]
    proj = {n: _dot(ref[...], wb_ref[n]) for n, ref in ((0, ysb_ref), (2, ygla_ref), (3, ydsa_ref))}

    for j in range(S5_CHUNK):
        for s in range(2):
            col = j * 256 + s * LANES
            slab_ref[s, pl.ds(j, tm // S5_CHUNK, stride=S5_CHUNK), :] = ys5_ref[:, col:col + LANES].astype(F32)
    y_s5 = jnp.concatenate([slab_ref[0], slab_ref[1]], axis=-1).astype(BF16)
    glu = _dot(y_s5, wglu_ref[...])
    y_s5 = glu[:, :BRANCH_WIDTH] * jax.nn.sigmoid(glu[:, BRANCH_WIDTH:])
    proj[1] = _dot(y_s5.astype(BF16), wb_ref[1])

    h = h_ref[...]
    ub = _rms_rows(h, nm_ref[...]).astype(BF16)
    mixed = jnp.zeros(h.shape, F32)
    for n in range(N_BRANCH):
        rows = PROJ_GATES + n * D_MODEL
        gate = jax.nn.sigmoid(_dot_nt(ub, wg_ref[rows:rows + D_MODEL, :]))
        mixed = mixed + gate * proj[n]
    h2 = h + _dot(mixed.astype(BF16), wo_ref[...])

    xb = _rms_rows(h2, n2_ref[...]).astype(BF16)
    _swiglu_into(acc_ref, xb, w1_ref, w2_ref)
    out_ref[...] = h2 + 0.5 * acc_ref[...]


def _merge_ffn(h, ysb, ys5, ygla, ydsa, nm, wg, wglu, wb, wo, n2, w1, w2, tm, layer):
    tokens = h.shape[0]
    row = lambda i: (i, 0)
    tile = lambda w: pl.BlockSpec((tm, w), row)
    consts = (nm, wg, wglu, wb, wo, n2, w1, w2)
    return pl.pallas_call(
        _merge_ffn_kernel,
        grid=(tokens // tm,),
        in_specs=[tile(D_MODEL), tile(BRANCH_WIDTH), pl.BlockSpec((tm // S5_CHUNK, S5_CHUNK * 256), row),
                  tile(BRANCH_WIDTH), tile(BRANCH_WIDTH)]
                 + [_const_spec(c.shape, None if c.ndim == 2 and c.shape[0] == 1 else layer) for c in consts],
        out_specs=tile(D_MODEL),
        out_shape=jax.ShapeDtypeStruct((tokens, D_MODEL), F32),
        scratch_shapes=[pltpu.VMEM((tm, D_MODEL), F32), pltpu.VMEM((2, tm, LANES), F32)],
        compiler_params=_cparams(("parallel",)),
        name="merge_ffn",
    )(h, ysb, ys5, ygla, ydsa, *consts)


def _later_key_matrix(n):
    j = lax.broadcasted_iota(jnp.int32, (n, n), 0)
    s = lax.broadcasted_iota(jnp.int32, (n, n), 1)
    return jnp.where(j > s, 1.0, 0.0).astype(BF16)


def _sb_scores(qm, k, keep=None):
    z = _dot_nt(qm, k)
    if keep is not None:
        z = jnp.where(keep, z, NEG_BIG)
    lk = -(jnp.maximum(z, 0.0) + jnp.log(1.0 + jnp.exp(-jnp.abs(z))))
    return z, lk


def _sb_window_stages(hist_ref, not_first, result):
    rows = hist_ref.shape[0] - SB_TAIL
    n_blk = rows // SB_Q
    u_win = _later_key_matrix(SB_WIN)
    lane_q = lax.broadcasted_iota(jnp.int32, (SB_Q, LANES), 1)
    col = lax.broadcasted_iota(jnp.int32, (2 * SB_Q, SB_WIN), 1)
    key_minus_query = col - lax.broadcasted_iota(jnp.int32, (2 * SB_Q, SB_WIN), 0) % SB_Q
    strict = key_minus_query < SB_TAIL
    first = lane_q < HEAD_DIM
    outs = [[None] * n_blk for _ in range(2)]
    cmax = jnp.float32(-jnp.inf)
    for pair in range(2):
        q_cols = slice(pair * LANES, (pair + 1) * LANES)
        k_cols = slice(256 + pair * LANES, 256 + (pair + 1) * LANES)
        v_cols = slice(512 + pair * LANES, 512 + (pair + 1) * LANES)
        zs, lks = [], []
        for qb in range(n_blk):
            qs = hist_ref[SB_TAIL + qb * SB_Q:SB_TAIL + (qb + 1) * SB_Q, q_cols] * (HEAD_DIM ** -0.5)
            qm2 = jnp.concatenate([jnp.where(first, qs, jnp.zeros_like(qs)),
                                   jnp.where(first, jnp.zeros_like(qs), qs)], axis=0)
            keep = strict
            if qb * SB_Q < SB_TAIL:
                first_col = jnp.where(not_first, 0, SB_TAIL - qb * SB_Q)
                keep = strict & (col >= first_col)
            z, lk = _sb_scores(qm2, hist_ref[qb * SB_Q:qb * SB_Q + SB_WIN, k_cols], keep)
            zs.append(z)
            lks.append(lk)
        yield
        later_all = _dot(jnp.concatenate(lks, axis=0).astype(BF16), u_win)
        yield
        for qb in range(n_blk):
            later = later_all[qb * 2 * SB_Q:(qb + 1) * 2 * SB_Q]
            w = jnp.exp(zs[qb] + lks[qb] + later)
            o2 = _dot(w.astype(BF16), hist_ref[qb * SB_Q:qb * SB_Q + SB_WIN, v_cols])
            outs[pair][qb] = jnp.where(first, o2[:SB_Q], o2[SB_Q:])
            left = jnp.max(later[:, 0:1] + lks[qb][:, 0:1])
            has_older = True if qb * SB_Q > SB_TAIL else not_first
            cmax = jnp.maximum(cmax, jnp.where(has_older, left, -jnp.inf))
        yield
    result["y"] = jnp.concatenate([jnp.concatenate(outs[pair], axis=0) for pair in range(2)], axis=1)
    result["cmax"] = cmax


def _sb_kernel(q_ref, k_ref, v_ref, o_ref):
    seq = q_ref.shape[1]
    u_win = _later_key_matrix(SB_WIN)
    u_blk = _later_key_matrix(SB_Q)
    lane_q = lax.broadcasted_iota(jnp.int32, (SB_Q, LANES), 1)
    n_streams = 2 * SB_GROUP
    key_minus_query = (lax.broadcasted_iota(jnp.int32, (2 * SB_Q, SB_WIN), 1)
                       - lax.broadcasted_iota(jnp.int32, (2 * SB_Q, SB_WIN), 0) % SB_Q)

    def q_group(qg, carry):
        starts, qms, zs, lks, vws = [], [], [], [], []
        for sub in range(SB_GROUP):
            t0 = pl.multiple_of((qg * SB_GROUP + sub) * SB_Q, SB_Q)
            start = pl.multiple_of(jnp.maximum(t0 - (SB_WIN - SB_Q), 0), SB_Q)
            q = q_ref[0, pl.ds(t0, SB_Q), :] * (HEAD_DIM ** -0.5)
            kw = k_ref[0, pl.ds(start, SB_WIN), :]
            vws.append(v_ref[0, pl.ds(start, SB_WIN), :])
            strict = key_minus_query < (t0 - start)
            starts.append(start)
            qm2 = jnp.concatenate(
                [jnp.where((lane_q // HEAD_DIM) == head, q, jnp.zeros_like(q)) for head in range(2)], axis=0)
            z, lk = _sb_scores(qm2, kw, strict)
            qms += [qm2[:SB_Q], qm2[SB_Q:]]
            zs.append(z)
            lks.append(lk)
        lk_all = jnp.concatenate(lks, axis=0)
        later_all = _dot(lk_all.astype(BF16), u_win)
        csums, outs = [], []
        for sub in range(SB_GROUP):
            rows = slice(sub * 2 * SB_Q, (sub + 1) * 2 * SB_Q)
            later = later_all[rows]
            w = jnp.exp(zs[sub] + lks[sub] + later)
            o2 = _dot(w.astype(BF16), vws[sub])
            c2 = later[:, 0:1] + lks[sub][:, 0:1]
            outs += [o2[:SB_Q], o2[SB_Q:]]
            csums += [c2[:SB_Q], c2[SB_Q:]]

        def cond(state):
            step, cmaxes, _, _ = state
            flags = [jnp.logical_and(starts[s // 2] - (step + 1) * SB_Q >= 0, cmaxes[s] > SB_LOG_ZERO)
                     for s in range(n_streams)]
            return functools.reduce(jnp.logical_or, flags)

        def body(state):
            step, _, csums, outs = state
            new_c, new_o = [], []
            for s in range(n_streams):
                nxt = starts[s // 2] - (step + 1) * SB_Q
                in_range = nxt >= 0
                p0 = pl.multiple_of(jnp.maximum(nxt, 0), SB_Q)
                kb = k_ref[0, pl.ds(p0, SB_Q), :]
                vb = v_ref[0, pl.ds(p0, SB_Q), :]
                z, lk = _sb_scores(qms[s], kb)
                later = _dot(lk.astype(BF16), u_blk)
                w = jnp.where(in_range, jnp.exp(z + lk + later + csums[s]), 0.0)
                new_o.append(outs[s] + _dot(w.astype(BF16), vb))
                new_c.append(csums[s] + jnp.where(in_range, later[:, 0:1] + lk[:, 0:1], 0.0))
            return step + 1, tuple(jnp.max(c) for c in new_c), tuple(new_c), tuple(new_o)

        init = (jnp.int32(0), tuple(jnp.max(c) for c in csums), tuple(csums), tuple(outs))
        _, _, _, outs = lax.while_loop(cond, body, init)
        for sub in range(SB_GROUP):
            t0 = pl.multiple_of((qg * SB_GROUP + sub) * SB_Q, SB_Q)
            o_ref[0, pl.ds(t0, SB_Q), :] = jnp.where(
                lane_q < HEAD_DIM, outs[2 * sub], outs[2 * sub + 1]).astype(o_ref.dtype)
        return carry

    lax.fori_loop(0, seq // (SB_Q * SB_GROUP), q_group, 0)


def _stick_breaking(sb3):
    bsz, seq, _ = sb3.shape
    spec = lambda off: pl.BlockSpec((1, seq, LANES), lambda b, hp: (b, 0, off + hp))
    return pl.pallas_call(
        _sb_kernel,
        grid=(bsz, 2),
        in_specs=[spec(0), spec(2), spec(4)],
        out_specs=pl.BlockSpec((1, seq, LANES), lambda b, hp: (b, 0, hp)),
        out_shape=jax.ShapeDtypeStruct((bsz, seq, BRANCH_WIDTH), BF16),
        compiler_params=_cparams(("parallel", "parallel")),
        name="stick_breaking",
    )(sb3, sb3, sb3)


def _s5_kernel(u_ref, m_ref, e_ref, g_ref, pw_ref, d_ref, y_ref, state_ref):
    half = S5_GROUPS * S5_STATE

    @pl.when(pl.program_id(1) == 0)
    def _():
        state_ref[...] = jnp.zeros_like(state_ref)

    u = u_ref[0]
    ub = u.astype(BF16)
    rows = u.shape[0]

    def cmul(ar, ai, xr, xi):
        return ar * xr - ai * xi, ar * xi + ai * xr

    w = _dot(ub, e_ref[...])
    wr, wi = w[:, :half], w[:, half:]

    n_c = S5_GROUPS * S5_GROUP_CH
    slots = [ub[:, j * n_c:(j + 1) * n_c] for j in range(S5_CHUNK)]
    cols = []
    for i in range(S5_CHUNK):
        acc = _dot(slots[0], m_ref[i])
        for j in range(1, i + 1):
            acc = acc + _dot(slots[j], m_ref[i - j])
        cols.append(acc)
    y = jnp.concatenate(cols, axis=1)

    row = lax.broadcasted_iota(jnp.int32, (rows, LANES), 0)
    xrs, xis = [], []
    for t in range(half // LANES):
        re, im = slice(t * LANES, (t + 1) * LANES), slice(half + t * LANES, half + (t + 1) * LANES)
        tr, ti = wr[:, re], wi[:, re]
        sr, si = state_ref[0:1, re], state_ref[0:1, im]
        cr, ci = cmul(pw_ref[0:1, re], pw_ref[0:1, im], sr, si)
        tr = jnp.where(row == 0, tr + cr, tr)
        ti = jnp.where(row == 0, ti + ci, ti)
        step = 0
        d = 1
        while d < rows:
            pr = jnp.where(row >= d, pltpu.roll(tr, d, axis=0), 0.0)
            pi = jnp.where(row >= d, pltpu.roll(ti, d, axis=0), 0.0)
            mr, mi = cmul(pw_ref[step:step + 1, re], pw_ref[step:step + 1, im], pr, pi)
            tr, ti = tr + mr, ti + mi
            d *= 2
            step += 1
        xrs.append(jnp.where(row == 0, sr, pltpu.roll(tr, 1, axis=0)))
        xis.append(jnp.where(row == 0, si, pltpu.roll(ti, 1, axis=0)))
        state_ref[0:1, re] = tr[rows - 1:rows, :]
        state_ref[0:1, im] = ti[rows - 1:rows, :]
    xr, xi = jnp.concatenate(xrs, axis=1), jnp.concatenate(xis, axis=1)

    y = y + _dot(xr.astype(BF16), g_ref[:half, :]) + _dot(xi.astype(BF16), g_ref[half:, :])
    y = y + d_ref[...] * u
    y_ref[0] = jax.nn.gelu(y).astype(y_ref.dtype)


def _s5(u3, m, e, g, pw, dskip, layer):
    bsz, rows, width = u3.shape
    tr = min(S5_ROW_TILE, rows)
    blk = pl.BlockSpec((1, tr, width), lambda b, i: (b, i, 0))
    return pl.pallas_call(
        _s5_kernel,
        grid=(bsz, rows // tr),
        in_specs=[blk] + [_const_spec(a.shape, layer) for a in (m, e, g, pw, dskip)],
        out_specs=blk,
        out_shape=jax.ShapeDtypeStruct(u3.shape, BF16),
        scratch_shapes=[pltpu.VMEM((8, width), F32)],
        compiler_params=_cparams(("parallel", "arbitrary")),
        name="s5",
    )(u3, m, e, g, pw, dskip)


def _s5_operators(lam_re, lam_im, log_step, b_re, b_im, c_re, c_im, d_skip, row_tile):
    n_g, n_p, n_h, L = S5_GROUPS, S5_STATE, S5_GROUP_CH, S5_CHUNK
    n_c, n_s = n_g * n_h, n_g * n_p
    dt = jnp.exp(log_step)[:, None]

    def lam_bar_pow(k):
        mag = jnp.exp(k[:, None, None] * (lam_re * dt)[None])
        ang = k[:, None, None] * (lam_im * dt)[None]
        return mag * jnp.cos(ang), mag * jnp.sin(ang)

    pr, pi = lam_bar_pow(jnp.arange(L + 1, dtype=F32))
    den = lam_re * lam_re + lam_im * lam_im
    fr = ((pr[1] - 1.0) * lam_re + pi[1] * lam_im) / den
    fi = (pi[1] * lam_re - (pr[1] - 1.0) * lam_im) / den
    bbr = fr[..., None] * b_re - fi[..., None] * b_im
    bbi = fr[..., None] * b_im + fi[..., None] * b_re
    tr = pr[:L, :, :, None] * bbr[None] - pi[:L, :, :, None] * bbi[None]
    ti = pr[:L, :, :, None] * bbi[None] + pi[:L, :, :, None] * bbr[None]

    def expand(small, rows_per_group, cols_per_group):
        rows = small.shape[-2]
        src = lax.broadcasted_iota(jnp.int32, (cols_per_group, n_g * cols_per_group), 0)
        dst = lax.broadcasted_iota(jnp.int32, (cols_per_group, n_g * cols_per_group), 1) % cols_per_group
        spread = jnp.where(src == dst, 1.0, 0.0).astype(BF16)
        tiled = jnp.einsum('...rc,cq->...rq', small.astype(BF16), spread, preferred_element_type=F32)
        r = lax.broadcasted_iota(jnp.int32, (rows, n_g * cols_per_group), 0) // rows_per_group
        c = lax.broadcasted_iota(jnp.int32, (rows, n_g * cols_per_group), 1) // cols_per_group
        return jnp.where(r == c, tiled, 0.0)

    hp = lax.Precision.HIGHEST
    kern = (jnp.einsum('ghp,kgpi->kghi', c_re, tr, precision=hp)
            - jnp.einsum('ghp,kgpi->kghi', c_im, ti, precision=hp))
    m_op = expand(kern.transpose(0, 1, 3, 2).reshape(L, n_c, n_h), n_h, n_h).astype(BF16)

    e_r = expand(tr[::-1].transpose(0, 1, 3, 2).reshape(L, n_c, n_p), n_h, n_p)
    e_i = expand(ti[::-1].transpose(0, 1, 3, 2).reshape(L, n_c, n_p), n_h, n_p)
    e_op = jnp.concatenate([e_r, e_i], axis=2).astype(BF16).reshape(L * n_c, 2 * n_s)

    clr = c_re[None] * pr[1:, :, None, :] - c_im[None] * pi[1:, :, None, :]
    cli = c_re[None] * pi[1:, :, None, :] + c_im[None] * pr[1:, :, None, :]
    g_r = expand(clr.transpose(0, 1, 3, 2).reshape(L, n_s, n_h), n_p, n_h)
    g_i = expand(cli.transpose(0, 1, 3, 2).reshape(L, n_s, n_h), n_p, n_h)
    g_blocks = jnp.concatenate([g_r, -g_i], axis=1).astype(BF16)
    g_op = jnp.concatenate([g_blocks[i] for i in range(L)], axis=1)

    n_steps = max(1, int(math.log2(row_tile)))
    sr, si = lam_bar_pow(L * 2.0 ** jnp.arange(n_steps, dtype=F32))
    pw = jnp.concatenate([sr.reshape(n_steps, n_s), si.reshape(n_steps, n_s)], axis=1)
    pw = jnp.pad(pw, ((0, (-n_steps) % 8), (0, 0)))
    dskip = jnp.tile(d_skip.reshape(1, n_c), (1, L))
    return m_op, e_op, g_op, pw.astype(F32), dskip.astype(F32)


def _gla_constants():
    n = GLA_SUB
    ri = lax.broadcasted_iota(jnp.int32, (n, n), 0)
    ci = lax.broadcasted_iota(jnp.int32, (n, n), 1)
    same = (ri // GLA_BLOCK) == (ci // GLA_BLOCK)
    mats = [same & (ci <= ri), same & (ci > ri)]
    masks = [ri == ci]
    ups, lows = [], []
    for lev in range(GLA_LEVELS):
        half = 1 << lev
        blk = 2 * half
        mid = (ri // blk) * blk + (half - 1)
        later = (ri % blk) >= half
        ups.append(later & (ci > mid) & (ci <= ri))
        lows.append(jnp.logical_not(later) & (ci > ri) & (ci <= mid))
        masks.append(((ri // blk) == (ci // blk)) & later & ((ci % blk) < half))
    stacked = jnp.concatenate([jnp.where(m, 1.0, 0.0).astype(BF16) for m in mats + ups + lows], axis=0)
    return stacked, masks


def _gla_stages(x, gg, state, wgate, bgate, norm, stacked, masks, result):
    n_rows = x.shape[0]
    n_sub = n_rows // GLA_SUB
    hk = GLA_HEADS * GLA_DK
    hv = GLA_HEADS * GLA_DV
    q = x[:, 0:hk].astype(F32) * (GLA_DK ** -0.5)
    k = x[:, hk:2 * hk].astype(F32)
    vb = x[:, 2 * hk:2 * hk + hv]
    r = x[:, 2 * hk + hv:].astype(F32)
    sub = lambda t, st: t[st * GLA_SUB:(st + 1) * GLA_SUB]

    z = _dot(gg.astype(BF16), wgate) + bgate
    g = -(jnp.maximum(-z, 0.0) + jnp.log(1.0 + jnp.exp(-jnp.abs(z)))) * (1.0 / GLA_GATE_NORM)
    gb = g.astype(BF16)
    yield

    def decay_sums(st, blocks):
        picked = jnp.concatenate([stacked[i * GLA_SUB:(i + 1) * GLA_SUB] for i in blocks], axis=0)
        both = _dot(picked, sub(gb, st))
        return [both[n * GLA_SUB:(n + 1) * GLA_SUB] for n in range(len(blocks))]

    def level_blocks(lev):
        return 2 + lev, 2 + GLA_LEVELS + lev

    lane_k = lax.broadcasted_iota(jnp.int32, (GLA_SUB, hk), 1) // GLA_DK
    att = [[jnp.zeros((GLA_SUB, GLA_SUB), F32) for _ in range(GLA_HEADS)] for _ in range(n_sub)]
    ahead = [decay_sums(st, level_blocks(0)) for st in range(n_sub)]
    for lev in range(-1, GLA_LEVELS):
        now, ahead = ahead, None
        for st in range(n_sub):
            qs, ks = sub(q, st), sub(k, st)
            if lev < 0:
                qt, kt = qs.astype(BF16), ks.astype(BF16)
            else:
                qt = (qs * jnp.exp(now[st][0])).astype(BF16)
                kt = (ks * jnp.exp(now[st][1])).astype(BF16)
            q_heads = jnp.concatenate(
                [jnp.where(lane_k == h, qt, jnp.zeros_like(qt)) for h in range(GLA_HEADS)], axis=0)
            a = _dot_nt(q_heads, kt)
            for h in range(GLA_HEADS):
                att[st][h] = jnp.where(masks[lev + 1], a[h * GLA_SUB:(h + 1) * GLA_SUB], att[st][h])
        if lev < 0:
            ahead = now
        elif lev + 1 < GLA_LEVELS:
            ahead = [decay_sums(st, level_blocks(lev + 1)) for st in range(n_sub)]
        else:
            ahead = [decay_sums(st, (0, 1)) for st in range(n_sub)]
        yield
    part = lambda st, i: ahead[st][i]

    lane_v = lax.broadcasted_iota(jnp.int32, (GLA_SUB, hv), 1) // GLA_DV
    inner = []
    for st in range(n_sub):
        vs = sub(vb, st)
        att_all = jnp.concatenate([a.astype(BF16) for a in att[st]], axis=1)
        v_heads = jnp.concatenate([jnp.where(lane_v == h, vs, jnp.zeros_like(vs)) for h in range(GLA_HEADS)], axis=0)
        inner.append(_dot(att_all, v_heads))
    yield

    row_blk = lax.broadcasted_iota(jnp.int32, (GLA_SUB, hk), 0) // GLA_BLOCK
    rs = lax.broadcasted_iota(jnp.int32, (hv, hk), 0) // GLA_DV
    cs = lax.broadcasted_iota(jnp.int32, (hv, hk), 1) // GLA_DK
    head_diag = rs == cs
    n_blk = GLA_SUB // GLA_BLOCK
    outs = []
    for st in range(n_sub):
        b, tail = part(st, 0), part(st, 1)
        q_dec = (sub(q, st) * jnp.exp(b)).astype(BF16)
        k_dec = (sub(k, st) * jnp.exp(tail)).astype(BF16)
        b_end = b + tail
        k_blocks = jnp.concatenate(
            [jnp.where(row_blk == n, k_dec, jnp.zeros_like(k_dec)) for n in range(n_blk)], axis=1)
        upd_all = _dot_tn(sub(vb, st), k_blocks)
        cross = []
        for blk in range(n_blk):
            r0 = blk * GLA_BLOCK
            cross.append(_dot_nt(q_dec[r0:r0 + GLA_BLOCK], state.astype(BF16)))
            upd = upd_all[:, blk * hk:(blk + 1) * hk]
            state = jnp.exp(b_end[r0:r0 + 1, :]) * state + jnp.where(head_diag, upd, 0.0)
        outs.append(inner[st] + jnp.concatenate(cross, axis=0))
    yield

    o = _group_rms(jnp.concatenate(outs, axis=0), norm)
    result["y"] = o * (r * jax.nn.sigmoid(r))
    result["state"] = state


def _dsa_band_bias(offset, n_steps):
    row = lax.broadcasted_iota(jnp.int32, (2 * DSA_BLOCK, 2 * DSA_BLOCK), 0) % DSA_BLOCK
    col = lax.broadcasted_iota(jnp.int32, (2 * DSA_BLOCK, 2 * DSA_BLOCK), 1)
    dist = offset + row - col
    return jnp.where((dist >= 0) & (dist <= n_steps), 0.0, NEG_BIG)


def _dsa_logits(q, kw, bias):
    lane_q = lax.broadcasted_iota(jnp.int32, (DSA_BLOCK, LANES), 1)
    q = q * (HEAD_DIM ** -0.5)
    qm2 = jnp.concatenate(
        [jnp.where((lane_q // HEAD_DIM) == head, q, jnp.zeros_like(q)) for head in range(2)], axis=0)
    return _dot_nt(qm2, kw) + bias


def _dsa_softmax_pv(s, vw, shift):
    if shift is None:
        m = jnp.max(s, axis=1, keepdims=True)
        p = jnp.exp(s - m)
        l = jnp.sum(p, axis=1, keepdims=True)
        o2 = _dot(p.astype(BF16), vw) / l
        lse2 = jnp.broadcast_to(m + jnp.log(l), (2 * DSA_BLOCK, LANES))
    else:
        p = jnp.exp(s)
        pv = _dot(p.astype(BF16), jnp.concatenate([vw, jnp.ones_like(vw)], axis=1))
        l = pv[:, LANES:]
        o2 = pv[:, :LANES] / l
        lse2 = shift + jnp.log(l)
    first = lax.broadcasted_iota(jnp.int32, (DSA_BLOCK, LANES), 1) < HEAD_DIM
    return (jnp.where(first, o2[:DSA_BLOCK], o2[DSA_BLOCK:]),
            jnp.where(first, lse2[:DSA_BLOCK], lse2[DSA_BLOCK:]))


def _dsa_kernel(bound_ref, q1_ref, k1_ref, v1_ref, q4_ref, k4_ref, v4_ref, q16_ref, k16_ref, v16_ref,
                y_ref, o_scr, l_scr):
    seq = q1_ref.shape[1]
    refs = ((q1_ref, k1_ref, v1_ref), (q4_ref, k4_ref, v4_ref), (q16_ref, k16_ref, v16_ref))
    blocks_per_span = DSA_SPAN // DSA_BLOCK
    n_steps = DSA_PATTERNS[0][0] // DSA_PATTERNS[0][1]
    assert all(w // d == n_steps for w, d in DSA_PATTERNS)
    bias_inner = _dsa_band_bias(DSA_BLOCK, n_steps)
    bias_first = _dsa_band_bias(0, n_steps)

    def run(shift):
        inner, first = (bias_inner, bias_first) if shift is None else (bias_inner - shift, bias_first - shift)

        def span(sp, carry):
            for p, (_, dil) in enumerate(DSA_PATTERNS):
                q_ref, k_ref, v_ref = refs[p]
                per_res = blocks_per_span // dil

                def group(gi, c, p=p, dil=dil, per_res=per_res, q_ref=q_ref, k_ref=k_ref, v_ref=v_ref):
                    units = []
                    for u in range(DSA_GROUP):
                        idx = gi * DSA_GROUP + u
                        r = idx // per_res
                        jb = idx % per_res
                        m0 = pl.multiple_of(sp * (DSA_SPAN // dil) + jb * DSA_BLOCK, DSA_BLOCK)
                        start = pl.multiple_of(jnp.maximum(m0 - DSA_BLOCK, 0), DSA_BLOCK)
                        if dil == 1:
                            q = q_ref[0, pl.ds(m0, DSA_BLOCK), :]
                            kw = k_ref[0, pl.ds(start, 2 * DSA_BLOCK), :]
                            vw = v_ref[0, pl.ds(start, 2 * DSA_BLOCK), :]
                        else:
                            q = q_ref[0, r, pl.ds(m0, DSA_BLOCK), :]
                            kw = k_ref[0, r, pl.ds(start, 2 * DSA_BLOCK), :]
                            vw = v_ref[0, r, pl.ds(start, 2 * DSA_BLOCK), :]
                        may_start_sequence = u % per_res == 0
                        s = _dsa_logits(q, kw, jnp.where(m0 == 0, first, inner) if may_start_sequence else inner)
                        rows = pl.ds(r + dil * jb * DSA_BLOCK, DSA_BLOCK, stride=dil) if dil > 1 else \
                            pl.ds(pl.multiple_of(jb * DSA_BLOCK, DSA_BLOCK), DSA_BLOCK)
                        units.append((s, vw, rows))
                    results = [_dsa_softmax_pv(s, vw, shift) for s, vw, _ in units]
                    for (o, lse), (_, _, rows) in zip(results, units):
                        o_scr[p, rows, :] = o
                        l_scr[p, rows, :] = lse
                    return c

                lax.fori_loop(0, blocks_per_span // DSA_GROUP, group, 0)

            l1, l2, l3 = l_scr[0], l_scr[1], l_scr[2]
            lm = jnp.maximum(jnp.maximum(l1, l2), l3)
            e1, e2, e3 = jnp.exp(l1 - lm), jnp.exp(l2 - lm), jnp.exp(l3 - lm)
            y = (e1 * o_scr[0] + e2 * o_scr[1] + e3 * o_scr[2]) / (e1 + e2 + e3)
            y_ref[0, pl.ds(pl.multiple_of(sp * DSA_SPAN, DSA_SPAN), DSA_SPAN), :] = y.astype(y_ref.dtype)
            return carry

        lax.fori_loop(0, seq // DSA_SPAN, span, 0)

    bound = bound_ref[0]

    @pl.when(bound <= DSA_MAX_SHIFT)
    def _():
        run(bound)

    @pl.when(bound > DSA_MAX_SHIFT)
    def _():
        run(None)


def _dsa(dsa1, dsa4, dsa16, bound):
    bsz, seq, _ = dsa1.shape
    nat = lambda off: pl.BlockSpec((1, seq, LANES), lambda b, hp: (b, 0, off + hp))
    res = lambda dil, off: pl.BlockSpec((1, dil, seq // dil, LANES), lambda b, hp: (b, 0, 0, off + hp))
    return pl.pallas_call(
        _dsa_kernel,
        grid=(bsz, 2),
        in_specs=[pl.BlockSpec(memory_space=pltpu.SMEM),
                  nat(0), nat(2), nat(4), res(4, 0), res(4, 2), res(4, 4), res(16, 0), res(16, 2), res(16, 4)],
        out_specs=pl.BlockSpec((1, seq, LANES), lambda b, hp: (b, 0, hp)),
        out_shape=jax.ShapeDtypeStruct((bsz, seq, BRANCH_WIDTH), BF16),
        scratch_shapes=[pltpu.VMEM((3, DSA_SPAN, LANES), F32), pltpu.VMEM((3, DSA_SPAN, LANES), F32)],
        compiler_params=_cparams(("parallel", "parallel")),
        name="dsa",
    )(bound, dsa1, dsa1, dsa1, dsa4, dsa4, dsa4, dsa16, dsa16, dsa16)


def _proj_weights(w_in):
    return jnp.swapaxes(w_in, 1, 2).astype(BF16)


def _rope_tables(seq):
    inv = ROPE_THETA ** (-jnp.arange(0, HEAD_DIM, 2, dtype=F32) / HEAD_DIM)
    ang = jnp.arange(seq, dtype=F32)[:, None] * inv[None, :]
    cos, sin = jnp.cos(ang), jnp.sin(ang)
    cos_t = jnp.tile(jnp.concatenate([cos, cos], axis=1), (1, 2))
    sin_t = jnp.tile(jnp.concatenate([-sin, sin], axis=1), (1, 2))
    return cos_t, sin_t


def _pick_tile(n, pref):
    t = min(pref, n)
    while n % t:
        t //= 2
    return t


def kernel(x, ffn1_norm, ffn1_w_in, ffn1_w_out, mix_norm, w_in, s5_lam_re, s5_lam_im, s5_log_step, s5_b_re, s5_b_im, s5_c_re, s5_c_im, s5_d, s5_w_glu, gla_w_gate, gla_b_gate, gla_norm, dsa_q_norm, dsa_k_norm, w_branch, w_out, ffn2_norm, ffn2_w_in, ffn2_w_out):
    bsz, seq, _ = x.shape
    tokens = bsz * seq
    depth = ffn1_norm.shape[0]
    assert seq % DSA_SPAN == 0 and seq // DSA_PATTERNS[-1][1] >= 2 * DSA_BLOCK and seq >= SB_WIN
    tm_a = _pick_tile(seq, 512)
    tm_b = _pick_tile(tokens, 512)
    s5_rows = seq // S5_CHUNK
    s5_tile = min(S5_ROW_TILE, s5_rows)
    cos_t, sin_t = _rope_tables(seq)

    wp_all = _proj_weights(w_in)
    w1a_all, w2a_all = ffn1_w_in.astype(BF16), ffn1_w_out.astype(BF16)
    w1b_all, w2b_all = ffn2_w_in.astype(BF16), ffn2_w_out.astype(BF16)
    wglu_all, wb_all, wo_all = s5_w_glu.astype(BF16), w_branch.astype(BF16), w_out.astype(BF16)
    s5_ops = jax.vmap(functools.partial(_s5_operators, row_tile=s5_tile))(
        s5_lam_re, s5_lam_im, s5_log_step, s5_b_re, s5_b_im, s5_c_re, s5_c_im, s5_d)

    h = x.reshape(tokens, D_MODEL)
    for l in range(depth):
        row = lambda p: p[l].reshape(1, -1)
        qn = jnp.tile(row(dsa_q_norm), (1, 4))
        kn = jnp.tile(row(dsa_k_norm), (1, 4))
        wgate = jnp.pad(gla_w_gate[l], ((0, LANES - GLA_GATE_RANK), (0, 0))).astype(BF16)
        h, sb, s5u, y_gla, y_sb_window, sb_left, dsa1, dsa4, dsa16 = _ffn_proj(
            h, row(ffn1_norm), w1a_all, w2a_all, row(mix_norm), wp_all, qn, kn, cos_t, sin_t,
            wgate, row(gla_b_gate), row(gla_norm), bsz, seq, tm_a, l)

        y_sb = lax.cond(jnp.max(sb_left) > SB_LOG_ZERO,
                        lambda: _stick_breaking(sb.reshape(bsz, seq, 768)).reshape(tokens, BRANCH_WIDTH),
                        lambda: y_sb_window)

        y_s5 = _s5(s5u.reshape(bsz, s5_rows, S5_CHUNK * 256), *s5_ops, l).reshape(tokens // S5_CHUNK, S5_CHUNK * 256)

        bound = (1.02 * HEAD_DIM ** 0.5 * jnp.max(jnp.abs(dsa_q_norm[l])) * jnp.max(jnp.abs(dsa_k_norm[l]))).reshape(1)
        y_dsa = _dsa(dsa1.reshape(bsz, seq, 768), dsa4, dsa16, bound).reshape(tokens, BRANCH_WIDTH)

        h = _merge_ffn(h, y_sb, y_s5, y_gla, y_dsa, row(mix_norm), wp_all, wglu_all, wb_all, wo_all,
                       row(ffn2_norm), w1b_all, w2b_all, tm_b, l)
    return h.reshape(bsz, seq, D_MODEL)
```

```python
import functools
import math

import jax
import jax.numpy as jnp
from jax import lax
from jax.experimental import pallas as pl
from jax.experimental.pallas import tpu as pltpu

D_MODEL = 1024
HEAD_DIM = 64
BRANCH_WIDTH = 256
N_BRANCH = 4
S5_GROUPS = 16
S5_GROUP_CH = 16
S5_STATE = 64
GLA_HEADS = 4
GLA_DK = 32
GLA_DV = 64
GLA_GATE_RANK = 16
GLA_GATE_NORM = 16.0
DSA_PATTERNS = ((128, 1), (512, 4), (2048, 16))
DSA_BLOCK = 128
ROPE_THETA = 10000.0
D_FF = 2816
RMS_EPS = 1e-6

LANES = 128
VMEM_LIMIT_BYTES = 58 * 1024 * 1024
FF_CHUNK = 256
N_FF_CHUNKS = D_FF // FF_CHUNK
S5_CHUNK = 8
S5_ROW_TILE = 256
GLA_BLOCK = 64
GLA_LEVELS = 6
GLA_SUB = 128
SB_Q = 64
SB_WIN = 4 * SB_Q
SB_GROUP = 8
SB_TAIL = SB_WIN - SB_Q
SB_LOG_ZERO = -104.0
NEG_BIG = -1e30
DSA_GROUP = 16
DSA_SPAN = DSA_PATTERNS[-1][1] * DSA_BLOCK
DSA_MAX_SHIFT = 30.0

PROJ_SB = (0, 768)
PROJ_S5 = (768, 1024)
PROJ_GLA = (1024, 1792)
PROJ_GG = (PROJ_GLA[1], PROJ_GLA[1] + LANES)
PROJ_DSA = (PROJ_GLA[1] + GLA_GATE_RANK, PROJ_GLA[1] + GLA_GATE_RANK + 768)
PROJ_ROWS = PROJ_DSA[1]
PROJ_GATES = PROJ_ROWS

F32 = jnp.float32
BF16 = jnp.bfloat16


def _cparams(sem):
    return pltpu.CompilerParams(dimension_semantics=sem, vmem_limit_bytes=VMEM_LIMIT_BYTES)


def _const_spec(shape, layer=None):
    if layer is None:
        zeros = (0,) * len(shape)
        return pl.BlockSpec(shape, lambda *_: zeros, pipeline_mode=pl.Buffered(1))
    index = (layer,) + (0,) * (len(shape) - 1)
    return pl.BlockSpec((None,) + tuple(shape[1:]), lambda *_: index, pipeline_mode=pl.Buffered(1))


def _rms_rows(x, g):
    ms = jnp.mean(x * x, axis=-1, keepdims=True)
    return x * lax.rsqrt(ms + RMS_EPS) * g


def _dot(a, b):
    return jnp.dot(a, b, preferred_element_type=F32)


def _dot_nt(a, b):
    return lax.dot_general(a, b, (((1,), (1,)), ((), ())), preferred_element_type=F32)


def _dot_tn(a, b):
    return lax.dot_general(a, b, (((0,), (0,)), ((), ())), preferred_element_type=F32)


def _round_robin(*stage_generators):
    live = list(stage_generators)
    while live:
        live = [g for g in live if next(g, StopIteration) is not StopIteration]
        yield


def _swiglu_into(acc_ref, xb, w1_ref, w2_ref, between=None):
    for c in range(N_FF_CHUNKS):
        lo, hi = c * FF_CHUNK, (c + 1) * FF_CHUNK
        a = _dot(xb, w1_ref[:, lo:hi])
        b = _dot(xb, w1_ref[:, D_FF + lo:D_FF + hi])
        hm = (a * jax.nn.sigmoid(a) * b).astype(BF16)
        part = _dot(hm, w2_ref[lo:hi, :])
        if c == 0:
            acc_ref[...] = part
        else:
            acc_ref[...] += part
        if between is not None:
            next(between, None)


def _group_mean_matrix(width, group):
    r = lax.broadcasted_iota(jnp.int32, (width, width), 0) // group
    c = lax.broadcasted_iota(jnp.int32, (width, width), 1) // group
    return jnp.where(r == c, 1.0 / group, 0.0).astype(BF16)


def _group_rms(x, gain):
    gm = _group_mean_matrix(x.shape[-1], HEAD_DIM)
    ms = _dot((x * x).astype(BF16), gm)
    return x * lax.rsqrt(ms + RMS_EPS) * gain


def _swap_half_heads(x):
    half = HEAD_DIM // 2
    outs = []
    for s in range(x.shape[-1] // LANES):
        xs = x[:, s * LANES:(s + 1) * LANES]
        lane = lax.broadcasted_iota(jnp.int32, xs.shape, 1)
        up = pltpu.roll(xs, LANES - half, axis=1)
        down = pltpu.roll(xs, half, axis=1)
        outs.append(jnp.where((lane % HEAD_DIM) < half, up, down))
    return jnp.concatenate(outs, axis=-1)


def _ffn_proj_tile(x_ref, n1_ref, w1_ref, w2_ref, nm_ref, wp_ref, qn_ref, kn_ref, cos_ref, sin_ref,
                   h_ref, sb_ref, s5_ref, dsa1_ref, dsa4_ref, dsa16_ref, acc_ref, slab_ref, glax_ref, glag_ref,
                   sbh_ref, between, after_ffn):
    tm = x_ref.shape[0]
    x = x_ref[...]
    xb = _rms_rows(x, n1_ref[...]).astype(BF16)
    _swiglu_into(acc_ref, xb, w1_ref, w2_ref, between=between)
    after_ffn()
    h = x + 0.5 * acc_ref[...]
    h_ref[...] = h
    ub = _rms_rows(h, nm_ref[...]).astype(BF16)

    dsa = _dot_nt(ub, wp_ref[PROJ_DSA[0]:PROJ_DSA[1], :])
    cos = jnp.concatenate([cos_ref[...]] * 2, axis=-1)
    sin = jnp.concatenate([sin_ref[...]] * 2, axis=-1)
    parts = []
    for i, gain_ref in enumerate((qn_ref, kn_ref)):
        t = _group_rms(dsa[:, i * 256:(i + 1) * 256], gain_ref[...])
        parts.append(t * cos + _swap_half_heads(t) * sin)
    parts.append(dsa[:, 512:768])
    dsa = jnp.concatenate(parts, axis=-1)
    dsa1_ref[...] = dsa.astype(BF16)
    for s in range(6):
        slab_ref[s] = dsa[:, s * LANES:(s + 1) * LANES]
    for dil, ref in ((4, dsa4_ref), (16, dsa16_ref)):
        for r in range(dil):
            for s in range(6):
                rows = slab_ref[s, pl.ds(r, tm // dil, stride=dil), :]
                ref[0, r, :, s * LANES:(s + 1) * LANES] = rows.astype(BF16)

    s5 = _dot_nt(ub, wp_ref[PROJ_S5[0]:PROJ_S5[1], :])
    for s in range(2):
        slab_ref[6 + s] = s5[:, s * LANES:(s + 1) * LANES]
    for j in range(S5_CHUNK):
        for s in range(2):
            col = j * 256 + s * LANES
            s5_ref[:, col:col + LANES] = slab_ref[6 + s, pl.ds(j, tm // S5_CHUNK, stride=S5_CHUNK), :]

    sb = _dot_nt(ub, wp_ref[PROJ_SB[0]:PROJ_SB[1], :]).astype(BF16)
    sb_ref[...] = sb
    tail = sbh_ref[tm:tm + SB_TAIL, :]
    sbh_ref[0:SB_TAIL, :] = tail
    sbh_ref[SB_TAIL:, :] = sb
    glax_ref[...] = _dot_nt(ub, wp_ref[PROJ_GLA[0]:PROJ_GLA[1], :]).astype(BF16)
    gg = _dot_nt(ub, wp_ref[PROJ_GG[0]:PROJ_GG[1], :])
    lane = lax.broadcasted_iota(jnp.int32, gg.shape, 1)
    glag_ref[...] = jnp.where(lane < GLA_GATE_RANK, gg, 0.0)


def _ffn_proj_kernel(x_ref, n1_ref, w1_ref, w2_ref, nm_ref, wp_ref, qn_ref, kn_ref, cos_ref, sin_ref,
                     wgate_ref, bgate_ref, gnorm_ref,
                     h_ref, sb_ref, s5_ref, ygla_ref, ysb_ref, sbf_ref, dsa1_ref, dsa4_ref, dsa16_ref,
                     acc_ref, slab_ref, glax_ref, glag_ref, glas_ref, sbh_ref, *, seq_tiles):
    step = pl.program_id(0)
    n_tiles = pl.num_programs(0) - 1

    @pl.when(step == 0)
    def _():
        glax_ref[...] = jnp.zeros_like(glax_ref)
        glag_ref[...] = jnp.zeros_like(glag_ref)
        glas_ref[...] = jnp.zeros_like(glas_ref)
        sbh_ref[...] = jnp.zeros_like(sbh_ref)

    def start_mixers():
        starts_sequence = lax.rem(step - 1, seq_tiles) == 0
        state = jnp.where(starts_sequence, 0.0, glas_ref[...])
        stacked, masks = _gla_constants()
        gla, sbw = {}, {}
        stages = _round_robin(
            _gla_stages(glax_ref[...], glag_ref[...], state, wgate_ref[...], bgate_ref[...], gnorm_ref[...],
                        stacked, masks, gla),
            _sb_window_stages(sbh_ref, jnp.logical_not(starts_sequence), sbw))
        return stages, gla, sbw

    def finish_mixers(stages, gla, sbw):
        for _ in stages:
            pass
        ygla_ref[...] = gla["y"].astype(ygla_ref.dtype)
        glas_ref[...] = gla["state"]
        ysb_ref[...] = sbw["y"].astype(ysb_ref.dtype)
        sbf_ref[...] = jnp.full(sbf_ref.shape, sbw["cmax"], F32)

    @pl.when(step < n_tiles)
    def _():
        stages, gla, sbw = start_mixers()
        _ffn_proj_tile(x_ref, n1_ref, w1_ref, w2_ref, nm_ref, wp_ref, qn_ref, kn_ref, cos_ref, sin_ref,
                       h_ref, sb_ref, s5_ref, dsa1_ref, dsa4_ref, dsa16_ref, acc_ref, slab_ref, glax_ref, glag_ref,
                       sbh_ref, between=stages, after_ffn=lambda: finish_mixers(stages, gla, sbw))

    @pl.when(step == n_tiles)
    def _():
        finish_mixers(*start_mixers())


def _ffn_proj(x2, n1, w1, w2, nm, wp, qn, kn, cos_t, sin_t, wgate, bgate, gnorm, bsz, seq, tm, layer):
    tokens = x2.shape[0]
    seq_tiles = seq // tm
    n_tiles = tokens // tm
    cur = lambda i: jnp.minimum(i, n_tiles - 1)
    row = lambda i: (cur(i), 0)
    lag = lambda i: (jnp.maximum(i - 1, 0), 0)
    tab = lambda i: (cur(i) % seq_tiles, 0)
    res = lambda i: (cur(i) // seq_tiles, 0, cur(i) % seq_tiles, 0)
    tile = lambda w: pl.BlockSpec((tm, w), row)
    s5_w = S5_CHUNK * 256
    return pl.pallas_call(
        functools.partial(_ffn_proj_kernel, seq_tiles=seq_tiles),
        grid=(n_tiles + 1,),
        in_specs=[tile(D_MODEL), _const_spec(n1.shape), _const_spec(w1.shape, layer), _const_spec(w2.shape, layer),
                  _const_spec(nm.shape), _const_spec((wp.shape[0], PROJ_ROWS, D_MODEL), layer),
                  _const_spec(qn.shape), _const_spec(kn.shape),
                  pl.BlockSpec((tm, LANES), tab), pl.BlockSpec((tm, LANES), tab),
                  _const_spec(wgate.shape), _const_spec(bgate.shape), _const_spec(gnorm.shape)],
        out_specs=[tile(D_MODEL), tile(768), pl.BlockSpec((tm // S5_CHUNK, s5_w), row),
                   pl.BlockSpec((tm, BRANCH_WIDTH), lag), pl.BlockSpec((tm, BRANCH_WIDTH), lag),
                   pl.BlockSpec((8, LANES), lag),
                   tile(768), pl.BlockSpec((1, 4, tm // 4, 768), res), pl.BlockSpec((1, 16, tm // 16, 768), res)],
        out_shape=[jax.ShapeDtypeStruct((tokens, D_MODEL), F32),
                   jax.ShapeDtypeStruct((tokens, 768), BF16),
                   jax.ShapeDtypeStruct((tokens // S5_CHUNK, s5_w), F32),
                   jax.ShapeDtypeStruct((tokens, BRANCH_WIDTH), BF16),
                   jax.ShapeDtypeStruct((tokens, BRANCH_WIDTH), BF16),
                   jax.ShapeDtypeStruct((n_tiles * 8, LANES), F32),
                   jax.ShapeDtypeStruct((tokens, 768), BF16),
                   jax.ShapeDtypeStruct((bsz, 4, seq // 4, 768), BF16),
                   jax.ShapeDtypeStruct((bsz, 16, seq // 16, 768), BF16)],
        scratch_shapes=[pltpu.VMEM((tm, D_MODEL), F32), pltpu.VMEM((8, tm, LANES), F32),
                        pltpu.VMEM((tm, 768), BF16), pltpu.VMEM((tm, LANES), F32),
                        pltpu.VMEM((GLA_HEADS * GLA_DV, GLA_HEADS * GLA_DK), F32),
                        pltpu.VMEM((SB_TAIL + tm, 768), BF16)],
        compiler_params=_cparams(("arbitrary",)),
        name="ffn_proj",
    )(x2, n1, w1, w2, nm, wp, qn, kn, cos_t, sin_t, wgate, bgate, gnorm)


def _merge_ffn_kernel(h_ref, ysb_ref, ys5_ref, ygla_ref, ydsa_ref,
                      nm_ref, wg_ref, wglu_ref, wb_ref, wo_ref, n2_ref, w1_ref, w2_ref,
                      out_ref, acc_ref, slab_ref):
    tm = h_ref.shape[0]
    proj = {n: _dot(ref[...], wb_ref[n]) for n, ref in ((0, ysb_ref), (2, ygla_ref), (3, ydsa_ref))}

    for j in range(S5_CHUNK):
        for s in range(2):
            col = j * 256 + s * LANES
            slab_ref[s, pl.ds(j, tm // S5_CHUNK, stride=S5_CHUNK), :] = ys5_ref[:, col:col + LANES].astype(F32)
    y_s5 = jnp.concatenate([slab_ref[0], slab_ref[1]], axis=-1).astype(BF16)
    glu = _dot(y_s5, wglu_ref[...])
    y_s5 = glu[:, :BRANCH_WIDTH] * jax.nn.sigmoid(glu[:, BRANCH_WIDTH:])
    proj[1] = _dot(y_s5.astype(BF16), wb_ref[1])

    h = h_ref[...]
    ub = _rms_rows(h, nm_ref[...]).astype(BF16)
    mixed = jnp.zeros(h.shape, F32)
    for n in range(N_BRANCH):
        rows = PROJ_GATES + n * D_MODEL
        gate = jax.nn.sigmoid(_dot_nt(ub, wg_ref[rows:rows + D_MODEL, :]))
        mixed = mixed + gate * proj[n]
    h2 = h + _dot(mixed.astype(BF16), wo_ref[...])

    xb = _rms_rows(h2, n2_ref[...]).astype(BF16)
    _swiglu_into(acc_ref, xb, w1_ref, w2_ref)
    out_ref[...] = h2 + 0.5 * acc_ref[...]


def _merge_ffn(h, ysb, ys5, ygla, ydsa, nm, wg, wglu, wb, wo, n2, w1, w2, tm, layer):
    tokens = h.shape[0]
    row = lambda i: (i, 0)
    tile = lambda w: pl.BlockSpec((tm, w), row)
    consts = (nm, wg, wglu, wb, wo, n2, w1, w2)
    return pl.pallas_call(
        _merge_ffn_kernel,
        grid=(tokens // tm,),
        in_specs=[tile(D_MODEL), tile(BRANCH_WIDTH), pl.BlockSpec((tm // S5_CHUNK, S5_CHUNK * 256), row),
                  tile(BRANCH_WIDTH), tile(BRANCH_WIDTH)]
                 + [_const_spec(c.shape, None if c.ndim == 2 and c.shape[0] == 1 else layer) for c in consts],
        out_specs=tile(D_MODEL),
        out_shape=jax.ShapeDtypeStruct((tokens, D_MODEL), F32),
        scratch_shapes=[pltpu.VMEM((tm, D_MODEL), F32), pltpu.VMEM((2, tm, LANES), F32)],
        compiler_params=_cparams(("parallel",)),
        name="merge_ffn",
    )(h, ysb, ys5, ygla, ydsa, *consts)


def _later_key_matrix(n):
    j = lax.broadcasted_iota(jnp.int32, (n, n), 0)
    s = lax.broadcasted_iota(jnp.int32, (n, n), 1)
    return jnp.where(j > s, 1.0, 0.0).astype(BF16)


def _sb_scores(qm, k, keep=None):
    z = _dot_nt(qm, k)
    if keep is not None:
        z = jnp.where(keep, z, NEG_BIG)
    lk = -(jnp.maximum(z, 0.0) + jnp.log(1.0 + jnp.exp(-jnp.abs(z))))
    return z, lk


def _sb_window_stages(hist_ref, not_first, result):
    rows = hist_ref.shape[0] - SB_TAIL
    n_blk = rows // SB_Q
    u_win = _later_key_matrix(SB_WIN)
    lane_q = lax.broadcasted_iota(jnp.int32, (SB_Q, LANES), 1)
    col = lax.broadcasted_iota(jnp.int32, (2 * SB_Q, SB_WIN), 1)
    key_minus_query = col - lax.broadcasted_iota(jnp.int32, (2 * SB_Q, SB_WIN), 0) % SB_Q
    strict = key_minus_query < SB_TAIL
    first = lane_q < HEAD_DIM
    outs = [[None] * n_blk for _ in range(2)]
    cmax = jnp.float32(-jnp.inf)
    for pair in range(2):
        q_cols = slice(pair * LANES, (pair + 1) * LANES)
        k_cols = slice(256 + pair * LANES, 256 + (pair + 1) * LANES)
        v_cols = slice(512 + pair * LANES, 512 + (pair + 1) * LANES)
        zs, lks = [], []
        for qb in range(n_blk):
            qs = hist_ref[SB_TAIL + qb * SB_Q:SB_TAIL + (qb + 1) * SB_Q, q_cols] * (HEAD_DIM ** -0.5)
            qm2 = jnp.concatenate([jnp.where(first, qs, jnp.zeros_like(qs)),
                                   jnp.where(first, jnp.zeros_like(qs), qs)], axis=0)
            keep = strict
            if qb * SB_Q < SB_TAIL:
                first_col = jnp.where(not_first, 0, SB_TAIL - qb * SB_Q)
                keep = strict & (col >= first_col)
            z, lk = _sb_scores(qm2, hist_ref[qb * SB_Q:qb * SB_Q + SB_WIN, k_cols], keep)
            zs.append(z)
            lks.append(lk)
        yield
        later_all = _dot(jnp.concatenate(lks, axis=0).astype(BF16), u_win)
        yield
        for qb in range(n_blk):
            later = later_all[qb * 2 * SB_Q:(qb + 1) * 2 * SB_Q]
            w = jnp.exp(zs[qb] + lks[qb] + later)
            o2 = _dot(w.astype(BF16), hist_ref[qb * SB_Q:qb * SB_Q + SB_WIN, v_cols])
            outs[pair][qb] = jnp.where(first, o2[:SB_Q], o2[SB_Q:])
            left = jnp.max(later[:, 0:1] + lks[qb][:, 0:1])
            has_older = True if qb * SB_Q > SB_TAIL else not_first
            cmax = jnp.maximum(cmax, jnp.where(has_older, left, -jnp.inf))
        yield
    result["y"] = jnp.concatenate([jnp.concatenate(outs[pair], axis=0) for pair in range(2)], axis=1)
    result["cmax"] = cmax


def _sb_kernel(q_ref, k_ref, v_ref, o_ref):
    seq = q_ref.shape[1]
    u_win = _later_key_matrix(SB_WIN)
    u_blk = _later_key_matrix(SB_Q)
    lane_q = lax.broadcasted_iota(jnp.int32, (SB_Q, LANES), 1)
    n_streams = 2 * SB_GROUP
    key_minus_query = (lax.broadcasted_iota(jnp.int32, (2 * SB_Q, SB_WIN), 1)
                       - lax.broadcasted_iota(jnp.int32, (2 * SB_Q, SB_WIN), 0) % SB_Q)

    def q_group(qg, carry):
        starts, qms, zs, lks, vws = [], [], [], [], []
        for sub in range(SB_GROUP):
            t0 = pl.multiple_of((qg * SB_GROUP + sub) * SB_Q, SB_Q)
            start = pl.multiple_of(jnp.maximum(t0 - (SB_WIN - SB_Q), 0), SB_Q)
            q = q_ref[0, pl.ds(t0, SB_Q), :] * (HEAD_DIM ** -0.5)
            kw = k_ref[0, pl.ds(start, SB_WIN), :]
            vws.append(v_ref[0, pl.ds(start, SB_WIN), :])
            strict = key_minus_query < (t0 - start)
            starts.append(start)
            qm2 = jnp.concatenate(
                [jnp.where((lane_q // HEAD_DIM) == head, q, jnp.zeros_like(q)) for head in range(2)], axis=0)
            z, lk = _sb_scores(qm2, kw, strict)
            qms += [qm2[:SB_Q], qm2[SB_Q:]]
            zs.append(z)
            lks.append(lk)
        lk_all = jnp.concatenate(lks, axis=0)
        later_all = _dot(lk_all.astype(BF16), u_win)
        csums, outs = [], []
        for sub in range(SB_GROUP):
            rows = slice(sub * 2 * SB_Q, (sub + 1) * 2 * SB_Q)
            later = later_all[rows]
            w = jnp.exp(zs[sub] + lks[sub] + later)
            o2 = _dot(w.astype(BF16), vws[sub])
            c2 = later[:, 0:1] + lks[sub][:, 0:1]
            outs += [o2[:SB_Q], o2[SB_Q:]]
            csums += [c2[:SB_Q], c2[SB_Q:]]

        def cond(state):
            step, cmaxes, _, _ = state
            flags = [jnp.logical_and(starts[s // 2] - (step + 1) * SB_Q >= 0, cmaxes[s] > SB_LOG_ZERO)
                     for s in range(n_streams)]
            return functools.reduce(jnp.logical_or, flags)

        def body(state):
            step, _, csums, outs = state
            new_c, new_o = [], []
            for s in range(n_streams):
                nxt = starts[s // 2] - (step + 1) * SB_Q
                in_range = nxt >= 0
                p0 = pl.multiple_of(jnp.maximum(nxt, 0), SB_Q)
                kb = k_ref[0, pl.ds(p0, SB_Q), :]
                vb = v_ref[0, pl.ds(p0, SB_Q), :]
                z, lk = _sb_scores(qms[s], kb)
                later = _dot(lk.astype(BF16), u_blk)
                w = jnp.where(in_range, jnp.exp(z + lk + later + csums[s]), 0.0)
                new_o.append(outs[s] + _dot(w.astype(BF16), vb))
                new_c.append(csums[s] + jnp.where(in_range, later[:, 0:1] + lk[:, 0:1], 0.0))
            return step + 1, tuple(jnp.max(c) for c in new_c), tuple(new_c), tuple(new_o)

        init = (jnp.int32(0), tuple(jnp.max(c) for c in csums), tuple(csums), tuple(outs))
        _, _, _, outs = lax.while_loop(cond, body, init)
        for sub in range(SB_GROUP):
            t0 = pl.multiple_of((qg * SB_GROUP + sub) * SB_Q, SB_Q)
            o_ref[0, pl.ds(t0, SB_Q), :] = jnp.where(
                lane_q < HEAD_DIM, outs[2 * sub], outs[2 * sub + 1]).astype(o_ref.dtype)
        return carry

    lax.fori_loop(0, seq // (SB_Q * SB_GROUP), q_group, 0)


def _stick_breaking(sb3):
    bsz, seq, _ = sb3.shape
    spec = lambda off: pl.BlockSpec((1, seq, LANES), lambda b, hp: (b, 0, off + hp))
    return pl.pallas_call(
        _sb_kernel,
        grid=(bsz, 2),
        in_specs=[spec(0), spec(2), spec(4)],
        out_specs=pl.BlockSpec((1, seq, LANES), lambda b, hp: (b, 0, hp)),
        out_shape=jax.ShapeDtypeStruct((bsz, seq, BRANCH_WIDTH), BF16),
        compiler_params=_cparams(("parallel", "parallel")),
        name="stick_breaking",
    )(sb3, sb3, sb3)


def _s5_kernel(u_ref, m_ref, e_ref, g_ref, pw_ref, d_ref, y_ref, state_ref):
    half = S5_GROUPS * S5_STATE

    @pl.when(pl.program_id(1) == 0)
    def _():
        state_ref[...] = jnp.zeros_like(state_ref)

    u = u_ref[0]
    ub = u.astype(BF16)
    rows = u.shape[0]

    def cmul(ar, ai, xr, xi):
        return ar * xr - ai * xi, ar * xi + ai * xr

    w = _dot(ub, e_ref[...])
    wr, wi = w[:, :half], w[:, half:]

    n_c = S5_GROUPS * S5_GROUP_CH
    slots = [ub[:, j * n_c:(j + 1) * n_c] for j in range(S5_CHUNK)]
    cols = []
    for i in range(S5_CHUNK):
        acc = _dot(slots[0], m_ref[i])
        for j in range(1, i + 1):
            acc = acc + _dot(slots[j], m_ref[i - j])
        cols.append(acc)
    y = jnp.concatenate(cols, axis=1)

    row = lax.broadcasted_iota(jnp.int32, (rows, LANES), 0)
    xrs, xis = [], []
    for t in range(half // LANES):
        re, im = slice(t * LANES, (t + 1) * LANES), slice(half + t * LANES, half + (t + 1) * LANES)
        tr, ti = wr[:, re], wi[:, re]
        sr, si = state_ref[0:1, re], state_ref[0:1, im]
        cr, ci = cmul(pw_ref[0:1, re], pw_ref[0:1, im], sr, si)
        tr = jnp.where(row == 0, tr + cr, tr)
        ti = jnp.where(row == 0, ti + ci, ti)
        step = 0
        d = 1
        while d < rows:
            pr = jnp.where(row >= d, pltpu.roll(tr, d, axis=0), 0.0)
            pi = jnp.where(row >= d, pltpu.roll(ti, d, axis=0), 0.0)
            mr, mi = cmul(pw_ref[step:step + 1, re], pw_ref[step:step + 1, im], pr, pi)
            tr, ti = tr + mr, ti + mi
            d *= 2
            step += 1
        xrs.append(jnp.where(row == 0, sr, pltpu.roll(tr, 1, axis=0)))
        xis.append(jnp.where(row == 0, si, pltpu.roll(ti, 1, axis=0)))
        state_ref[0:1, re] = tr[rows - 1:rows, :]
        state_ref[0:1, im] = ti[rows - 1:rows, :]
    xr, xi = jnp.concatenate(xrs, axis=1), jnp.concatenate(xis, axis=1)

    y = y + _dot(xr.astype(BF16), g_ref[:half, :]) + _dot(xi.astype(BF16), g_ref[half:, :])
    y = y + d_ref[...] * u
    y_ref[0] = jax.nn.gelu(y).astype(y_ref.dtype)


def _s5(u3, m, e, g, pw, dskip, layer):
    bsz, rows, width = u3.shape
    tr = min(S5_ROW_TILE, rows)
    blk = pl.BlockSpec((1, tr, width), lambda b, i: (b, i, 0))
    return pl.pallas_call(
        _s5_kernel,
        grid=(bsz, rows // tr),
        in_specs=[blk] + [_const_spec(a.shape, layer) for a in (m, e, g, pw, dskip)],
        out_specs=blk,
        out_shape=jax.ShapeDtypeStruct(u3.shape, BF16),
        scratch_shapes=[pltpu.VMEM((8, width), F32)],
        compiler_params=_cparams(("parallel", "arbitrary")),
        name="s5",
    )(u3, m, e, g, pw, dskip)


def _s5_operators(lam_re, lam_im, log_step, b_re, b_im, c_re, c_im, d_skip, row_tile):
    n_g, n_p, n_h, L = S5_GROUPS, S5_STATE, S5_GROUP_CH, S5_CHUNK
    n_c, n_s = n_g * n_h, n_g * n_p
    dt = jnp.exp(log_step)[:, None]

    def lam_bar_pow(k):
        mag = jnp.exp(k[:, None, None] * (lam_re * dt)[None])
        ang = k[:, None, None] * (lam_im * dt)[None]
        return mag * jnp.cos(ang), mag * jnp.sin(ang)

    pr, pi = lam_bar_pow(jnp.arange(L + 1, dtype=F32))
    den = lam_re * lam_re + lam_im * lam_im
    fr = ((pr[1] - 1.0) * lam_re + pi[1] * lam_im) / den
    fi = (pi[1] * lam_re - (pr[1] - 1.0) * lam_im) / den
    bbr = fr[..., None] * b_re - fi[..., None] * b_im
    bbi = fr[..., None] * b_im + fi[..., None] * b_re
    tr = pr[:L, :, :, None] * bbr[None] - pi[:L, :, :, None] * bbi[None]
    ti = pr[:L, :, :, None] * bbi[None] + pi[:L, :, :, None] * bbr[None]

    def expand(small, rows_per_group, cols_per_group):
        rows = small.shape[-2]
        src = lax.broadcasted_iota(jnp.int32, (cols_per_group, n_g * cols_per_group), 0)
        dst = lax.broadcasted_iota(jnp.int32, (cols_per_group, n_g * cols_per_group), 1) % cols_per_group
        spread = jnp.where(src == dst, 1.0, 0.0).astype(BF16)
        tiled = jnp.einsum('...rc,cq->...rq', small.astype(BF16), spread, preferred_element_type=F32)
        r = lax.broadcasted_iota(jnp.int32, (rows, n_g * cols_per_group), 0) // rows_per_group
        c = lax.broadcasted_iota(jnp.int32, (rows, n_g * cols_per_group), 1) // cols_per_group
        return jnp.where(r == c, tiled, 0.0)

    hp = lax.Precision.HIGHEST
    kern = (jnp.einsum('ghp,kgpi->kghi', c_re, tr, precision=hp)
            - jnp.einsum('ghp,kgpi->kghi', c_im, ti, precision=hp))
    m_op = expand(kern.transpose(0, 1, 3, 2).reshape(L, n_c, n_h), n_h, n_h).astype(BF16)

    e_r = expand(tr[::-1].transpose(0, 1, 3, 2).reshape(L, n_c, n_p), n_h, n_p)
    e_i = expand(ti[::-1].transpose(0, 1, 3, 2).reshape(L, n_c, n_p), n_h, n_p)
    e_op = jnp.concatenate([e_r, e_i], axis=2).astype(BF16).reshape(L * n_c, 2 * n_s)

    clr = c_re[None] * pr[1:, :, None, :] - c_im[None] * pi[1:, :, None, :]
    cli = c_re[None] * pi[1:, :, None, :] + c_im[None] * pr[1:, :, None, :]
    g_r = expand(clr.transpose(0, 1, 3, 2).reshape(L, n_s, n_h), n_p, n_h)
    g_i = expand(cli.transpose(0, 1, 3, 2).reshape(L, n_s, n_h), n_p, n_h)
    g_blocks = jnp.concatenate([g_r, -g_i], axis=1).astype(BF16)
    g_op = jnp.concatenate([g_blocks[i] for i in range(L)], axis=1)

    n_steps = max(1, int(math.log2(row_tile)))
    sr, si = lam_bar_pow(L * 2.0 ** jnp.arange(n_steps, dtype=F32))
    pw = jnp.concatenate([sr.reshape(n_steps, n_s), si.reshape(n_steps, n_s)], axis=1)
    pw = jnp.pad(pw, ((0, (-n_steps) % 8), (0, 0)))
    dskip = jnp.tile(d_skip.reshape(1, n_c), (1, L))
    return m_op, e_op, g_op, pw.astype(F32), dskip.astype(F32)


def _gla_constants():
    n = GLA_SUB
    ri = lax.broadcasted_iota(jnp.int32, (n, n), 0)
    ci = lax.broadcasted_iota(jnp.int32, (n, n), 1)
    same = (ri // GLA_BLOCK) == (ci // GLA_BLOCK)
    mats = [same & (ci <= ri), same & (ci > ri)]
    masks = [ri == ci]
    ups, lows = [], []
    for lev in range(GLA_LEVELS):
        half = 1 << lev
        blk = 2 * half
        mid = (ri // blk) * blk + (half - 1)
        later = (ri % blk) >= half
        ups.append(later & (ci > mid) & (ci <= ri))
        lows.append(jnp.logical_not(later) & (ci > ri) & (ci <= mid))
        masks.append(((ri // blk) == (ci // blk)) & later & ((ci % blk) < half))
    stacked = jnp.concatenate([jnp.where(m, 1.0, 0.0).astype(BF16) for m in mats + ups + lows], axis=0)
    return stacked, masks


def _gla_stages(x, gg, state, wgate, bgate, norm, stacked, masks, result):
    n_rows = x.shape[0]
    n_sub = n_rows // GLA_SUB
    hk = GLA_HEADS * GLA_DK
    hv = GLA_HEADS * GLA_DV
    q = x[:, 0:hk].astype(F32) * (GLA_DK ** -0.5)
    k = x[:, hk:2 * hk].astype(F32)
    vb = x[:, 2 * hk:2 * hk + hv]
    r = x[:, 2 * hk + hv:].astype(F32)
    sub = lambda t, st: t[st * GLA_SUB:(st + 1) * GLA_SUB]

    z = _dot(gg.astype(BF16), wgate) + bgate
    g = -(jnp.maximum(-z, 0.0) + jnp.log(1.0 + jnp.exp(-jnp.abs(z)))) * (1.0 / GLA_GATE_NORM)
    gb = g.astype(BF16)
    yield

    def decay_sums(st, blocks):
        picked = jnp.concatenate([stacked[i * GLA_SUB:(i + 1) * GLA_SUB] for i in blocks], axis=0)
        both = _dot(picked, sub(gb, st))
        return [both[n * GLA_SUB:(n + 1) * GLA_SUB] for n in range(len(blocks))]

    def level_blocks(lev):
        return 2 + lev, 2 + GLA_LEVELS + lev

    lane_k = lax.broadcasted_iota(jnp.int32, (GLA_SUB, hk), 1) // GLA_DK
    att = [[jnp.zeros((GLA_SUB, GLA_SUB), F32) for _ in range(GLA_HEADS)] for _ in range(n_sub)]
    ahead = [decay_sums(st, level_blocks(0)) for st in range(n_sub)]
    for lev in range(-1, GLA_LEVELS):
        now, ahead = ahead, None
        for st in range(n_sub):
            qs, ks = sub(q, st), sub(k, st)
            if lev < 0:
                qt, kt = qs.astype(BF16), ks.astype(BF16)
            else:
                qt = (qs * jnp.exp(now[st][0])).astype(BF16)
                kt = (ks * jnp.exp(now[st][1])).astype(BF16)
            q_heads = jnp.concatenate(
                [jnp.where(lane_k == h, qt, jnp.zeros_like(qt)) for h in range(GLA_HEADS)], axis=0)
            a = _dot_nt(q_heads, kt)
            for h in range(GLA_HEADS):
                att[st][h] = jnp.where(masks[lev + 1], a[h * GLA_SUB:(h + 1) * GLA_SUB], att[st][h])
        if lev < 0:
            ahead = now
        elif lev + 1 < GLA_LEVELS:
            ahead = [decay_sums(st, level_blocks(lev + 1)) for st in range(n_sub)]
        else:
            ahead = [decay_sums(st, (0, 1)) for st in range(n_sub)]
        yield
    part = lambda st, i: ahead[st][i]

    lane_v = lax.broadcasted_iota(jnp.int32, (GLA_SUB, hv), 1) // GLA_DV
    inner = []
    for st in range(n_sub):
        vs = sub(vb, st)
        att_all = jnp.concatenate([a.astype(BF16) for a in att[st]], axis=1)
        v_heads = jnp.concatenate([jnp.where(lane_v == h, vs, jnp.zeros_like(vs)) for h in range(GLA_HEADS)], axis=0)
        inner.append(_dot(att_all, v_heads))
    yield

    row_blk = lax.broadcasted_iota(jnp.int32, (GLA_SUB, hk), 0) // GLA_BLOCK
    rs = lax.broadcasted_iota(jnp.int32, (hv, hk), 0) // GLA_DV
    cs = lax.broadcasted_iota(jnp.int32, (hv, hk), 1) // GLA_DK
    head_diag = rs == cs
    n_blk = GLA_SUB // GLA_BLOCK
    outs = []
    for st in range(n_sub):
        b, tail = part(st, 0), part(st, 1)
        q_dec = (sub(q, st) * jnp.exp(b)).astype(BF16)
        k_dec = (sub(k, st) * jnp.exp(tail)).astype(BF16)
        b_end = b + tail
        k_blocks = jnp.concatenate(
            [jnp.where(row_blk == n, k_dec, jnp.zeros_like(k_dec)) for n in range(n_blk)], axis=1)
        upd_all = _dot_tn(sub(vb, st), k_blocks)
        cross = []
        for blk in range(n_blk):
            r0 = blk * GLA_BLOCK
            cross.append(_dot_nt(q_dec[r0:r0 + GLA_BLOCK], state.astype(BF16)))
            upd = upd_all[:, blk * hk:(blk + 1) * hk]
            state = jnp.exp(b_end[r0:r0 + 1, :]) * state + jnp.where(head_diag, upd, 0.0)
        outs.append(inner[st] + jnp.concatenate(cross, axis=0))
    yield

    o = _group_rms(jnp.concatenate(outs, axis=0), norm)
    result["y"] = o * (r * jax.nn.sigmoid(r))
    result["state"] = state


def _dsa_band_bias(offset, n_steps):
    row = lax.broadcasted_iota(jnp.int32, (2 * DSA_BLOCK, 2 * DSA_BLOCK), 0) % DSA_BLOCK
    col = lax.broadcasted_iota(jnp.int32, (2 * DSA_BLOCK, 2 * DSA_BLOCK), 1)
    dist = offset + row - col
    return jnp.where((dist >= 0) & (dist <= n_steps), 0.0, NEG_BIG)


def _dsa_logits(q, kw, bias):
    lane_q = lax.broadcasted_iota(jnp.int32, (DSA_BLOCK, LANES), 1)
    q = q * (HEAD_DIM ** -0.5)
    qm2 = jnp.concatenate(
        [jnp.where((lane_q // HEAD_DIM) == head, q, jnp.zeros_like(q)) for head in range(2)], axis=0)
    return _dot_nt(qm2, kw) + bias


def _dsa_softmax_pv(s, vw, shift):
    if shift is None:
        m = jnp.max(s, axis=1, keepdims=True)
        p = jnp.exp(s - m)
        l = jnp.sum(p, axis=1, keepdims=True)
        o2 = _dot(p.astype(BF16), vw) / l
        lse2 = jnp.broadcast_to(m + jnp.log(l), (2 * DSA_BLOCK, LANES))
    else:
        p = jnp.exp(s)
        pv = _dot(p.astype(BF16), jnp.concatenate([vw, jnp.ones_like(vw)], axis=1))
        l = pv[:, LANES:]
        o2 = pv[:, :LANES] / l
        lse2 = shift + jnp.log(l)
    first = lax.broadcasted_iota(jnp.int32, (DSA_BLOCK, LANES), 1) < HEAD_DIM
    return (jnp.where(first, o2[:DSA_BLOCK], o2[DSA_BLOCK:]),
            jnp.where(first, lse2[:DSA_BLOCK], lse2[DSA_BLOCK:]))


def _dsa_kernel(bound_ref, q1_ref, k1_ref, v1_ref, q4_ref, k4_ref, v4_ref, q16_ref, k16_ref, v16_ref,
                y_ref, o_scr, l_scr):
    seq = q1_ref.shape[1]
    refs = ((q1_ref, k1_ref, v1_ref), (q4_ref, k4_ref, v4_ref), (q16_ref, k16_ref, v16_ref))
    blocks_per_span = DSA_SPAN // DSA_BLOCK
    n_steps = DSA_PATTERNS[0][0] // DSA_PATTERNS[0][1]
    assert all(w // d == n_steps for w, d in DSA_PATTERNS)
    bias_inner = _dsa_band_bias(DSA_BLOCK, n_steps)
    bias_first = _dsa_band_bias(0, n_steps)

    def run(shift):
        inner, first = (bias_inner, bias_first) if shift is None else (bias_inner - shift, bias_first - shift)

        def span(sp, carry):
            for p, (_, dil) in enumerate(DSA_PATTERNS):
                q_ref, k_ref, v_ref = refs[p]
                per_res = blocks_per_span // dil

                def group(gi, c, p=p, dil=dil, per_res=per_res, q_ref=q_ref, k_ref=k_ref, v_ref=v_ref):
                    units = []
                    for u in range(DSA_GROUP):
                        idx = gi * DSA_GROUP + u
                        r = idx // per_res
                        jb = idx % per_res
                        m0 = pl.multiple_of(sp * (DSA_SPAN // dil) + jb * DSA_BLOCK, DSA_BLOCK)
                        start = pl.multiple_of(jnp.maximum(m0 - DSA_BLOCK, 0), DSA_BLOCK)
                        if dil == 1:
                            q = q_ref[0, pl.ds(m0, DSA_BLOCK), :]
                            kw = k_ref[0, pl.ds(start, 2 * DSA_BLOCK), :]
                            vw = v_ref[0, pl.ds(start, 2 * DSA_BLOCK), :]
                        else:
                            q = q_ref[0, r, pl.ds(m0, DSA_BLOCK), :]
                            kw = k_ref[0, r, pl.ds(start, 2 * DSA_BLOCK), :]
                            vw = v_ref[0, r, pl.ds(start, 2 * DSA_BLOCK), :]
                        may_start_sequence = u % per_res == 0
                        s = _dsa_logits(q, kw, jnp.where(m0 == 0, first, inner) if may_start_sequence else inner)
                        rows = pl.ds(r + dil * jb * DSA_BLOCK, DSA_BLOCK, stride=dil) if dil > 1 else \
                            pl.ds(pl.multiple_of(jb * DSA_BLOCK, DSA_BLOCK), DSA_BLOCK)
                        units.append((s, vw, rows))
                    results = [_dsa_softmax_pv(s, vw, shift) for s, vw, _ in units]
                    for (o, lse), (_, _, rows) in zip(results, units):
                        o_scr[p, rows, :] = o
                        l_scr[p, rows, :] = lse
                    return c

                lax.fori_loop(0, blocks_per_span // DSA_GROUP, group, 0)

            l1, l2, l3 = l_scr[0], l_scr[1], l_scr[2]
            lm = jnp.maximum(jnp.maximum(l1, l2), l3)
            e1, e2, e3 = jnp.exp(l1 - lm), jnp.exp(l2 - lm), jnp.exp(l3 - lm)
            y = (e1 * o_scr[0] + e2 * o_scr[1] + e3 * o_scr[2]) / (e1 + e2 + e3)
            y_ref[0, pl.ds(pl.multiple_of(sp * DSA_SPAN, DSA_SPAN), DSA_SPAN), :] = y.astype(y_ref.dtype)
            return carry

        lax.fori_loop(0, seq // DSA_SPAN, span, 0)

    bound = bound_ref[0]

    @pl.when(bound <= DSA_MAX_SHIFT)
    def _():
        run(bound)

    @pl.when(bound > DSA_MAX_SHIFT)
    def _():
        run(None)


def _dsa(dsa1, dsa4, dsa16, bound):
    bsz, seq, _ = dsa1.shape
    nat = lambda off: pl.BlockSpec((1, seq, LANES), lambda b, hp: (b, 0, off + hp))
    res = lambda dil, off: pl.BlockSpec((1, dil, seq // dil, LANES), lambda b, hp: (b, 0, 0, off + hp))
    return pl.pallas_call(
        _dsa_kernel,
        grid=(bsz, 2),
        in_specs=[pl.BlockSpec(memory_space=pltpu.SMEM),
                  nat(0), nat(2), nat(4), res(4, 0), res(4, 2), res(4, 4), res(16, 0), res(16, 2), res(16, 4)],
        out_specs=pl.BlockSpec((1, seq, LANES), lambda b, hp: (b, 0, hp)),
        out_shape=jax.ShapeDtypeStruct((bsz, seq, BRANCH_WIDTH), BF16),
        scratch_shapes=[pltpu.VMEM((3, DSA_SPAN, LANES), F32), pltpu.VMEM((3, DSA_SPAN, LANES), F32)],
        compiler_params=_cparams(("parallel", "parallel")),
        name="dsa",
    )(bound, dsa1, dsa1, dsa1, dsa4, dsa4, dsa4, dsa16, dsa16, dsa16)


def _proj_weights(w_in):
    return jnp.swapaxes(w_in, 1, 2).astype(BF16)


def _rope_tables(seq):
    inv = ROPE_THETA ** (-jnp.arange(0, HEAD_DIM, 2, dtype=F32) / HEAD_DIM)
    ang = jnp.arange(seq, dtype=F32)[:, None] * inv[None, :]
    cos, sin = jnp.cos(ang), jnp.sin(ang)
    cos_t = jnp.tile(jnp.concatenate([cos, cos], axis=1), (1, 2))
    sin_t = jnp.tile(jnp.concatenate([-sin, sin], axis=1), (1, 2))
    return cos_t, sin_t


def _pick_tile(n, pref):
    t = min(pref, n)
    while n % t:
        t //= 2
    return t


def kernel(x, ffn1_norm, ffn1_w_in, ffn1_w_out, mix_norm, w_in, s5_lam_re, s5_lam_im, s5_log_step, s5_b_re, s5_b_im, s5_c_re, s5_c_im, s5_d, s5_w_glu, gla_w_gate, gla_b_gate, gla_norm, dsa_q_norm, dsa_k_norm, w_branch, w_out, ffn2_norm, ffn2_w_in, ffn2_w_out):
    bsz, seq, _ = x.shape
    tokens = bsz * seq
    depth = ffn1_norm.shape[0]
    assert seq % DSA_SPAN == 0 and seq // DSA_PATTERNS[-1][1] >= 2 * DSA_BLOCK and seq >= SB_WIN
    tm_a = _pick_tile(seq, 512)
    tm_b = _pick_tile(tokens, 512)
    s5_rows = seq // S5_CHUNK
    s5_tile = min(S5_ROW_TILE, s5_rows)
    cos_t, sin_t = _rope_tables(seq)

    wp_all = _proj_weights(w_in)
    w1a_all, w2a_all = ffn1_w_in.astype(BF16), ffn1_w_out.astype(BF16)
    w1b_all, w2b_all = ffn2_w_in.astype(BF16), ffn2_w_out.astype(BF16)
    wglu_all, wb_all, wo_all = s5_w_glu.astype(BF16), w_branch.astype(BF16), w_out.astype(BF16)
    s5_ops = jax.vmap(functools.partial(_s5_operators, row_tile=s5_tile))(
        s5_lam_re, s5_lam_im, s5_log_step, s5_b_re, s5_b_im, s5_c_re, s5_c_im, s5_d)

    h = x.reshape(tokens, D_MODEL)
    for l in range(depth):
        row = lambda p: p[l].reshape(1, -1)
        qn = jnp.tile(row(dsa_q_norm), (1, 4))
        kn = jnp.tile(row(dsa_k_norm), (1, 4))
        wgate = jnp.pad(gla_w_gate[l], ((0, LANES - GLA_GATE_RANK), (0, 0))).astype(BF16)
        h, sb, s5u, y_gla, y_sb_window, sb_left, dsa1, dsa4, dsa16 = _ffn_proj(
            h, row(ffn1_norm), w1a_all, w2a_all, row(mix_norm), wp_all, qn, kn, cos_t, sin_t,
            wgate, row(gla_b_gate), row(gla_norm), bsz, seq, tm_a, l)

        y_sb = lax.cond(jnp.max(sb_left) > SB_LOG_ZERO,
                        lambda: _stick_breaking(sb.reshape(bsz, seq, 768)).reshape(tokens, BRANCH_WIDTH),
                        lambda: y_sb_window)

        y_s5 = _s5(s5u.reshape(bsz, s5_rows, S5_CHUNK * 256), *s5_ops, l).reshape(tokens // S5_CHUNK, S5_CHUNK * 256)

        bound = (1.02 * HEAD_DIM ** 0.5 * jnp.max(jnp.abs(dsa_q_norm[l])) * jnp.max(jnp.abs(dsa_k_norm[l]))).reshape(1)
        y_dsa = _dsa(dsa1.reshape(bsz, seq, 768), dsa4, dsa16, bound).reshape(tokens, BRANCH_WIDTH)

        h = _merge_ffn(h, y_sb, y_s5, y_gla, y_dsa, row(mix_norm), wp_all, wglu_all, wb_all, wo_all,
                       row(ffn2_norm), w1b_all, w2b_all, tm_b, l)
    return h.reshape(bsz, seq, D_MODEL)
```

```python
import functools
import math

import jax
import jax.numpy as jnp
from jax import lax
from jax.experimental import pallas as pl
from jax.experimental.pallas import tpu as pltpu

D_MODEL = 1024
HEAD_DIM = 64
BRANCH_WIDTH = 256
N_BRANCH = 4
S5_GROUPS = 16
S5_GROUP_CH = 16
S5_STATE = 64
GLA_HEADS = 4
GLA_DK = 32
GLA_DV = 64
GLA_GATE_RANK = 16
GLA_GATE_NORM = 16.0
DSA_PATTERNS = ((128, 1), (512, 4), (2048, 16))
DSA_BLOCK = 128
ROPE_THETA = 10000.0
D_FF = 2816
RMS_EPS = 1e-6

LANES = 128
VMEM_LIMIT_BYTES = 58 * 1024 * 1024
FF_CHUNK = 256
N_FF_CHUNKS = D_FF // FF_CHUNK
S5_CHUNK = 8
S5_ROW_TILE = 256
GLA_BLOCK = 64
GLA_LEVELS = 6
GLA_SUB = 128
SB_Q = 64
SB_WIN = 4 * SB_Q
SB_GROUP = 8
SB_TAIL = SB_WIN - SB_Q
SB_LOG_ZERO = -104.0
NEG_BIG = -1e30
DSA_GROUP = 16
DSA_SPAN = DSA_PATTERNS[-1][1] * DSA_BLOCK
DSA_MAX_SHIFT = 30.0

PROJ_SB = (0, 768)
PROJ_S5 = (768, 1024)
PROJ_GLA = (1024, 1792)
PROJ_GG = (PROJ_GLA[1], PROJ_GLA[1] + LANES)
PROJ_DSA = (PROJ_GLA[1] + GLA_GATE_RANK, PROJ_GLA[1] + GLA_GATE_RANK + 768)
PROJ_ROWS = PROJ_DSA[1]
PROJ_GATES = PROJ_ROWS

F32 = jnp.float32
BF16 = jnp.bfloat16


def _cparams(sem):
    return pltpu.CompilerParams(dimension_semantics=sem, vmem_limit_bytes=VMEM_LIMIT_BYTES)


def _const_spec(shape, layer=None):
    if layer is None:
        zeros = (0,) * len(shape)
        return pl.BlockSpec(shape, lambda *_: zeros, pipeline_mode=pl.Buffered(1))
    index = (layer,) + (0,) * (len(shape) - 1)
    return pl.BlockSpec((None,) + tuple(shape[1:]), lambda *_: index, pipeline_mode=pl.Buffered(1))


def _rms_rows(x, g):
    ms = jnp.mean(x * x, axis=-1, keepdims=True)
    return x * lax.rsqrt(ms + RMS_EPS) * g


def _dot(a, b):
    return jnp.dot(a, b, preferred_element_type=F32)


def _dot_nt(a, b):
    return lax.dot_general(a, b, (((1,), (1,)), ((), ())), preferred_element_type=F32)


def _dot_tn(a, b):
    return lax.dot_general(a, b, (((0,), (0,)), ((), ())), preferred_element_type=F32)


def _round_robin(*stage_generators):
    live = list(stage_generators)
    while live:
        live = [g for g in live if next(g, StopIteration) is not StopIteration]
        yield


def _swiglu_into(acc_ref, xb, w1_ref, w2_ref, between=None):
    for c in range(N_FF_CHUNKS):
        lo, hi = c * FF_CHUNK, (c + 1) * FF_CHUNK
        a = _dot(xb, w1_ref[:, lo:hi])
        b = _dot(xb, w1_ref[:, D_FF + lo:D_FF + hi])
        hm = (a * jax.nn.sigmoid(a) * b).astype(BF16)
        part = _dot(hm, w2_ref[lo:hi, :])
        if c == 0:
            acc_ref[...] = part
        else:
            acc_ref[...] += part
        if between is not None:
            next(between, None)


def _load_ffn_weights(w1_hbm, w2_hbm, w1_ref, w2_ref, stage1_ref, stage2_ref, sem, layer):
    half = FF_CHUNK // 2
    n_pieces = D_FF // half

    def copies(i, slot):
        lo = i * half
        return (pltpu.make_async_copy(w1_hbm.at[layer, :, lo:lo + half], stage1_ref.at[slot, 0], sem.at[slot, 0]),
                pltpu.make_async_copy(w1_hbm.at[layer, :, D_FF + lo:D_FF + lo + half], stage1_ref.at[slot, 1],
                                      sem.at[slot, 1]),
                pltpu.make_async_copy(w2_hbm.at[layer, lo:lo + half, :], stage2_ref.at[slot], sem.at[slot, 2]))

    for c in copies(0, 0):
        c.start()
    for i in range(n_pieces):
        slot = i % 2
        if i + 1 < n_pieces:
            for c in copies(i + 1, 1 - slot):
                c.start()
        for c in copies(i, slot):
            c.wait()
        lo = i * half
        w1_ref[:, lo:lo + half] = stage1_ref[slot, 0].astype(BF16)
        w1_ref[:, D_FF + lo:D_FF + lo + half] = stage1_ref[slot, 1].astype(BF16)
        w2_ref[lo:lo + half, :] = stage2_ref[slot].astype(BF16)


def _group_mean_matrix(width, group):
    r = lax.broadcasted_iota(jnp.int32, (width, width), 0) // group
    c = lax.broadcasted_iota(jnp.int32, (width, width), 1) // group
    return jnp.where(r == c, 1.0 / group, 0.0).astype(BF16)


def _group_rms(x, gain):
    gm = _group_mean_matrix(x.shape[-1], HEAD_DIM)
    ms = _dot((x * x).astype(BF16), gm)
    return x * lax.rsqrt(ms + RMS_EPS) * gain


def _swap_half_heads(x):
    half = HEAD_DIM // 2
    outs = []
    for s in range(x.shape[-1] // LANES):
        xs = x[:, s * LANES:(s + 1) * LANES]
        lane = lax.broadcasted_iota(jnp.int32, xs.shape, 1)
        up = pltpu.roll(xs, LANES - half, axis=1)
        down = pltpu.roll(xs, half, axis=1)
        outs.append(jnp.where((lane % HEAD_DIM) < half, up, down))
    return jnp.concatenate(outs, axis=-1)


def _ffn_proj_tile(x_ref, n1_ref, w1_ref, w2_ref, nm_ref, wp_ref, qn_ref, kn_ref, cos_ref, sin_ref,
                   h_ref, sb_ref, s5_ref, dsa1_ref, dsa4_ref, dsa16_ref, acc_ref, slab_ref, glax_ref, glag_ref,
                   sbh_ref, between, after_ffn):
    tm = x_ref.shape[0]
    x = x_ref[...]
    xb = _rms_rows(x, n1_ref[...]).astype(BF16)
    _swiglu_into(acc_ref, xb, w1_ref, w2_ref, between=between)
    after_ffn()
    h = x + 0.5 * acc_ref[...]
    h_ref[...] = h
    ub = _rms_rows(h, nm_ref[...]).astype(BF16)

    dsa = _dot_nt(ub, wp_ref[PROJ_DSA[0]:PROJ_DSA[1], :])
    cos = jnp.concatenate([cos_ref[...]] * 2, axis=-1)
    sin = jnp.concatenate([sin_ref[...]] * 2, axis=-1)
    parts = []
    for i, gain_ref in enumerate((qn_ref, kn_ref)):
        t = _group_rms(dsa[:, i * 256:(i + 1) * 256], gain_ref[...])
        parts.append(t * cos + _swap_half_heads(t) * sin)
    parts.append(dsa[:, 512:768])
    dsa = jnp.concatenate(parts, axis=-1)
    dsa1_ref[...] = dsa.astype(BF16)
    for s in range(6):
        slab_ref[s] = dsa[:, s * LANES:(s + 1) * LANES]
    for dil, ref in ((4, dsa4_ref), (16, dsa16_ref)):
        for r in range(dil):
            for s in range(6):
                rows = slab_ref[s, pl.ds(r, tm // dil, stride=dil), :]
                ref[0, r, :, s * LANES:(s + 1) * LANES] = rows.astype(BF16)

    s5 = _dot_nt(ub, wp_ref[PROJ_S5[0]:PROJ_S5[1], :])
    for s in range(2):
        slab_ref[6 + s] = s5[:, s * LANES:(s + 1) * LANES]
    for j in range(S5_CHUNK):
        for s in range(2):
            col = j * 256 + s * LANES
            s5_ref[:, col:col + LANES] = slab_ref[6 + s, pl.ds(j, tm // S5_CHUNK, stride=S5_CHUNK), :]

    sb = _dot_nt(ub, wp_ref[PROJ_SB[0]:PROJ_SB[1], :]).astype(BF16)
    sb_ref[...] = sb
    tail = sbh_ref[tm:tm + SB_TAIL, :]
    sbh_ref[0:SB_TAIL, :] = tail
    sbh_ref[SB_TAIL:, :] = sb
    glax_ref[...] = _dot_nt(ub, wp_ref[PROJ_GLA[0]:PROJ_GLA[1], :]).astype(BF16)
    gg = _dot_nt(ub, wp_ref[PROJ_GG[0]:PROJ_GG[1], :])
    lane = lax.broadcasted_iota(jnp.int32, gg.shape, 1)
    glag_ref[...] = jnp.where(lane < GLA_GATE_RANK, gg, 0.0)


def _ffn_proj_kernel(x_ref, n1_ref, w1_hbm, w2_hbm, nm_ref, wp_ref, qn_ref, kn_ref, cos_ref, sin_ref,
                     wgate_ref, bgate_ref, gnorm_ref,
                     h_ref, sb_ref, s5_ref, ygla_ref, ysb_ref, sbf_ref, dsa1_ref, dsa4_ref, dsa16_ref,
                     acc_ref, slab_ref, glax_ref, glag_ref, glas_ref, sbh_ref,
                     w1_ref, w2_ref, stage1_ref, stage2_ref, sem, *, seq_tiles, layer):
    step = pl.program_id(0)
    n_tiles = pl.num_programs(0) - 1

    @pl.when(step == 0)
    def _():
        _load_ffn_weights(w1_hbm, w2_hbm, w1_ref, w2_ref, stage1_ref, stage2_ref, sem, layer)
        glax_ref[...] = jnp.zeros_like(glax_ref)
        glag_ref[...] = jnp.zeros_like(glag_ref)
        glas_ref[...] = jnp.zeros_like(glas_ref)
        sbh_ref[...] = jnp.zeros_like(sbh_ref)

    def start_mixers():
        starts_sequence = lax.rem(step - 1, seq_tiles) == 0
        state = jnp.where(starts_sequence, 0.0, glas_ref[...])
        stacked, masks = _gla_constants()
        gla, sbw = {}, {}
        stages = _round_robin(
            _gla_stages(glax_ref[...], glag_ref[...], state, wgate_ref[...], bgate_ref[...], gnorm_ref[...],
                        stacked, masks, gla),
            _sb_window_stages(sbh_ref, jnp.logical_not(starts_sequence), sbw))
        return stages, gla, sbw

    def finish_mixers(stages, gla, sbw):
        for _ in stages:
            pass
        ygla_ref[...] = gla["y"].astype(ygla_ref.dtype)
        glas_ref[...] = gla["state"]
        ysb_ref[...] = sbw["y"].astype(ysb_ref.dtype)
        sbf_ref[...] = jnp.full(sbf_ref.shape, sbw["cmax"], F32)

    @pl.when(step < n_tiles)
    def _():
        stages, gla, sbw = start_mixers()
        _ffn_proj_tile(x_ref, n1_ref, w1_ref, w2_ref, nm_ref, wp_ref, qn_ref, kn_ref, cos_ref, sin_ref,
                       h_ref, sb_ref, s5_ref, dsa1_ref, dsa4_ref, dsa16_ref, acc_ref, slab_ref, glax_ref, glag_ref,
                       sbh_ref, between=stages, after_ffn=lambda: finish_mixers(stages, gla, sbw))

    @pl.when(step == n_tiles)
    def _():
        finish_mixers(*start_mixers())


def _ffn_proj(x2, n1, w1, w2, nm, wp, qn, kn, cos_t, sin_t, wgate, bgate, gnorm, bsz, seq, tm, layer):
    tokens = x2.shape[0]
    seq_tiles = seq // tm
    n_tiles = tokens // tm
    cur = lambda i: jnp.minimum(i, n_tiles - 1)
    row = lambda i: (cur(i), 0)
    lag = lambda i: (jnp.maximum(i - 1, 0), 0)
    tab = lambda i: (cur(i) % seq_tiles, 0)
    res = lambda i: (cur(i) // seq_tiles, 0, cur(i) % seq_tiles, 0)
    tile = lambda w: pl.BlockSpec((tm, w), row)
    s5_w = S5_CHUNK * 256
    return pl.pallas_call(
        functools.partial(_ffn_proj_kernel, seq_tiles=seq_tiles, layer=layer),
        grid=(n_tiles + 1,),
        in_specs=[tile(D_MODEL), _const_spec(n1.shape),
                  pl.BlockSpec(memory_space=pl.ANY), pl.BlockSpec(memory_space=pl.ANY),
                  _const_spec(nm.shape), _const_spec((wp.shape[0], PROJ_ROWS, D_MODEL), layer),
                  _const_spec(qn.shape), _const_spec(kn.shape),
                  pl.BlockSpec((tm, LANES), tab), pl.BlockSpec((tm, LANES), tab),
                  _const_spec(wgate.shape), _const_spec(bgate.shape), _const_spec(gnorm.shape)],
        out_specs=[tile(D_MODEL), tile(768), pl.BlockSpec((tm // S5_CHUNK, s5_w), row),
                   pl.BlockSpec((tm, BRANCH_WIDTH), lag), pl.BlockSpec((tm, BRANCH_WIDTH), lag),
                   pl.BlockSpec((8, LANES), lag),
                   tile(768), pl.BlockSpec((1, 4, tm // 4, 768), res), pl.BlockSpec((1, 16, tm // 16, 768), res)],
        out_shape=[jax.ShapeDtypeStruct((tokens, D_MODEL), F32),
                   jax.ShapeDtypeStruct((tokens, 768), BF16),
                   jax.ShapeDtypeStruct((tokens // S5_CHUNK, s5_w), F32),
                   jax.ShapeDtypeStruct((tokens, BRANCH_WIDTH), BF16),
                   jax.ShapeDtypeStruct((tokens, BRANCH_WIDTH), BF16),
                   jax.ShapeDtypeStruct((n_tiles * 8, LANES), F32),
                   jax.ShapeDtypeStruct((tokens, 768), BF16),
                   jax.ShapeDtypeStruct((bsz, 4, seq // 4, 768), BF16),
                   jax.ShapeDtypeStruct((bsz, 16, seq // 16, 768), BF16)],
        scratch_shapes=[pltpu.VMEM((tm, D_MODEL), F32), pltpu.VMEM((8, tm, LANES), F32),
                        pltpu.VMEM((tm, 768), BF16), pltpu.VMEM((tm, LANES), F32),
                        pltpu.VMEM((GLA_HEADS * GLA_DV, GLA_HEADS * GLA_DK), F32),
                        pltpu.VMEM((SB_TAIL + tm, 768), BF16),
                        pltpu.VMEM(w1.shape[1:], BF16), pltpu.VMEM(w2.shape[1:], BF16),
                        pltpu.VMEM((2, 2, D_MODEL, FF_CHUNK // 2), F32), pltpu.VMEM((2, FF_CHUNK // 2, D_MODEL), F32),
                        pltpu.SemaphoreType.DMA((2, 3))],
        compiler_params=_cparams(("arbitrary",)),
        name="ffn_proj",
    )(x2, n1, w1, w2, nm, wp, qn, kn, cos_t, sin_t, wgate, bgate, gnorm)


def _merge_ffn_kernel(h_ref, ysb_ref, ys5_ref, ygla_ref, ydsa_ref,
                      nm_ref, wg_ref, wglu_ref, wb_ref, wo_ref, n2_ref, w1_ref, w2_ref,
                      out_ref, acc_ref, slab_ref):
    tm = h_ref.shape[0]
    proj = {n: _dot(ref[...], wb_ref[n]) for n, ref in ((0, ysb_ref), (2, ygla_ref), (3, ydsa_ref))}

    for j in range(S5_CHUNK):
        for s in range(2):
            col = j * 256 + s * LANES
            slab_ref[s, pl.ds(j, tm // S5_CHUNK, stride=S5_CHUNK), :] = ys5_ref[:, col:col + LANES].astype(F32)
    y_s5 = jnp.concatenate([slab_ref[0], slab_ref[1]], axis=-1).astype(BF16)
    glu = _dot(y_s5, wglu_ref[...])
    y_s5 = glu[:, :BRANCH_WIDTH] * jax.nn.sigmoid(glu[:, BRANCH_WIDTH:])
    proj[1] = _dot(y_s5.astype(BF16), wb_ref[1])

    h = h_ref[...]
    ub = _rms_rows(h, nm_ref[...]).astype(BF16)
    mixed = jnp.zeros(h.shape, F32)
    for n in range(N_BRANCH):
        rows = PROJ_GATES + n * D_MODEL
        gate = jax.nn.sigmoid(_dot_nt(ub, wg_ref[rows:rows + D_MODEL, :]))
        mixed = mixed + gate * proj[n]
    h2 = h + _dot(mixed.astype(BF16), wo_ref[...])

    xb = _rms_rows(h2, n2_ref[...]).astype(BF16)
    _swiglu_into(acc_ref, xb, w1_ref, w2_ref)
    out_ref[...] = h2 + 0.5 * acc_ref[...]


def _merge_ffn(h, ysb, ys5, ygla, ydsa, nm, wg, wglu, wb, wo, n2, w1, w2, tm, layer):
    tokens = h.shape[0]
    row = lambda i: (i, 0)
    tile = lambda w: pl.BlockSpec((tm, w), row)
    consts = (nm, wg, wglu, wb, wo, n2, w1, w2)
    return pl.pallas_call(
        _merge_ffn_kernel,
        grid=(tokens // tm,),
        in_specs=[tile(D_MODEL), tile(BRANCH_WIDTH), pl.BlockSpec((tm // S5_CHUNK, S5_CHUNK * 256), row),
                  tile(BRANCH_WIDTH), tile(BRANCH_WIDTH)]
                 + [_const_spec(c.shape, None if c.ndim == 2 and c.shape[0] == 1 else layer) for c in consts],
        out_specs=tile(D_MODEL),
        out_shape=jax.ShapeDtypeStruct((tokens, D_MODEL), F32),
        scratch_shapes=[pltpu.VMEM((tm, D_MODEL), F32), pltpu.VMEM((2, tm, LANES), F32)],
        compiler_params=_cparams(("parallel",)),
        name="merge_ffn",
    )(h, ysb, ys5, ygla, ydsa, *consts)


def _later_key_matrix(n):
    j = lax.broadcasted_iota(jnp.int32, (n, n), 0)
    s = lax.broadcasted_iota(jnp.int32, (n, n), 1)
    return jnp.where(j > s, 1.0, 0.0).astype(BF16)


def _sb_scores(qm, k, keep=None):
    z = _dot_nt(qm, k)
    if keep is not None:
        z = jnp.where(keep, z, NEG_BIG)
    lk = -(jnp.maximum(z, 0.0) + jnp.log(1.0 + jnp.exp(-jnp.abs(z))))
    return z, lk


def _sb_window_stages(hist_ref, not_first, result):
    rows = hist_ref.shape[0] - SB_TAIL
    n_blk = rows // SB_Q
    u_win = _later_key_matrix(SB_WIN)
    lane_q = lax.broadcasted_iota(jnp.int32, (SB_Q, LANES), 1)
    col = lax.broadcasted_iota(jnp.int32, (2 * SB_Q, SB_WIN), 1)
    key_minus_query = col - lax.broadcasted_iota(jnp.int32, (2 * SB_Q, SB_WIN), 0) % SB_Q
    strict = key_minus_query < SB_TAIL
    first = lane_q < HEAD_DIM
    outs = [[None] * n_blk for _ in range(2)]
    cmax = jnp.float32(-jnp.inf)
    for pair in range(2):
        q_cols = slice(pair * LANES, (pair + 1) * LANES)
        k_cols = slice(256 + pair * LANES, 256 + (pair + 1) * LANES)
        v_cols = slice(512 + pair * LANES, 512 + (pair + 1) * LANES)
        zs, lks = [], []
        for qb in range(n_blk):
            qs = hist_ref[SB_TAIL + qb * SB_Q:SB_TAIL + (qb + 1) * SB_Q, q_cols] * (HEAD_DIM ** -0.5)
            qm2 = jnp.concatenate([jnp.where(first, qs, jnp.zeros_like(qs)),
                                   jnp.where(first, jnp.zeros_like(qs), qs)], axis=0)
            keep = strict
            if qb * SB_Q < SB_TAIL:
                first_col = jnp.where(not_first, 0, SB_TAIL - qb * SB_Q)
                keep = strict & (col >= first_col)
            z, lk = _sb_scores(qm2, hist_ref[qb * SB_Q:qb * SB_Q + SB_WIN, k_cols], keep)
            zs.append(z)
            lks.append(lk)
        yield
        later_all = _dot(jnp.concatenate(lks, axis=0).astype(BF16), u_win)
        yield
        for qb in range(n_blk):
            later = later_all[qb * 2 * SB_Q:(qb + 1) * 2 * SB_Q]
            w = jnp.exp(zs[qb] + lks[qb] + later)
            o2 = _dot(w.astype(BF16), hist_ref[qb * SB_Q:qb * SB_Q + SB_WIN, v_cols])
            outs[pair][qb] = jnp.where(first, o2[:SB_Q], o2[SB_Q:])
            left = jnp.max(later[:, 0:1] + lks[qb][:, 0:1])
            has_older = True if qb * SB_Q > SB_TAIL else not_first
            cmax = jnp.maximum(cmax, jnp.where(has_older, left, -jnp.inf))
        yield
    result["y"] = jnp.concatenate([jnp.concatenate(outs[pair], axis=0) for pair in range(2)], axis=1)
    result["cmax"] = cmax


def _sb_kernel(q_ref, k_ref, v_ref, o_ref):
    seq = q_ref.shape[1]
    u_win = _later_key_matrix(SB_WIN)
    u_blk = _later_key_matrix(SB_Q)
    lane_q = lax.broadcasted_iota(jnp.int32, (SB_Q, LANES), 1)
    n_streams = 2 * SB_GROUP
    key_minus_query = (lax.broadcasted_iota(jnp.int32, (2 * SB_Q, SB_WIN), 1)
                       - lax.broadcasted_iota(jnp.int32, (2 * SB_Q, SB_WIN), 0) % SB_Q)

    def q_group(qg, carry):
        starts, qms, zs, lks, vws = [], [], [], [], []
        for sub in range(SB_GROUP):
            t0 = pl.multiple_of((qg * SB_GROUP + sub) * SB_Q, SB_Q)
            start = pl.multiple_of(jnp.maximum(t0 - (SB_WIN - SB_Q), 0), SB_Q)
            q = q_ref[0, pl.ds(t0, SB_Q), :] * (HEAD_DIM ** -0.5)
            kw = k_ref[0, pl.ds(start, SB_WIN), :]
            vws.append(v_ref[0, pl.ds(start, SB_WIN), :])
            strict = key_minus_query < (t0 - start)
            starts.append(start)
            qm2 = jnp.concatenate(
                [jnp.where((lane_q // HEAD_DIM) == head, q, jnp.zeros_like(q)) for head in range(2)], axis=0)
            z, lk = _sb_scores(qm2, kw, strict)
            qms += [qm2[:SB_Q], qm2[SB_Q:]]
            zs.append(z)
            lks.append(lk)
        lk_all = jnp.concatenate(lks, axis=0)
        later_all = _dot(lk_all.astype(BF16), u_win)
        csums, outs = [], []
        for sub in range(SB_GROUP):
            rows = slice(sub * 2 * SB_Q, (sub + 1) * 2 * SB_Q)
            later = later_all[rows]
            w = jnp.exp(zs[sub] + lks[sub] + later)
            o2 = _dot(w.astype(BF16), vws[sub])
            c2 = later[:, 0:1] + lks[sub][:, 0:1]
            outs += [o2[:SB_Q], o2[SB_Q:]]
            csums += [c2[:SB_Q], c2[SB_Q:]]

        def cond(state):
            step, cmaxes, _, _ = state
            flags = [jnp.logical_and(starts[s // 2] - (step + 1) * SB_Q >= 0, cmaxes[s] > SB_LOG_ZERO)
                     for s in range(n_streams)]
            return functools.reduce(jnp.logical_or, flags)

        def body(state):
            step, _, csums, outs = state
            new_c, new_o = [], []
            for s in range(n_streams):
                nxt = starts[s // 2] - (step + 1) * SB_Q
                in_range = nxt >= 0
                p0 = pl.multiple_of(jnp.maximum(nxt, 0), SB_Q)
                kb = k_ref[0, pl.ds(p0, SB_Q), :]
                vb = v_ref[0, pl.ds(p0, SB_Q), :]
                z, lk = _sb_scores(qms[s], kb)
                later = _dot(lk.astype(BF16), u_blk)
                w = jnp.where(in_range, jnp.exp(z + lk + later + csums[s]), 0.0)
                new_o.append(outs[s] + _dot(w.astype(BF16), vb))
                new_c.append(csums[s] + jnp.where(in_range, later[:, 0:1] + lk[:, 0:1], 0.0))
            return step + 1, tuple(jnp.max(c) for c in new_c), tuple(new_c), tuple(new_o)

        init = (jnp.int32(0), tuple(jnp.max(c) for c in csums), tuple(csums), tuple(outs))
        _, _, _, outs = lax.while_loop(cond, body, init)
        for sub in range(SB_GROUP):
            t0 = pl.multiple_of((qg * SB_GROUP + sub) * SB_Q, SB_Q)
            o_ref[0, pl.ds(t0, SB_Q), :] = jnp.where(
                lane_q < HEAD_DIM, outs[2 * sub], outs[2 * sub + 1]).astype(o_ref.dtype)
        return carry

    lax.fori_loop(0, seq // (SB_Q * SB_GROUP), q_group, 0)


def _stick_breaking(sb3):
    bsz, seq, _ = sb3.shape
    spec = lambda off: pl.BlockSpec((1, seq, LANES), lambda b, hp: (b, 0, off + hp))
    return pl.pallas_call(
        _sb_kernel,
        grid=(bsz, 2),
        in_specs=[spec(0), spec(2), spec(4)],
        out_specs=pl.BlockSpec((1, seq, LANES), lambda b, hp: (b, 0, hp)),
        out_shape=jax.ShapeDtypeStruct((bsz, seq, BRANCH_WIDTH), BF16),
        compiler_params=_cparams(("parallel", "parallel")),
        name="stick_breaking",
    )(sb3, sb3, sb3)


def _s5_kernel(u_ref, m_ref, e_ref, g_ref, pw_ref, d_ref, y_ref, state_ref):
    half = S5_GROUPS * S5_STATE

    @pl.when(pl.program_id(1) == 0)
    def _():
        state_ref[...] = jnp.zeros_like(state_ref)

    u = u_ref[0]
    ub = u.astype(BF16)
    rows = u.shape[0]

    def cmul(ar, ai, xr, xi):
        return ar * xr - ai * xi, ar * xi + ai * xr

    w = _dot(ub, e_ref[...])
    wr, wi = w[:, :half], w[:, half:]

    n_c = S5_GROUPS * S5_GROUP_CH
    slots = [ub[:, j * n_c:(j + 1) * n_c] for j in range(S5_CHUNK)]
    cols = []
    for i in range(S5_CHUNK):
        acc = _dot(slots[0], m_ref[i])
        for j in range(1, i + 1):
            acc = acc + _dot(slots[j], m_ref[i - j])
        cols.append(acc)
    y = jnp.concatenate(cols, axis=1)

    row = lax.broadcasted_iota(jnp.int32, (rows, LANES), 0)
    xrs, xis = [], []
    for t in range(half // LANES):
        re, im = slice(t * LANES, (t + 1) * LANES), slice(half + t * LANES, half + (t + 1) * LANES)
        tr, ti = wr[:, re], wi[:, re]
        sr, si = state_ref[0:1, re], state_ref[0:1, im]
        cr, ci = cmul(pw_ref[0:1, re], pw_ref[0:1, im], sr, si)
        tr = jnp.where(row == 0, tr + cr, tr)
        ti = jnp.where(row == 0, ti + ci, ti)
        step = 0
        d = 1
        while d < rows:
            pr = jnp.where(row >= d, pltpu.roll(tr, d, axis=0), 0.0)
            pi = jnp.where(row >= d, pltpu.roll(ti, d, axis=0), 0.0)
            mr, mi = cmul(pw_ref[step:step + 1, re], pw_ref[step:step + 1, im], pr, pi)
            tr, ti = tr + mr, ti + mi
            d *= 2
            step += 1
        xrs.append(jnp.where(row == 0, sr, pltpu.roll(tr, 1, axis=0)))
        xis.append(jnp.where(row == 0, si, pltpu.roll(ti, 1, axis=0)))
        state_ref[0:1, re] = tr[rows - 1:rows, :]
        state_ref[0:1, im] = ti[rows - 1:rows, :]
    xr, xi = jnp.concatenate(xrs, axis=1), jnp.concatenate(xis, axis=1)

    y = y + _dot(xr.astype(BF16), g_ref[:half, :]) + _dot(xi.astype(BF16), g_ref[half:, :])
    y = y + d_ref[...] * u
    y_ref[0] = jax.nn.gelu(y).astype(y_ref.dtype)


def _s5(u3, m, e, g, pw, dskip, layer):
    bsz, rows, width = u3.shape
    tr = min(S5_ROW_TILE, rows)
    blk = pl.BlockSpec((1, tr, width), lambda b, i: (b, i, 0))
    return pl.pallas_call(
        _s5_kernel,
        grid=(bsz, rows // tr),
        in_specs=[blk] + [_const_spec(a.shape, layer) for a in (m, e, g, pw, dskip)],
        out_specs=blk,
        out_shape=jax.ShapeDtypeStruct(u3.shape, BF16),
        scratch_shapes=[pltpu.VMEM((8, width), F32)],
        compiler_params=_cparams(("parallel", "arbitrary")),
        name="s5",
    )(u3, m, e, g, pw, dskip)


def _s5_operators(lam_re, lam_im, log_step, b_re, b_im, c_re, c_im, d_skip, row_tile):
    n_g, n_p, n_h, L = S5_GROUPS, S5_STATE, S5_GROUP_CH, S5_CHUNK
    n_c, n_s = n_g * n_h, n_g * n_p
    dt = jnp.exp(log_step)[:, None]

    def lam_bar_pow(k):
        mag = jnp.exp(k[:, None, None] * (lam_re * dt)[None])
        ang = k[:, None, None] * (lam_im * dt)[None]
        return mag * jnp.cos(ang), mag * jnp.sin(ang)

    pr, pi = lam_bar_pow(jnp.arange(L + 1, dtype=F32))
    den = lam_re * lam_re + lam_im * lam_im
    fr = ((pr[1] - 1.0) * lam_re + pi[1] * lam_im) / den
    fi = (pi[1] * lam_re - (pr[1] - 1.0) * lam_im) / den
    bbr = fr[..., None] * b_re - fi[..., None] * b_im
    bbi = fr[..., None] * b_im + fi[..., None] * b_re
    tr = pr[:L, :, :, None] * bbr[None] - pi[:L, :, :, None] * bbi[None]
    ti = pr[:L, :, :, None] * bbi[None] + pi[:L, :, :, None] * bbr[None]

    def expand(small, rows_per_group, cols_per_group):
        rows = small.shape[-2]
        src = lax.broadcasted_iota(jnp.int32, (cols_per_group, n_g * cols_per_group), 0)
        dst = lax.broadcasted_iota(jnp.int32, (cols_per_group, n_g * cols_per_group), 1) % cols_per_group
        spread = jnp.where(src == dst, 1.0, 0.0).astype(BF16)
        tiled = jnp.einsum('...rc,cq->...rq', small.astype(BF16), spread, preferred_element_type=F32)
        r = lax.broadcasted_iota(jnp.int32, (rows, n_g * cols_per_group), 0) // rows_per_group
        c = lax.broadcasted_iota(jnp.int32, (rows, n_g * cols_per_group), 1) // cols_per_group
        return jnp.where(r == c, tiled, 0.0)

    hp = lax.Precision.HIGHEST
    kern = (jnp.einsum('ghp,kgpi->kghi', c_re, tr, precision=hp)
            - jnp.einsum('ghp,kgpi->kghi', c_im, ti, precision=hp))
    m_op = expand(kern.transpose(0, 1, 3, 2).reshape(L, n_c, n_h), n_h, n_h).astype(BF16)

    e_r = expand(tr[::-1].transpose(0, 1, 3, 2).reshape(L, n_c, n_p), n_h, n_p)
    e_i = expand(ti[::-1].transpose(0, 1, 3, 2).reshape(L, n_c, n_p), n_h, n_p)
    e_op = jnp.concatenate([e_r, e_i], axis=2).astype(BF16).reshape(L * n_c, 2 * n_s)

    clr = c_re[None] * pr[1:, :, None, :] - c_im[None] * pi[1:, :, None, :]
    cli = c_re[None] * pi[1:, :, None, :] + c_im[None] * pr[1:, :, None, :]
    g_r = expand(clr.transpose(0, 1, 3, 2).reshape(L, n_s, n_h), n_p, n_h)
    g_i = expand(cli.transpose(0, 1, 3, 2).reshape(L, n_s, n_h), n_p, n_h)
    g_blocks = jnp.concatenate([g_r, -g_i], axis=1).astype(BF16)
    g_op = jnp.concatenate([g_blocks[i] for i in range(L)], axis=1)

    n_steps = max(1, int(math.log2(row_tile)))
    sr, si = lam_bar_pow(L * 2.0 ** jnp.arange(n_steps, dtype=F32))
    pw = jnp.concatenate([sr.reshape(n_steps, n_s), si.reshape(n_steps, n_s)], axis=1)
    pw = jnp.pad(pw, ((0, (-n_steps) % 8), (0, 0)))
    dskip = jnp.tile(d_skip.reshape(1, n_c), (1, L))
    return m_op, e_op, g_op, pw.astype(F32), dskip.astype(F32)


def _gla_constants():
    n = GLA_SUB
    ri = lax.broadcasted_iota(jnp.int32, (n, n), 0)
    ci = lax.broadcasted_iota(jnp.int32, (n, n), 1)
    same = (ri // GLA_BLOCK) == (ci // GLA_BLOCK)
    mats = [same & (ci <= ri), same & (ci > ri)]
    masks = [ri == ci]
    ups, lows = [], []
    for lev in range(GLA_LEVELS):
        half = 1 << lev
        blk = 2 * half
        mid = (ri // blk) * blk + (half - 1)
        later = (ri % blk) >= half
        ups.append(later & (ci > mid) & (ci <= ri))
        lows.append(jnp.logical_not(later) & (ci > ri) & (ci <= mid))
        masks.append(((ri // blk) == (ci // blk)) & later & ((ci % blk) < half))
    stacked = jnp.concatenate([jnp.where(m, 1.0, 0.0).astype(BF16) for m in mats + ups + lows], axis=0)
    return stacked, masks


def _gla_stages(x, gg, state, wgate, bgate, norm, stacked, masks, result):
    n_rows = x.shape[0]
    n_sub = n_rows // GLA_SUB
    hk = GLA_HEADS * GLA_DK
    hv = GLA_HEADS * GLA_DV
    q = x[:, 0:hk].astype(F32) * (GLA_DK ** -0.5)
    k = x[:, hk:2 * hk].astype(F32)
    vb = x[:, 2 * hk:2 * hk + hv]
    r = x[:, 2 * hk + hv:].astype(F32)
    sub = lambda t, st: t[st * GLA_SUB:(st + 1) * GLA_SUB]

    z = _dot(gg.astype(BF16), wgate) + bgate
    g = -(jnp.maximum(-z, 0.0) + jnp.log(1.0 + jnp.exp(-jnp.abs(z)))) * (1.0 / GLA_GATE_NORM)
    gb = g.astype(BF16)
    yield

    def decay_sums(st, blocks):
        picked = jnp.concatenate([stacked[i * GLA_SUB:(i + 1) * GLA_SUB] for i in blocks], axis=0)
        both = _dot(picked, sub(gb, st))
        return [both[n * GLA_SUB:(n + 1) * GLA_SUB] for n in range(len(blocks))]

    def level_blocks(lev):
        return 2 + lev, 2 + GLA_LEVELS + lev

    lane_k = lax.broadcasted_iota(jnp.int32, (GLA_SUB, hk), 1) // GLA_DK
    att = [[jnp.zeros((GLA_SUB, GLA_SUB), F32) for _ in range(GLA_HEADS)] for _ in range(n_sub)]
    ahead = [decay_sums(st, level_blocks(0)) for st in range(n_sub)]
    for lev in range(-1, GLA_LEVELS):
        now, ahead = ahead, None
        for st in range(n_sub):
            qs, ks = sub(q, st), sub(k, st)
            if lev < 0:
                qt, kt = qs.astype(BF16), ks.astype(BF16)
            else:
                qt = (qs * jnp.exp(now[st][0])).astype(BF16)
                kt = (ks * jnp.exp(now[st][1])).astype(BF16)
            q_heads = jnp.concatenate(
                [jnp.where(lane_k == h, qt, jnp.zeros_like(qt)) for h in range(GLA_HEADS)], axis=0)
            a = _dot_nt(q_heads, kt)
            for h in range(GLA_HEADS):
                att[st][h] = jnp.where(masks[lev + 1], a[h * GLA_SUB:(h + 1) * GLA_SUB], att[st][h])
        if lev < 0:
            ahead = now
        elif lev + 1 < GLA_LEVELS:
            ahead = [decay_sums(st, level_blocks(lev + 1)) for st in range(n_sub)]
        else:
            ahead = [decay_sums(st, (0, 1)) for st in range(n_sub)]
        yield
    part = lambda st, i: ahead[st][i]

    lane_v = lax.broadcasted_iota(jnp.int32, (GLA_SUB, hv), 1) // GLA_DV
    inner = []
    for st in range(n_sub):
        vs = sub(vb, st)
        att_all = jnp.concatenate([a.astype(BF16) for a in att[st]], axis=1)
        v_heads = jnp.concatenate([jnp.where(lane_v == h, vs, jnp.zeros_like(vs)) for h in range(GLA_HEADS)], axis=0)
        inner.append(_dot(att_all, v_heads))
    yield

    row_blk = lax.broadcasted_iota(jnp.int32, (GLA_SUB, hk), 0) // GLA_BLOCK
    rs = lax.broadcasted_iota(jnp.int32, (hv, hk), 0) // GLA_DV
    cs = lax.broadcasted_iota(jnp.int32, (hv, hk), 1) // GLA_DK
    head_diag = rs == cs
    n_blk = GLA_SUB // GLA_BLOCK
    outs = []
    for st in range(n_sub):
        b, tail = part(st, 0), part(st, 1)
        q_dec = (sub(q, st) * jnp.exp(b)).astype(BF16)
        k_dec = (sub(k, st) * jnp.exp(tail)).astype(BF16)
        b_end = b + tail
        k_blocks = jnp.concatenate(
            [jnp.where(row_blk == n, k_dec, jnp.zeros_like(k_dec)) for n in range(n_blk)], axis=1)
        upd_all = _dot_tn(sub(vb, st), k_blocks)
        cross = []
        for blk in range(n_blk):
            r0 = blk * GLA_BLOCK
            cross.append(_dot_nt(q_dec[r0:r0 + GLA_BLOCK], state.astype(BF16)))
            upd = upd_all[:, blk * hk:(blk + 1) * hk]
            state = jnp.exp(b_end[r0:r0 + 1, :]) * state + jnp.where(head_diag, upd, 0.0)
        outs.append(inner[st] + jnp.concatenate(cross, axis=0))
    yield

    o = _group_rms(jnp.concatenate(outs, axis=0), norm)
    result["y"] = o * (r * jax.nn.sigmoid(r))
    result["state"] = state


def _dsa_band_bias(offset, n_steps):
    row = lax.broadcasted_iota(jnp.int32, (2 * DSA_BLOCK, 2 * DSA_BLOCK), 0) % DSA_BLOCK
    col = lax.broadcasted_iota(jnp.int32, (2 * DSA_BLOCK, 2 * DSA_BLOCK), 1)
    dist = offset + row - col
    return jnp.where((dist >= 0) & (dist <= n_steps), 0.0, NEG_BIG)


def _dsa_logits(q, kw, bias):
    lane_q = lax.broadcasted_iota(jnp.int32, (DSA_BLOCK, LANES), 1)
    q = q * (HEAD_DIM ** -0.5)
    qm2 = jnp.concatenate(
        [jnp.where((lane_q // HEAD_DIM) == head, q, jnp.zeros_like(q)) for head in range(2)], axis=0)
    return _dot_nt(qm2, kw) + bias


def _dsa_softmax_pv(s, vw, shift):
    if shift is None:
        m = jnp.max(s, axis=1, keepdims=True)
        p = jnp.exp(s - m)
        l = jnp.sum(p, axis=1, keepdims=True)
        o2 = _dot(p.astype(BF16), vw) / l
        lse2 = jnp.broadcast_to(m + jnp.log(l), (2 * DSA_BLOCK, LANES))
    else:
        p = jnp.exp(s)
        pv = _dot(p.astype(BF16), jnp.concatenate([vw, jnp.ones_like(vw)], axis=1))
        l = pv[:, LANES:]
        o2 = pv[:, :LANES] / l
        lse2 = shift + jnp.log(l)
    first = lax.broadcasted_iota(jnp.int32, (DSA_BLOCK, LANES), 1) < HEAD_DIM
    return (jnp.where(first, o2[:DSA_BLOCK], o2[DSA_BLOCK:]),
            jnp.where(first, lse2[:DSA_BLOCK], lse2[DSA_BLOCK:]))


def _dsa_kernel(bound_ref, q1_ref, k1_ref, v1_ref, q4_ref, k4_ref, v4_ref, q16_ref, k16_ref, v16_ref,
                y_ref, o_scr, l_scr):
    seq = q1_ref.shape[1]
    refs = ((q1_ref, k1_ref, v1_ref), (q4_ref, k4_ref, v4_ref), (q16_ref, k16_ref, v16_ref))
    blocks_per_span = DSA_SPAN // DSA_BLOCK
    n_steps = DSA_PATTERNS[0][0] // DSA_PATTERNS[0][1]
    assert all(w // d == n_steps for w, d in DSA_PATTERNS)
    bias_inner = _dsa_band_bias(DSA_BLOCK, n_steps)
    bias_first = _dsa_band_bias(0, n_steps)

    def run(shift):
        inner, first = (bias_inner, bias_first) if shift is None else (bias_inner - shift, bias_first - shift)

        def span(sp, carry):
            for p, (_, dil) in enumerate(DSA_PATTERNS):
                q_ref, k_ref, v_ref = refs[p]
                per_res = blocks_per_span // dil

                def group(gi, c, p=p, dil=dil, per_res=per_res, q_ref=q_ref, k_ref=k_ref, v_ref=v_ref):
                    units = []
                    for u in range(DSA_GROUP):
                        idx = gi * DSA_GROUP + u
                        r = idx // per_res
                        jb = idx % per_res
                        m0 = pl.multiple_of(sp * (DSA_SPAN // dil) + jb * DSA_BLOCK, DSA_BLOCK)
                        start = pl.multiple_of(jnp.maximum(m0 - DSA_BLOCK, 0), DSA_BLOCK)
                        if dil == 1:
                            q = q_ref[0, pl.ds(m0, DSA_BLOCK), :]
                            kw = k_ref[0, pl.ds(start, 2 * DSA_BLOCK), :]
                            vw = v_ref[0, pl.ds(start, 2 * DSA_BLOCK), :]
                        else:
                            q = q_ref[0, r, pl.ds(m0, DSA_BLOCK), :]
                            kw = k_ref[0, r, pl.ds(start, 2 * DSA_BLOCK), :]
                            vw = v_ref[0, r, pl.ds(start, 2 * DSA_BLOCK), :]
                        may_start_sequence = u % per_res == 0
                        s = _dsa_logits(q, kw, jnp.where(m0 == 0, first, inner) if may_start_sequence else inner)
                        rows = pl.ds(r + dil * jb * DSA_BLOCK, DSA_BLOCK, stride=dil) if dil > 1 else \
                            pl.ds(pl.multiple_of(jb * DSA_BLOCK, DSA_BLOCK), DSA_BLOCK)
                        units.append((s, vw, rows))
                    results = [_dsa_softmax_pv(s, vw, shift) for s, vw, _ in units]
                    for (o, lse), (_, _, rows) in zip(results, units):
                        o_scr[p, rows, :] = o
                        l_scr[p, rows, :] = lse
                    return c

                lax.fori_loop(0, blocks_per_span // DSA_GROUP, group, 0)

            l1, l2, l3 = l_scr[0], l_scr[1], l_scr[2]
            lm = jnp.maximum(jnp.maximum(l1, l2), l3)
            e1, e2, e3 = jnp.exp(l1 - lm), jnp.exp(l2 - lm), jnp.exp(l3 - lm)
            y = (e1 * o_scr[0] + e2 * o_scr[1] + e3 * o_scr[2]) / (e1 + e2 + e3)
            y_ref[0, pl.ds(pl.multiple_of(sp * DSA_SPAN, DSA_SPAN), DSA_SPAN), :] = y.astype(y_ref.dtype)
            return carry

        lax.fori_loop(0, seq // DSA_SPAN, span, 0)

    bound = bound_ref[0]

    @pl.when(bound <= DSA_MAX_SHIFT)
    def _():
        run(bound)

    @pl.when(bound > DSA_MAX_SHIFT)
    def _():
        run(None)


def _dsa(dsa1, dsa4, dsa16, bound):
    bsz, seq, _ = dsa1.shape
    nat = lambda off: pl.BlockSpec((1, seq, LANES), lambda b, hp: (b, 0, off + hp))
    res = lambda dil, off: pl.BlockSpec((1, dil, seq // dil, LANES), lambda b, hp: (b, 0, 0, off + hp))
    return pl.pallas_call(
        _dsa_kernel,
        grid=(bsz, 2),
        in_specs=[pl.BlockSpec(memory_space=pltpu.SMEM),
                  nat(0), nat(2), nat(4), res(4, 0), res(4, 2), res(4, 4), res(16, 0), res(16, 2), res(16, 4)],
        out_specs=pl.BlockSpec((1, seq, LANES), lambda b, hp: (b, 0, hp)),
        out_shape=jax.ShapeDtypeStruct((bsz, seq, BRANCH_WIDTH), BF16),
        scratch_shapes=[pltpu.VMEM((3, DSA_SPAN, LANES), F32), pltpu.VMEM((3, DSA_SPAN, LANES), F32)],
        compiler_params=_cparams(("parallel", "parallel")),
        name="dsa",
    )(bound, dsa1, dsa1, dsa1, dsa4, dsa4, dsa4, dsa16, dsa16, dsa16)


def _proj_weights(w_in):
    return jnp.swapaxes(w_in, 1, 2).astype(BF16)


def _rope_tables(seq):
    inv = ROPE_THETA ** (-jnp.arange(0, HEAD_DIM, 2, dtype=F32) / HEAD_DIM)
    ang = jnp.arange(seq, dtype=F32)[:, None] * inv[None, :]
    cos, sin = jnp.cos(ang), jnp.sin(ang)
    cos_t = jnp.tile(jnp.concatenate([cos, cos], axis=1), (1, 2))
    sin_t = jnp.tile(jnp.concatenate([-sin, sin], axis=1), (1, 2))
    return cos_t, sin_t


def _pick_tile(n, pref):
    t = min(pref, n)
    while n % t:
        t //= 2
    return t


def kernel(x, ffn1_norm, ffn1_w_in, ffn1_w_out, mix_norm, w_in, s5_lam_re, s5_lam_im, s5_log_step, s5_b_re, s5_b_im, s5_c_re, s5_c_im, s5_d, s5_w_glu, gla_w_gate, gla_b_gate, gla_norm, dsa_q_norm, dsa_k_norm, w_branch, w_out, ffn2_norm, ffn2_w_in, ffn2_w_out):
    bsz, seq, _ = x.shape
    tokens = bsz * seq
    depth = ffn1_norm.shape[0]
    assert seq % DSA_SPAN == 0 and seq // DSA_PATTERNS[-1][1] >= 2 * DSA_BLOCK and seq >= SB_WIN
    tm_a = _pick_tile(seq, 512)
    tm_b = _pick_tile(tokens, 512)
    s5_rows = seq // S5_CHUNK
    s5_tile = min(S5_ROW_TILE, s5_rows)
    cos_t, sin_t = _rope_tables(seq)

    wp_all = _proj_weights(w_in)
    w1b_all, w2b_all = ffn2_w_in.astype(BF16), ffn2_w_out.astype(BF16)
    wglu_all, wb_all, wo_all = s5_w_glu.astype(BF16), w_branch.astype(BF16), w_out.astype(BF16)
    s5_ops = jax.vmap(functools.partial(_s5_operators, row_tile=s5_tile))(
        s5_lam_re, s5_lam_im, s5_log_step, s5_b_re, s5_b_im, s5_c_re, s5_c_im, s5_d)

    h = x.reshape(tokens, D_MODEL)
    for l in range(depth):
        row = lambda p: p[l].reshape(1, -1)
        qn = jnp.tile(row(dsa_q_norm), (1, 4))
        kn = jnp.tile(row(dsa_k_norm), (1, 4))
        wgate = jnp.pad(gla_w_gate[l], ((0, LANES - GLA_GATE_RANK), (0, 0))).astype(BF16)
        h, sb, s5u, y_gla, y_sb_window, sb_left, dsa1, dsa4, dsa16 = _ffn_proj(
            h, row(ffn1_norm), ffn1_w_in, ffn1_w_out, row(mix_norm), wp_all, qn, kn, cos_t, sin_t,
            wgate, row(gla_b_gate), row(gla_norm), bsz, seq, tm_a, l)

        y_sb = lax.cond(jnp.max(sb_left) > SB_LOG_ZERO,
                        lambda: _stick_breaking(sb.reshape(bsz, seq, 768)).reshape(tokens, BRANCH_WIDTH),
                        lambda: y_sb_window)

        y_s5 = _s5(s5u.reshape(bsz, s5_rows, S5_CHUNK * 256), *s5_ops, l).reshape(tokens // S5_CHUNK, S5_CHUNK * 256)

        bound = (1.02 * HEAD_DIM ** 0.5 * jnp.max(jnp.abs(dsa_q_norm[l])) * jnp.max(jnp.abs(dsa_k_norm[l]))).reshape(1)
        y_dsa = _dsa(dsa1.reshape(bsz, seq, 768), dsa4, dsa16, bound).reshape(tokens, BRANCH_WIDTH)

        h = _merge_ffn(h, y_sb, y_s5, y_gla, y_dsa, row(mix_norm), wp_all, wglu_all, wb_all, wo_all,
                       row(ffn2_norm), w1b_all, w2b_all, tm_b, l)
    return h.reshape(bsz, seq, D_MODEL)
```
